```python
import math
import jax, jax.numpy as jnp
from jax import lax
import numpy as np

D_MODEL = 1024
BATCH = 16
SEQ = 256
DEPTH = 4
DEC_BATCH = 4
DEC_SEQ = 1024
PAST_LEN = 512

GRID_W = 64
N_MIXERS = 3
N_ATTN_LAYERS = (DEPTH + 2) // N_MIXERS
N_HGRN_LAYERS = (DEPTH + 1) // N_MIXERS
N_SSM_LAYERS = DEPTH // N_MIXERS

ATTN_HEADS = 16
ATTN_KV_HEADS = 4
ATTN_GROUP = ATTN_HEADS // ATTN_KV_HEADS
HEAD_DIM = D_MODEL // ATTN_HEADS
WINDOW = 128
ATTN_BLOCK = 128
ROPE_BASE = 10000.0

HGRN_EXPAND = 128
HGRN_HEADS = D_MODEL // HGRN_EXPAND
HGRN_DK = HGRN_EXPAND
HGRN_DV = D_MODEL // HGRN_HEADS
HGRN_FDIM = HGRN_HEADS * HGRN_DK
HGRN_VDIM = HGRN_HEADS * HGRN_DV
HGRN_CHUNK = 32

SSM_GROUP = 16
SSM_GROUPS = D_MODEL // SSM_GROUP
SSM_STATE = 64

D_FF = 2816
CONV_WIDTH = 3
NORM_EPS = 1e-6

F32 = jnp.float32

kernel_name = 'hybrid_diffusion_swa_hgrn2_s5_step'


def rmsnorm(x, g):
    xf = x.astype(F32)
    y = xf * lax.rsqrt(jnp.mean(xf * xf, axis=-1, keepdims=True) + NORM_EPS)
    return (y * g.astype(F32)).astype(x.dtype)


def ada_modulation(cond, w, b):
    m = jax.nn.silu(cond) @ w + b
    m = m.reshape(m.shape[:-1] + (1, m.shape[-1]))
    return jnp.split(m, 6, axis=-1)


def modulate(x, shift, scale):
    return x * (1 + scale) + shift


def conv_ffn(xn, w_up, conv_w, conv_b, w_down):
    s = xn.shape[1]
    h = xn @ w_up
    half = CONV_WIDTH // 2
    hp = jnp.pad(h, ((0, 0), (half, half), (0, 0)))
    h = sum(hp[:, j:j + s] * conv_w[j] for j in range(CONV_WIDTH)) + conv_b
    gate, val = jnp.split(h, 2, axis=-1)
    return (jax.nn.silu(gate) * val) @ w_down


def axial_rope(s):
    n_rows = s // GRID_W
    row = jnp.repeat(jnp.arange(n_rows, dtype=F32), GRID_W)
    col = jnp.tile(jnp.arange(GRID_W, dtype=F32), n_rows)
    half = HEAD_DIM // 2
    inv_freq = 1.0 / (ROPE_BASE ** (jnp.arange(0, half, 2, dtype=F32) / half))
    ar = row[:, None] * inv_freq
    ac = col[:, None] * inv_freq
    ang = jnp.concatenate([ar, ar, ac, ac], axis=-1)
    return jnp.cos(ang), jnp.sin(ang)


def _rotate_half(x):
    a, b = jnp.split(x, 2, axis=-1)
    return jnp.concatenate([-b, a], axis=-1)


def apply_rope(x, cos, sin):
    shape = (x.shape[1],) + (1,) * (x.ndim - 3) + (HEAD_DIM,)
    cos = cos.reshape(shape)
    sin = sin.reshape(shape)
    xf = x.astype(F32)
    xr, xc = jnp.split(xf, 2, axis=-1)
    rot = jnp.concatenate([_rotate_half(xr), _rotate_half(xc)], axis=-1)
    return (xf * cos + rot * sin).astype(x.dtype)


def attn_qkv(xn, w):
    b, s, _ = xn.shape
    nq = ATTN_HEADS * HEAD_DIM
    nk = ATTN_KV_HEADS * HEAD_DIM
    qkv = xn @ w
    q = qkv[..., :nq].reshape(b, s, ATTN_KV_HEADS, ATTN_GROUP, HEAD_DIM)
    k = qkv[..., nq:nq + nk].reshape(b, s, ATTN_KV_HEADS, HEAD_DIM)
    v = qkv[..., nq + nk:].reshape(b, s, ATTN_KV_HEADS, HEAD_DIM)
    return q, k, v


def context_attention(q, k, v, sink):
    b, l = q.shape[:2]
    s = jnp.einsum('blhgd,bmhd->bhglm', q, k).astype(F32) * HEAD_DIM ** -0.5
    s_sink = jnp.broadcast_to(sink.astype(F32).reshape(1, ATTN_KV_HEADS, ATTN_GROUP, 1, 1), s.shape[:-1] + (1,))
    p = jax.nn.softmax(jnp.concatenate([s, s_sink], axis=-1), axis=-1)[..., :l]
    o = jnp.einsum('bhglm,bmhd->blhgd', p.astype(v.dtype), v)
    return o.reshape(b, l, ATTN_HEADS * HEAD_DIM)


def latent_attention(q, k, v, ck, cv, sink):
    b, s = q.shape[:2]
    blk = ATTN_BLOCK
    nb = s // blk
    scale = HEAD_DIM ** -0.5
    qb = q.reshape(b, nb, blk, ATTN_KV_HEADS, ATTN_GROUP, HEAD_DIM)
    pad = ((0, 0), (blk, blk), (0, 0), (0, 0))
    kp = jnp.pad(k, pad).reshape(b, nb + 2, blk, ATTN_KV_HEADS, HEAD_DIM)
    vp = jnp.pad(v, pad).reshape(b, nb + 2, blk, ATTN_KV_HEADS, HEAD_DIM)
    kb = jnp.concatenate([kp[:, 0:nb], kp[:, 1:nb + 1], kp[:, 2:nb + 2]], axis=2)
    vb = jnp.concatenate([vp[:, 0:nb], vp[:, 1:nb + 1], vp[:, 2:nb + 2]], axis=2)
    rel = jnp.arange(3 * blk)[None, :] - blk - jnp.arange(blk)[:, None]
    band = jnp.abs(rel) <= WINDOW
    kpos = jnp.arange(nb)[:, None] * blk - blk + jnp.arange(3 * blk)[None, :]
    valid = (kpos >= 0) & (kpos < s)
    mask = band[None, :, :] & valid[:, None, :]
    s_lat = jnp.einsum('bnqhgd,bnkhd->bnhgqk', qb, kb).astype(F32) * scale
    s_lat = jnp.where(mask[None, :, None, None], s_lat, -jnp.inf)
    s_ctx = jnp.einsum('bnqhgd,blhd->bnhgql', qb, ck).astype(F32) * scale
    s_sink = jnp.broadcast_to(sink.astype(F32).reshape(1, 1, ATTN_KV_HEADS, ATTN_GROUP, 1, 1), s_lat.shape[:-1] + (1,))
    p = jax.nn.softmax(jnp.concatenate([s_lat, s_ctx, s_sink], axis=-1), axis=-1)
    n_lat = 3 * blk
    n_ctx = ck.shape[1]
    p_lat = p[..., :n_lat].astype(v.dtype)
    p_ctx = p[..., n_lat:n_lat + n_ctx].astype(cv.dtype)
    o = jnp.einsum('bnhgqk,bnkhd->bnqhgd', p_lat, vb) + jnp.einsum('bnhgql,blhd->bnqhgd', p_ctx, cv)
    return o.reshape(b, s, ATTN_HEADS * HEAD_DIM)


def hgrn_chunk_scan(q, k, v, log_f, s0):
    b, s, h, dk = q.shape
    dv = v.shape[-1]
    c = HGRN_CHUNK
    nc = s // c

    def chunks(t):
        return t.reshape(b, nc, c, h, t.shape[-1]).transpose(1, 0, 3, 2, 4)

    causal = jnp.tril(jnp.ones((c, c), dtype=bool))[:, :, None]

    def step(state, xs):
        qc, kc, vc, gc = xs
        cum = jnp.cumsum(gc, axis=2)
        inter = jnp.einsum('bhtd,bhdv->bhtv', qc * jnp.exp(cum), state)
        diff = cum[:, :, :, None, :] - cum[:, :, None, :, :]
        decay = jnp.exp(jnp.where(causal, diff, -jnp.inf))
        att = jnp.einsum('bhtsd,bhsd->bhts', qc[:, :, :, None, :] * decay, kc)
        out = inter + jnp.einsum('bhts,bhsv->bhtv', att, vc)
        last = cum[:, :, -1, :]
        k_dec = kc * jnp.exp(last[:, :, None, :] - cum)
        new_state = jnp.exp(last)[..., None] * state + jnp.einsum('bhsd,bhsv->bhdv', k_dec, vc)
        return new_state, out

    s_fin, o = lax.scan(step, s0, (chunks(q), chunks(k), chunks(v), chunks(log_f)))
    o = o.transpose(1, 0, 3, 2, 4).reshape(b, s, h, dv)
    return o, s_fin


def hgrn_mix(xn, w_in, lb, g_norm, w_o, s0_f, s0_b):
    b, s, _ = xn.shape
    proj = (xn @ w_in).astype(F32)
    cuts = (HGRN_FDIM, HGRN_FDIM + HGRN_VDIM, 2 * HGRN_FDIM + HGRN_VDIM, 3 * HGRN_FDIM + HGRN_VDIM)
    q, i, zf, zb, g = jnp.split(proj, cuts, axis=-1)

    def heads(t):
        return t.reshape(b, s, HGRN_HEADS, t.shape[-1] // HGRN_HEADS)

    q, i, zf, zb, g = heads(q), heads(i), heads(zf), heads(zb), heads(g)

    def log_forget(z, lbd):
        lbd = lbd.astype(F32).reshape(HGRN_HEADS, HGRN_DK)
        return jnp.logaddexp(jnp.log(lbd), jnp.log1p(-lbd) + jax.nn.log_sigmoid(z))

    lf_f = log_forget(zf, lb[0])
    lf_b = log_forget(zb, lb[1])
    o_f, s_f = hgrn_chunk_scan(q, -jnp.expm1(lf_f), i, lf_f, s0_f)
    o_b, s_b = hgrn_chunk_scan(jnp.flip(q, 1), jnp.flip(-jnp.expm1(lf_b), 1), jnp.flip(i, 1), jnp.flip(lf_b, 1), s0_b)
    o = o_f + jnp.flip(o_b, 1)
    o = o * lax.rsqrt(jnp.mean(o * o, axis=-1, keepdims=True) + NORM_EPS) * g_norm.astype(F32) * jax.nn.silu(g)
    return o.reshape(b, s, HGRN_VDIM).astype(xn.dtype) @ w_o, s_f, s_b


def ssm_discretize(a_re, a_im, log_dt, b_re, b_im):
    lam = lax.complex(jnp.minimum(a_re.astype(F32), -1e-4), a_im.astype(F32))
    dt = jnp.exp(log_dt.astype(F32))[:, None]
    lam_bar = jnp.exp(lam * dt)
    b_bar = ((lam_bar - 1.0) / lam)[..., None] * lax.complex(b_re.astype(F32), b_im.astype(F32))
    return lam_bar, b_bar


def ssm_scan(u, lam_bar, b_bar, h0):
    bu = jnp.einsum('bsgi,gpi->bsgp', u.astype(jnp.complex64), b_bar)
    bu = bu.at[:, 0].add(lam_bar * h0)
    a = jnp.broadcast_to(lam_bar, bu.shape)

    def combine(e1, e2):
        a1, b1 = e1
        a2, b2 = e2
        return a1 * a2, a2 * b1 + b2

    _, h = lax.associative_scan(combine, (a, bu), axis=1)
    return h


def ssm_mix(xn, a_re, a_im, log_dt, b_re, b_im, c_re, c_im, d, w_glu, h0_f, h0_b):
    b, s, dm = xn.shape
    u = xn.astype(F32).reshape(b, s, SSM_GROUPS, SSM_GROUP)
    y = d.astype(F32) * xn.astype(F32)
    finals = []
    for direction, h0 in enumerate((h0_f, h0_b)):
        lam_bar, b_bar = ssm_discretize(a_re[direction], a_im[direction], log_dt[direction], b_re[direction], b_im[direction])
        ud = u if direction == 0 else jnp.flip(u, axis=1)
        h = ssm_scan(ud, lam_bar, b_bar, h0)
        c_mat = lax.complex(c_re[direction].astype(F32), c_im[direction].astype(F32))
        yd = jnp.einsum('gip,bsgp->bsgi', c_mat, h).real.reshape(b, s, dm)
        y = y + (yd if direction == 0 else jnp.flip(yd, axis=1))
        finals.append(h[:, -1])
    g = jax.nn.gelu(y).astype(xn.dtype)
    val, gate = jnp.split(g @ w_glu, 2, axis=-1)
    return val * jax.nn.sigmoid(gate), finals[0], finals[1]


def to_complex(st):
    st = st.astype(F32)
    return lax.complex(st[..., 0], st[..., 1])


def from_complex(h):
    return jnp.stack([h.real, h.imag], axis=-1)


def setup_inputs(seed: int = 0) -> dict:
    key = jax.random.key(seed)
    ks = iter(jax.random.split(key, 40))

    def nrm(shape, scale):
        return scale * jax.random.normal(next(ks), shape, F32)

    D = D_MODEL
    qkv_w = (ATTN_HEADS + 2 * ATTN_KV_HEADS) * HEAD_DIM
    hg_in = 3 * HGRN_FDIM + 2 * HGRN_VDIM
    ssm_gp = (N_SSM_LAYERS, 2, SSM_GROUPS, SSM_STATE)
    n_idx = jnp.arange(SSM_STATE, dtype=F32)
    return {
        'x_prompt': nrm((BATCH, SEQ, D), 1.0),
        'x_sample': nrm((DEC_BATCH, DEC_SEQ, D), 1.0),
        'cache_k': nrm((DEC_BATCH, N_ATTN_LAYERS, PAST_LEN, ATTN_KV_HEADS, HEAD_DIM), 1.0),
        'cache_v': nrm((DEC_BATCH, N_ATTN_LAYERS, PAST_LEN, ATTN_KV_HEADS, HEAD_DIM), 1.0),
        'state_hgrn': nrm((DEC_BATCH, N_HGRN_LAYERS, 2, HGRN_HEADS, HGRN_DK, HGRN_DV), 0.5),
        'state_ssm': nrm((DEC_BATCH, N_SSM_LAYERS, 2, SSM_GROUPS, SSM_STATE, 2), 0.1),
        'c': nrm((DEC_BATCH, D), 1.0),
        'c_ctx': nrm((D,), 1.0),
        'ada_w': nrm((DEPTH, D, 6 * D), 0.5 * D ** -0.5),
        'ada_b': nrm((DEPTH, 6 * D), 0.01),
        'norm1_g': 1.0 + nrm((DEPTH, D), 0.02),
        'norm2_g': 1.0 + nrm((DEPTH, D), 0.02),
        'attn_wqkv': nrm((N_ATTN_LAYERS, D, qkv_w), D ** -0.5),
        'attn_wo': nrm((N_ATTN_LAYERS, ATTN_HEADS * HEAD_DIM, D), (ATTN_HEADS * HEAD_DIM) ** -0.5),
        'attn_sink': nrm((N_ATTN_LAYERS, ATTN_HEADS), 0.5),
        'hgrn_w_in': nrm((N_HGRN_LAYERS, D, hg_in), D ** -0.5),
        'hgrn_lb': nrm((DEPTH, 2, HGRN_FDIM), 0.5),
        'hgrn_g_norm': 1.0 + nrm((N_HGRN_LAYERS, HGRN_DV), 0.02),
        'hgrn_wo': nrm((N_HGRN_LAYERS, HGRN_VDIM, D), HGRN_VDIM ** -0.5),
        'ssm_a_re': -0.5 + nrm(ssm_gp, 0.01),
        'ssm_a_im': math.pi * n_idx + nrm(ssm_gp, 0.01),
        'ssm_log_dt': jax.random.uniform(next(ks), (N_SSM_LAYERS, 2, SSM_GROUPS), F32, math.log(1e-3), math.log(1e-1)),
        'ssm_b_re': nrm(ssm_gp + (SSM_GROUP,), (2 * SSM_GROUP) ** -0.5),
        'ssm_b_im': nrm(ssm_gp + (SSM_GROUP,), (2 * SSM_GROUP) ** -0.5),
        'ssm_c_re': nrm((N_SSM_LAYERS, 2, SSM_GROUPS, SSM_GROUP, SSM_STATE), SSM_STATE ** -0.5),
        'ssm_c_im': nrm((N_SSM_LAYERS, 2, SSM_GROUPS, SSM_GROUP, SSM_STATE), SSM_STATE ** -0.5),
        'ssm_d': nrm((N_SSM_LAYERS, D), 0.5),
        'ssm_w_glu': nrm((N_SSM_LAYERS, D, 2 * D), D ** -0.5),
        'ffn_w_up': nrm((DEPTH, D, 2 * D_FF), D ** -0.5),
        'ffn_conv_w': nrm((DEPTH, CONV_WIDTH, 2 * D_FF), 0.6),
        'ffn_conv_b': nrm((DEPTH, 2 * D_FF), 0.01),
        'ffn_w_down': nrm((DEPTH, D_FF, D), D_FF ** -0.5),
        'final_g': 1.0 + nrm((D,), 0.02),
    }


def reference(x_prompt, x_sample, cache_k, cache_v, state_hgrn, state_ssm, c, c_ctx,
              ada_w, ada_b, norm1_g, norm2_g, attn_wqkv, attn_wo, attn_sink,
              hgrn_w_in, hgrn_lb, hgrn_g_norm, hgrn_wo,
              ssm_a_re, ssm_a_im, ssm_log_dt, ssm_b_re, ssm_b_im, ssm_c_re, ssm_c_im, ssm_d, ssm_w_glu,
              ffn_w_up, ffn_conv_w, ffn_conv_b, ffn_w_down, final_g):
    bp = x_prompt.shape[0]
    s_lat = x_sample.shape[1]
    rope_cos, rope_sin = axial_rope(s_lat)
    lb_p = jax.nn.softmax(hgrn_lb.astype(F32), axis=0)
    lb_all = jnp.cumsum(lb_p, axis=0) - lb_p[:1]

    xp, xs = x_prompt, x_sample
    new_k, new_v, new_hgrn, new_ssm = [], [], [], []
    for l in range(DEPTH):
        kind, j = l % N_MIXERS, l // N_MIXERS
        mod_p = ada_modulation(c_ctx, ada_w[l], ada_b[l])
        mod_s = ada_modulation(c, ada_w[l], ada_b[l])
        hp = modulate(rmsnorm(xp, norm1_g[l]), mod_p[0], mod_p[1])
        hs = modulate(rmsnorm(xs, norm1_g[l]), mod_s[0], mod_s[1])
        if kind == 0:
            q, k, v = attn_qkv(hp, attn_wqkv[j])
            op = context_attention(q, k, v, attn_sink[j]) @ attn_wo[j]
            new_k.append(k)
            new_v.append(v)
            q, k, v = attn_qkv(hs, attn_wqkv[j])
            q = apply_rope(q, rope_cos, rope_sin)
            k = apply_rope(k, rope_cos, rope_sin)
            os_ = latent_attention(q, k, v, cache_k[:, j], cache_v[:, j], attn_sink[j]) @ attn_wo[j]
        elif kind == 1:
            z = jnp.zeros((bp, HGRN_HEADS, HGRN_DK, HGRN_DV), F32)
            op, sf, sb = hgrn_mix(hp, hgrn_w_in[j], lb_all[l], hgrn_g_norm[j], hgrn_wo[j], z, z)
            new_hgrn.append(jnp.stack([sf, sb], axis=1))
            os_, _, _ = hgrn_mix(hs, hgrn_w_in[j], lb_all[l], hgrn_g_norm[j], hgrn_wo[j],
                                 state_hgrn[:, j, 0].astype(F32), state_hgrn[:, j, 1].astype(F32))
        else:
            ssm_args = (ssm_a_re[j], ssm_a_im[j], ssm_log_dt[j], ssm_b_re[j], ssm_b_im[j],
                        ssm_c_re[j], ssm_c_im[j], ssm_d[j], ssm_w_glu[j])
            z = jnp.zeros((bp, SSM_GROUPS, SSM_STATE), jnp.complex64)
            op, hf, hb = ssm_mix(hp, *ssm_args, z, z)
            new_ssm.append(jnp.stack([from_complex(hf), from_complex(hb)], axis=1))
            os_, _, _ = ssm_mix(hs, *ssm_args, to_complex(state_ssm[:, j, 0]), to_complex(state_ssm[:, j, 1]))
        xp = xp + mod_p[2] * op
        xs = xs + mod_s[2] * os_
        hp = modulate(rmsnorm(xp, norm2_g[l]), mod_p[3], mod_p[4])
        hs = modulate(rmsnorm(xs, norm2_g[l]), mod_s[3], mod_s[4])
        xp = xp + mod_p[5] * conv_ffn(hp, ffn_w_up[l], ffn_conv_w[l], ffn_conv_b[l], ffn_w_down[l])
        xs = xs + mod_s[5] * conv_ffn(hs, ffn_w_up[l], ffn_conv_w[l], ffn_conv_b[l], ffn_w_down[l])

    y_prompt = rmsnorm(xp, final_g)
    y_sample = rmsnorm(xs, final_g)
    new_cache_k = jnp.stack(new_k, axis=1)
    new_cache_v = jnp.stack(new_v, axis=1)
    new_state_hgrn = jnp.stack(new_hgrn, axis=1)
    new_state_ssm = jnp.stack(new_ssm, axis=1)
    return (y_prompt, y_sample, new_cache_k, new_cache_v, new_state_hgrn, new_state_ssm)
```

```python
import functools
import math

import jax
import jax.numpy as jnp
import numpy as np
from jax import lax
from jax.experimental import pallas as pl
from jax.experimental.pallas import tpu as pltpu

F32 = jnp.float32
BF16 = jnp.bfloat16

D = 1024
BATCH = 16
SEQ = 256
DEPTH = 4
DEC_BATCH = 4
DEC_SEQ = 1024
PAST_LEN = 512
GRID_W = 64
N_MIXERS = 3
ATTN_HEADS = 16
ATTN_KV_HEADS = 4
ATTN_GROUP = ATTN_HEADS // ATTN_KV_HEADS
HEAD_DIM = D // ATTN_HEADS
WINDOW = 128
ROPE_BASE = 10000.0
HGRN_HEADS = 8
HGRN_DK = 128
HGRN_DV = 128
SSM_GROUP = 16
SSM_GROUPS = D // SSM_GROUP
SSM_STATE = 64
D_FF = 2816
NORM_EPS = 1e-6

ROWS_P = BATCH * SEQ
ROWS_S = DEC_BATCH * DEC_SEQ
ROWS = ROWS_P + ROWS_S
ROW_TILE = 1024
N_ROW_TILES = ROWS // ROW_TILE
N_ROW_TILES_P = ROWS_P // ROW_TILE
MOD_ROWS = 8
V7X_VMEM_BYTES = 64 * 1024 * 1024


def _mod_row(i):
    return jnp.where(i < N_ROW_TILES_P, 0, i - N_ROW_TILES_P + 1)


def _cparams(semantics, vmem_bytes):
    vmem = int(min(max(vmem_bytes * 5 // 4 + (4 << 20), 16 << 20), V7X_VMEM_BYTES - (6 << 20)))
    return pltpu.CompilerParams(dimension_semantics=semantics, vmem_limit_bytes=vmem)


def _bdot(a, b):
    return jnp.dot(a.astype(BF16), b.astype(BF16), preferred_element_type=F32)


def _norm_mod(x, g, shift, scale):
    y = x * lax.rsqrt(jnp.mean(x * x, axis=-1, keepdims=True) + NORM_EPS) * g
    return y * (1.0 + scale) + shift


def _sigmoid(x):
    return 1.0 / (1.0 + jnp.exp(-x))


def _ada_kernel(c_ref, w_ref, b_ref, o_ref):
    c = c_ref[...]
    o_ref[...] = _bdot(c * _sigmoid(c), w_ref[...]) + b_ref[...]


def ada_modulation(cond8, ada_w, ada_b):
    tn = 1024
    out = pl.pallas_call(
        _ada_kernel,
        out_shape=jax.ShapeDtypeStruct((DEPTH, MOD_ROWS, 6 * D), F32),
        grid=(DEPTH, 6 * D // tn),
        in_specs=[
            pl.BlockSpec((MOD_ROWS, D), lambda l, j: (0, 0)),
            pl.BlockSpec((None, D, tn), lambda l, j: (l, 0, j)),
            pl.BlockSpec((None, 1, tn), lambda l, j: (l, 0, j)),
        ],
        out_specs=pl.BlockSpec((None, MOD_ROWS, tn), lambda l, j: (l, 0, j)),
        compiler_params=_cparams(("arbitrary", "arbitrary"), 2 * D * tn * 4),
        name="ada_modulation",
    )(cond8, ada_w, ada_b.reshape(DEPTH, 1, 6 * D))
    return out.reshape(DEPTH, MOD_ROWS, 6, D)


def _nmm_kernel(x_ref, g_ref, mod_ref, w_ref, o_ref, h_ref, *, shift_idx):
    @pl.when(pl.program_id(1) == 0)
    def _():
        h = _norm_mod(x_ref[...], g_ref[...], mod_ref[shift_idx:shift_idx + 1, :],
                      mod_ref[shift_idx + 1:shift_idx + 2, :])
        h_ref[...] = h.astype(BF16)

    o_ref[...] = jnp.dot(h_ref[...], w_ref[...].astype(BF16), preferred_element_type=F32)


def norm_mod_matmul(x, g, mod, w, shift_idx, name):
    n = w.shape[1]
    tn = 512
    return pl.pallas_call(
        functools.partial(_nmm_kernel, shift_idx=shift_idx),
        out_shape=jax.ShapeDtypeStruct((ROWS, n), F32),
        grid=(N_ROW_TILES, n // tn),
        in_specs=[
            pl.BlockSpec((ROW_TILE, D), lambda i, j: (i, 0)),
            pl.BlockSpec((1, D), lambda i, j: (0, 0)),
            pl.BlockSpec((None, 6, D), lambda i, j: (_mod_row(i), 0, 0)),
            pl.BlockSpec((D, tn), lambda i, j: (0, j)),
        ],
        out_specs=pl.BlockSpec((ROW_TILE, tn), lambda i, j: (i, j)),
        scratch_shapes=[pltpu.VMEM((ROW_TILE, D), BF16)],
        compiler_params=_cparams(("arbitrary", "arbitrary"),
                                 2 * ROW_TILE * D * 4 + ROW_TILE * D * 2 + 2 * D * tn * 4 + 2 * ROW_TILE * tn * 4),
        name=name,
    )(x, g.reshape(1, D), mod, w)


def _mm_res_kernel(a_ref, w_ref, x_ref, mod_ref, o_ref, *, gate_idx):
    y = _bdot(a_ref[...], w_ref[...])
    o_ref[...] = x_ref[...] + mod_ref[gate_idx:gate_idx + 1, :] * y


def matmul_gated_residual(a, w, x, mod, gate_idx, name):
    k = a.shape[1]
    return pl.pallas_call(
        functools.partial(_mm_res_kernel, gate_idx=gate_idx),
        out_shape=jax.ShapeDtypeStruct((ROWS, D), F32),
        grid=(N_ROW_TILES,),
        in_specs=[
            pl.BlockSpec((ROW_TILE, k), lambda i: (i, 0)),
            pl.BlockSpec((k, D), lambda i: (0, 0)),
            pl.BlockSpec((ROW_TILE, D), lambda i: (i, 0)),
            pl.BlockSpec((None, 6, D), lambda i: (_mod_row(i), 0, 0)),
        ],
        out_specs=pl.BlockSpec((ROW_TILE, D), lambda i: (i, 0)),
        compiler_params=_cparams(("arbitrary",), 2 * ROW_TILE * k * 4 + 2 * k * D * 4 + 4 * ROW_TILE * D * 4),
        name=name,
    )(a, w, x, mod)


FFN_CHUNK = 256
CONV_PAD = 8


def _ffn_kernel(x_ref, g_ref, mod_ref, wg_ref, wv_ref, cwg_ref, cwv_ref, cbg_ref, cbv_ref, wd_ref,
                o_ref, h_ref, pad_ref):
    i = pl.program_id(0)
    f = pl.program_id(1)

    @pl.when(f == 0)
    def _():
        h = _norm_mod(x_ref[...], g_ref[...], mod_ref[3:4, :], mod_ref[4:5, :])
        h_ref[...] = h.astype(BF16)
        pad_ref[0:CONV_PAD, :] = jnp.zeros((CONV_PAD, 2 * FFN_CHUNK), F32)
        pad_ref[CONV_PAD + ROW_TILE:, :] = jnp.zeros((CONV_PAD, 2 * FFN_CHUNK), F32)

    hb = h_ref[...]
    up = jnp.concatenate(
        [jnp.dot(hb, wg_ref[...].astype(BF16), preferred_element_type=F32),
         jnp.dot(hb, wv_ref[...].astype(BF16), preferred_element_type=F32)], axis=-1)
    pad_ref[CONV_PAD:CONV_PAD + ROW_TILE, :] = up
    seq_len = jnp.where(i < N_ROW_TILES_P, SEQ, DEC_SEQ)
    pos = lax.broadcasted_iota(jnp.int32, (ROW_TILE, 1), 0) & (seq_len - 1)
    prev = jnp.where(pos == 0, 0.0, pad_ref[CONV_PAD - 1:CONV_PAD - 1 + ROW_TILE, :])
    nxt = jnp.where(pos == seq_len - 1, 0.0, pad_ref[CONV_PAD + 1:CONV_PAD + 1 + ROW_TILE, :])
    cw = jnp.concatenate([cwg_ref[...], cwv_ref[...]], axis=-1)
    cb = jnp.concatenate([cbg_ref[...], cbv_ref[...]], axis=-1)
    conv = prev * cw[0:1, :] + up * cw[1:2, :] + nxt * cw[2:3, :] + cb
    gate = conv[:, :FFN_CHUNK]
    val = conv[:, FFN_CHUNK:]
    act = (gate * _sigmoid(gate)) * val
    part = _bdot(act, wd_ref[...])

    @pl.when(f == 0)
    def _():
        o_ref[...] = part

    @pl.when(f > 0)
    def _():
        o_ref[...] += part

    @pl.when(f == pl.num_programs(1) - 1)
    def _():
        o_ref[...] = x_ref[...] + mod_ref[5:6, :] * o_ref[...]


def conv_ffn_residual(x, g, mod, w_up, conv_w, conv_b, w_down):
    nf = D_FF // FFN_CHUNK
    cb = conv_b.reshape(1, 2 * D_FF)
    vmem = (4 * ROW_TILE * D * 4 + ROW_TILE * D * 2 + (ROW_TILE + 2 * CONV_PAD) * 2 * FFN_CHUNK * 4
            + 2 * 3 * D * FFN_CHUNK * 4 + 8 * ROW_TILE * 2 * FFN_CHUNK * 4)
    return pl.pallas_call(
        _ffn_kernel,
        out_shape=jax.ShapeDtypeStruct((ROWS, D), F32),
        grid=(N_ROW_TILES, nf),
        in_specs=[
            pl.BlockSpec((ROW_TILE, D), lambda i, f: (i, 0)),
            pl.BlockSpec((1, D), lambda i, f: (0, 0)),
            pl.BlockSpec((None, 6, D), lambda i, f: (_mod_row(i), 0, 0)),
            pl.BlockSpec((D, FFN_CHUNK), lambda i, f: (0, f)),
            pl.BlockSpec((D, FFN_CHUNK), lambda i, f: (0, nf + f)),
            pl.BlockSpec((3, FFN_CHUNK), lambda i, f: (0, f)),
            pl.BlockSpec((3, FFN_CHUNK), lambda i, f: (0, nf + f)),
            pl.BlockSpec((1, FFN_CHUNK), lambda i, f: (0, f)),
            pl.BlockSpec((1, FFN_CHUNK), lambda i, f: (0, nf + f)),
            pl.BlockSpec((FFN_CHUNK, D), lambda i, f: (f, 0)),
        ],
        out_specs=pl.BlockSpec((ROW_TILE, D), lambda i, f: (i, 0)),
        scratch_shapes=[pltpu.VMEM((ROW_TILE, D), BF16),
                        pltpu.VMEM((ROW_TILE + 2 * CONV_PAD, 2 * FFN_CHUNK), F32)],
        compiler_params=_cparams(("arbitrary", "arbitrary"), vmem),
        name="conv_ffn",
    )(x, g.reshape(1, D), mod, w_up, w_up, conv_w, conv_w, cb, cb, w_down)


def _final_norm_kernel(x_ref, g_ref, o_ref):
    x = x_ref[...]
    o_ref[...] = x * lax.rsqrt(jnp.mean(x * x, axis=-1, keepdims=True) + NORM_EPS) * g_ref[...]


def final_norm(x, g):
    return pl.pallas_call(
        _final_norm_kernel,
        out_shape=jax.ShapeDtypeStruct((ROWS, D), F32),
        grid=(N_ROW_TILES,),
        in_specs=[pl.BlockSpec((ROW_TILE, D), lambda i: (i, 0)),
                  pl.BlockSpec((1, D), lambda i: (0, 0))],
        out_specs=pl.BlockSpec((ROW_TILE, D), lambda i: (i, 0)),
        compiler_params=_cparams(("arbitrary",), 4 * ROW_TILE * D * 4),
        name="final_norm",
    )(x, g.reshape(1, D))


NQ = ATTN_HEADS * HEAD_DIM
NKV = ATTN_KV_HEADS * HEAD_DIM
ATTN_SCALE = HEAD_DIM ** -0.5
Q_BLOCK = 128
MASKED = -1e30


def _dot_t(a, b):
    return lax.dot_general(a.astype(BF16), b.astype(BF16), (((1,), (1,)), ((), ())),
                           preferred_element_type=F32)


def _group_rows(q, h, rows):
    return jnp.concatenate(
        [q[:, (ATTN_GROUP * h + g) * HEAD_DIM:(ATTN_GROUP * h + g + 1) * HEAD_DIM] for g in range(ATTN_GROUP)], axis=0)


def _sink_rows(sink_ref, h, rows):
    return jnp.concatenate(
        [jnp.broadcast_to(sink_ref[0:1, ATTN_GROUP * h + g:ATTN_GROUP * h + g + 1], (rows, 1))
         for g in range(ATTN_GROUP)], axis=0)


def _ctx_attn_kernel(qkv_ref, sink_ref, o_ref):
    outs = []
    for h in range(ATTN_KV_HEADS):
        k = qkv_ref[:, NQ + h * HEAD_DIM:NQ + (h + 1) * HEAD_DIM]
        v = qkv_ref[:, NQ + NKV + h * HEAD_DIM:NQ + NKV + (h + 1) * HEAD_DIM]
        q4 = _group_rows(qkv_ref[:, ATTN_GROUP * h * HEAD_DIM:ATTN_GROUP * (h + 1) * HEAD_DIM], 0, SEQ)
        s = _dot_t(q4, k) * ATTN_SCALE
        sk = _sink_rows(sink_ref, h, SEQ)
        m = jnp.maximum(jnp.max(s, axis=-1, keepdims=True), sk)
        p = jnp.exp(s - m)
        denom = jnp.sum(p, axis=-1, keepdims=True) + jnp.exp(sk - m)
        o4 = _bdot(p, v) / denom
        outs += [o4[g * SEQ:(g + 1) * SEQ, :] for g in range(ATTN_GROUP)]
    o_ref[...] = jnp.concatenate(outs, axis=-1)


def context_attention(qkv, sink):
    return pl.pallas_call(
        _ctx_attn_kernel,
        out_shape=jax.ShapeDtypeStruct((ROWS, NQ), F32),
        grid=(BATCH,),
        in_specs=[pl.BlockSpec((SEQ, NQ + 2 * NKV), lambda b: (b, 0)),
                  pl.BlockSpec((1, ATTN_HEADS), lambda b: (0, 0))],
        out_specs=pl.BlockSpec((SEQ, NQ), lambda b: (b, 0)),
        compiler_params=_cparams(("arbitrary",), 2 * SEQ * (2 * NQ + 2 * NKV) * 4 + 24 * SEQ * ATTN_GROUP * SEQ * 4),
        name="context_attention",
    )(qkv, sink.reshape(1, ATTN_HEADS))


def _rope(x, cos, sin_a, sin_b):
    outs = []
    for c in range(x.shape[1] // 128):
        s = x[:, c * 128:(c + 1) * 128]
        outs.append(s * cos + pltpu.roll(s, 128 - HEAD_DIM // 4, 1) * sin_a + pltpu.roll(s, HEAD_DIM // 4, 1) * sin_b)
    return jnp.concatenate(outs, axis=-1)


def _lat_attn_kernel(q_ref, kp_ref, kc_ref, kn_ref, vp_ref, vc_ref, vn_ref, ck_ref, cv_ref,
                     cos_ref, sa_ref, sb_ref, sink_ref, prev_ref, o_ref):
    del prev_ref
    n = pl.program_id(1)
    nb = pl.num_programs(1)

    def tables(blk):
        r = pl.ds(pl.multiple_of(blk * Q_BLOCK, Q_BLOCK), Q_BLOCK)
        return cos_ref[r, :], sa_ref[r, :], sb_ref[r, :]

    qr = _rope(q_ref[...], *tables(n))
    k3 = jnp.concatenate([
        _rope(kp_ref[...], *tables(jnp.maximum(n - 1, 0))),
        _rope(kc_ref[...], *tables(n)),
        _rope(kn_ref[...], *tables(jnp.minimum(n + 1, nb - 1)))], axis=0)
    v3 = jnp.concatenate([vp_ref[...], vc_ref[...], vn_ref[...]], axis=0)

    rows = ATTN_GROUP * Q_BLOCK
    qpos = lax.broadcasted_iota(jnp.int32, (rows, 3 * Q_BLOCK), 0) & (Q_BLOCK - 1)
    kcol = lax.broadcasted_iota(jnp.int32, (rows, 3 * Q_BLOCK), 1)
    rel = kcol - Q_BLOCK - qpos
    kpos = kcol + (n - 1) * Q_BLOCK
    mask = (jnp.abs(rel) <= WINDOW) & (kpos >= 0) & (kpos < nb * Q_BLOCK)

    outs = []
    for h in range(ATTN_KV_HEADS):
        hs = slice(h * HEAD_DIM, (h + 1) * HEAD_DIM)
        q4 = _group_rows(qr[:, ATTN_GROUP * h * HEAD_DIM:ATTN_GROUP * (h + 1) * HEAD_DIM], 0, Q_BLOCK)
        s_lat = jnp.where(mask, _dot_t(q4, k3[:, hs]) * ATTN_SCALE, MASKED)
        s_ctx = _dot_t(q4, ck_ref[:, hs]) * ATTN_SCALE
        sk = _sink_rows(sink_ref, h, Q_BLOCK)
        m = jnp.maximum(jnp.maximum(jnp.max(s_lat, axis=-1, keepdims=True),
                                    jnp.max(s_ctx, axis=-1, keepdims=True)), sk)
        p_lat = jnp.exp(s_lat - m)
        p_ctx = jnp.exp(s_ctx - m)
        denom = (jnp.sum(p_lat, axis=-1, keepdims=True) + jnp.sum(p_ctx, axis=-1, keepdims=True)
                 + jnp.exp(sk - m))
        o4 = (_bdot(p_lat, v3[:, hs]) + _bdot(p_ctx, cv_ref[:, hs])) / denom
        outs += [o4[g * Q_BLOCK:(g + 1) * Q_BLOCK, :] for g in range(ATTN_GROUP)]
    o_ref[...] = jnp.concatenate(outs, axis=-1)


def _rope_tables():
    t = np.arange(DEC_SEQ)
    half = HEAD_DIM // 2
    inv_freq = 1.0 / (ROPE_BASE ** (np.arange(0, half, 2, dtype=np.float32) / half))
    ar = (t // GRID_W).astype(np.float32)[:, None] * inv_freq
    ac = (t % GRID_W).astype(np.float32)[:, None] * inv_freq
    return jnp.concatenate([jnp.asarray(a) for a in (ar, ar, ac, ac)] * 2, axis=-1)


def latent_attention(qkv, cache_k, cache_v, layer_j, sink, attn_out):
    ang = _rope_tables()
    cos, sin = jnp.cos(ang), jnp.sin(ang)
    first = (lax.broadcasted_iota(jnp.int32, ang.shape, 1) % (HEAD_DIM // 2)) < HEAD_DIM // 4
    sin_a = jnp.where(first, -sin, 0.0)
    sin_b = jnp.where(first, 0.0, sin)
    nb = DEC_SEQ // Q_BLOCK
    base = ROWS_P // Q_BLOCK
    kcol, vcol = NQ // NKV, NQ // NKV + 1
    ck = cache_k.reshape(DEC_BATCH, -1, PAST_LEN, NKV)
    cv = cache_v.reshape(DEC_BATCH, -1, PAST_LEN, NKV)

    def kv_spec(col, off):
        return pl.BlockSpec((Q_BLOCK, NKV),
                            lambda b, n: (base + b * nb + jnp.clip(n + off, 0, nb - 1), col))

    table = pl.BlockSpec((DEC_SEQ, 128), lambda b, n: (0, 0))
    return pl.pallas_call(
        _lat_attn_kernel,
        out_shape=jax.ShapeDtypeStruct((ROWS, NQ), F32),
        grid=(DEC_BATCH, nb),
        in_specs=[pl.BlockSpec((Q_BLOCK, NQ), lambda b, n: (base + b * nb + n, 0)),
                  kv_spec(kcol, -1), kv_spec(kcol, 0), kv_spec(kcol, 1),
                  kv_spec(vcol, -1), kv_spec(vcol, 0), kv_spec(vcol, 1),
                  pl.BlockSpec((None, None, PAST_LEN, NKV), lambda b, n: (b, layer_j, 0, 0)),
                  pl.BlockSpec((None, None, PAST_LEN, NKV), lambda b, n: (b, layer_j, 0, 0)),
                  table, table, table,
                  pl.BlockSpec((1, ATTN_HEADS), lambda b, n: (0, 0)),
                  pl.BlockSpec(memory_space=pl.ANY)],
        out_specs=pl.BlockSpec((Q_BLOCK, NQ), lambda b, n: (base + b * nb + n, 0)),
        input_output_aliases={13: 0},
        compiler_params=_cparams(("arbitrary", "arbitrary"),
                                 4 * Q_BLOCK * NQ * 4 + 12 * Q_BLOCK * NKV * 4 + 4 * PAST_LEN * NKV * 4
                                 + 6 * DEC_SEQ * 128 * 4 + 24 * ATTN_GROUP * Q_BLOCK * (3 * Q_BLOCK + PAST_LEN) * 4),
        name="latent_attention",
    )(qkv, qkv, qkv, qkv, qkv, qkv, qkv, ck, cv, cos, sin_a, sin_b, sink.reshape(1, ATTN_HEADS), attn_out)


def attention_layer(x, mod, norm_g, wqkv, wo, sink, cache_k, cache_v, layer_j):
    qkv = norm_mod_matmul(x, norm_g, mod, wqkv, 0, "attn_qkv")
    a = context_attention(qkv, sink)
    a = latent_attention(qkv, cache_k, cache_v, layer_j, sink, a)
    x = matmul_gated_residual(a, wo, x, mod, 2, "attn_wo")
    new_k = qkv[:ROWS_P, NQ:NQ + NKV].reshape(BATCH, SEQ, ATTN_KV_HEADS, HEAD_DIM)
    new_v = qkv[:ROWS_P, NQ + NKV:].reshape(BATCH, SEQ, ATTN_KV_HEADS, HEAD_DIM)
    return x, new_k, new_v


HG_TILE = 256
HG_LEVELS = 8
HG_IN = 3 * 1024 + 2 * 1024


def _hgrn_consts():
    t = np.arange(HG_TILE)
    x = t[:, None] ^ t[None, :]
    hb = np.where(x == 0, -1, np.floor(np.log2(np.maximum(x, 1))).astype(np.int64))
    later = t[:, None] > t[None, :]
    masks, tris = [], []
    for reverse in (False, True):
        side = ~later & (x != 0) if reverse else later
        masks.append(np.stack([hb == -1] + [(hb == lvl) & side for lvl in range(HG_LEVELS)]).astype(np.float32))
        tris.append((t[None, :] >= t[:, None]) if reverse else (t[None, :] <= t[:, None]))
    return jnp.asarray(np.stack(masks)), jnp.asarray(np.stack(tris).astype(np.float32), dtype=BF16)


def _split3(x):
    hi = x.astype(BF16)
    r = x - hi.astype(F32)
    mid = r.astype(BF16)
    lo = (r - mid.astype(F32)).astype(BF16)
    return hi, mid, lo


def _block_row(x, blk, idx):
    t = x.shape[0]
    x3 = x.reshape(t // blk, blk, x.shape[1])
    return jnp.broadcast_to(x3[:, idx:idx + 1, :], x3.shape).reshape(x.shape)


def _lower_bound(lb_ref, layer, direction):
    x = lb_ref[direction]
    e = jnp.exp(x - jnp.max(x, axis=0, keepdims=True))
    p = e / jnp.sum(e, axis=0, keepdims=True)
    return jnp.sum(p[1:layer + 1, :], axis=0, keepdims=True)


def _hgrn_tile(q, v, z, lb, s_in, mask_ref, tri, reverse):
    t = HG_TILE
    sg = _sigmoid(z)
    f = lb + (1.0 - lb) * sg
    k = (1.0 - lb) * (1.0 - sg)
    lf3 = _split3(jnp.log(f))
    cum = sum(jnp.dot(tri, p, preferred_element_type=F32) for p in lf3)
    rows = lax.broadcasted_iota(jnp.int32, (t, 1), 0)
    att = mask_ref[0] * _dot_t(q, k)
    for lvl in range(HG_LEVELS):
        half = 1 << lvl
        bit = (rows & half) != 0
        qside = ~bit if reverse else bit
        if lvl == 0:
            e = jnp.where(qside, f, 1.0)
        else:
            ref = _block_row(cum, 2 * half, half if reverse else half - 1)
            e = jnp.exp(jnp.where(qside, cum - ref, ref - cum))
        qt = jnp.where(qside, q * e, 0.0)
        kt = jnp.where(qside, 0.0, k * e)
        att = att + mask_ref[lvl + 1] * _dot_t(qt, kt)
    o = _bdot(att, v)
    last = cum[0:1, :] if reverse else cum[t - 1:t, :]
    kd = (k * jnp.exp(last - cum)).astype(BF16)
    s_out = lax.dot_general(kd, v.astype(BF16), (((0,), (0,)), ((), ())), preferred_element_type=F32)
    if s_in is not None:
        o = o + _bdot(q * jnp.exp(cum), s_in)
        ones = jnp.ones((t, HGRN_DV), BF16)
        last_col = sum(lax.dot_general(p, ones, (((0,), (0,)), ((), ())), preferred_element_type=F32) for p in lf3)
        s_out = jnp.exp(last_col) * s_in + s_out
    return o, s_out


def _hgrn_prompt_kernel(q_ref, v_ref, zf_ref, zb_ref, lb_ref, mask_ref, tri_ref, of_ref, ob_ref, sfin_ref, *, layer):
    q, v = q_ref[...], v_ref[...]
    of_ref[...], sfin_ref[0] = _hgrn_tile(q, v, zf_ref[...], _lower_bound(lb_ref, layer, 0), None, mask_ref.at[0], tri_ref[0], False)
    ob_ref[...], sfin_ref[1] = _hgrn_tile(q, v, zb_ref[...], _lower_bound(lb_ref, layer, 1), None, mask_ref.at[1], tri_ref[1], True)


def _hgrn_sample_kernel(qf_ref, vf_ref, zf_ref, qb_ref, vb_ref, zb_ref, lb_ref, s0_ref, mask_ref, tri_ref,
                        pf_ref, pb_ref, of_ref, ob_ref, state_ref, *, layer):
    del pf_ref, pb_ref
    tiles = DEC_SEQ // HG_TILE

    @pl.when(pl.program_id(1) % tiles == 0)
    def _():
        state_ref[...] = s0_ref[...]

    of_ref[...], state_ref[0] = _hgrn_tile(qf_ref[...], vf_ref[...], zf_ref[...], _lower_bound(lb_ref, layer, 0), state_ref[0],
                                           mask_ref.at[0], tri_ref[0], False)
    ob_ref[...], state_ref[1] = _hgrn_tile(qb_ref[...], vb_ref[...], zb_ref[...], _lower_bound(lb_ref, layer, 1), state_ref[1],
                                           mask_ref.at[1], tri_ref[1], True)


def hgrn_scan(proj, lb_raw, layer, state_hgrn, layer_j):
    masks, tris = _hgrn_consts()
    hw = HGRN_DK
    qc, vc, zfc, zbc = 0, 1024 // hw, 2048 // hw, 3072 // hw
    const_specs = [pl.BlockSpec((2, DEPTH, hw), lambda h, i: (0, 0, h))]
    mask_specs = [pl.BlockSpec((2, HG_LEVELS + 1, HG_TILE, HG_TILE), lambda h, i: (0, 0, 0, 0)),
                  pl.BlockSpec((2, HG_TILE, HG_TILE), lambda h, i: (0, 0, 0))]
    vmem = (2 * 2 * (HG_LEVELS + 1) * HG_TILE * HG_TILE * 4 + 16 * HG_TILE * hw * 4 + 8 * hw * hw * 4
            + 24 * HG_TILE * HG_TILE * 4)

    def col(cb, row_fn):
        return pl.BlockSpec((HG_TILE, hw), lambda h, i: (row_fn(i), cb + h))

    o_shape = jax.ShapeDtypeStruct((ROWS, HGRN_HEADS * HGRN_DV), F32)
    same = lambda i: i
    of, ob, sfin = pl.pallas_call(
        functools.partial(_hgrn_prompt_kernel, layer=layer),
        out_shape=(o_shape, o_shape, jax.ShapeDtypeStruct((BATCH, 2, HGRN_HEADS, HGRN_DK, HGRN_DV), F32)),
        grid=(HGRN_HEADS, BATCH),
        in_specs=[col(qc, same), col(vc, same), col(zfc, same), col(zbc, same)] + const_specs + mask_specs,
        out_specs=(col(0, same), col(0, same),
                   pl.BlockSpec((None, 2, None, HGRN_DK, HGRN_DV), lambda h, i: (i, 0, h, 0, 0))),
        compiler_params=_cparams(("arbitrary", "arbitrary"), vmem),
        name="hgrn_scan_prompt",
    )(proj, proj, proj, proj, lb_raw, masks, tris)

    tiles = DEC_SEQ // HG_TILE
    base = ROWS_P // HG_TILE
    fwd = lambda i: base + i
    bwd = lambda i: base + (i // tiles) * tiles + (tiles - 1 - i % tiles)
    of, ob = pl.pallas_call(
        functools.partial(_hgrn_sample_kernel, layer=layer),
        out_shape=(o_shape, o_shape),
        grid=(HGRN_HEADS, DEC_BATCH * tiles),
        in_specs=[col(qc, fwd), col(vc, fwd), col(zfc, fwd), col(qc, bwd), col(vc, bwd), col(zbc, bwd)]
        + const_specs
        + [pl.BlockSpec((None, None, 2, None, HGRN_DK, HGRN_DV), lambda h, i: (i // tiles, layer_j, 0, h, 0, 0))]
        + mask_specs + [pl.BlockSpec(memory_space=pl.ANY), pl.BlockSpec(memory_space=pl.ANY)],
        out_specs=(col(0, fwd), col(0, bwd)),
        scratch_shapes=[pltpu.VMEM((2, HGRN_DK, HGRN_DV), F32)],
        input_output_aliases={10: 0, 11: 1},
        compiler_params=_cparams(("arbitrary", "arbitrary"), vmem),
        name="hgrn_scan_sample",
    )(proj, proj, proj, proj, proj, proj, lb_raw, state_hgrn, masks, tris, of, ob)
    return of, ob, sfin


def _hgrn_out_kernel(of_ref, ob_ref, g_ref, gn_ref, w_ref, x_ref, mod_ref, o_ref):
    gn = gn_ref[...]
    parts = []
    for h in range(HGRN_HEADS):
        hs = slice(h * HGRN_DV, (h + 1) * HGRN_DV)
        o = of_ref[:, hs] + ob_ref[:, hs]
        g = g_ref[:, hs]
        o = o * lax.rsqrt(jnp.mean(o * o, axis=-1, keepdims=True) + NORM_EPS) * gn * (g * _sigmoid(g))
        parts.append(o.astype(BF16))
    y = jnp.dot(jnp.concatenate(parts, axis=-1), w_ref[...].astype(BF16), preferred_element_type=F32)
    o_ref[...] = x_ref[...] + mod_ref[2:3, :] * y


def hgrn_out(of, ob, proj, g_norm, wo, x, mod):
    gcol = 4096 // D
    return pl.pallas_call(
        _hgrn_out_kernel,
        out_shape=jax.ShapeDtypeStruct((ROWS, D), F32),
        grid=(N_ROW_TILES,),
        in_specs=[pl.BlockSpec((ROW_TILE, D), lambda i: (i, 0)),
                  pl.BlockSpec((ROW_TILE, D), lambda i: (i, 0)),
                  pl.BlockSpec((ROW_TILE, D), lambda i: (i, gcol)),
                  pl.BlockSpec((1, HGRN_DV), lambda i: (0, 0)),
                  pl.BlockSpec((D, D), lambda i: (0, 0)),
                  pl.BlockSpec((ROW_TILE, D), lambda i: (i, 0)),
                  pl.BlockSpec((None, 6, D), lambda i: (_mod_row(i), 0, 0))],
        out_specs=pl.BlockSpec((ROW_TILE, D), lambda i: (i, 0)),
        compiler_params=_cparams(("arbitrary",), 14 * ROW_TILE * D * 4),
        name="hgrn_out",
    )(of, ob, proj, g_norm.reshape(1, HGRN_DV), wo, x, mod)


def hgrn_layer(x, mod, norm_g, w_in, hgrn_lb, layer, g_norm, wo, state_hgrn, layer_j):
    proj = norm_mod_matmul(x, norm_g, mod, w_in, 0, "hgrn_in")
    of, ob, sfin = hgrn_scan(proj, jnp.transpose(hgrn_lb, (1, 0, 2)), layer, state_hgrn, layer_j)
    x = hgrn_out(of, ob, proj, g_norm, wo, x, mod)
    return x, sfin


SSM_N = SSM_GROUPS * SSM_STATE
SSM_KT = 8
SSM_ROWS = 256
SSM_PAD = 16


def _ssm_prep_kernel(are_ref, aim_ref, ldt_ref, bre_ref, bim_ref, lre_ref, lim_ref, bbre_ref, bbim_ref):
    a_re = jnp.minimum(are_ref[...], -1e-4)
    a_im = aim_ref[...]
    dt = jnp.exp(ldt_ref[...])
    mag = jnp.exp(a_re * dt)
    l_re = mag * jnp.cos(a_im * dt)
    l_im = mag * jnp.sin(a_im * dt)
    lre_ref[...] = l_re
    lim_ref[...] = l_im
    den = a_re * a_re + a_im * a_im
    c_re = ((l_re - 1.0) * a_re + l_im * a_im) / den
    c_im = (l_im * a_re - (l_re - 1.0) * a_im) / den
    b_re, b_im = bre_ref[...], bim_ref[...]
    bbre_ref[...] = c_re[:, None, :] * b_re - c_im[:, None, :] * b_im
    bbim_ref[...] = c_re[:, None, :] * b_im + c_im[:, None, :] * b_re


def ssm_discretize(a_re, a_im, log_dt, b_re, b_im):
    g2 = 2 * SSM_GROUPS
    sh = jax.ShapeDtypeStruct((g2, SSM_STATE), F32)
    shb = jax.ShapeDtypeStruct((g2, SSM_GROUP, SSM_STATE), F32)
    bt = lambda b: jnp.transpose(b, (0, 1, 3, 2)).reshape(g2, SSM_GROUP, SSM_STATE)
    return pl.pallas_call(_ssm_prep_kernel, out_shape=(sh, sh, shb, shb), name="ssm_discretize")(
        a_re.reshape(g2, SSM_STATE), a_im.reshape(g2, SSM_STATE), log_dt.reshape(g2, 1), bt(b_re), bt(b_im))


def _ssm_block_diag(l_re, l_im, bb_re, bb_im, c_re, c_im):
    nk = SSM_GROUPS // SSM_KT
    eye = jnp.eye(SSM_KT, dtype=F32)

    def bmat(b):
        b = b.reshape(2, nk, SSM_KT, SSM_GROUP, SSM_STATE)
        return jnp.einsum('dkgip,gh->dkgihp', b, eye).reshape(2, nk, SSM_KT * SSM_GROUP, SSM_KT * SSM_STATE)

    def cmat(c):
        c = c.reshape(2, nk, SSM_KT, SSM_GROUP, SSM_STATE)
        return jnp.einsum('dkgip,gh->dkhpgi', c, eye).reshape(2, nk, SSM_KT * SSM_STATE, SSM_KT * SSM_GROUP)

    b_mat = jnp.concatenate([bmat(bb_re), bmat(bb_im)], axis=-1).astype(BF16)
    c_mat = jnp.concatenate([cmat(c_re), cmat(-c_im)], axis=-2).astype(BF16)
    lam = jnp.stack([l_re.reshape(2, SSM_N), l_im.reshape(2, SSM_N)], axis=1)
    return b_mat, c_mat, lam


def _normmod_tm_kernel(x_ref, g_ref, mod_ref, o_ref):
    o_ref[...] = _norm_mod(x_ref[...], g_ref[...], mod_ref[0:1, :], mod_ref[1:2, :])


def norm_mod_time_major(x, g, mod, prompt):
    if prompt:
        batch, seq, grid = BATCH, SEQ, (BATCH, 1)
        row = lambda b, t: b
        mrow = lambda b, t: 0
    else:
        batch, seq, grid = DEC_BATCH, DEC_SEQ, (DEC_BATCH, DEC_SEQ // SSM_ROWS)
        row = lambda b, t: ROWS_P // SSM_ROWS + b * (DEC_SEQ // SSM_ROWS) + t
        mrow = lambda b, t: 1 + b
    return pl.pallas_call(
        _normmod_tm_kernel,
        out_shape=jax.ShapeDtypeStruct((seq, batch * D), F32),
        grid=grid,
        in_specs=[pl.BlockSpec((SSM_ROWS, D), lambda b, t: (row(b, t), 0)),
                  pl.BlockSpec((1, D), lambda b, t: (0, 0)),
                  pl.BlockSpec((None, 6, D), lambda b, t: (mrow(b, t), 0, 0))],
        out_specs=pl.BlockSpec((SSM_ROWS, D), lambda b, t: (t, b)),
        compiler_params=_cparams(("arbitrary", "arbitrary"), 6 * SSM_ROWS * D * 4),
        name="ssm_norm_time_major",
    )(x, g.reshape(1, D), mod)


def _ssm_scan_kernel(xf_ref, xb_ref, bm_ref, cm_ref, lam_ref, h0_ref, yf_ref, yb_ref, hfin_ref,
                     hre_ref, him_ref, st_ref, *, batch):
    i = pl.program_id(0)
    steps = SSM_ROWS // batch
    nk = SSM_GROUPS // SSM_KT
    kw = SSM_KT * SSM_STATE

    @pl.when(i == 0)
    def _():
        st_ref[...] = h0_ref[...]

    def step(d, prev_rows, cur_rows):
        l_re, l_im = lam_ref[d, 0], lam_ref[d, 1]
        p_re, p_im = hre_ref[prev_rows, :], him_ref[prev_rows, :]
        hre_ref[cur_rows, :] = l_re * p_re - l_im * p_im + hre_ref[cur_rows, :]
        him_ref[cur_rows, :] = l_re * p_im + l_im * p_re + him_ref[cur_rows, :]

    def run(d, x_ref, y_ref, reverse):
        base = 0 if reverse else SSM_PAD
        srow = SSM_ROWS if reverse else SSM_PAD - batch
        xb = x_ref[...].astype(BF16)
        for k in range(nk):
            bu = jnp.dot(xb[:, k * 128:(k + 1) * 128], bm_ref[d, k], preferred_element_type=F32)
            hre_ref[base:base + SSM_ROWS, k * kw:(k + 1) * kw] = bu[:, :kw]
            him_ref[base:base + SSM_ROWS, k * kw:(k + 1) * kw] = bu[:, kw:]
        hre_ref[srow:srow + batch, :] = st_ref[d, 0]
        him_ref[srow:srow + batch, :] = st_ref[d, 1]

        if batch % 8 == 0:
            def body(s, carry):
                t = (steps - 1 - s) if reverse else s
                cur = pl.ds(pl.multiple_of(base + t * batch, batch), batch)
                prev = pl.ds(pl.multiple_of(base + (t + 1) * batch if reverse else base + (t - 1) * batch, batch),
                             batch)
                step(d, prev, cur)
                return carry
            lax.fori_loop(0, steps, body, 0)
        else:
            per = 8 // batch

            def body(s, carry):
                g = (steps // per - 1 - s) if reverse else s
                slab = pl.multiple_of(base + g * 8, 8)
                l_re, l_im = lam_ref[d, 0], lam_ref[d, 1]
                cur_re, cur_im = hre_ref[pl.ds(slab, 8), :], him_ref[pl.ds(slab, 8), :]
                nb_slab = pl.multiple_of(slab + 8 if reverse else slab - 8, 8)
                nb_re, nb_im = hre_ref[pl.ds(nb_slab, 8), :], him_ref[pl.ds(nb_slab, 8), :]
                if reverse:
                    p_re, p_im = nb_re[0:batch], nb_im[0:batch]
                    order = range(per - 1, -1, -1)
                else:
                    p_re, p_im = nb_re[8 - batch:8], nb_im[8 - batch:8]
                    order = range(per)
                outs_re, outs_im = [None] * per, [None] * per
                for j in order:
                    b_re, b_im = cur_re[j * batch:(j + 1) * batch], cur_im[j * batch:(j + 1) * batch]
                    p_re, p_im = l_re * p_re - l_im * p_im + b_re, l_re * p_im + l_im * p_re + b_im
                    outs_re[j], outs_im[j] = p_re, p_im
                hre_ref[pl.ds(slab, 8), :] = jnp.concatenate(outs_re, axis=0)
                him_ref[pl.ds(slab, 8), :] = jnp.concatenate(outs_im, axis=0)
                return carry
            lax.fori_loop(0, steps // per, body, 0)

        erow = 0 if reverse else SSM_PAD + SSM_ROWS - batch
        st_ref[d, 0] = hre_ref[erow:erow + batch, :]
        st_ref[d, 1] = him_ref[erow:erow + batch, :]
        for k in range(nk):
            hk = jnp.concatenate([hre_ref[base:base + SSM_ROWS, k * kw:(k + 1) * kw],
                                  him_ref[base:base + SSM_ROWS, k * kw:(k + 1) * kw]], axis=-1).astype(BF16)
            y_ref[:, k * 128:(k + 1) * 128] = jnp.dot(hk, cm_ref[d, k], preferred_element_type=F32)

    run(0, xf_ref, yf_ref, False)
    run(1, xb_ref, yb_ref, True)

    @pl.when(i == pl.num_programs(0) - 1)
    def _():
        hfin_ref[...] = st_ref[...]


def ssm_scan(xn_tm, b_mat, c_mat, lam, h0, batch):
    rows = xn_tm.shape[0]
    n = rows // SSM_ROWS
    lam_b = jnp.broadcast_to(lam[:, :, None, :], (2, 2, batch, SSM_N))
    y_shape = jax.ShapeDtypeStruct((rows, D), F32)
    full = lambda a: pl.BlockSpec(a.shape, lambda i: (0,) * a.ndim)
    vmem = (2 * (SSM_ROWS + 2 * SSM_PAD) * SSM_N * 4 + 8 * SSM_ROWS * D * 4 + 2 * (b_mat.size + c_mat.size) * 2
            + 12 * batch * SSM_N * 4 * 2 + 8 * SSM_ROWS * 1024 * 4)
    return pl.pallas_call(
        functools.partial(_ssm_scan_kernel, batch=batch),
        out_shape=(y_shape, y_shape, jax.ShapeDtypeStruct((2, 2, batch, SSM_N), F32)),
        grid=(n,),
        in_specs=[pl.BlockSpec((SSM_ROWS, D), lambda i: (i, 0)),
                  pl.BlockSpec((SSM_ROWS, D), lambda i: (n - 1 - i, 0)),
                  full(b_mat), full(c_mat), full(lam_b), full(h0)],
        out_specs=(pl.BlockSpec((SSM_ROWS, D), lambda i: (i, 0)),
                   pl.BlockSpec((SSM_ROWS, D), lambda i: (n - 1 - i, 0)),
                   pl.BlockSpec((2, 2, batch, SSM_N), lambda i: (0, 0, 0, 0))),
        scratch_shapes=[pltpu.VMEM((SSM_ROWS + 2 * SSM_PAD, SSM_N), F32),
                        pltpu.VMEM((SSM_ROWS + 2 * SSM_PAD, SSM_N), F32),
                        pltpu.VMEM((2, 2, batch, SSM_N), F32)],
        compiler_params=_cparams(("arbitrary",), vmem),
        name="ssm_scan",
    )(xn_tm, xn_tm, b_mat, c_mat, lam_b, h0)


def _gelu_tanh(x):
    return 0.5 * x * (1.0 + jnp.tanh(math.sqrt(2.0 / math.pi) * (x + 0.044715 * (x * x * x))))


def _ssm_glu_kernel(yfp_ref, ybp_ref, xnp_ref, yfs_ref, ybs_ref, xns_ref, d_ref, w_ref, x_ref, mod_ref, o_ref):
    is_p = pl.program_id(0) < ROWS_P // SSM_ROWS
    d = d_ref[...]
    yp = yfp_ref[...] + ybp_ref[...] + d * xnp_ref[...]
    ys = yfs_ref[...] + ybs_ref[...] + d * xns_ref[...]
    g = _gelu_tanh(jnp.where(is_p, yp, ys))
    u = _bdot(g, w_ref[...])
    o_ref[...] = x_ref[...] + mod_ref[2:3, :] * (u[:, :D] * _sigmoid(u[:, D:]))


def ssm_glu(y_p, y_s, xn_p, xn_s, d, w_glu, x, mod):
    np_tiles = ROWS_P // SSM_ROWS
    st = DEC_SEQ // SSM_ROWS
    p_spec = pl.BlockSpec((SSM_ROWS, D), lambda i: (0, jnp.minimum(i, np_tiles - 1)))
    s_spec = pl.BlockSpec((SSM_ROWS, D), lambda i: (jnp.maximum(i - np_tiles, 0) % st,
                                                    jnp.maximum(i - np_tiles, 0) // st))
    tm = lambda a, b: a.reshape(-1, b * D)
    return pl.pallas_call(
        _ssm_glu_kernel,
        out_shape=jax.ShapeDtypeStruct((ROWS, D), F32),
        grid=(ROWS // SSM_ROWS,),
        in_specs=[p_spec, p_spec, p_spec, s_spec, s_spec, s_spec,
                  pl.BlockSpec((1, D), lambda i: (0, 0)),
                  pl.BlockSpec((D, 2 * D), lambda i: (0, 0)),
                  pl.BlockSpec((SSM_ROWS, D), lambda i: (i, 0)),
                  pl.BlockSpec((None, 6, D), lambda i: (jnp.where(i < np_tiles, 0, 1 + (i - np_tiles) // st), 0, 0))],
        out_specs=pl.BlockSpec((SSM_ROWS, D), lambda i: (i, 0)),
        compiler_params=_cparams(("arbitrary",), 20 * SSM_ROWS * D * 4 + 2 * D * 2 * D * 4),
        name="ssm_glu",
    )(tm(y_p[0], BATCH), tm(y_p[1], BATCH), xn_p, tm(y_s[0], DEC_BATCH), tm(y_s[1], DEC_BATCH), xn_s,
      d.reshape(1, D), w_glu, x, mod)


def ssm_layer(x, mod, norm_g, a_re, a_im, log_dt, b_re, b_im, c_re, c_im, d, w_glu, state_ssm_j):
    l_re, l_im, bb_re, bb_im = ssm_discretize(a_re, a_im, log_dt, b_re, b_im)
    b_mat, c_mat, lam = _ssm_block_diag(l_re, l_im, bb_re, bb_im, c_re, c_im)
    xn_p = norm_mod_time_major(x, norm_g, mod, True)
    xn_s = norm_mod_time_major(x, norm_g, mod, False)
    h0_p = jnp.zeros((2, 2, BATCH, SSM_N), F32)
    h0_s = jnp.transpose(state_ssm_j.reshape(DEC_BATCH, 2, SSM_N, 2), (1, 3, 0, 2))
    yfp, ybp, hfin = ssm_scan(xn_p.reshape(SEQ * BATCH, D), b_mat, c_mat, lam, h0_p, BATCH)
    yfs, ybs, _ = ssm_scan(xn_s.reshape(DEC_SEQ * DEC_BATCH, D), b_mat, c_mat, lam, h0_s, DEC_BATCH)
    x = ssm_glu((yfp, ybp), (yfs, ybs), xn_p, xn_s, d, w_glu, x, mod)
    new_state = jnp.transpose(hfin, (2, 0, 3, 1)).reshape(BATCH, 2, SSM_GROUPS, SSM_STATE, 2)
    return x, new_state


def kernel(x_prompt, x_sample, cache_k, cache_v, state_hgrn, state_ssm, c, c_ctx, ada_w, ada_b, norm1_g, norm2_g, attn_wqkv, attn_wo, attn_sink, hgrn_w_in, hgrn_lb, hgrn_g_norm, hgrn_wo, ssm_a_re, ssm_a_im, ssm_log_dt, ssm_b_re, ssm_b_im, ssm_c_re, ssm_c_im, ssm_d, ssm_w_glu, ffn_w_up, ffn_conv_w, ffn_conv_b, ffn_w_down, final_g):
    cond8 = jnp.zeros((MOD_ROWS, D), F32).at[0].set(c_ctx).at[1:1 + DEC_BATCH].set(c)
    mods = ada_modulation(cond8, ada_w, ada_b)
    x = jnp.concatenate([x_prompt.reshape(ROWS_P, D), x_sample.reshape(ROWS_S, D)], axis=0)
    new_k, new_v, new_hgrn, new_ssm = [], [], [], []
    for l in range(DEPTH):
        kind, j = l % N_MIXERS, l // N_MIXERS
        if kind == 0:
            x, k, v = attention_layer(x, mods[l], norm1_g[l], attn_wqkv[j], attn_wo[j], attn_sink[j],
                                      cache_k, cache_v, j)
            new_k.append(k)
            new_v.append(v)
        elif kind == 1:
            x, s = hgrn_layer(x, mods[l], norm1_g[l], hgrn_w_in[j], hgrn_lb, l, hgrn_g_norm[j], hgrn_wo[j],
                              state_hgrn, j)
            new_hgrn.append(s)
        else:
            x, s = ssm_layer(x, mods[l], norm1_g[l], ssm_a_re[j], ssm_a_im[j], ssm_log_dt[j], ssm_b_re[j],
                             ssm_b_im[j], ssm_c_re[j], ssm_c_im[j], ssm_d[j], ssm_w_glu[j], state_ssm[:, j])
            new_ssm.append(s)
        x = conv_ffn_residual(x, norm2_g[l], mods[l], ffn_w_up[l], ffn_conv_w[l], ffn_conv_b[l], ffn_w_down[l])
    y = final_norm(x, final_g)
    y_prompt = y[:ROWS_P].reshape(BATCH, SEQ, D)
    y_sample = y[ROWS_P:].reshape(DEC_BATCH, DEC_SEQ, D)
    return (y_prompt, y_sample, jnp.stack(new_k, axis=1), jnp.stack(new_v, axis=1),
            jnp.stack(new_hgrn, axis=1), jnp.stack(new_ssm, axis=1))
```

```python
import functools
import math

import jax
import jax.numpy as jnp
import numpy as np
from jax import lax
from jax.experimental import pallas as pl
from jax.experimental.pallas import tpu as pltpu

F32 = jnp.float32
BF16 = jnp.bfloat16

D = 1024
BATCH = 16
SEQ = 256
DEPTH = 4
DEC_BATCH = 4
DEC_SEQ = 1024
PAST_LEN = 512
GRID_W = 64
N_MIXERS = 3
ATTN_HEADS = 16
ATTN_KV_HEADS = 4
ATTN_GROUP = ATTN_HEADS // ATTN_KV_HEADS
HEAD_DIM = D // ATTN_HEADS
WINDOW = 128
ROPE_BASE = 10000.0
HGRN_HEADS = 8
HGRN_DK = 128
HGRN_DV = 128
SSM_GROUP = 16
SSM_GROUPS = D // SSM_GROUP
SSM_STATE = 64
D_FF = 2816
NORM_EPS = 1e-6

ROWS_P = BATCH * SEQ
ROWS_S = DEC_BATCH * DEC_SEQ
ROWS = ROWS_P + ROWS_S
ROW_TILE = 1024
N_ROW_TILES = ROWS // ROW_TILE
N_ROW_TILES_P = ROWS_P // ROW_TILE
MOD_ROWS = 8
V7X_VMEM_BYTES = 64 * 1024 * 1024


def _mod_row(i):
    return jnp.where(i < N_ROW_TILES_P, 0, i - N_ROW_TILES_P + 1)


def _cparams(semantics, vmem_bytes):
    vmem = int(min(max(vmem_bytes * 5 // 4 + (4 << 20), 16 << 20), V7X_VMEM_BYTES - (6 << 20)))
    return pltpu.CompilerParams(dimension_semantics=semantics, vmem_limit_bytes=vmem)


def _bdot(a, b):
    return jnp.dot(a.astype(BF16), b.astype(BF16), preferred_element_type=F32)


def _norm_mod(x, g, shift, scale):
    y = x * lax.rsqrt(jnp.mean(x * x, axis=-1, keepdims=True) + NORM_EPS) * g
    return y * (1.0 + scale) + shift


def _sigmoid(x):
    return 1.0 / (1.0 + jnp.exp(-x))


def _ada_kernel(c_ref, w_ref, b_ref, o_ref):
    c = c_ref[...]
    o_ref[...] = _bdot(c * _sigmoid(c), w_ref[...]) + b_ref[...]


def ada_modulation(cond8, ada_w, ada_b):
    tn = 1024
    out = pl.pallas_call(
        _ada_kernel,
        out_shape=jax.ShapeDtypeStruct((DEPTH, MOD_ROWS, 6 * D), F32),
        grid=(DEPTH, 6 * D // tn),
        in_specs=[
            pl.BlockSpec((MOD_ROWS, D), lambda l, j: (0, 0)),
            pl.BlockSpec((None, D, tn), lambda l, j: (l, 0, j)),
            pl.BlockSpec((None, 1, tn), lambda l, j: (l, 0, j)),
        ],
        out_specs=pl.BlockSpec((None, MOD_ROWS, tn), lambda l, j: (l, 0, j)),
        compiler_params=_cparams(("arbitrary", "arbitrary"), 2 * D * tn * 4),
        name="ada_modulation",
    )(cond8, ada_w, ada_b.reshape(DEPTH, 1, 6 * D))
    return out.reshape(DEPTH, MOD_ROWS, 6, D)


def _nmm_kernel(x_ref, g_ref, mod_ref, w_ref, o_ref, h_ref):
    @pl.when(pl.program_id(1) == 0)
    def _():
        h_ref[...] = _norm_mod(x_ref[...], g_ref[...], mod_ref[0:1, :], mod_ref[1:2, :]).astype(BF16)

    o_ref[...] = jnp.dot(h_ref[...], w_ref[...], preferred_element_type=F32)


def _mod_spec(layer, n_grid):
    if n_grid == 1:
        return pl.BlockSpec((None, None, 6, D), lambda i: (layer, _mod_row(i), 0, 0))
    return pl.BlockSpec((None, None, 6, D), lambda i, j: (layer, _mod_row(i), 0, 0))


def norm_mod_matmul(x, layer, norm_g, mods, w, w_idx, name):
    n = w.shape[-1]
    tn = 512
    return pl.pallas_call(
        _nmm_kernel,
        out_shape=jax.ShapeDtypeStruct((ROWS, n), F32),
        grid=(N_ROW_TILES, n // tn),
        in_specs=[
            pl.BlockSpec((ROW_TILE, D), lambda i, j: (i, 0)),
            pl.BlockSpec((None, 1, D), lambda i, j: (layer, 0, 0)),
            _mod_spec(layer, 2),
            pl.BlockSpec((None, D, tn), lambda i, j: (w_idx, 0, j)),
        ],
        out_specs=pl.BlockSpec((ROW_TILE, tn), lambda i, j: (i, j)),
        scratch_shapes=[pltpu.VMEM((ROW_TILE, D), BF16)],
        compiler_params=_cparams(("arbitrary", "arbitrary"),
                                 2 * ROW_TILE * D * 4 + ROW_TILE * D * 2 + 2 * D * tn * 2 + 2 * ROW_TILE * tn * 4),
        name=name,
    )(x, norm_g.reshape(DEPTH, 1, D), mods, w)


def _mm_res_kernel(a_ref, w_ref, x_ref, mod_ref, o_ref):
    y = jnp.dot(a_ref[...].astype(BF16), w_ref[...], preferred_element_type=F32)
    o_ref[...] = x_ref[...] + mod_ref[2:3, :] * y


def matmul_gated_residual(a, w, w_idx, x, layer, mods, name):
    k = a.shape[1]
    return pl.pallas_call(
        _mm_res_kernel,
        out_shape=jax.ShapeDtypeStruct((ROWS, D), F32),
        grid=(N_ROW_TILES,),
        in_specs=[
            pl.BlockSpec((ROW_TILE, k), lambda i: (i, 0)),
            pl.BlockSpec((None, k, D), lambda i: (w_idx, 0, 0), pipeline_mode=pl.Buffered(1)),
            pl.BlockSpec((ROW_TILE, D), lambda i: (i, 0)),
            _mod_spec(layer, 1),
        ],
        out_specs=pl.BlockSpec((ROW_TILE, D), lambda i: (i, 0)),
        compiler_params=_cparams(("arbitrary",), 2 * ROW_TILE * k * 4 + k * D * 2 + 5 * ROW_TILE * D * 4),
        name=name,
    )(a, w, x, mods)


FFN_CHUNK = 256
FFN_CHUNKS = D_FF // FFN_CHUNK
CONV_PAD = 8


def _ffn_kernel(x_ref, g_ref, mod_ref, wup_ref, cw_ref, cb_ref, wd_ref, o_ref, h_ref, pad_a, pad_b, act_ref):
    i = pl.program_id(0)
    h_ref[...] = _norm_mod(x_ref[...], g_ref[...], mod_ref[3:4, :], mod_ref[4:5, :]).astype(BF16)
    zeros = jnp.zeros((CONV_PAD, 2 * FFN_CHUNK), F32)
    for pad_ref in (pad_a, pad_b):
        pad_ref[0:CONV_PAD, :] = zeros
        pad_ref[CONV_PAD + ROW_TILE:, :] = zeros
    sub = lax.broadcasted_iota(jnp.int32, (8, 1), 0)
    is_prompt = i < N_ROW_TILES_P
    keep_first = jnp.where((sub == 0) & is_prompt, 0.0, 1.0)
    keep_last = jnp.where((sub == 7) & is_prompt, 0.0, 1.0)

    def cut_sequences(v, keep, row):
        parts, at = [], 0
        for b in range(SEQ, ROW_TILE, SEQ):
            lo = b if row == 0 else b - 8
            parts += [v[at:lo], v[lo:lo + 8] * keep]
            at = lo + 8
        return jnp.concatenate(parts + [v[at:]], axis=0)

    def cols(ref, c):
        off = pl.multiple_of(c * FFN_CHUNK, FFN_CHUNK)
        return ref[:, pl.ds(off, FFN_CHUNK)], ref[:, pl.ds(D_FF + off, FFN_CHUNK)]

    def up_proj(c, pad_ref):
        hb = h_ref[...]
        wg, wv = cols(wup_ref, c)
        pad_ref[CONV_PAD:CONV_PAD + ROW_TILE, :FFN_CHUNK] = jnp.dot(hb, wg, preferred_element_type=F32)
        pad_ref[CONV_PAD:CONV_PAD + ROW_TILE, FFN_CHUNK:] = jnp.dot(hb, wv, preferred_element_type=F32)

    def conv_act(c, pad_ref):
        up = pad_ref[CONV_PAD:CONV_PAD + ROW_TILE, :]
        prev = cut_sequences(pad_ref[CONV_PAD - 1:CONV_PAD - 1 + ROW_TILE, :], keep_first, 0)
        nxt = cut_sequences(pad_ref[CONV_PAD + 1:CONV_PAD + 1 + ROW_TILE, :], keep_last, 7)
        cw = jnp.concatenate(cols(cw_ref, c), axis=-1)
        cb = jnp.concatenate(cols(cb_ref, c), axis=-1)
        conv = prev * cw[0:1, :] + up * cw[1:2, :] + nxt * cw[2:3, :] + cb
        gate = conv[:, :FFN_CHUNK]
        act = (gate * _sigmoid(gate)) * conv[:, FFN_CHUNK:]
        act_ref[:, pl.ds(pl.multiple_of(c * FFN_CHUNK, FFN_CHUNK), FFN_CHUNK)] = act.astype(BF16)

    up_proj(0, pad_a)

    def body(k, carry):
        c = 2 * k
        up_proj(c + 1, pad_b)
        conv_act(c, pad_a)
        up_proj(c + 2, pad_a)
        conv_act(c + 1, pad_b)
        return carry

    lax.fori_loop(0, (FFN_CHUNKS - 1) // 2, body, 0)
    conv_act(FFN_CHUNKS - 1, pad_a)
    y = jnp.dot(act_ref[...], wd_ref[...], preferred_element_type=F32)
    o_ref[...] = x_ref[...] + mod_ref[5:6, :] * y


def conv_ffn_residual(x, layer, norm_g, mods, w_up, conv_w, conv_b, w_down):
    once = pl.Buffered(1)
    vmem = (4 * ROW_TILE * D * 4 + ROW_TILE * D * 2 + 2 * (ROW_TILE + 2 * CONV_PAD) * 2 * FFN_CHUNK * 4
            + ROW_TILE * D_FF * 2 + 3 * D * D_FF * 2 + 5 * ROW_TILE * 2 * FFN_CHUNK * 4)
    return pl.pallas_call(
        _ffn_kernel,
        out_shape=jax.ShapeDtypeStruct((ROWS, D), F32),
        grid=(N_ROW_TILES,),
        in_specs=[
            pl.BlockSpec((ROW_TILE, D), lambda i: (i, 0)),
            pl.BlockSpec((None, 1, D), lambda i: (layer, 0, 0)),
            pl.BlockSpec((None, None, 6, D), lambda i: (layer, _mod_row(i), 0, 0)),
            pl.BlockSpec((None, D, 2 * D_FF), lambda i: (layer, 0, 0), pipeline_mode=once),
            pl.BlockSpec((None, 3, 2 * D_FF), lambda i: (layer, 0, 0), pipeline_mode=once),
            pl.BlockSpec((None, 1, 2 * D_FF), lambda i: (layer, 0, 0), pipeline_mode=once),
            pl.BlockSpec((None, D_FF, D), lambda i: (layer, 0, 0), pipeline_mode=once),
        ],
        out_specs=pl.BlockSpec((ROW_TILE, D), lambda i: (i, 0)),
        scratch_shapes=[pltpu.VMEM((ROW_TILE, D), BF16),
                        pltpu.VMEM((ROW_TILE + 2 * CONV_PAD, 2 * FFN_CHUNK), F32),
                        pltpu.VMEM((ROW_TILE + 2 * CONV_PAD, 2 * FFN_CHUNK), F32),
                        pltpu.VMEM((ROW_TILE, D_FF), BF16)],
        compiler_params=_cparams(("arbitrary",), vmem),
        name="conv_ffn",
    )(x, norm_g.reshape(DEPTH, 1, D), mods, w_up, conv_w, conv_b.reshape(DEPTH, 1, 2 * D_FF), w_down)


def _final_norm_kernel(x_ref, g_ref, o_ref):
    x = x_ref[...]
    o_ref[...] = x * lax.rsqrt(jnp.mean(x * x, axis=-1, keepdims=True) + NORM_EPS) * g_ref[...]


def final_norm(x, g, first_tile, n_tiles):
    return pl.pallas_call(
        _final_norm_kernel,
        out_shape=jax.ShapeDtypeStruct((n_tiles * ROW_TILE, D), F32),
        grid=(n_tiles,),
        in_specs=[pl.BlockSpec((ROW_TILE, D), lambda i: (first_tile + i, 0)),
                  pl.BlockSpec((1, D), lambda i: (0, 0))],
        out_specs=pl.BlockSpec((ROW_TILE, D), lambda i: (i, 0)),
        compiler_params=_cparams(("arbitrary",), 4 * ROW_TILE * D * 4),
        name="final_norm",
    )(x, g.reshape(1, D))


NQ = ATTN_HEADS * HEAD_DIM
NKV = ATTN_KV_HEADS * HEAD_DIM
ATTN_SCALE = HEAD_DIM ** -0.5
Q_BLOCK = 128
MASKED = -1e30


def _dot_t(a, b):
    return lax.dot_general(a.astype(BF16), b.astype(BF16), (((1,), (1,)), ((), ())),
                           preferred_element_type=F32)


def _group_rows(q, h, rows):
    return jnp.concatenate(
        [q[:, (ATTN_GROUP * h + g) * HEAD_DIM:(ATTN_GROUP * h + g + 1) * HEAD_DIM] for g in range(ATTN_GROUP)], axis=0)


def _sink_rows(sink_ref, h, rows):
    return jnp.concatenate(
        [jnp.broadcast_to(sink_ref[0:1, ATTN_GROUP * h + g:ATTN_GROUP * h + g + 1], (rows, 1))
         for g in range(ATTN_GROUP)], axis=0)


def _ctx_attn_kernel(qkv_ref, sink_ref, o_ref):
    outs = []
    for h in range(ATTN_KV_HEADS):
        k = qkv_ref[:, NQ + h * HEAD_DIM:NQ + (h + 1) * HEAD_DIM]
        v = qkv_ref[:, NQ + NKV + h * HEAD_DIM:NQ + NKV + (h + 1) * HEAD_DIM]
        q4 = _group_rows(qkv_ref[:, ATTN_GROUP * h * HEAD_DIM:ATTN_GROUP * (h + 1) * HEAD_DIM], 0, SEQ)
        s = _dot_t(q4, k) * ATTN_SCALE
        sk = _sink_rows(sink_ref, h, SEQ)
        m = jnp.maximum(jnp.max(s, axis=-1, keepdims=True), sk)
        p = jnp.exp(s - m)
        denom = jnp.sum(p, axis=-1, keepdims=True) + jnp.exp(sk - m)
        o4 = _bdot(p, v) / denom
        outs += [o4[g * SEQ:(g + 1) * SEQ, :] for g in range(ATTN_GROUP)]
    o_ref[...] = jnp.concatenate(outs, axis=-1)


def context_attention(qkv, sink):
    return pl.pallas_call(
        _ctx_attn_kernel,
        out_shape=jax.ShapeDtypeStruct((ROWS, NQ), F32),
        grid=(BATCH,),
        in_specs=[pl.BlockSpec((SEQ, NQ + 2 * NKV), lambda b: (b, 0)),
                  pl.BlockSpec((1, ATTN_HEADS), lambda b: (0, 0))],
        out_specs=pl.BlockSpec((SEQ, NQ), lambda b: (b, 0)),
        compiler_params=_cparams(("arbitrary",), 2 * SEQ * (2 * NQ + 2 * NKV) * 4 + 24 * SEQ * ATTN_GROUP * SEQ * 4),
        name="context_attention",
    )(qkv, sink.reshape(1, ATTN_HEADS))


def _rope(x, cos, sin_a, sin_b):
    outs = []
    for c in range(x.shape[1] // 128):
        s = x[:, c * 128:(c + 1) * 128]
        outs.append(s * cos + pltpu.roll(s, 128 - HEAD_DIM // 4, 1) * sin_a + pltpu.roll(s, HEAD_DIM // 4, 1) * sin_b)
    return jnp.concatenate(outs, axis=-1)


def _lat_attn_kernel(q_ref, kp_ref, kc_ref, kn_ref, vp_ref, vc_ref, vn_ref, ck_ref, cv_ref,
                     cos_ref, sa_ref, sb_ref, sink_ref, prev_ref, o_ref):
    del prev_ref
    n = pl.program_id(1)
    nb = pl.num_programs(1)

    def tables(blk):
        r = pl.ds(pl.multiple_of(blk * Q_BLOCK, Q_BLOCK), Q_BLOCK)
        return cos_ref[r, :], sa_ref[r, :], sb_ref[r, :]

    qr = _rope(q_ref[...], *tables(n))
    k3 = jnp.concatenate([
        _rope(kp_ref[...], *tables(jnp.maximum(n - 1, 0))),
        _rope(kc_ref[...], *tables(n)),
        _rope(kn_ref[...], *tables(jnp.minimum(n + 1, nb - 1)))], axis=0)
    v3 = jnp.concatenate([vp_ref[...], vc_ref[...], vn_ref[...]], axis=0)

    rows = ATTN_GROUP * Q_BLOCK
    qpos = lax.broadcasted_iota(jnp.int32, (rows, 3 * Q_BLOCK), 0) & (Q_BLOCK - 1)
    kcol = lax.broadcasted_iota(jnp.int32, (rows, 3 * Q_BLOCK), 1)
    rel = kcol - Q_BLOCK - qpos
    kpos = kcol + (n - 1) * Q_BLOCK
    mask = (jnp.abs(rel) <= WINDOW) & (kpos >= 0) & (kpos < nb * Q_BLOCK)

    outs = []
    for h in range(ATTN_KV_HEADS):
        hs = slice(h * HEAD_DIM, (h + 1) * HEAD_DIM)
        q4 = _group_rows(qr[:, ATTN_GROUP * h * HEAD_DIM:ATTN_GROUP * (h + 1) * HEAD_DIM], 0, Q_BLOCK)
        s_lat = jnp.where(mask, _dot_t(q4, k3[:, hs]) * ATTN_SCALE, MASKED)
        s_ctx = _dot_t(q4, ck_ref[:, hs]) * ATTN_SCALE
        sk = _sink_rows(sink_ref, h, Q_BLOCK)
        m = jnp.maximum(jnp.maximum(jnp.max(s_lat, axis=-1, keepdims=True),
                                    jnp.max(s_ctx, axis=-1, keepdims=True)), sk)
        p_lat = jnp.exp(s_lat - m)
        p_ctx = jnp.exp(s_ctx - m)
        denom = (jnp.sum(p_lat, axis=-1, keepdims=True) + jnp.sum(p_ctx, axis=-1, keepdims=True)
                 + jnp.exp(sk - m))
        o4 = (_bdot(p_lat, v3[:, hs]) + _bdot(p_ctx, cv_ref[:, hs])) / denom
        outs += [o4[g * Q_BLOCK:(g + 1) * Q_BLOCK, :] for g in range(ATTN_GROUP)]
    o_ref[...] = jnp.concatenate(outs, axis=-1)


def _rope_tables():
    t = np.arange(DEC_SEQ)
    half = HEAD_DIM // 2
    inv_freq = 1.0 / (ROPE_BASE ** (np.arange(0, half, 2, dtype=np.float32) / half))
    ar = (t // GRID_W).astype(np.float32)[:, None] * inv_freq
    ac = (t % GRID_W).astype(np.float32)[:, None] * inv_freq
    return jnp.concatenate([jnp.asarray(a) for a in (ar, ar, ac, ac)] * 2, axis=-1)


def latent_attention(qkv, cache_k, cache_v, layer_j, sink, attn_out):
    ang = _rope_tables()
    cos, sin = jnp.cos(ang), jnp.sin(ang)
    first = (lax.broadcasted_iota(jnp.int32, ang.shape, 1) % (HEAD_DIM // 2)) < HEAD_DIM // 4
    sin_a = jnp.where(first, -sin, 0.0)
    sin_b = jnp.where(first, 0.0, sin)
    nb = DEC_SEQ // Q_BLOCK
    base = ROWS_P // Q_BLOCK
    kcol, vcol = NQ // NKV, NQ // NKV + 1
    ck = cache_k.reshape(DEC_BATCH, -1, PAST_LEN, NKV)
    cv = cache_v.reshape(DEC_BATCH, -1, PAST_LEN, NKV)

    def kv_spec(col, off):
        return pl.BlockSpec((Q_BLOCK, NKV),
                            lambda b, n: (base + b * nb + jnp.clip(n + off, 0, nb - 1), col))

    table = pl.BlockSpec((DEC_SEQ, 128), lambda b, n: (0, 0))
    return pl.pallas_call(
        _lat_attn_kernel,
        out_shape=jax.ShapeDtypeStruct((ROWS, NQ), F32),
        grid=(DEC_BATCH, nb),
        in_specs=[pl.BlockSpec((Q_BLOCK, NQ), lambda b, n: (base + b * nb + n, 0)),
                  kv_spec(kcol, -1), kv_spec(kcol, 0), kv_spec(kcol, 1),
                  kv_spec(vcol, -1), kv_spec(vcol, 0), kv_spec(vcol, 1),
                  pl.BlockSpec((None, None, PAST_LEN, NKV), lambda b, n: (b, layer_j, 0, 0)),
                  pl.BlockSpec((None, None, PAST_LEN, NKV), lambda b, n: (b, layer_j, 0, 0)),
                  table, table, table,
                  pl.BlockSpec((1, ATTN_HEADS), lambda b, n: (0, 0)),
                  pl.BlockSpec(memory_space=pl.ANY)],
        out_specs=pl.BlockSpec((Q_BLOCK, NQ), lambda b, n: (base + b * nb + n, 0)),
        input_output_aliases={13: 0},
        compiler_params=_cparams(("arbitrary", "arbitrary"),
                                 4 * Q_BLOCK * NQ * 4 + 12 * Q_BLOCK * NKV * 4 + 4 * PAST_LEN * NKV * 4
                                 + 6 * DEC_SEQ * 128 * 4 + 24 * ATTN_GROUP * Q_BLOCK * (3 * Q_BLOCK + PAST_LEN) * 4),
        name="latent_attention",
    )(qkv, qkv, qkv, qkv, qkv, qkv, qkv, ck, cv, cos, sin_a, sin_b, sink.reshape(1, ATTN_HEADS), attn_out)


def attention_layer(x, layer, layer_j, mods, norm_g, wqkv, wo, sink, cache_k, cache_v):
    qkv = norm_mod_matmul(x, layer, norm_g, mods, wqkv, layer_j, "attn_qkv")
    a = context_attention(qkv, sink)
    a = latent_attention(qkv, cache_k, cache_v, layer_j, sink, a)
    x = matmul_gated_residual(a, wo, layer_j, x, layer, mods, "attn_wo")
    new_k = qkv[:ROWS_P, NQ:NQ + NKV].reshape(BATCH, SEQ, ATTN_KV_HEADS, HEAD_DIM)
    new_v = qkv[:ROWS_P, NQ + NKV:].reshape(BATCH, SEQ, ATTN_KV_HEADS, HEAD_DIM)
    return x, new_k, new_v


HG_TILE = 256
HG_LEVELS = 8
HG_IN = 3 * 1024 + 2 * 1024


def _hgrn_consts():
    t = np.arange(HG_TILE)
    x = t[:, None] ^ t[None, :]
    hb = np.where(x == 0, -1, np.floor(np.log2(np.maximum(x, 1))).astype(np.int64))
    later = t[:, None] > t[None, :]
    masks, tris = [], []
    for reverse in (False, True):
        side = ~later & (x != 0) if reverse else later
        masks.append(np.stack([hb == -1] + [(hb == lvl) & side for lvl in range(HG_LEVELS)]).astype(np.float32))
        tris.append((t[None, :] >= t[:, None]) if reverse else (t[None, :] <= t[:, None]))
    return jnp.asarray(np.stack(masks)), jnp.asarray(np.stack(tris).astype(np.float32), dtype=BF16)


def _split3(x):
    hi = x.astype(BF16)
    r = x - hi.astype(F32)
    mid = r.astype(BF16)
    lo = (r - mid.astype(F32)).astype(BF16)
    return hi, mid, lo


def _block_row(x, blk, idx):
    t = x.shape[0]
    x3 = x.reshape(t // blk, blk, x.shape[1])
    return jnp.broadcast_to(x3[:, idx:idx + 1, :], x3.shape).reshape(x.shape)


def _lower_bound(lb_ref, layer, direction):
    x = lb_ref[direction]
    e = jnp.exp(x - jnp.max(x, axis=0, keepdims=True))
    p = e / jnp.sum(e, axis=0, keepdims=True)
    return jnp.sum(p[1:layer + 1, :], axis=0, keepdims=True)


def _hgrn_tile(q, v, z, lb, s_in, mask_ref, tri, reverse):
    t = HG_TILE
    sg = _sigmoid(z)
    f = lb + (1.0 - lb) * sg
    k = (1.0 - lb) * (1.0 - sg)
    lf3 = _split3(jnp.log(f))
    cum = sum(jnp.dot(tri, p, preferred_element_type=F32) for p in lf3)
    rows = lax.broadcasted_iota(jnp.int32, (t, 1), 0)
    att = mask_ref[0] * _dot_t(q, k)
    for lvl in range(HG_LEVELS):
        half = 1 << lvl
        bit = (rows & half) != 0
        qside = ~bit if reverse else bit
        if lvl == 0:
            e = jnp.where(qside, f, 1.0)
        else:
            ref = _block_row(cum, 2 * half, half if reverse else half - 1)
            e = jnp.exp(jnp.where(qside, cum - ref, ref - cum))
        qt = jnp.where(qside, q * e, 0.0)
        kt = jnp.where(qside, 0.0, k * e)
        att = att + mask_ref[lvl + 1] * _dot_t(qt, kt)
    o = _bdot(att, v)
    last = cum[0:1, :] if reverse else cum[t - 1:t, :]
    kd = (k * jnp.exp(last - cum)).astype(BF16)
    s_out = lax.dot_general(kd, v.astype(BF16), (((0,), (0,)), ((), ())), preferred_element_type=F32)
    if s_in is not None:
        o = o + _bdot(q * jnp.exp(cum), s_in)
        ones = jnp.ones((t, HGRN_DV), BF16)
        last_col = sum(lax.dot_general(p, ones, (((0,), (0,)), ((), ())), preferred_element_type=F32) for p in lf3)
        s_out = jnp.exp(last_col) * s_in + s_out
    return o, s_out


def _hgrn_prompt_kernel(q_ref, v_ref, zf_ref, zb_ref, lb_ref, mask_ref, tri_ref, of_ref, ob_ref, sfin_ref, *, layer):
    q, v = q_ref[...], v_ref[...]
    of_ref[...], sfin_ref[0] = _hgrn_tile(q, v, zf_ref[...], _lower_bound(lb_ref, layer, 0), None, mask_ref.at[0], tri_ref[0], False)
    ob_ref[...], sfin_ref[1] = _hgrn_tile(q, v, zb_ref[...], _lower_bound(lb_ref, layer, 1), None, mask_ref.at[1], tri_ref[1], True)


def _hgrn_sample_kernel(qf_ref, vf_ref, zf_ref, qb_ref, vb_ref, zb_ref, lb_ref, s0_ref, mask_ref, tri_ref,
                        pf_ref, pb_ref, of_ref, ob_ref, state_ref, *, layer):
    del pf_ref, pb_ref
    tiles = DEC_SEQ // HG_TILE

    @pl.when(pl.program_id(1) % tiles == 0)
    def _():
        state_ref[...] = s0_ref[...]

    of_ref[...], state_ref[0] = _hgrn_tile(qf_ref[...], vf_ref[...], zf_ref[...], _lower_bound(lb_ref, layer, 0), state_ref[0],
                                           mask_ref.at[0], tri_ref[0], False)
    ob_ref[...], state_ref[1] = _hgrn_tile(qb_ref[...], vb_ref[...], zb_ref[...], _lower_bound(lb_ref, layer, 1), state_ref[1],
                                           mask_ref.at[1], tri_ref[1], True)


def hgrn_scan(proj, lb_raw, layer, state_hgrn, layer_j):
    masks, tris = _hgrn_consts()
    hw = HGRN_DK
    qc, vc, zfc, zbc = 0, 1024 // hw, 2048 // hw, 3072 // hw
    const_specs = [pl.BlockSpec((2, DEPTH, hw), lambda h, i: (0, 0, h))]
    mask_specs = [pl.BlockSpec((2, HG_LEVELS + 1, HG_TILE, HG_TILE), lambda h, i: (0, 0, 0, 0)),
                  pl.BlockSpec((2, HG_TILE, HG_TILE), lambda h, i: (0, 0, 0))]
    vmem = (2 * 2 * (HG_LEVELS + 1) * HG_TILE * HG_TILE * 4 + 16 * HG_TILE * hw * 4 + 8 * hw * hw * 4
            + 24 * HG_TILE * HG_TILE * 4)

    def col(cb, row_fn):
        return pl.BlockSpec((HG_TILE, hw), lambda h, i: (row_fn(i), cb + h))

    o_shape = jax.ShapeDtypeStruct((ROWS, HGRN_HEADS * HGRN_DV), F32)
    same = lambda i: i
    of, ob, sfin = pl.pallas_call(
        functools.partial(_hgrn_prompt_kernel, layer=layer),
        out_shape=(o_shape, o_shape, jax.ShapeDtypeStruct((BATCH, 2, HGRN_HEADS, HGRN_DK, HGRN_DV), F32)),
        grid=(HGRN_HEADS, BATCH),
        in_specs=[col(qc, same), col(vc, same), col(zfc, same), col(zbc, same)] + const_specs + mask_specs,
        out_specs=(col(0, same), col(0, same),
                   pl.BlockSpec((None, 2, None, HGRN_DK, HGRN_DV), lambda h, i: (i, 0, h, 0, 0))),
        compiler_params=_cparams(("arbitrary", "arbitrary"), vmem),
        name="hgrn_scan_prompt",
    )(proj, proj, proj, proj, lb_raw, masks, tris)

    tiles = DEC_SEQ // HG_TILE
    base = ROWS_P // HG_TILE
    fwd = lambda i: base + i
    bwd = lambda i: base + (i // tiles) * tiles + (tiles - 1 - i % tiles)
    of, ob = pl.pallas_call(
        functools.partial(_hgrn_sample_kernel, layer=layer),
        out_shape=(o_shape, o_shape),
        grid=(HGRN_HEADS, DEC_BATCH * tiles),
        in_specs=[col(qc, fwd), col(vc, fwd), col(zfc, fwd), col(qc, bwd), col(vc, bwd), col(zbc, bwd)]
        + const_specs
        + [pl.BlockSpec((None, None, 2, None, HGRN_DK, HGRN_DV), lambda h, i: (i // tiles, layer_j, 0, h, 0, 0))]
        + mask_specs + [pl.BlockSpec(memory_space=pl.ANY), pl.BlockSpec(memory_space=pl.ANY)],
        out_specs=(col(0, fwd), col(0, bwd)),
        scratch_shapes=[pltpu.VMEM((2, HGRN_DK, HGRN_DV), F32)],
        input_output_aliases={10: 0, 11: 1},
        compiler_params=_cparams(("arbitrary", "arbitrary"), vmem),
        name="hgrn_scan_sample",
    )(proj, proj, proj, proj, proj, proj, lb_raw, state_hgrn, masks, tris, of, ob)
    return of, ob, sfin


def _hgrn_out_kernel(of_ref, ob_ref, g_ref, gn_ref, w_ref, x_ref, mod_ref, o_ref):
    gn = gn_ref[...]
    parts = []
    for h in range(HGRN_HEADS):
        hs = slice(h * HGRN_DV, (h + 1) * HGRN_DV)
        o = of_ref[:, hs] + ob_ref[:, hs]
        g = g_ref[:, hs]
        o = o * lax.rsqrt(jnp.mean(o * o, axis=-1, keepdims=True) + NORM_EPS) * gn * (g * _sigmoid(g))
        parts.append(o.astype(BF16))
    y = jnp.dot(jnp.concatenate(parts, axis=-1), w_ref[...], preferred_element_type=F32)
    o_ref[...] = x_ref[...] + mod_ref[2:3, :] * y


def hgrn_out(of, ob, proj, g_norm, wo, w_idx, x, layer, mods):
    gcol = 4096 // D
    return pl.pallas_call(
        _hgrn_out_kernel,
        out_shape=jax.ShapeDtypeStruct((ROWS, D), F32),
        grid=(N_ROW_TILES,),
        in_specs=[pl.BlockSpec((ROW_TILE, D), lambda i: (i, 0)),
                  pl.BlockSpec((ROW_TILE, D), lambda i: (i, 0)),
                  pl.BlockSpec((ROW_TILE, D), lambda i: (i, gcol)),
                  pl.BlockSpec((1, HGRN_DV), lambda i: (0, 0)),
                  pl.BlockSpec((None, D, D), lambda i: (w_idx, 0, 0), pipeline_mode=pl.Buffered(1)),
                  pl.BlockSpec((ROW_TILE, D), lambda i: (i, 0)),
                  _mod_spec(layer, 1)],
        out_specs=pl.BlockSpec((ROW_TILE, D), lambda i: (i, 0)),
        compiler_params=_cparams(("arbitrary",), 14 * ROW_TILE * D * 4),
        name="hgrn_out",
    )(of, ob, proj, g_norm.reshape(1, HGRN_DV), wo, x, mods)


def hgrn_layer(x, layer, layer_j, mods, norm_g, w_in, hgrn_lb, g_norm, wo, state_hgrn):
    proj = norm_mod_matmul(x, layer, norm_g, mods, w_in, layer_j, "hgrn_in")
    of, ob, sfin = hgrn_scan(proj, jnp.transpose(hgrn_lb, (1, 0, 2)), layer, state_hgrn, layer_j)
    x = hgrn_out(of, ob, proj, g_norm[layer_j], wo, layer_j, x, layer, mods)
    return x, sfin


SSM_N = SSM_GROUPS * SSM_STATE
SSM_KT = 8
SSM_ROWS = 256
SSM_PAD = 16


def _ssm_prep_kernel(are_ref, aim_ref, ldt_ref, bre_ref, bim_ref, lre_ref, lim_ref, bbre_ref, bbim_ref):
    a_re = jnp.minimum(are_ref[...], -1e-4)
    a_im = aim_ref[...]
    dt = jnp.exp(ldt_ref[...])
    mag = jnp.exp(a_re * dt)
    l_re = mag * jnp.cos(a_im * dt)
    l_im = mag * jnp.sin(a_im * dt)
    lre_ref[...] = l_re
    lim_ref[...] = l_im
    den = a_re * a_re + a_im * a_im
    c_re = ((l_re - 1.0) * a_re + l_im * a_im) / den
    c_im = (l_im * a_re - (l_re - 1.0) * a_im) / den
    b_re, b_im = bre_ref[...], bim_ref[...]
    bbre_ref[...] = c_re[:, None, :] * b_re - c_im[:, None, :] * b_im
    bbim_ref[...] = c_re[:, None, :] * b_im + c_im[:, None, :] * b_re


def ssm_discretize(a_re, a_im, log_dt, b_re, b_im):
    g2 = 2 * SSM_GROUPS
    sh = jax.ShapeDtypeStruct((g2, SSM_STATE), F32)
    shb = jax.ShapeDtypeStruct((g2, SSM_GROUP, SSM_STATE), F32)
    bt = lambda b: jnp.transpose(b, (0, 1, 3, 2)).reshape(g2, SSM_GROUP, SSM_STATE)
    return pl.pallas_call(_ssm_prep_kernel, out_shape=(sh, sh, shb, shb), name="ssm_discretize")(
        a_re.reshape(g2, SSM_STATE), a_im.reshape(g2, SSM_STATE), log_dt.reshape(g2, 1), bt(b_re), bt(b_im))


def _ssm_block_diag(l_re, l_im, bb_re, bb_im, c_re, c_im):
    nk = SSM_GROUPS // SSM_KT
    eye = jnp.eye(SSM_KT, dtype=F32)

    def bmat(b):
        b = b.reshape(2, nk, SSM_KT, SSM_GROUP, SSM_STATE)
        return jnp.einsum('dkgip,gh->dkgihp', b, eye).reshape(2, nk, SSM_KT * SSM_GROUP, SSM_KT * SSM_STATE)

    def cmat(c):
        c = c.reshape(2, nk, SSM_KT, SSM_GROUP, SSM_STATE)
        return jnp.einsum('dkgip,gh->dkhpgi', c, eye).reshape(2, nk, SSM_KT * SSM_STATE, SSM_KT * SSM_GROUP)

    b_mat = jnp.concatenate([bmat(bb_re), bmat(bb_im)], axis=-1).astype(BF16)
    c_mat = jnp.concatenate([cmat(c_re), cmat(-c_im)], axis=-2).astype(BF16)
    lam = jnp.stack([l_re.reshape(2, SSM_N), l_im.reshape(2, SSM_N)], axis=1)
    return b_mat, c_mat, lam


X4_SHAPE = (ROWS // (4 * SEQ), 4, SEQ, D)


def _tm_geometry(prompt):
    if prompt:
        batch = BATCH
        steps = SSM_ROWS // batch
        return batch, steps, (4, 4, steps, D), (lambda i: (0, 0, i, 0)), SEQ // steps, (0,) * batch
    batch = DEC_BATCH
    steps = SSM_ROWS // batch
    per_q = SEQ // steps
    return (batch, steps, (4, 1, steps, D), (lambda i: (1, i // per_q, i % per_q, 0)), DEC_SEQ // steps,
            tuple(range(1, 1 + batch)))


def _x4_seq(ref, b):
    return ref.at[b // ref.shape[1], b % ref.shape[1]]


LANE_SLABS = D // 128


def _slab_store(s_ref, rows, val):
    for c in range(LANE_SLABS):
        s_ref[c, rows, :] = val[:, c * 128:(c + 1) * 128]


def _slab_load(s_ref, rows):
    return jnp.concatenate([s_ref[c, rows, :] for c in range(LANE_SLABS)], axis=-1)


def _normmod_tm_kernel(x_ref, g_ref, mod_ref, o_ref, s_ref, *, batch, steps, mod_rows):
    g = g_ref[...]
    for b in range(batch):
        m = mod_ref.at[mod_rows[b]]
        _slab_store(s_ref, slice(b * steps, (b + 1) * steps),
                    _norm_mod(_x4_seq(x_ref, b)[...], g, m[0:1, :], m[1:2, :]))
    for t in range(steps):
        o_ref[t * batch:(t + 1) * batch, :] = _slab_load(s_ref, pl.ds(t, batch, stride=steps))


def norm_mod_time_major(x4, layer, norm_g, mods, prompt):
    batch, steps, blk, idx, tiles, mod_rows = _tm_geometry(prompt)
    return pl.pallas_call(
        functools.partial(_normmod_tm_kernel, batch=batch, steps=steps, mod_rows=mod_rows),
        out_shape=jax.ShapeDtypeStruct((tiles * SSM_ROWS, D), F32),
        grid=(tiles,),
        in_specs=[pl.BlockSpec(blk, idx),
                  pl.BlockSpec((None, 1, D), lambda i: (layer, 0, 0)),
                  pl.BlockSpec((None, MOD_ROWS, 6, D), lambda i: (layer, 0, 0, 0))],
        out_specs=pl.BlockSpec((SSM_ROWS, D), lambda i: (i, 0)),
        scratch_shapes=[pltpu.VMEM((LANE_SLABS, SSM_ROWS, 128), F32)],
        compiler_params=_cparams(("arbitrary",), 8 * SSM_ROWS * D * 4),
        name="ssm_norm_time_major",
    )(x4, norm_g.reshape(DEPTH, 1, D), mods)


def _ssm_scan_kernel(xf_ref, xb_ref, bm_ref, cm_ref, lam_ref, h0_ref, yf_ref, yb_ref, hfin_ref,
                     hre_ref, him_ref, st_ref, *, batch):
    i = pl.program_id(0)
    steps = SSM_ROWS // batch
    nk = SSM_GROUPS // SSM_KT
    kw = SSM_KT * SSM_STATE

    @pl.when(i == 0)
    def _():
        st_ref[...] = h0_ref[...]

    def step(d, prev_rows, cur_rows):
        l_re, l_im = lam_ref[d, 0], lam_ref[d, 1]
        p_re, p_im = hre_ref[prev_rows, :], him_ref[prev_rows, :]
        hre_ref[cur_rows, :] = l_re * p_re - l_im * p_im + hre_ref[cur_rows, :]
        him_ref[cur_rows, :] = l_re * p_im + l_im * p_re + him_ref[cur_rows, :]

    def run(d, x_ref, y_ref, reverse):
        base = 0 if reverse else SSM_PAD
        srow = SSM_ROWS if reverse else SSM_PAD - batch
        xb = x_ref[...].astype(BF16)
        for k in range(nk):
            bu = jnp.dot(xb[:, k * 128:(k + 1) * 128], bm_ref[d, k], preferred_element_type=F32)
            hre_ref[base:base + SSM_ROWS, k * kw:(k + 1) * kw] = bu[:, :kw]
            him_ref[base:base + SSM_ROWS, k * kw:(k + 1) * kw] = bu[:, kw:]
        hre_ref[srow:srow + batch, :] = st_ref[d, 0]
        him_ref[srow:srow + batch, :] = st_ref[d, 1]

        if batch % 8 == 0:
            def body(s, carry):
                t = (steps - 1 - s) if reverse else s
                cur = pl.ds(pl.multiple_of(base + t * batch, batch), batch)
                prev = pl.ds(pl.multiple_of(base + (t + 1) * batch if reverse else base + (t - 1) * batch, batch),
                             batch)
                step(d, prev, cur)
                return carry
            lax.fori_loop(0, steps, body, 0)
        else:
            per = 8 // batch

            def body(s, carry):
                g = (steps // per - 1 - s) if reverse else s
                slab = pl.multiple_of(base + g * 8, 8)
                l_re, l_im = lam_ref[d, 0], lam_ref[d, 1]
                cur_re, cur_im = hre_ref[pl.ds(slab, 8), :], him_ref[pl.ds(slab, 8), :]
                nb_slab = pl.multiple_of(slab + 8 if reverse else slab - 8, 8)
                nb_re, nb_im = hre_ref[pl.ds(nb_slab, 8), :], him_ref[pl.ds(nb_slab, 8), :]
                if reverse:
                    p_re, p_im = nb_re[0:batch], nb_im[0:batch]
                    order = range(per - 1, -1, -1)
                else:
                    p_re, p_im = nb_re[8 - batch:8], nb_im[8 - batch:8]
                    order = range(per)
                outs_re, outs_im = [None] * per, [None] * per
                for j in order:
                    b_re, b_im = cur_re[j * batch:(j + 1) * batch], cur_im[j * batch:(j + 1) * batch]
                    p_re, p_im = l_re * p_re - l_im * p_im + b_re, l_re * p_im + l_im * p_re + b_im
                    outs_re[j], outs_im[j] = p_re, p_im
                hre_ref[pl.ds(slab, 8), :] = jnp.concatenate(outs_re, axis=0)
                him_ref[pl.ds(slab, 8), :] = jnp.concatenate(outs_im, axis=0)
                return carry
            lax.fori_loop(0, steps // per, body, 0)

        erow = 0 if reverse else SSM_PAD + SSM_ROWS - batch
        st_ref[d, 0] = hre_ref[erow:erow + batch, :]
        st_ref[d, 1] = him_ref[erow:erow + batch, :]
        for k in range(nk):
            hk = jnp.concatenate([hre_ref[base:base + SSM_ROWS, k * kw:(k + 1) * kw],
                                  him_ref[base:base + SSM_ROWS, k * kw:(k + 1) * kw]], axis=-1).astype(BF16)
            y_ref[:, k * 128:(k + 1) * 128] = jnp.dot(hk, cm_ref[d, k], preferred_element_type=F32)

    run(0, xf_ref, yf_ref, False)
    run(1, xb_ref, yb_ref, True)

    @pl.when(i == pl.num_programs(0) - 1)
    def _():
        hfin_ref[...] = st_ref[...]


def ssm_scan(xn_tm, b_mat, c_mat, lam, h0, batch):
    rows = xn_tm.shape[0]
    n = rows // SSM_ROWS
    lam_b = jnp.broadcast_to(lam[:, :, None, :], (2, 2, batch, SSM_N))
    y_shape = jax.ShapeDtypeStruct((rows, D), F32)
    full = lambda a: pl.BlockSpec(a.shape, lambda i: (0,) * a.ndim)
    vmem = (2 * (SSM_ROWS + 2 * SSM_PAD) * SSM_N * 4 + 8 * SSM_ROWS * D * 4 + 2 * (b_mat.size + c_mat.size) * 2
            + 12 * batch * SSM_N * 4 * 2 + 8 * SSM_ROWS * 1024 * 4)
    return pl.pallas_call(
        functools.partial(_ssm_scan_kernel, batch=batch),
        out_shape=(y_shape, y_shape, jax.ShapeDtypeStruct((2, 2, batch, SSM_N), F32)),
        grid=(n,),
        in_specs=[pl.BlockSpec((SSM_ROWS, D), lambda i: (i, 0)),
                  pl.BlockSpec((SSM_ROWS, D), lambda i: (n - 1 - i, 0)),
                  full(b_mat), full(c_mat), full(lam_b), full(h0)],
        out_specs=(pl.BlockSpec((SSM_ROWS, D), lambda i: (i, 0)),
                   pl.BlockSpec((SSM_ROWS, D), lambda i: (n - 1 - i, 0)),
                   pl.BlockSpec((2, 2, batch, SSM_N), lambda i: (0, 0, 0, 0))),
        scratch_shapes=[pltpu.VMEM((SSM_ROWS + 2 * SSM_PAD, SSM_N), F32),
                        pltpu.VMEM((SSM_ROWS + 2 * SSM_PAD, SSM_N), F32),
                        pltpu.VMEM((2, 2, batch, SSM_N), F32)],
        compiler_params=_cparams(("arbitrary",), vmem),
        name="ssm_scan",
    )(xn_tm, xn_tm, b_mat, c_mat, lam_b, h0)


def _gelu_tanh(x):
    return 0.5 * x * (1.0 + jnp.tanh(math.sqrt(2.0 / math.pi) * (x + 0.044715 * (x * x * x))))


def _ssm_glu_kernel(yf_ref, yb_ref, xn_ref, d_ref, w_ref, x_ref, mod_ref, prev_ref, o_ref, s_ref,
                    *, batch, steps, mod_rows):
    del prev_ref
    g = _gelu_tanh(yf_ref[...] + yb_ref[...] + d_ref[...] * xn_ref[...])
    u = jnp.dot(g.astype(BF16), w_ref[...], preferred_element_type=F32)
    _slab_store(s_ref, slice(None), u[:, :D] * _sigmoid(u[:, D:]))
    for b in range(batch):
        gate = mod_ref[mod_rows[b], 2:3, :]
        _x4_seq(o_ref, b)[...] = (_x4_seq(x_ref, b)[...]
                                  + gate * _slab_load(s_ref, pl.ds(b, steps, stride=batch)))


def ssm_glu(yf, yb, xn, d, w_glu, w_idx, x4, layer, mods, prev, alias_prev, prompt):
    batch, steps, blk, idx, tiles, mod_rows = _tm_geometry(prompt)
    tm_spec = pl.BlockSpec((SSM_ROWS, D), lambda i: (i, 0))
    return pl.pallas_call(
        functools.partial(_ssm_glu_kernel, batch=batch, steps=steps, mod_rows=mod_rows),
        out_shape=jax.ShapeDtypeStruct(X4_SHAPE, F32),
        grid=(tiles,),
        in_specs=[tm_spec, tm_spec, tm_spec,
                  pl.BlockSpec((None, 1, D), lambda i: (w_idx, 0, 0)),
                  pl.BlockSpec((None, D, 2 * D), lambda i: (w_idx, 0, 0), pipeline_mode=pl.Buffered(1)),
                  pl.BlockSpec(blk, idx),
                  pl.BlockSpec((None, MOD_ROWS, 6, D), lambda i: (layer, 0, 0, 0)),
                  pl.BlockSpec(memory_space=pl.ANY)],
        out_specs=pl.BlockSpec(blk, idx),
        scratch_shapes=[pltpu.VMEM((LANE_SLABS, SSM_ROWS, 128), F32)],
        input_output_aliases={7: 0} if alias_prev else {},
        compiler_params=_cparams(("arbitrary",), 24 * SSM_ROWS * D * 4 + D * 2 * D * 2),
        name="ssm_glu",
    )(yf, yb, xn, d.reshape(-1, 1, D), w_glu, x4, mods, prev)


def ssm_layer(x, layer, layer_j, mods, norm_g, a_re, a_im, log_dt, b_re, b_im, c_re, c_im, d, w_glu, state_ssm):
    l_re, l_im, bb_re, bb_im = ssm_discretize(a_re[layer_j], a_im[layer_j], log_dt[layer_j], b_re[layer_j],
                                              b_im[layer_j])
    b_mat, c_mat, lam = _ssm_block_diag(l_re, l_im, bb_re, bb_im, c_re[layer_j], c_im[layer_j])
    x4 = x.reshape(X4_SHAPE)
    xn_p = norm_mod_time_major(x4, layer, norm_g, mods, True)
    xn_s = norm_mod_time_major(x4, layer, norm_g, mods, False)
    h0_p = jnp.zeros((2, 2, BATCH, SSM_N), F32)
    h0_s = jnp.transpose(state_ssm[:, layer_j].reshape(DEC_BATCH, 2, SSM_N, 2), (1, 3, 0, 2))
    yfp, ybp, hfin = ssm_scan(xn_p, b_mat, c_mat, lam, h0_p, BATCH)
    yfs, ybs, _ = ssm_scan(xn_s, b_mat, c_mat, lam, h0_s, DEC_BATCH)
    out = ssm_glu(yfp, ybp, xn_p, d, w_glu, layer_j, x4, layer, mods, x4, False, True)
    out = ssm_glu(yfs, ybs, xn_s, d, w_glu, layer_j, x4, layer, mods, out, True, False)
    new_state = jnp.transpose(hfin, (2, 0, 3, 1)).reshape(BATCH, 2, SSM_GROUPS, SSM_STATE, 2)
    return out.reshape(ROWS, D), new_state


def kernel(x_prompt, x_sample, cache_k, cache_v, state_hgrn, state_ssm, c, c_ctx, ada_w, ada_b, norm1_g, norm2_g, attn_wqkv, attn_wo, attn_sink, hgrn_w_in, hgrn_lb, hgrn_g_norm, hgrn_wo, ssm_a_re, ssm_a_im, ssm_log_dt, ssm_b_re, ssm_b_im, ssm_c_re, ssm_c_im, ssm_d, ssm_w_glu, ffn_w_up, ffn_conv_w, ffn_conv_b, ffn_w_down, final_g):
    cond8 = jnp.zeros((MOD_ROWS, D), F32).at[0].set(c_ctx).at[1:1 + DEC_BATCH].set(c)
    mods = ada_modulation(cond8, ada_w, ada_b)
    x = jnp.concatenate([x_prompt.reshape(ROWS_P, D), x_sample.reshape(ROWS_S, D)], axis=0)
    wqkv, wo, w_in, hwo, w_glu, w_up, w_down = (w.astype(BF16) for w in (
        attn_wqkv, attn_wo, hgrn_w_in, hgrn_wo, ssm_w_glu, ffn_w_up, ffn_w_down))
    new_k, new_v, new_hgrn, new_ssm = [], [], [], []
    for l in range(DEPTH):
        kind, j = l % N_MIXERS, l // N_MIXERS
        if kind == 0:
            x, k, v = attention_layer(x, l, j, mods, norm1_g, wqkv, wo, attn_sink[j], cache_k, cache_v)
            new_k.append(k)
            new_v.append(v)
        elif kind == 1:
            x, s = hgrn_layer(x, l, j, mods, norm1_g, w_in, hgrn_lb, hgrn_g_norm, hwo, state_hgrn)
            new_hgrn.append(s)
        else:
            x, s = ssm_layer(x, l, j, mods, norm1_g, ssm_a_re, ssm_a_im, ssm_log_dt, ssm_b_re, ssm_b_im,
                             ssm_c_re, ssm_c_im, ssm_d, w_glu, state_ssm)
            new_ssm.append(s)
        x = conv_ffn_residual(x, l, norm2_g, mods, w_up, ffn_conv_w, ffn_conv_b, w_down)
    y_prompt = final_norm(x, final_g, 0, N_ROW_TILES_P).reshape(BATCH, SEQ, D)
    y_sample = final_norm(x, final_g, N_ROW_TILES_P, N_ROW_TILES - N_ROW_TILES_P).reshape(DEC_BATCH, DEC_SEQ, D)
    return (y_prompt, y_sample, jnp.stack(new_k, axis=1), jnp.stack(new_v, axis=1),
            jnp.stack(new_hgrn, axis=1), jnp.stack(new_ssm, axis=1))
```

```python
import functools
import math

import jax
import jax.numpy as jnp
import numpy as np
from jax import lax
from jax.experimental import pallas as pl
from jax.experimental.pallas import tpu as pltpu

F32 = jnp.float32
BF16 = jnp.bfloat16

D = 1024
BATCH = 16
SEQ = 256
DEPTH = 4
DEC_BATCH = 4
DEC_SEQ = 1024
PAST_LEN = 512
GRID_W = 64
N_MIXERS = 3
ATTN_HEADS = 16
ATTN_KV_HEADS = 4
ATTN_GROUP = ATTN_HEADS // ATTN_KV_HEADS
HEAD_DIM = D // ATTN_HEADS
WINDOW = 128
ROPE_BASE = 10000.0
HGRN_HEADS = 8
HGRN_DK = 128
HGRN_DV = 128
SSM_GROUP = 16
SSM_GROUPS = D // SSM_GROUP
SSM_STATE = 64
D_FF = 2816
NORM_EPS = 1e-6

ROWS_P = BATCH * SEQ
ROWS_S = DEC_BATCH * DEC_SEQ
ROWS = ROWS_P + ROWS_S
ROW_TILE = 1024
N_ROW_TILES = ROWS // ROW_TILE
N_ROW_TILES_P = ROWS_P // ROW_TILE
MOD_ROWS = 8
V7X_VMEM_BYTES = 64 * 1024 * 1024


def _mod_row(i, tile=ROW_TILE):
    return jnp.where(i < ROWS_P // tile, 0, (i - ROWS_P // tile) // (DEC_SEQ // tile) + 1)


def _cparams(semantics, vmem_bytes):
    vmem = int(min(max(vmem_bytes * 5 // 4 + (4 << 20), 16 << 20), V7X_VMEM_BYTES - (6 << 20)))
    return pltpu.CompilerParams(dimension_semantics=semantics, vmem_limit_bytes=vmem)


def _bdot(a, b):
    return jnp.dot(a.astype(BF16), b.astype(BF16), preferred_element_type=F32)


def _norm_mod(x, g, shift, scale):
    y = x * lax.rsqrt(jnp.mean(x * x, axis=-1, keepdims=True) + NORM_EPS) * g
    return y * (1.0 + scale) + shift


def _sigmoid(x):
    return 0.5 + 0.5 * jnp.tanh(0.5 * x)


def _silu(x):
    h = 0.5 * x
    return h + h * jnp.tanh(h)


def _ada_kernel(c_ref, w_ref, b_ref, o_ref):
    c = c_ref[...]
    o_ref[...] = _bdot(_silu(c), w_ref[...]) + b_ref[...]


def ada_modulation(cond8, ada_w, ada_b):
    tn = 1024
    out = pl.pallas_call(
        _ada_kernel,
        out_shape=jax.ShapeDtypeStruct((DEPTH, MOD_ROWS, 6 * D), F32),
        grid=(DEPTH, 6 * D // tn),
        in_specs=[
            pl.BlockSpec((MOD_ROWS, D), lambda l, j: (0, 0)),
            pl.BlockSpec((None, D, tn), lambda l, j: (l, 0, j)),
            pl.BlockSpec((None, 1, tn), lambda l, j: (l, 0, j)),
        ],
        out_specs=pl.BlockSpec((None, MOD_ROWS, tn), lambda l, j: (l, 0, j)),
        compiler_params=_cparams(("arbitrary", "arbitrary"), 2 * D * tn * 4),
        name="ada_modulation",
    )(cond8, ada_w, ada_b.reshape(DEPTH, 1, 6 * D))
    return out.reshape(DEPTH, MOD_ROWS, 6, D)


def _nmm_kernel(x_ref, g_ref, mod_ref, w_ref, o_ref, h_ref):
    @pl.when(pl.program_id(1) == 0)
    def _():
        h_ref[...] = _norm_mod(x_ref[...], g_ref[...], mod_ref[0:1, :], mod_ref[1:2, :]).astype(BF16)

    o_ref[...] = jnp.dot(h_ref[...], w_ref[...], preferred_element_type=F32)


def _mod_spec(layer, n_grid, tile=ROW_TILE):
    if n_grid == 1:
        return pl.BlockSpec((None, None, 6, D), lambda i: (layer, _mod_row(i, tile), 0, 0))
    return pl.BlockSpec((None, None, 6, D), lambda i, j: (layer, _mod_row(i, tile), 0, 0))


def norm_mod_matmul(x, layer, norm_g, mods, w, w_idx, name):
    n = w.shape[-1]
    tn = 512
    return pl.pallas_call(
        _nmm_kernel,
        out_shape=jax.ShapeDtypeStruct((ROWS, n), F32),
        grid=(N_ROW_TILES, n // tn),
        in_specs=[
            pl.BlockSpec((ROW_TILE, D), lambda i, j: (i, 0)),
            pl.BlockSpec((None, 1, D), lambda i, j: (layer, 0, 0)),
            _mod_spec(layer, 2),
            pl.BlockSpec((None, D, tn), lambda i, j: (w_idx, 0, j)),
        ],
        out_specs=pl.BlockSpec((ROW_TILE, tn), lambda i, j: (i, j)),
        scratch_shapes=[pltpu.VMEM((ROW_TILE, D), BF16)],
        compiler_params=_cparams(("arbitrary", "arbitrary"),
                                 2 * ROW_TILE * D * 4 + ROW_TILE * D * 2 + 2 * D * tn * 2 + 2 * ROW_TILE * tn * 4),
        name=name,
    )(x, norm_g.reshape(DEPTH, 1, D), mods, w)


def _mm_res_kernel(ap_ref, as_ref, w_ref, x_ref, mod_ref, o_ref):
    def run(a_ref):
        y = jnp.dot(a_ref[...].astype(BF16), w_ref[...], preferred_element_type=F32)
        o_ref[...] = x_ref[...] + mod_ref[2:3, :] * y

    is_prompt = pl.program_id(0) < N_ROW_TILES_P
    pl.when(is_prompt)(lambda: run(ap_ref))
    pl.when(jnp.logical_not(is_prompt))(lambda: run(as_ref))


def _group_specs(k, tile=ROW_TILE):
    n_p = ROWS_P // tile
    return (pl.BlockSpec((tile, k), lambda i: (jnp.minimum(i, n_p - 1), 0)),
            pl.BlockSpec((tile, k), lambda i: (jnp.maximum(i - n_p, 0), 0)))


def matmul_gated_residual(a_p, a_s, w, w_idx, x, layer, mods, name):
    k = a_p.shape[1]
    return pl.pallas_call(
        _mm_res_kernel,
        out_shape=jax.ShapeDtypeStruct((ROWS, D), F32),
        grid=(N_ROW_TILES,),
        in_specs=[
            *_group_specs(k),
            pl.BlockSpec((None, k, D), lambda i: (w_idx, 0, 0), pipeline_mode=pl.Buffered(1)),
            pl.BlockSpec((ROW_TILE, D), lambda i: (i, 0)),
            _mod_spec(layer, 1),
        ],
        out_specs=pl.BlockSpec((ROW_TILE, D), lambda i: (i, 0)),
        compiler_params=_cparams(("arbitrary",), 4 * ROW_TILE * k * 2 + k * D * 2 + 5 * ROW_TILE * D * 4),
        name=name,
    )(a_p, a_s, w, x, mods)


FFN_CHUNK = 256
FFN_CHUNKS = D_FF // FFN_CHUNK
CONV_PAD = 8


def _ffn_kernel(*refs, split_x):
    x_refs, (g_ref, mod_ref, wup_ref, cw_ref, cb_ref, wd_ref, o_ref, h_ref, pad_a, pad_b, act_ref) = (
        refs[:1 + split_x], refs[1 + split_x:])
    i = pl.program_id(0)

    def read_x():
        if not split_x:
            return x_refs[0][...]
        return jnp.where(i < N_ROW_TILES_P, x_refs[0][...], x_refs[1][...])

    h_ref[...] = _norm_mod(read_x(), g_ref[...], mod_ref[3:4, :], mod_ref[4:5, :]).astype(BF16)
    zeros = jnp.zeros((CONV_PAD, 2 * FFN_CHUNK), F32)
    for pad_ref in (pad_a, pad_b):
        pad_ref[0:CONV_PAD, :] = zeros
        pad_ref[CONV_PAD + ROW_TILE:, :] = zeros
    sub = lax.broadcasted_iota(jnp.int32, (8, 1), 0)
    is_prompt = i < N_ROW_TILES_P
    keep_first = jnp.where((sub == 0) & is_prompt, 0.0, 1.0)
    keep_last = jnp.where((sub == 7) & is_prompt, 0.0, 1.0)

    def cut_sequences(v, keep, row):
        parts, at = [], 0
        for b in range(SEQ, ROW_TILE, SEQ):
            lo = b if row == 0 else b - 8
            parts += [v[at:lo], v[lo:lo + 8] * keep]
            at = lo + 8
        return jnp.concatenate(parts + [v[at:]], axis=0)

    def cols(ref, c):
        off = pl.multiple_of(c * FFN_CHUNK, FFN_CHUNK)
        return ref[:, pl.ds(off, FFN_CHUNK)], ref[:, pl.ds(D_FF + off, FFN_CHUNK)]

    def up_proj(c, pad_ref):
        hb = h_ref[...]
        wg, wv = cols(wup_ref, c)
        pad_ref[CONV_PAD:CONV_PAD + ROW_TILE, :FFN_CHUNK] = jnp.dot(hb, wg, preferred_element_type=F32)
        pad_ref[CONV_PAD:CONV_PAD + ROW_TILE, FFN_CHUNK:] = jnp.dot(hb, wv, preferred_element_type=F32)

    def conv_act(c, pad_ref):
        up = pad_ref[CONV_PAD:CONV_PAD + ROW_TILE, :]
        prev = cut_sequences(pad_ref[CONV_PAD - 1:CONV_PAD - 1 + ROW_TILE, :], keep_first, 0)
        nxt = cut_sequences(pad_ref[CONV_PAD + 1:CONV_PAD + 1 + ROW_TILE, :], keep_last, 7)
        cw = jnp.concatenate(cols(cw_ref, c), axis=-1)
        cb = jnp.concatenate(cols(cb_ref, c), axis=-1)
        conv = prev * cw[0:1, :] + up * cw[1:2, :] + nxt * cw[2:3, :] + cb
        gate = conv[:, :FFN_CHUNK]
        act = _silu(gate) * conv[:, FFN_CHUNK:]
        act_ref[:, pl.ds(pl.multiple_of(c * FFN_CHUNK, FFN_CHUNK), FFN_CHUNK)] = act.astype(BF16)

    up_proj(0, pad_a)

    def body(k, carry):
        c = 2 * k
        up_proj(c + 1, pad_b)
        conv_act(c, pad_a)
        up_proj(c + 2, pad_a)
        conv_act(c + 1, pad_b)
        return carry

    lax.fori_loop(0, (FFN_CHUNKS - 1) // 2, body, 0)
    conv_act(FFN_CHUNKS - 1, pad_a)
    y = jnp.dot(act_ref[...], wd_ref[...], preferred_element_type=F32)
    o_ref[...] = read_x() + mod_ref[5:6, :] * y


def conv_ffn_residual(x, layer, norm_g, mods, w_up, conv_w, conv_b, w_down):
    split_x = isinstance(x, tuple)
    x_ops = x if split_x else (x,)
    x_specs = list(_group_specs(D)) if split_x else [pl.BlockSpec((ROW_TILE, D), lambda i: (i, 0))]
    once = pl.Buffered(1)
    vmem = (4 * ROW_TILE * D * 4 + ROW_TILE * D * 2 + 2 * (ROW_TILE + 2 * CONV_PAD) * 2 * FFN_CHUNK * 4
            + ROW_TILE * D_FF * 2 + 3 * D * D_FF * 2 + 5 * ROW_TILE * 2 * FFN_CHUNK * 4)
    return pl.pallas_call(
        functools.partial(_ffn_kernel, split_x=split_x),
        out_shape=jax.ShapeDtypeStruct((ROWS, D), F32),
        grid=(N_ROW_TILES,),
        in_specs=x_specs + [
            pl.BlockSpec((None, 1, D), lambda i: (layer, 0, 0)),
            pl.BlockSpec((None, None, 6, D), lambda i: (layer, _mod_row(i), 0, 0)),
            pl.BlockSpec((None, D, 2 * D_FF), lambda i: (layer, 0, 0), pipeline_mode=once),
            pl.BlockSpec((None, 3, 2 * D_FF), lambda i: (layer, 0, 0), pipeline_mode=once),
            pl.BlockSpec((None, 1, 2 * D_FF), lambda i: (layer, 0, 0), pipeline_mode=once),
            pl.BlockSpec((None, D_FF, D), lambda i: (layer, 0, 0), pipeline_mode=once),
        ],
        out_specs=pl.BlockSpec((ROW_TILE, D), lambda i: (i, 0)),
        scratch_shapes=[pltpu.VMEM((ROW_TILE, D), BF16),
                        pltpu.VMEM((ROW_TILE + 2 * CONV_PAD, 2 * FFN_CHUNK), F32),
                        pltpu.VMEM((ROW_TILE + 2 * CONV_PAD, 2 * FFN_CHUNK), F32),
                        pltpu.VMEM((ROW_TILE, D_FF), BF16)],
        compiler_params=_cparams(("arbitrary",), vmem),
        name="conv_ffn",
    )(*x_ops, norm_g.reshape(DEPTH, 1, D), mods, w_up, conv_w, conv_b.reshape(DEPTH, 1, 2 * D_FF), w_down)


def _final_norm_kernel(x_ref, g_ref, o_ref):
    x = x_ref[...]
    o_ref[...] = x * lax.rsqrt(jnp.mean(x * x, axis=-1, keepdims=True) + NORM_EPS) * g_ref[...]


def final_norm(x, g, first_tile, n_tiles):
    return pl.pallas_call(
        _final_norm_kernel,
        out_shape=jax.ShapeDtypeStruct((n_tiles * ROW_TILE, D), F32),
        grid=(n_tiles,),
        in_specs=[pl.BlockSpec((ROW_TILE, D), lambda i: (first_tile + i, 0)),
                  pl.BlockSpec((1, D), lambda i: (0, 0))],
        out_specs=pl.BlockSpec((ROW_TILE, D), lambda i: (i, 0)),
        compiler_params=_cparams(("arbitrary",), 4 * ROW_TILE * D * 4),
        name="final_norm",
    )(x, g.reshape(1, D))


NQ = ATTN_HEADS * HEAD_DIM
NKV = ATTN_KV_HEADS * HEAD_DIM
Q_BLOCK = 128
MASKED = -1e30
LOG2E = 1.0 / math.log(2.0)
Q_PRESCALE = HEAD_DIM ** -0.5 * LOG2E


def _dot_t(a, b):
    return lax.dot_general(a.astype(BF16), b.astype(BF16), (((1,), (1,)), ((), ())),
                           preferred_element_type=F32)


def _group_rows(q):
    return jnp.concatenate([q[:, g * HEAD_DIM:(g + 1) * HEAD_DIM] for g in range(ATTN_GROUP)], axis=0)


def _sink_rows(sink_ref, h, rows):
    return LOG2E * jnp.concatenate(
        [jnp.broadcast_to(sink_ref[0:1, ATTN_GROUP * h + g:ATTN_GROUP * h + g + 1], (rows, 1))
         for g in range(ATTN_GROUP)], axis=0)


def _with_ones(v):
    return jnp.concatenate([v, jnp.ones_like(v)], axis=-1).astype(BF16)


def _softmax_pv(q4, key_sets, sink2):
    scores = []
    for k, _, bias in key_sets:
        s = _dot_t(q4, k)
        if bias is not None:
            s = jnp.concatenate([s[:, c * 128:(c + 1) * 128] if bc is None else s[:, c * 128:(c + 1) * 128] + bc
                                 for c, bc in enumerate(bias)], axis=-1)
        scores.append(s)
    blocks = [s[:, c * 128:(c + 1) * 128] for s in scores for c in range(s.shape[1] // 128)]
    m = jnp.maximum(sink2, jnp.max(functools.reduce(jnp.maximum, blocks), axis=-1, keepdims=True))
    acc = None
    for s, (_, v1, _) in zip(scores, key_sets):
        t = jnp.dot(jnp.exp2(s - m).astype(BF16), v1, preferred_element_type=F32)
        acc = t if acc is None else acc + t
    denom = pltpu.roll(acc, HEAD_DIM, 1) + jnp.exp2(sink2 - m)
    return (acc * (1.0 / denom))[:, :HEAD_DIM]


def _ctx_attn_kernel(qkv_ref, sink_ref, o_ref):
    outs = []
    for h in range(ATTN_KV_HEADS):
        k = qkv_ref[:, NQ + h * HEAD_DIM:NQ + (h + 1) * HEAD_DIM]
        v = qkv_ref[:, NQ + NKV + h * HEAD_DIM:NQ + NKV + (h + 1) * HEAD_DIM]
        q4 = _group_rows(qkv_ref[:, ATTN_GROUP * h * HEAD_DIM:ATTN_GROUP * (h + 1) * HEAD_DIM] * Q_PRESCALE)
        o4 = _softmax_pv(q4.astype(BF16), [(k.astype(BF16), _with_ones(v), None)], _sink_rows(sink_ref, h, SEQ))
        outs += [o4[g * SEQ:(g + 1) * SEQ, :] for g in range(ATTN_GROUP)]
    o_ref[...] = jnp.concatenate(outs, axis=-1).astype(BF16)


def context_attention(qkv, sink):
    return pl.pallas_call(
        _ctx_attn_kernel,
        out_shape=jax.ShapeDtypeStruct((ROWS_P, NQ), BF16),
        grid=(BATCH,),
        in_specs=[pl.BlockSpec((SEQ, NQ + 2 * NKV), lambda b: (b, 0)),
                  pl.BlockSpec((1, ATTN_HEADS), lambda b: (0, 0))],
        out_specs=pl.BlockSpec((SEQ, NQ), lambda b: (b, 0)),
        compiler_params=_cparams(("arbitrary",), 2 * SEQ * (2 * NQ + 2 * NKV) * 4 + 24 * SEQ * ATTN_GROUP * SEQ * 4),
        name="context_attention",
    )(qkv, sink.reshape(1, ATTN_HEADS))


def _rope(x, cos, sin_a, sin_b):
    outs = []
    for c in range(x.shape[1] // 128):
        s = x[:, c * 128:(c + 1) * 128]
        outs.append(s * cos + pltpu.roll(s, 128 - HEAD_DIM // 4, 1) * sin_a + pltpu.roll(s, HEAD_DIM // 4, 1) * sin_b)
    return jnp.concatenate(outs, axis=-1)


def _lat_attn_kernel(q_ref, kp_ref, kc_ref, kn_ref, vp_ref, vc_ref, vn_ref, ck_ref, cv_ref,
                     cos_ref, sa_ref, sb_ref, sink_ref, o_ref):
    n = pl.program_id(1)
    nb = pl.num_programs(1)

    def tables(blk):
        r = pl.ds(pl.multiple_of(blk * Q_BLOCK, Q_BLOCK), Q_BLOCK)
        return cos_ref[r, :], sa_ref[r, :], sb_ref[r, :]

    cos, sa, sb = tables(n)
    qr = (_rope(q_ref[...], cos, sa, sb) * Q_PRESCALE).astype(BF16)
    k3 = jnp.concatenate([
        _rope(kp_ref[...], *tables(jnp.maximum(n - 1, 0))),
        _rope(kc_ref[...], cos, sa, sb),
        _rope(kn_ref[...], *tables(jnp.minimum(n + 1, nb - 1)))], axis=0).astype(BF16)
    v3 = jnp.concatenate([vp_ref[...], vc_ref[...], vn_ref[...]], axis=0)

    rows = ATTN_GROUP * Q_BLOCK
    qoff = lax.broadcasted_iota(jnp.int32, (rows, Q_BLOCK), 0) & (Q_BLOCK - 1)
    koff = lax.broadcasted_iota(jnp.int32, (rows, Q_BLOCK), 1)
    bias_prev = jnp.where((koff >= qoff) & (n > 0), 0.0, MASKED)
    bias_next = jnp.where((koff <= qoff) & (n < nb - 1), 0.0, MASKED)

    outs = []
    for h in range(ATTN_KV_HEADS):
        hs = slice(h * HEAD_DIM, (h + 1) * HEAD_DIM)
        q4 = _group_rows(qr[:, ATTN_GROUP * h * HEAD_DIM:ATTN_GROUP * (h + 1) * HEAD_DIM])
        o4 = _softmax_pv(q4, [(k3[:, hs], _with_ones(v3[:, hs]), [bias_prev, None, bias_next]),
                              (ck_ref[:, hs].astype(BF16), _with_ones(cv_ref[:, hs]), None)],
                         _sink_rows(sink_ref, h, Q_BLOCK))
        outs += [o4[g * Q_BLOCK:(g + 1) * Q_BLOCK, :] for g in range(ATTN_GROUP)]
    o_ref[...] = jnp.concatenate(outs, axis=-1).astype(BF16)


def _rope_tables():
    t = np.arange(DEC_SEQ)
    half = HEAD_DIM // 2
    inv_freq = 1.0 / (ROPE_BASE ** (np.arange(0, half, 2, dtype=np.float32) / half))
    ar = (t // GRID_W).astype(np.float32)[:, None] * inv_freq
    ac = (t % GRID_W).astype(np.float32)[:, None] * inv_freq
    return jnp.concatenate([jnp.asarray(a) for a in (ar, ar, ac, ac)] * 2, axis=-1)


def latent_attention(qkv, cache_k, cache_v, layer_j, sink):
    ang = _rope_tables()
    cos, sin = jnp.cos(ang), jnp.sin(ang)
    first = (lax.broadcasted_iota(jnp.int32, ang.shape, 1) % (HEAD_DIM // 2)) < HEAD_DIM // 4
    sin_a = jnp.where(first, -sin, 0.0)
    sin_b = jnp.where(first, 0.0, sin)
    nb = DEC_SEQ // Q_BLOCK
    base = ROWS_P // Q_BLOCK
    kcol, vcol = NQ // NKV, NQ // NKV + 1
    ck = cache_k.reshape(DEC_BATCH, -1, PAST_LEN, NKV)
    cv = cache_v.reshape(DEC_BATCH, -1, PAST_LEN, NKV)

    def kv_spec(col, off):
        return pl.BlockSpec((Q_BLOCK, NKV),
                            lambda b, n: (base + b * nb + jnp.clip(n + off, 0, nb - 1), col))

    table = pl.BlockSpec((DEC_SEQ, 128), lambda b, n: (0, 0))
    return pl.pallas_call(
        _lat_attn_kernel,
        out_shape=jax.ShapeDtypeStruct((ROWS_S, NQ), BF16),
        grid=(DEC_BATCH, nb),
        in_specs=[pl.BlockSpec((Q_BLOCK, NQ), lambda b, n: (base + b * nb + n, 0)),
                  kv_spec(kcol, -1), kv_spec(kcol, 0), kv_spec(kcol, 1),
                  kv_spec(vcol, -1), kv_spec(vcol, 0), kv_spec(vcol, 1),
                  pl.BlockSpec((None, None, PAST_LEN, NKV), lambda b, n: (b, layer_j, 0, 0)),
                  pl.BlockSpec((None, None, PAST_LEN, NKV), lambda b, n: (b, layer_j, 0, 0)),
                  table, table, table,
                  pl.BlockSpec((1, ATTN_HEADS), lambda b, n: (0, 0))],
        out_specs=pl.BlockSpec((Q_BLOCK, NQ), lambda b, n: (b * nb + n, 0)),
        compiler_params=_cparams(("arbitrary", "arbitrary"),
                                 4 * Q_BLOCK * NQ * 4 + 12 * Q_BLOCK * NKV * 4 + 4 * PAST_LEN * NKV * 4
                                 + 6 * DEC_SEQ * 128 * 4 + 24 * ATTN_GROUP * Q_BLOCK * (3 * Q_BLOCK + PAST_LEN) * 4),
        name="latent_attention",
    )(qkv, qkv, qkv, qkv, qkv, qkv, qkv, ck, cv, cos, sin_a, sin_b, sink.reshape(1, ATTN_HEADS))


def attention_layer(x, layer, layer_j, mods, norm_g, wqkv, wo, sink, cache_k, cache_v):
    qkv = norm_mod_matmul(x, layer, norm_g, mods, wqkv, layer_j, "attn_qkv")
    a_p = context_attention(qkv, sink)
    a_s = latent_attention(qkv, cache_k, cache_v, layer_j, sink)
    x = matmul_gated_residual(a_p, a_s, wo, layer_j, x, layer, mods, "attn_wo")
    new_k = qkv[:ROWS_P, NQ:NQ + NKV].reshape(BATCH, SEQ, ATTN_KV_HEADS, HEAD_DIM)
    new_v = qkv[:ROWS_P, NQ + NKV:].reshape(BATCH, SEQ, ATTN_KV_HEADS, HEAD_DIM)
    return x, new_k, new_v


HG_TILE = 256
HG_LEVELS = 8
HG_IN = 3 * 1024 + 2 * 1024
HG_OUT_TILE = 512
HG_HEADS_PER_STEP = 2


def _hgrn_consts():
    t = np.arange(HG_TILE)
    x = t[:, None] ^ t[None, :]
    hb = np.where(x == 0, -1, np.floor(np.log2(np.maximum(x, 1))).astype(np.int64))
    later = t[:, None] > t[None, :]
    half = HG_TILE // 2
    masks, tris = [], []
    for reverse in (False, True):
        side = ~later & (x != 0) if reverse else later
        lv = [hb == -1] + [(hb == lvl) & side for lvl in range(HG_LEVELS - 1)]
        masks.append(np.stack([m[:half, :half] for m in lv]).astype(np.float32))
        tris.append((t[None, :] >= t[:, None]) if reverse else (t[None, :] <= t[:, None]))
    return jnp.asarray(np.stack(masks)), jnp.asarray(np.stack(tris).astype(np.float32), dtype=BF16)


def _split_bf16(x):
    def top(v):
        bits = lax.bitcast_convert_type(v, jnp.uint32) & jnp.uint32(0xFFFF0000)
        return lax.bitcast_convert_type(bits, F32)

    hi = top(x)
    r = x - hi
    mid = top(r)
    return hi.astype(BF16), mid.astype(BF16), (r - mid).astype(BF16)


def _block_row(x, blk, idx):
    t = x.shape[0]
    x3 = x.reshape(t // blk, blk, x.shape[1])
    return jnp.broadcast_to(x3[:, idx:idx + 1, :], x3.shape).reshape(x.shape)


def _lower_bound(lb_ref, layer, direction):
    x = lb_ref[direction]
    e = jnp.exp(x - jnp.max(x, axis=0, keepdims=True))
    p = e / jnp.sum(e, axis=0, keepdims=True)
    return jnp.sum(p[1:layer + 1, :], axis=0, keepdims=True)


def _hgrn_tile(q, v, z, lb, s_in, mask_ref, tri, reverse):
    t = HG_TILE
    lo, hi = slice(0, t // 2), slice(t // 2, t)
    sg = _sigmoid(z)
    f = lb + (1.0 - lb) * sg
    k = (1.0 - lb) * (1.0 - sg)
    lf3 = _split_bf16(jnp.log(f))
    cum = sum(jnp.dot(tri, p, preferred_element_type=F32) for p in lf3)
    cum2 = cum * LOG2E
    rows = lax.broadcasted_iota(jnp.int32, (t, 1), 0)
    att = [mask_ref[0] * _dot_t(q[r], k[r]) for r in (lo, hi)]
    top = None
    for lvl in range(HG_LEVELS):
        half = 1 << lvl
        bit = (rows & half) != 0
        qside = ~bit if reverse else bit
        if lvl == 0:
            e = jnp.where(qside, f, 1.0)
        else:
            ref = _block_row(cum2, 2 * half, half if reverse else half - 1)
            e = jnp.exp2(-jnp.abs(cum2 - ref))
        w = (jnp.where(qside, q, k) * e).astype(BF16)
        if lvl < HG_LEVELS - 1:
            att = [a + mask_ref[lvl + 1] * _dot_t(w[r], w[r]) for a, r in zip(att, (lo, hi))]
        else:
            top = _dot_t(w[lo], w[hi]) if reverse else _dot_t(w[hi], w[lo])
    vb = v.astype(BF16)
    o_lo, o_hi = _bdot(att[0], vb[lo]), _bdot(att[1], vb[hi])
    if reverse:
        o_lo = o_lo + _bdot(top, vb[hi])
    else:
        o_hi = o_hi + _bdot(top, vb[lo])
    o = jnp.concatenate([o_lo, o_hi], axis=0)
    last = cum[0:1, :] if reverse else cum[t - 1:t, :]
    kd = (k * jnp.exp(last - cum)).astype(BF16)
    s_out = lax.dot_general(kd, vb, (((0,), (0,)), ((), ())), preferred_element_type=F32)
    if s_in is not None:
        o = o + _bdot(q * jnp.exp(cum), s_in)
        ones = jnp.ones((t, HGRN_DV), BF16)
        last_col = sum(lax.dot_general(p, ones, (((0,), (0,)), ((), ())), preferred_element_type=F32) for p in lf3)
        s_out = jnp.exp(last_col) * s_in + s_out
    return o, s_out


def _hgrn_prompt_kernel(q_ref, v_ref, zf_ref, zb_ref, lb_ref, mask_ref, tri_ref, of_ref, ob_ref, sfin_ref, *, layer):
    for j in range(HG_HEADS_PER_STEP):
        hs = slice(j * HGRN_DK, (j + 1) * HGRN_DK)
        q, v = q_ref[:, hs], v_ref[:, hs]
        lb_f, lb_b = (_lower_bound(lb_ref.at[:, :, hs], layer, d) for d in range(2))
        of_ref[:, hs], sfin_ref[0, j] = _hgrn_tile(q, v, zf_ref[:, hs], lb_f, None, mask_ref.at[0], tri_ref[0], False)
        ob_ref[:, hs], sfin_ref[1, j] = _hgrn_tile(q, v, zb_ref[:, hs], lb_b, None, mask_ref.at[1], tri_ref[1], True)


def _hgrn_sample_kernel(qf_ref, vf_ref, zf_ref, qb_ref, vb_ref, zb_ref, lb_ref, s0_ref, mask_ref, tri_ref,
                        of_ref, ob_ref, state_ref, *, layer):
    tiles = DEC_SEQ // HG_TILE

    @pl.when(pl.program_id(1) % tiles == 0)
    def _():
        state_ref[...] = s0_ref[...]

    for j in range(HG_HEADS_PER_STEP):
        hs = slice(j * HGRN_DK, (j + 1) * HGRN_DK)
        lb_f, lb_b = (_lower_bound(lb_ref.at[:, :, hs], layer, d) for d in range(2))
        of_ref[:, hs], state_ref[0, j] = _hgrn_tile(qf_ref[:, hs], vf_ref[:, hs], zf_ref[:, hs], lb_f,
                                                    state_ref[0, j], mask_ref.at[0], tri_ref[0], False)
        ob_ref[:, hs], state_ref[1, j] = _hgrn_tile(qb_ref[:, hs], vb_ref[:, hs], zb_ref[:, hs], lb_b,
                                                    state_ref[1, j], mask_ref.at[1], tri_ref[1], True)


def hgrn_scan(proj, lb_raw, layer, state_hgrn, layer_j):
    masks, tris = _hgrn_consts()
    hps = HG_HEADS_PER_STEP
    hw = hps * HGRN_DK
    qc, vc, zfc, zbc = 0, 1024 // hw, 2048 // hw, 3072 // hw
    const_specs = [pl.BlockSpec((2, DEPTH, hw), lambda h, i: (0, 0, h))]
    mask_specs = [pl.BlockSpec(masks.shape, lambda h, i: (0, 0, 0, 0)),
                  pl.BlockSpec((2, HG_TILE, HG_TILE), lambda h, i: (0, 0, 0))]
    vmem = 4 * masks.size * 4 + 16 * HG_TILE * hw * 4 + 8 * hw * HGRN_DV * 4 + 24 * hps * HG_TILE * HG_TILE * 4

    def col(cb, row_fn):
        return pl.BlockSpec((HG_TILE, hw), lambda h, i: (row_fn(i), cb + h))

    o_shape = jax.ShapeDtypeStruct((ROWS_P, HGRN_HEADS * HGRN_DV), F32)
    same = lambda i: i
    of_p, ob_p, sfin = pl.pallas_call(
        functools.partial(_hgrn_prompt_kernel, layer=layer),
        out_shape=(o_shape, o_shape, jax.ShapeDtypeStruct((BATCH, 2, HGRN_HEADS, HGRN_DK, HGRN_DV), F32)),
        grid=(HGRN_HEADS // hps, BATCH),
        in_specs=[col(qc, same), col(vc, same), col(zfc, same), col(zbc, same)] + const_specs + mask_specs,
        out_specs=(col(0, same), col(0, same),
                   pl.BlockSpec((None, 2, hps, HGRN_DK, HGRN_DV), lambda h, i: (i, 0, h, 0, 0))),
        compiler_params=_cparams(("arbitrary", "arbitrary"), vmem),
        name="hgrn_scan_prompt",
    )(proj, proj, proj, proj, lb_raw, masks, tris)

    tiles = DEC_SEQ // HG_TILE
    base = ROWS_P // HG_TILE
    bwd = lambda i: (i // tiles) * tiles + (tiles - 1 - i % tiles)
    fwd_in = lambda i: base + i
    bwd_in = lambda i: base + bwd(i)
    o_shape = jax.ShapeDtypeStruct((ROWS_S, HGRN_HEADS * HGRN_DV), F32)
    of_s, ob_s = pl.pallas_call(
        functools.partial(_hgrn_sample_kernel, layer=layer),
        out_shape=(o_shape, o_shape),
        grid=(HGRN_HEADS // hps, DEC_BATCH * tiles),
        in_specs=[col(qc, fwd_in), col(vc, fwd_in), col(zfc, fwd_in), col(qc, bwd_in), col(vc, bwd_in),
                  col(zbc, bwd_in)]
        + const_specs
        + [pl.BlockSpec((None, None, 2, hps, HGRN_DK, HGRN_DV), lambda h, i: (i // tiles, layer_j, 0, h, 0, 0))]
        + mask_specs,
        out_specs=(col(0, same), col(0, bwd)),
        scratch_shapes=[pltpu.VMEM((2, hps, HGRN_DK, HGRN_DV), F32)],
        compiler_params=_cparams(("arbitrary", "arbitrary"), vmem),
        name="hgrn_scan_sample",
    )(proj, proj, proj, proj, proj, proj, lb_raw, state_hgrn, masks, tris)
    return (of_p, ob_p), (of_s, ob_s), sfin


def _hgrn_out_kernel(ofp_ref, obp_ref, ofs_ref, obs_ref, g_ref, gn_ref, w_ref, x_ref, mod_ref, o_ref):
    def run(of_ref, ob_ref):
        gn = gn_ref[...]
        parts = []
        for h in range(HGRN_HEADS):
            hs = slice(h * HGRN_DV, (h + 1) * HGRN_DV)
            o = of_ref[:, hs] + ob_ref[:, hs]
            g = g_ref[:, hs]
            o = o * lax.rsqrt(jnp.mean(o * o, axis=-1, keepdims=True) + NORM_EPS) * gn * _silu(g)
            parts.append(o.astype(BF16))
        y = jnp.dot(jnp.concatenate(parts, axis=-1), w_ref[...], preferred_element_type=F32)
        o_ref[...] = x_ref[...] + mod_ref[2:3, :] * y

    is_prompt = pl.program_id(0) < ROWS_P // HG_OUT_TILE
    pl.when(is_prompt)(lambda: run(ofp_ref, obp_ref))
    pl.when(jnp.logical_not(is_prompt))(lambda: run(ofs_ref, obs_ref))


def hgrn_out(o_p, o_s, proj, g_norm, wo, w_idx, x, layer, mods):
    gcol = 4096 // D
    tile = HG_OUT_TILE
    p_spec, s_spec = _group_specs(D, tile)
    return pl.pallas_call(
        _hgrn_out_kernel,
        out_shape=jax.ShapeDtypeStruct((ROWS, D), F32),
        grid=(ROWS // tile,),
        in_specs=[p_spec, p_spec, s_spec, s_spec,
                  pl.BlockSpec((tile, D), lambda i: (i, gcol)),
                  pl.BlockSpec((1, HGRN_DV), lambda i: (0, 0)),
                  pl.BlockSpec((None, D, D), lambda i: (w_idx, 0, 0), pipeline_mode=pl.Buffered(1)),
                  pl.BlockSpec((tile, D), lambda i: (i, 0)),
                  _mod_spec(layer, 1, tile)],
        out_specs=pl.BlockSpec((tile, D), lambda i: (i, 0)),
        compiler_params=_cparams(("arbitrary",), 22 * tile * D * 4),
        name="hgrn_out",
    )(*o_p, *o_s, proj, g_norm.reshape(1, HGRN_DV), wo, x, mods)


def hgrn_layer(x, layer, layer_j, mods, norm_g, w_in, hgrn_lb, g_norm, wo, state_hgrn):
    proj = norm_mod_matmul(x, layer, norm_g, mods, w_in, layer_j, "hgrn_in")
    o_p, o_s, sfin = hgrn_scan(proj, jnp.transpose(hgrn_lb, (1, 0, 2)), layer, state_hgrn, layer_j)
    x = hgrn_out(o_p, o_s, proj, g_norm[layer_j], wo, layer_j, x, layer, mods)
    return x, sfin


SSM_N = SSM_GROUPS * SSM_STATE
SSM_KT = 8
SSM_ROWS = 256
SSM_PAD = 16


def _ssm_prep_kernel(are_ref, aim_ref, ldt_ref, bre_ref, bim_ref, lre_ref, lim_ref, bbre_ref, bbim_ref):
    a_re = jnp.minimum(are_ref[...], -1e-4)
    a_im = aim_ref[...]
    dt = jnp.exp(ldt_ref[...])
    mag = jnp.exp(a_re * dt)
    l_re = mag * jnp.cos(a_im * dt)
    l_im = mag * jnp.sin(a_im * dt)
    lre_ref[...] = l_re
    lim_ref[...] = l_im
    den = a_re * a_re + a_im * a_im
    c_re = ((l_re - 1.0) * a_re + l_im * a_im) / den
    c_im = (l_im * a_re - (l_re - 1.0) * a_im) / den
    b_re, b_im = bre_ref[...], bim_ref[...]
    bbre_ref[...] = c_re[:, None, :] * b_re - c_im[:, None, :] * b_im
    bbim_ref[...] = c_re[:, None, :] * b_im + c_im[:, None, :] * b_re


def ssm_discretize(a_re, a_im, log_dt, b_re, b_im):
    g2 = 2 * SSM_GROUPS
    sh = jax.ShapeDtypeStruct((g2, SSM_STATE), F32)
    shb = jax.ShapeDtypeStruct((g2, SSM_GROUP, SSM_STATE), F32)
    bt = lambda b: jnp.transpose(b, (0, 1, 3, 2)).reshape(g2, SSM_GROUP, SSM_STATE)
    return pl.pallas_call(_ssm_prep_kernel, out_shape=(sh, sh, shb, shb), name="ssm_discretize")(
        a_re.reshape(g2, SSM_STATE), a_im.reshape(g2, SSM_STATE), log_dt.reshape(g2, 1), bt(b_re), bt(b_im))


def _ssm_block_diag(l_re, l_im, bb_re, bb_im, c_re, c_im):
    nk = SSM_GROUPS // SSM_KT
    eye = jnp.eye(SSM_KT, dtype=F32)

    def bmat(b):
        b = b.reshape(2, nk, SSM_KT, SSM_GROUP, SSM_STATE)
        return jnp.einsum('dkgip,gh->dkgihp', b, eye).reshape(2, nk, SSM_KT * SSM_GROUP, SSM_KT * SSM_STATE)

    def cmat(c):
        c = c.reshape(2, nk, SSM_KT, SSM_GROUP, SSM_STATE)
        return jnp.einsum('dkgip,gh->dkhpgi', c, eye).reshape(2, nk, SSM_KT * SSM_STATE, SSM_KT * SSM_GROUP)

    b_mat = jnp.concatenate([bmat(bb_re), bmat(bb_im)], axis=-1).astype(BF16)
    c_mat = jnp.concatenate([cmat(c_re), cmat(-c_im)], axis=-2).astype(BF16)
    lam = jnp.stack([l_re.reshape(2, SSM_N), l_im.reshape(2, SSM_N)], axis=1)
    return b_mat, c_mat, lam


X4_SHAPE = (ROWS // (4 * SEQ), 4, SEQ, D)


def _tm_geometry(prompt):
    if prompt:
        batch = BATCH
        steps = SSM_ROWS // batch
        return batch, steps, (4, 4, steps, D), (lambda i: (0, 0, i, 0)), SEQ // steps, (0,) * batch
    batch = DEC_BATCH
    steps = SSM_ROWS // batch
    per_q = SEQ // steps
    return (batch, steps, (4, 1, steps, D), (lambda i: (1, i // per_q, i % per_q, 0)), DEC_SEQ // steps,
            tuple(range(1, 1 + batch)))


def _x4_seq(ref, b):
    return ref.at[b // ref.shape[1], b % ref.shape[1]]


LANE_SLABS = D // 128


def _slab_store(s_ref, rows, val):
    for c in range(LANE_SLABS):
        s_ref[c, rows, :] = val[:, c * 128:(c + 1) * 128]


def _slab_load(s_ref, rows):
    return jnp.concatenate([s_ref[c, rows, :] for c in range(LANE_SLABS)], axis=-1)


def _normmod_tm_kernel(x_ref, g_ref, mod_ref, o_ref, s_ref, *, batch, steps, mod_rows):
    g = g_ref[...]
    for b in range(batch):
        m = mod_ref.at[mod_rows[b]]
        _slab_store(s_ref, slice(b * steps, (b + 1) * steps),
                    _norm_mod(_x4_seq(x_ref, b)[...], g, m[0:1, :], m[1:2, :]))
    for t in range(steps):
        o_ref[t * batch:(t + 1) * batch, :] = _slab_load(s_ref, pl.ds(t, batch, stride=steps))


def norm_mod_time_major(x4, layer, norm_g, mods, prompt):
    batch, steps, blk, idx, tiles, mod_rows = _tm_geometry(prompt)
    return pl.pallas_call(
        functools.partial(_normmod_tm_kernel, batch=batch, steps=steps, mod_rows=mod_rows),
        out_shape=jax.ShapeDtypeStruct((tiles * SSM_ROWS, D), F32),
        grid=(tiles,),
        in_specs=[pl.BlockSpec(blk, idx),
                  pl.BlockSpec((None, 1, D), lambda i: (layer, 0, 0)),
                  pl.BlockSpec((None, MOD_ROWS, 6, D), lambda i: (layer, 0, 0, 0))],
        out_specs=pl.BlockSpec((SSM_ROWS, D), lambda i: (i, 0)),
        scratch_shapes=[pltpu.VMEM((LANE_SLABS, SSM_ROWS, 128), F32)],
        compiler_params=_cparams(("arbitrary",), 8 * SSM_ROWS * D * 4),
        name="ssm_norm_time_major",
    )(x4, norm_g.reshape(DEPTH, 1, D), mods)


def _ssm_scan_kernel(xf_ref, xb_ref, bm_ref, cm_ref, lam_ref, h0_ref, yf_ref, yb_ref, hfin_ref,
                     hre_ref, him_ref, st_ref, *, batch):
    i = pl.program_id(0)
    steps = SSM_ROWS // batch
    nk = SSM_GROUPS // SSM_KT
    kw = SSM_KT * SSM_STATE

    @pl.when(i == 0)
    def _():
        st_ref[...] = h0_ref[...]

    def step(d, prev_rows, cur_rows):
        l_re, l_im = lam_ref[d, 0], lam_ref[d, 1]
        p_re, p_im = hre_ref[prev_rows, :], him_ref[prev_rows, :]
        hre_ref[cur_rows, :] = l_re * p_re - l_im * p_im + hre_ref[cur_rows, :]
        him_ref[cur_rows, :] = l_re * p_im + l_im * p_re + him_ref[cur_rows, :]

    def run(d, x_ref, y_ref, reverse):
        base = 0 if reverse else SSM_PAD
        srow = SSM_ROWS if reverse else SSM_PAD - batch
        xb = x_ref[...].astype(BF16)
        for k in range(nk):
            bu = jnp.dot(xb[:, k * 128:(k + 1) * 128], bm_ref[d, k], preferred_element_type=F32)
            hre_ref[base:base + SSM_ROWS, k * kw:(k + 1) * kw] = bu[:, :kw]
            him_ref[base:base + SSM_ROWS, k * kw:(k + 1) * kw] = bu[:, kw:]
        hre_ref[srow:srow + batch, :] = st_ref[d, 0]
        him_ref[srow:srow + batch, :] = st_ref[d, 1]

        if batch % 8 == 0:
            def body(s, carry):
                t = (steps - 1 - s) if reverse else s
                cur = pl.ds(pl.multiple_of(base + t * batch, batch), batch)
                prev = pl.ds(pl.multiple_of(base + (t + 1) * batch if reverse else base + (t - 1) * batch, batch),
                             batch)
                step(d, prev, cur)
                return carry
            lax.fori_loop(0, steps, body, 0)
        else:
            per = 8 // batch

            def body(s, carry):
                g = (steps // per - 1 - s) if reverse else s
                slab = pl.multiple_of(base + g * 8, 8)
                l_re, l_im = lam_ref[d, 0], lam_ref[d, 1]
                cur_re, cur_im = hre_ref[pl.ds(slab, 8), :], him_ref[pl.ds(slab, 8), :]
                nb_slab = pl.multiple_of(slab + 8 if reverse else slab - 8, 8)
                nb_re, nb_im = hre_ref[pl.ds(nb_slab, 8), :], him_ref[pl.ds(nb_slab, 8), :]
                if reverse:
                    p_re, p_im = nb_re[0:batch], nb_im[0:batch]
                    order = range(per - 1, -1, -1)
                else:
                    p_re, p_im = nb_re[8 - batch:8], nb_im[8 - batch:8]
                    order = range(per)
                outs_re, outs_im = [None] * per, [None] * per
                for j in order:
                    b_re, b_im = cur_re[j * batch:(j + 1) * batch], cur_im[j * batch:(j + 1) * batch]
                    p_re, p_im = l_re * p_re - l_im * p_im + b_re, l_re * p_im + l_im * p_re + b_im
                    outs_re[j], outs_im[j] = p_re, p_im
                hre_ref[pl.ds(slab, 8), :] = jnp.concatenate(outs_re, axis=0)
                him_ref[pl.ds(slab, 8), :] = jnp.concatenate(outs_im, axis=0)
                return carry
            lax.fori_loop(0, steps // per, body, 0)

        erow = 0 if reverse else SSM_PAD + SSM_ROWS - batch
        st_ref[d, 0] = hre_ref[erow:erow + batch, :]
        st_ref[d, 1] = him_ref[erow:erow + batch, :]
        for k in range(nk):
            hk = jnp.concatenate([hre_ref[base:base + SSM_ROWS, k * kw:(k + 1) * kw],
                                  him_ref[base:base + SSM_ROWS, k * kw:(k + 1) * kw]], axis=-1).astype(BF16)
            y_ref[:, k * 128:(k + 1) * 128] = jnp.dot(hk, cm_ref[d, k], preferred_element_type=F32)

    run(0, xf_ref, yf_ref, False)
    run(1, xb_ref, yb_ref, True)

    @pl.when(i == pl.num_programs(0) - 1)
    def _():
        hfin_ref[...] = st_ref[...]


def ssm_scan(xn_tm, b_mat, c_mat, lam, h0, batch):
    rows = xn_tm.shape[0]
    n = rows // SSM_ROWS
    lam_b = jnp.broadcast_to(lam[:, :, None, :], (2, 2, batch, SSM_N))
    y_shape = jax.ShapeDtypeStruct((rows, D), F32)
    full = lambda a: pl.BlockSpec(a.shape, lambda i: (0,) * a.ndim)
    vmem = (2 * (SSM_ROWS + 2 * SSM_PAD) * SSM_N * 4 + 8 * SSM_ROWS * D * 4 + 2 * (b_mat.size + c_mat.size) * 2
            + 12 * batch * SSM_N * 4 * 2 + 8 * SSM_ROWS * 1024 * 4)
    return pl.pallas_call(
        functools.partial(_ssm_scan_kernel, batch=batch),
        out_shape=(y_shape, y_shape, jax.ShapeDtypeStruct((2, 2, batch, SSM_N), F32)),
        grid=(n,),
        in_specs=[pl.BlockSpec((SSM_ROWS, D), lambda i: (i, 0)),
                  pl.BlockSpec((SSM_ROWS, D), lambda i: (n - 1 - i, 0)),
                  full(b_mat), full(c_mat), full(lam_b), full(h0)],
        out_specs=(pl.BlockSpec((SSM_ROWS, D), lambda i: (i, 0)),
                   pl.BlockSpec((SSM_ROWS, D), lambda i: (n - 1 - i, 0)),
                   pl.BlockSpec((2, 2, batch, SSM_N), lambda i: (0, 0, 0, 0))),
        scratch_shapes=[pltpu.VMEM((SSM_ROWS + 2 * SSM_PAD, SSM_N), F32),
                        pltpu.VMEM((SSM_ROWS + 2 * SSM_PAD, SSM_N), F32),
                        pltpu.VMEM((2, 2, batch, SSM_N), F32)],
        compiler_params=_cparams(("arbitrary",), vmem),
        name="ssm_scan",
    )(xn_tm, xn_tm, b_mat, c_mat, lam_b, h0)


def _gelu_tanh(x):
    return 0.5 * x * (1.0 + jnp.tanh(math.sqrt(2.0 / math.pi) * (x + 0.044715 * (x * x * x))))


def _ssm_glu_kernel(yf_ref, yb_ref, xn_ref, d_ref, w_ref, x_ref, mod_ref, o_ref, s_ref,
                    *, batch, steps, mod_rows):
    g = _gelu_tanh(yf_ref[...] + yb_ref[...] + d_ref[...] * xn_ref[...])
    u = jnp.dot(g.astype(BF16), w_ref[...], preferred_element_type=F32)
    _slab_store(s_ref, slice(None), u[:, :D] * _sigmoid(u[:, D:]))
    for b in range(batch):
        gate = mod_ref[mod_rows[b], 2:3, :]
        _x4_seq(o_ref, b)[...] = (_x4_seq(x_ref, b)[...]
                                  + gate * _slab_load(s_ref, pl.ds(b, steps, stride=batch)))


def ssm_glu(yf, yb, xn, d, w_glu, w_idx, x4, layer, mods, prompt):
    batch, steps, blk, idx, tiles, mod_rows = _tm_geometry(prompt)
    tm_spec = pl.BlockSpec((SSM_ROWS, D), lambda i: (i, 0))
    out = pl.pallas_call(
        functools.partial(_ssm_glu_kernel, batch=batch, steps=steps, mod_rows=mod_rows),
        out_shape=jax.ShapeDtypeStruct((4,) + X4_SHAPE[1:], F32),
        grid=(tiles,),
        in_specs=[tm_spec, tm_spec, tm_spec,
                  pl.BlockSpec((None, 1, D), lambda i: (w_idx, 0, 0)),
                  pl.BlockSpec((None, D, 2 * D), lambda i: (w_idx, 0, 0), pipeline_mode=pl.Buffered(1)),
                  pl.BlockSpec(blk, idx),
                  pl.BlockSpec((None, MOD_ROWS, 6, D), lambda i: (layer, 0, 0, 0))],
        out_specs=pl.BlockSpec(blk, lambda i: (0,) + idx(i)[1:]),
        scratch_shapes=[pltpu.VMEM((LANE_SLABS, SSM_ROWS, 128), F32)],
        compiler_params=_cparams(("arbitrary",), 24 * SSM_ROWS * D * 4 + D * 2 * D * 2),
        name="ssm_glu",
    )(yf, yb, xn, d.reshape(-1, 1, D), w_glu, x4, mods)
    return out.reshape(-1, D)


def ssm_layer(x, layer, layer_j, mods, norm_g, a_re, a_im, log_dt, b_re, b_im, c_re, c_im, d, w_glu, state_ssm):
    l_re, l_im, bb_re, bb_im = ssm_discretize(a_re[layer_j], a_im[layer_j], log_dt[layer_j], b_re[layer_j],
                                              b_im[layer_j])
    b_mat, c_mat, lam = _ssm_block_diag(l_re, l_im, bb_re, bb_im, c_re[layer_j], c_im[layer_j])
    x4 = x.reshape(X4_SHAPE)
    xn_p = norm_mod_time_major(x4, layer, norm_g, mods, True)
    xn_s = norm_mod_time_major(x4, layer, norm_g, mods, False)
    h0_p = jnp.zeros((2, 2, BATCH, SSM_N), F32)
    h0_s = jnp.transpose(state_ssm[:, layer_j].reshape(DEC_BATCH, 2, SSM_N, 2), (1, 3, 0, 2))
    yfp, ybp, hfin = ssm_scan(xn_p, b_mat, c_mat, lam, h0_p, BATCH)
    yfs, ybs, _ = ssm_scan(xn_s, b_mat, c_mat, lam, h0_s, DEC_BATCH)
    out_p = ssm_glu(yfp, ybp, xn_p, d, w_glu, layer_j, x4, layer, mods, True)
    out_s = ssm_glu(yfs, ybs, xn_s, d, w_glu, layer_j, x4, layer, mods, False)
    new_state = jnp.transpose(hfin, (2, 0, 3, 1)).reshape(BATCH, 2, SSM_GROUPS, SSM_STATE, 2)
    return (out_p, out_s), new_state


def kernel(x_prompt, x_sample, cache_k, cache_v, state_hgrn, state_ssm, c, c_ctx, ada_w, ada_b, norm1_g, norm2_g, attn_wqkv, attn_wo, attn_sink, hgrn_w_in, hgrn_lb, hgrn_g_norm, hgrn_wo, ssm_a_re, ssm_a_im, ssm_log_dt, ssm_b_re, ssm_b_im, ssm_c_re, ssm_c_im, ssm_d, ssm_w_glu, ffn_w_up, ffn_conv_w, ffn_conv_b, ffn_w_down, final_g):
    cond8 = jnp.zeros((MOD_ROWS, D), F32).at[0].set(c_ctx).at[1:1 + DEC_BATCH].set(c)
    mods = ada_modulation(cond8, ada_w, ada_b)
    x = jnp.concatenate([x_prompt.reshape(ROWS_P, D), x_sample.reshape(ROWS_S, D)], axis=0)
    wqkv, wo, w_in, hwo, w_glu, w_up, w_down = (w.astype(BF16) for w in (
        attn_wqkv, attn_wo, hgrn_w_in, hgrn_wo, ssm_w_glu, ffn_w_up, ffn_w_down))
    new_k, new_v, new_hgrn, new_ssm = [], [], [], []
    for l in range(DEPTH):
        kind, j = l % N_MIXERS, l // N_MIXERS
        if kind == 0:
            x, k, v = attention_layer(x, l, j, mods, norm1_g, wqkv, wo, attn_sink[j], cache_k, cache_v)
            new_k.append(k)
            new_v.append(v)
        elif kind == 1:
            x, s = hgrn_layer(x, l, j, mods, norm1_g, w_in, hgrn_lb, hgrn_g_norm, hwo, state_hgrn)
            new_hgrn.append(s)
        else:
            x, s = ssm_layer(x, l, j, mods, norm1_g, ssm_a_re, ssm_a_im, ssm_log_dt, ssm_b_re, ssm_b_im,
                             ssm_c_re, ssm_c_im, ssm_d, w_glu, state_ssm)
            new_ssm.append(s)
        x = conv_ffn_residual(x, l, norm2_g, mods, w_up, ffn_conv_w, ffn_conv_b, w_down)
    y_prompt = final_norm(x, final_g, 0, N_ROW_TILES_P).reshape(BATCH, SEQ, D)
    y_sample = final_norm(x, final_g, N_ROW_TILES_P, N_ROW_TILES - N_ROW_TILES_P).reshape(DEC_BATCH, DEC_SEQ, D)
    return (y_prompt, y_sample, jnp.stack(new_k, axis=1), jnp.stack(new_v, axis=1),
            jnp.stack(new_hgrn, axis=1), jnp.stack(new_ssm, axis=1))
```

```python
import functools
import math

import jax
import jax.numpy as jnp
import numpy as np
from jax import lax
from jax.experimental import pallas as pl
from jax.experimental.pallas import tpu as pltpu

F32 = jnp.float32
BF16 = jnp.bfloat16

D = 1024
BATCH = 16
SEQ = 256
DEPTH = 4
DEC_BATCH = 4
DEC_SEQ = 1024
PAST_LEN = 512
GRID_W = 64
N_MIXERS = 3
ATTN_HEADS = 16
ATTN_KV_HEADS = 4
ATTN_GROUP = ATTN_HEADS // ATTN_KV_HEADS
HEAD_DIM = D // ATTN_HEADS
WINDOW = 128
ROPE_BASE = 10000.0
HGRN_HEADS = 8
HGRN_DK = 128
HGRN_DV = 128
SSM_GROUP = 16
SSM_GROUPS = D // SSM_GROUP
SSM_STATE = 64
D_FF = 2816
NORM_EPS = 1e-6

ROWS_P = BATCH * SEQ
ROWS_S = DEC_BATCH * DEC_SEQ
ROWS = ROWS_P + ROWS_S
ROW_TILE = 1024
N_ROW_TILES = ROWS // ROW_TILE
N_ROW_TILES_P = ROWS_P // ROW_TILE
MOD_ROWS = 8
V7X_VMEM_BYTES = 64 * 1024 * 1024


def _mod_row(i, tile=ROW_TILE):
    return jnp.where(i < ROWS_P // tile, 0, (i - ROWS_P // tile) // (DEC_SEQ // tile) + 1)


def _cparams(semantics, vmem_bytes):
    vmem = int(min(max(vmem_bytes * 5 // 4 + (4 << 20), 16 << 20), V7X_VMEM_BYTES - (6 << 20)))
    return pltpu.CompilerParams(dimension_semantics=semantics, vmem_limit_bytes=vmem)


def _bdot(a, b):
    return jnp.dot(a.astype(BF16), b.astype(BF16), preferred_element_type=F32)


def _norm_mod(x, g, shift, scale):
    y = x * lax.rsqrt(jnp.mean(x * x, axis=-1, keepdims=True) + NORM_EPS) * g
    return y * (1.0 + scale) + shift


def _sigmoid(x):
    return 0.5 + 0.5 * jnp.tanh(0.5 * x)


def _silu(x):
    h = 0.5 * x
    return h + h * jnp.tanh(h)


def _ada_kernel(c_ref, w_ref, b_ref, o_ref):
    c = c_ref[...]
    o_ref[...] = _bdot(_silu(c), w_ref[...]) + b_ref[...]


def ada_modulation(cond8, ada_w, ada_b):
    tn = 1024
    out = pl.pallas_call(
        _ada_kernel,
        out_shape=jax.ShapeDtypeStruct((DEPTH, MOD_ROWS, 6 * D), F32),
        grid=(DEPTH, 6 * D // tn),
        in_specs=[
            pl.BlockSpec((MOD_ROWS, D), lambda l, j: (0, 0)),
            pl.BlockSpec((None, D, tn), lambda l, j: (l, 0, j)),
            pl.BlockSpec((None, 1, tn), lambda l, j: (l, 0, j)),
        ],
        out_specs=pl.BlockSpec((None, MOD_ROWS, tn), lambda l, j: (l, 0, j)),
        compiler_params=_cparams(("arbitrary", "arbitrary"), 2 * D * tn * 4),
        name="ada_modulation",
    )(cond8, ada_w, ada_b.reshape(DEPTH, 1, 6 * D))
    return out.reshape(DEPTH, MOD_ROWS, 6, D)


def _nmm_kernel(x_ref, g_ref, mod_ref, w_ref, o_ref, h_ref):
    @pl.when(pl.program_id(1) == 0)
    def _():
        h_ref[...] = _norm_mod(x_ref[...], g_ref[...], mod_ref[0:1, :], mod_ref[1:2, :]).astype(BF16)

    o_ref[...] = jnp.dot(h_ref[...], w_ref[...], preferred_element_type=F32)


def _mod_spec(layer, n_grid, tile=ROW_TILE):
    if n_grid == 1:
        return pl.BlockSpec((None, None, 6, D), lambda i: (layer, _mod_row(i, tile), 0, 0))
    return pl.BlockSpec((None, None, 6, D), lambda i, j: (layer, _mod_row(i, tile), 0, 0))


def norm_mod_matmul(x, layer, norm_g, mods, w, w_idx, name):
    n = w.shape[-1]
    tn = 1024 if n % 1024 == 0 else 768
    return pl.pallas_call(
        _nmm_kernel,
        out_shape=jax.ShapeDtypeStruct((ROWS, n), F32),
        grid=(N_ROW_TILES, n // tn),
        in_specs=[
            pl.BlockSpec((ROW_TILE, D), lambda i, j: (i, 0)),
            pl.BlockSpec((None, 1, D), lambda i, j: (layer, 0, 0)),
            _mod_spec(layer, 2),
            pl.BlockSpec((None, D, tn), lambda i, j: (w_idx, 0, j)),
        ],
        out_specs=pl.BlockSpec((ROW_TILE, tn), lambda i, j: (i, j)),
        scratch_shapes=[pltpu.VMEM((ROW_TILE, D), BF16)],
        compiler_params=_cparams(("arbitrary", "arbitrary"),
                                 2 * ROW_TILE * D * 4 + ROW_TILE * D * 2 + 2 * D * tn * 2 + 2 * ROW_TILE * tn * 4),
        name=name,
    )(x, norm_g.reshape(DEPTH, 1, D), mods, w)


def _mm_res_kernel(ap_ref, as_ref, w_ref, x_ref, mod_ref, o_ref):
    def run(a_ref):
        y = jnp.dot(a_ref[...].astype(BF16), w_ref[...], preferred_element_type=F32)
        o_ref[...] = x_ref[...] + mod_ref[2:3, :] * y

    is_prompt = pl.program_id(0) < N_ROW_TILES_P
    pl.when(is_prompt)(lambda: run(ap_ref))
    pl.when(jnp.logical_not(is_prompt))(lambda: run(as_ref))


def _group_specs(k, tile=ROW_TILE):
    n_p = ROWS_P // tile
    return (pl.BlockSpec((tile, k), lambda i: (jnp.minimum(i, n_p - 1), 0)),
            pl.BlockSpec((tile, k), lambda i: (jnp.maximum(i - n_p, 0), 0)))


def matmul_gated_residual(a_p, a_s, w, w_idx, x, layer, mods, name):
    k = a_p.shape[1]
    return pl.pallas_call(
        _mm_res_kernel,
        out_shape=jax.ShapeDtypeStruct((ROWS, D), F32),
        grid=(N_ROW_TILES,),
        in_specs=[
            *_group_specs(k),
            pl.BlockSpec((None, k, D), lambda i: (w_idx, 0, 0), pipeline_mode=pl.Buffered(1)),
            pl.BlockSpec((ROW_TILE, D), lambda i: (i, 0)),
            _mod_spec(layer, 1),
        ],
        out_specs=pl.BlockSpec((ROW_TILE, D), lambda i: (i, 0)),
        compiler_params=_cparams(("arbitrary",), 4 * ROW_TILE * k * 2 + k * D * 2 + 5 * ROW_TILE * D * 4),
        name=name,
    )(a_p, a_s, w, x, mods)


FFN_CHUNK = 256
FFN_CHUNKS = D_FF // FFN_CHUNK
CONV_PAD = 8


def _ffn_kernel(*refs, split_x):
    x_refs, (g_ref, mod_ref, wup_ref, cw_ref, cb_ref, wd_ref, o_ref, h_ref, pad_a, pad_b, act_ref) = (
        refs[:1 + split_x], refs[1 + split_x:])
    i = pl.program_id(0)

    def read_x():
        if not split_x:
            return x_refs[0][...]
        return jnp.where(i < N_ROW_TILES_P, x_refs[0][...], x_refs[1][...])

    h_ref[...] = _norm_mod(read_x(), g_ref[...], mod_ref[3:4, :], mod_ref[4:5, :]).astype(BF16)
    zeros = jnp.zeros((CONV_PAD, 2 * FFN_CHUNK), F32)
    for pad_ref in (pad_a, pad_b):
        pad_ref[0:CONV_PAD, :] = zeros
        pad_ref[CONV_PAD + ROW_TILE:, :] = zeros
    sub = lax.broadcasted_iota(jnp.int32, (8, 1), 0)
    is_prompt = i < N_ROW_TILES_P
    keep_first = jnp.where((sub == 0) & is_prompt, 0.0, 1.0)
    keep_last = jnp.where((sub == 7) & is_prompt, 0.0, 1.0)

    def cut_sequences(v, keep, row):
        parts, at = [], 0
        for b in range(SEQ, ROW_TILE, SEQ):
            lo = b if row == 0 else b - 8
            parts += [v[at:lo], v[lo:lo + 8] * keep]
            at = lo + 8
        return jnp.concatenate(parts + [v[at:]], axis=0)

    def cols(ref, c):
        off = pl.multiple_of(c * FFN_CHUNK, FFN_CHUNK)
        return ref[:, pl.ds(off, FFN_CHUNK)], ref[:, pl.ds(D_FF + off, FFN_CHUNK)]

    def up_proj(c, pad_ref):
        hb = h_ref[...]
        wg, wv = cols(wup_ref, c)
        pad_ref[CONV_PAD:CONV_PAD + ROW_TILE, :FFN_CHUNK] = jnp.dot(hb, wg, preferred_element_type=F32)
        pad_ref[CONV_PAD:CONV_PAD + ROW_TILE, FFN_CHUNK:] = jnp.dot(hb, wv, preferred_element_type=F32)

    def conv_act(c, pad_ref):
        up = pad_ref[CONV_PAD:CONV_PAD + ROW_TILE, :]
        prev = cut_sequences(pad_ref[CONV_PAD - 1:CONV_PAD - 1 + ROW_TILE, :], keep_first, 0)
        nxt = cut_sequences(pad_ref[CONV_PAD + 1:CONV_PAD + 1 + ROW_TILE, :], keep_last, 7)
        cw = jnp.concatenate(cols(cw_ref, c), axis=-1)
        cb = jnp.concatenate(cols(cb_ref, c), axis=-1)
        conv = prev * cw[0:1, :] + up * cw[1:2, :] + nxt * cw[2:3, :] + cb
        gate = conv[:, :FFN_CHUNK]
        act = _silu(gate) * conv[:, FFN_CHUNK:]
        act_ref[:, pl.ds(pl.multiple_of(c * FFN_CHUNK, FFN_CHUNK), FFN_CHUNK)] = act.astype(BF16)

    up_proj(0, pad_a)

    def body(k, carry):
        c = 2 * k
        up_proj(c + 1, pad_b)
        conv_act(c, pad_a)
        up_proj(c + 2, pad_a)
        conv_act(c + 1, pad_b)
        return carry

    lax.fori_loop(0, (FFN_CHUNKS - 1) // 2, body, 0)
    conv_act(FFN_CHUNKS - 1, pad_a)
    y = jnp.dot(act_ref[...], wd_ref[...], preferred_element_type=F32)
    o_ref[...] = read_x() + mod_ref[5:6, :] * y


def conv_ffn_residual(x, layer, norm_g, mods, w_up, conv_w, conv_b, w_down):
    split_x = isinstance(x, tuple)
    x_ops = x if split_x else (x,)
    x_specs = list(_group_specs(D)) if split_x else [pl.BlockSpec((ROW_TILE, D), lambda i: (i, 0))]
    once = pl.Buffered(1)
    vmem = (4 * ROW_TILE * D * 4 + ROW_TILE * D * 2 + 2 * (ROW_TILE + 2 * CONV_PAD) * 2 * FFN_CHUNK * 4
            + ROW_TILE * D_FF * 2 + 3 * D * D_FF * 2 + 5 * ROW_TILE * 2 * FFN_CHUNK * 4)
    return pl.pallas_call(
        functools.partial(_ffn_kernel, split_x=split_x),
        out_shape=jax.ShapeDtypeStruct((ROWS, D), F32),
        grid=(N_ROW_TILES,),
        in_specs=x_specs + [
            pl.BlockSpec((None, 1, D), lambda i: (layer, 0, 0)),
            pl.BlockSpec((None, None, 6, D), lambda i: (layer, _mod_row(i), 0, 0)),
            pl.BlockSpec((None, D, 2 * D_FF), lambda i: (layer, 0, 0), pipeline_mode=once),
            pl.BlockSpec((None, 3, 2 * D_FF), lambda i: (layer, 0, 0), pipeline_mode=once),
            pl.BlockSpec((None, 1, 2 * D_FF), lambda i: (layer, 0, 0), pipeline_mode=once),
            pl.BlockSpec((None, D_FF, D), lambda i: (layer, 0, 0), pipeline_mode=once),
        ],
        out_specs=pl.BlockSpec((ROW_TILE, D), lambda i: (i, 0)),
        scratch_shapes=[pltpu.VMEM((ROW_TILE, D), BF16),
                        pltpu.VMEM((ROW_TILE + 2 * CONV_PAD, 2 * FFN_CHUNK), F32),
                        pltpu.VMEM((ROW_TILE + 2 * CONV_PAD, 2 * FFN_CHUNK), F32),
                        pltpu.VMEM((ROW_TILE, D_FF), BF16)],
        compiler_params=_cparams(("arbitrary",), vmem),
        name="conv_ffn",
    )(*x_ops, norm_g.reshape(DEPTH, 1, D), mods, w_up, conv_w, conv_b.reshape(DEPTH, 1, 2 * D_FF), w_down)


def _final_norm_kernel(x_ref, g_ref, o_ref):
    x = x_ref[...]
    o_ref[...] = x * lax.rsqrt(jnp.mean(x * x, axis=-1, keepdims=True) + NORM_EPS) * g_ref[...]


def final_norm(x, g, first_tile, n_tiles):
    return pl.pallas_call(
        _final_norm_kernel,
        out_shape=jax.ShapeDtypeStruct((n_tiles * ROW_TILE, D), F32),
        grid=(n_tiles,),
        in_specs=[pl.BlockSpec((ROW_TILE, D), lambda i: (first_tile + i, 0)),
                  pl.BlockSpec((1, D), lambda i: (0, 0))],
        out_specs=pl.BlockSpec((ROW_TILE, D), lambda i: (i, 0)),
        compiler_params=_cparams(("arbitrary",), 4 * ROW_TILE * D * 4),
        name="final_norm",
    )(x, g.reshape(1, D))


NQ = ATTN_HEADS * HEAD_DIM
NKV = ATTN_KV_HEADS * HEAD_DIM
Q_BLOCK = 128
MASKED = -1e30
LOG2E = 1.0 / math.log(2.0)
Q_PRESCALE = HEAD_DIM ** -0.5 * LOG2E


def _dot_t(a, b):
    return lax.dot_general(a.astype(BF16), b.astype(BF16), (((1,), (1,)), ((), ())),
                           preferred_element_type=F32)


def _group_rows(q):
    return jnp.concatenate([q[:, g * HEAD_DIM:(g + 1) * HEAD_DIM] for g in range(ATTN_GROUP)], axis=0)


def _sink_lanes(sink_ref, h, rows):
    return LOG2E * jnp.concatenate(
        [jnp.broadcast_to(sink_ref[0:1, ATTN_GROUP * h + g:ATTN_GROUP * h + g + 1], (1, rows))
         for g in range(ATTN_GROUP)], axis=-1)


ONES_ROWS = 16


def _values_t(v):
    ones = jnp.ones((ONES_ROWS, v.shape[0]), F32)
    out = []
    for c in range(NKV // 128):
        vt = v[:, c * 128:(c + 1) * 128].T
        out += [jnp.concatenate([vt[j * HEAD_DIM:(j + 1) * HEAD_DIM], ones], axis=0).astype(BF16)
                for j in range(128 // HEAD_DIM)]
    return out


def _softmax_pv(q4, key_sets, sink2):
    scores = []
    for k, _, bias in key_sets:
        s = _dot_t(k, q4)
        if bias is not None:
            s = jnp.concatenate([s[c * 128:(c + 1) * 128] if bc is None else s[c * 128:(c + 1) * 128] + bc
                                 for c, bc in enumerate(bias)], axis=0)
        scores.append(s)
    m = sink2
    for s in scores:
        m = jnp.maximum(m, jnp.max(s, axis=0, keepdims=True))
    acc = None
    for s, (_, v1t, _) in zip(scores, key_sets):
        t = jnp.dot(v1t, jnp.exp2(s - m).astype(BF16), preferred_element_type=F32)
        acc = t if acc is None else acc + t
    denom = acc[HEAD_DIM:HEAD_DIM + 1] + jnp.exp2(sink2 - m)
    return acc[:HEAD_DIM] * (1.0 / denom)


def _heads_to_columns(o_t, rows):
    slabs = []
    for g in range(0, ATTN_GROUP, 128 // HEAD_DIM):
        pair = jnp.concatenate([o_t[:, (g + j) * rows:(g + j + 1) * rows] for j in range(128 // HEAD_DIM)], axis=0)
        slabs.append(pair.T)
    return jnp.concatenate(slabs, axis=-1)


def _ctx_attn_kernel(qkv_ref, sink_ref, o_ref):
    v1t = _values_t(qkv_ref[:, NQ + NKV:])
    outs = []
    for h in range(ATTN_KV_HEADS):
        k = qkv_ref[:, NQ + h * HEAD_DIM:NQ + (h + 1) * HEAD_DIM].astype(BF16)
        q4 = _group_rows(qkv_ref[:, ATTN_GROUP * h * HEAD_DIM:ATTN_GROUP * (h + 1) * HEAD_DIM] * Q_PRESCALE)
        o_t = _softmax_pv(q4.astype(BF16), [(k, v1t[h], None)], _sink_lanes(sink_ref, h, SEQ))
        outs.append(_heads_to_columns(o_t, SEQ))
    o_ref[...] = jnp.concatenate(outs, axis=-1).astype(BF16)


def context_attention(qkv, sink):
    return pl.pallas_call(
        _ctx_attn_kernel,
        out_shape=jax.ShapeDtypeStruct((ROWS_P, NQ), BF16),
        grid=(BATCH,),
        in_specs=[pl.BlockSpec((SEQ, NQ + 2 * NKV), lambda b: (b, 0)),
                  pl.BlockSpec((1, ATTN_HEADS), lambda b: (0, 0))],
        out_specs=pl.BlockSpec((SEQ, NQ), lambda b: (b, 0)),
        compiler_params=_cparams(("arbitrary",), 2 * SEQ * (2 * NQ + 2 * NKV) * 4 + 24 * SEQ * ATTN_GROUP * SEQ * 4),
        name="context_attention",
    )(qkv, sink.reshape(1, ATTN_HEADS))


def _rope(x, cos, sin_a, sin_b):
    outs = []
    for c in range(x.shape[1] // 128):
        s = x[:, c * 128:(c + 1) * 128]
        outs.append(s * cos + pltpu.roll(s, 128 - HEAD_DIM // 4, 1) * sin_a + pltpu.roll(s, HEAD_DIM // 4, 1) * sin_b)
    return jnp.concatenate(outs, axis=-1)


def _lat_attn_kernel(q_ref, kp_ref, kc_ref, kn_ref, vp_ref, vc_ref, vn_ref, ck_ref, cv_ref,
                     cos_ref, sa_ref, sb_ref, sink_ref, o_ref):
    n = pl.program_id(1)
    nb = pl.num_programs(1)

    def tables(blk):
        r = pl.ds(pl.multiple_of(blk * Q_BLOCK, Q_BLOCK), Q_BLOCK)
        return cos_ref[r, :], sa_ref[r, :], sb_ref[r, :]

    cos, sa, sb = tables(n)
    qr = (_rope(q_ref[...], cos, sa, sb) * Q_PRESCALE).astype(BF16)
    k3 = jnp.concatenate([
        _rope(kp_ref[...], *tables(jnp.maximum(n - 1, 0))),
        _rope(kc_ref[...], cos, sa, sb),
        _rope(kn_ref[...], *tables(jnp.minimum(n + 1, nb - 1)))], axis=0).astype(BF16)
    v3 = jnp.concatenate([vp_ref[...], vc_ref[...], vn_ref[...]], axis=0)

    cols = ATTN_GROUP * Q_BLOCK
    koff = lax.broadcasted_iota(jnp.int32, (Q_BLOCK, cols), 0)
    qoff = lax.broadcasted_iota(jnp.int32, (Q_BLOCK, cols), 1) & (Q_BLOCK - 1)
    bias_prev = jnp.where((koff >= qoff) & (n > 0), 0.0, MASKED)
    bias_next = jnp.where((koff <= qoff) & (n < nb - 1), 0.0, MASKED)

    v1t = _values_t(v3)
    cv1t = _values_t(cv_ref[...])
    outs = []
    for h in range(ATTN_KV_HEADS):
        hs = slice(h * HEAD_DIM, (h + 1) * HEAD_DIM)
        q4 = _group_rows(qr[:, ATTN_GROUP * h * HEAD_DIM:ATTN_GROUP * (h + 1) * HEAD_DIM])
        o_t = _softmax_pv(q4, [(k3[:, hs], v1t[h], [bias_prev, None, bias_next]),
                               (ck_ref[:, hs].astype(BF16), cv1t[h], None)],
                          _sink_lanes(sink_ref, h, Q_BLOCK))
        outs.append(_heads_to_columns(o_t, Q_BLOCK))
    o_ref[...] = jnp.concatenate(outs, axis=-1).astype(BF16)


def _rope_tables():
    t = np.arange(DEC_SEQ)
    half = HEAD_DIM // 2
    inv_freq = 1.0 / (ROPE_BASE ** (np.arange(0, half, 2, dtype=np.float32) / half))
    ar = (t // GRID_W).astype(np.float32)[:, None] * inv_freq
    ac = (t % GRID_W).astype(np.float32)[:, None] * inv_freq
    return jnp.concatenate([jnp.asarray(a) for a in (ar, ar, ac, ac)] * 2, axis=-1)


def latent_attention(qkv, cache_k, cache_v, layer_j, sink):
    ang = _rope_tables()
    cos, sin = jnp.cos(ang), jnp.sin(ang)
    first = (lax.broadcasted_iota(jnp.int32, ang.shape, 1) % (HEAD_DIM // 2)) < HEAD_DIM // 4
    sin_a = jnp.where(first, -sin, 0.0)
    sin_b = jnp.where(first, 0.0, sin)
    nb = DEC_SEQ // Q_BLOCK
    base = ROWS_P // Q_BLOCK
    kcol, vcol = NQ // NKV, NQ // NKV + 1
    ck = cache_k.reshape(DEC_BATCH, -1, PAST_LEN, NKV)
    cv = cache_v.reshape(DEC_BATCH, -1, PAST_LEN, NKV)

    def kv_spec(col, off):
        return pl.BlockSpec((Q_BLOCK, NKV),
                            lambda b, n: (base + b * nb + jnp.clip(n + off, 0, nb - 1), col))

    table = pl.BlockSpec((DEC_SEQ, 128), lambda b, n: (0, 0))
    return pl.pallas_call(
        _lat_attn_kernel,
        out_shape=jax.ShapeDtypeStruct((ROWS_S, NQ), BF16),
        grid=(DEC_BATCH, nb),
        in_specs=[pl.BlockSpec((Q_BLOCK, NQ), lambda b, n: (base + b * nb + n, 0)),
                  kv_spec(kcol, -1), kv_spec(kcol, 0), kv_spec(kcol, 1),
                  kv_spec(vcol, -1), kv_spec(vcol, 0), kv_spec(vcol, 1),
                  pl.BlockSpec((None, None, PAST_LEN, NKV), lambda b, n: (b, layer_j, 0, 0)),
                  pl.BlockSpec((None, None, PAST_LEN, NKV), lambda b, n: (b, layer_j, 0, 0)),
                  table, table, table,
                  pl.BlockSpec((1, ATTN_HEADS), lambda b, n: (0, 0))],
        out_specs=pl.BlockSpec((Q_BLOCK, NQ), lambda b, n: (b * nb + n, 0)),
        compiler_params=_cparams(("arbitrary", "arbitrary"),
                                 4 * Q_BLOCK * NQ * 4 + 12 * Q_BLOCK * NKV * 4 + 4 * PAST_LEN * NKV * 4
                                 + 6 * DEC_SEQ * 128 * 4 + 24 * ATTN_GROUP * Q_BLOCK * (3 * Q_BLOCK + PAST_LEN) * 4),
        name="latent_attention",
    )(qkv, qkv, qkv, qkv, qkv, qkv, qkv, ck, cv, cos, sin_a, sin_b, sink.reshape(1, ATTN_HEADS))


def attention_layer(x, layer, layer_j, mods, norm_g, wqkv, wo, sink, cache_k, cache_v):
    qkv = norm_mod_matmul(x, layer, norm_g, mods, wqkv, layer_j, "attn_qkv")
    a_p = context_attention(qkv, sink)
    a_s = latent_attention(qkv, cache_k, cache_v, layer_j, sink)
    x = matmul_gated_residual(a_p, a_s, wo, layer_j, x, layer, mods, "attn_wo")
    new_k = qkv[:ROWS_P, NQ:NQ + NKV].reshape(BATCH, SEQ, ATTN_KV_HEADS, HEAD_DIM)
    new_v = qkv[:ROWS_P, NQ + NKV:].reshape(BATCH, SEQ, ATTN_KV_HEADS, HEAD_DIM)
    return x, new_k, new_v


HG_TILE = 256
HG_LEVELS = 8
HG_IN = 3 * 1024 + 2 * 1024
HG_OUT_TILE = 512
HG_HEADS_PER_STEP = 2


def _hgrn_consts():
    t = np.arange(HG_TILE)
    x = t[:, None] ^ t[None, :]
    hb = np.where(x == 0, -1, np.floor(np.log2(np.maximum(x, 1))).astype(np.int64))
    later = t[:, None] > t[None, :]
    half = HG_TILE // 2
    masks, tris = [], []
    for reverse in (False, True):
        side = ~later & (x != 0) if reverse else later
        lv = [hb == -1] + [(hb == lvl) & side for lvl in range(HG_LEVELS - 1)]
        masks.append(np.stack([m[:half, :half] for m in lv]).astype(np.float32))
        tris.append((t[None, :] >= t[:, None]) if reverse else (t[None, :] <= t[:, None]))
    return jnp.asarray(np.stack(masks)), jnp.asarray(np.stack(tris).astype(np.float32), dtype=BF16)


def _split_bf16(x):
    def top(v):
        bits = lax.bitcast_convert_type(v, jnp.uint32) & jnp.uint32(0xFFFF0000)
        return lax.bitcast_convert_type(bits, F32)

    hi = top(x)
    r = x - hi
    mid = top(r)
    return hi.astype(BF16), mid.astype(BF16), (r - mid).astype(BF16)


def _block_row(x, blk, idx):
    t = x.shape[0]
    x3 = x.reshape(t // blk, blk, x.shape[1])
    return jnp.broadcast_to(x3[:, idx:idx + 1, :], x3.shape).reshape(x.shape)


def _lower_bound(lb_ref, layer, direction):
    x = lb_ref[direction]
    e = jnp.exp(x - jnp.max(x, axis=0, keepdims=True))
    p = e / jnp.sum(e, axis=0, keepdims=True)
    return jnp.sum(p[1:layer + 1, :], axis=0, keepdims=True)


def _hgrn_tile(q, v, z, lb, s_in, mask_ref, tri, reverse):
    t = HG_TILE
    lo, hi = slice(0, t // 2), slice(t // 2, t)
    sg = _sigmoid(z)
    f = lb + (1.0 - lb) * sg
    k = (1.0 - lb) * (1.0 - sg)
    lf3 = _split_bf16(jnp.log(f))
    cum = sum(jnp.dot(tri, p, preferred_element_type=F32) for p in lf3)
    cum2 = cum * LOG2E
    rows = lax.broadcasted_iota(jnp.int32, (t, 1), 0)
    att = [mask_ref[0] * _dot_t(q[r], k[r]) for r in (lo, hi)]
    top = None
    for lvl in range(HG_LEVELS):
        half = 1 << lvl
        bit = (rows & half) != 0
        qside = ~bit if reverse else bit
        if lvl == 0:
            e = jnp.where(qside, f, 1.0)
        else:
            ref = _block_row(cum2, 2 * half, half if reverse else half - 1)
            e = jnp.exp2(-jnp.abs(cum2 - ref))
        w = (jnp.where(qside, q, k) * e).astype(BF16)
        if lvl < HG_LEVELS - 1:
            att = [a + mask_ref[lvl + 1] * _dot_t(w[r], w[r]) for a, r in zip(att, (lo, hi))]
        else:
            top = _dot_t(w[lo], w[hi]) if reverse else _dot_t(w[hi], w[lo])
    vb = v.astype(BF16)
    o_lo, o_hi = _bdot(att[0], vb[lo]), _bdot(att[1], vb[hi])
    if reverse:
        o_lo = o_lo + _bdot(top, vb[hi])
    else:
        o_hi = o_hi + _bdot(top, vb[lo])
    o = jnp.concatenate([o_lo, o_hi], axis=0)
    last = cum[0:1, :] if reverse else cum[t - 1:t, :]
    kd = (k * jnp.exp(last - cum)).astype(BF16)
    s_out = lax.dot_general(kd, vb, (((0,), (0,)), ((), ())), preferred_element_type=F32)
    if s_in is not None:
        o = o + _bdot(q * jnp.exp(cum), s_in)
        ones = jnp.ones((t, HGRN_DV), BF16)
        last_col = sum(lax.dot_general(p, ones, (((0,), (0,)), ((), ())), preferred_element_type=F32) for p in lf3)
        s_out = jnp.exp(last_col) * s_in + s_out
    return o, s_out


def _hgrn_prompt_kernel(q_ref, v_ref, zf_ref, zb_ref, lb_ref, mask_ref, tri_ref, of_ref, ob_ref, sfin_ref, *, layer):
    for j in range(HG_HEADS_PER_STEP):
        hs = slice(j * HGRN_DK, (j + 1) * HGRN_DK)
        q, v = q_ref[:, hs], v_ref[:, hs]
        lb_f, lb_b = (_lower_bound(lb_ref.at[:, :, hs], layer, d) for d in range(2))
        of_ref[:, hs], sfin_ref[0, j] = _hgrn_tile(q, v, zf_ref[:, hs], lb_f, None, mask_ref.at[0], tri_ref[0], False)
        ob_ref[:, hs], sfin_ref[1, j] = _hgrn_tile(q, v, zb_ref[:, hs], lb_b, None, mask_ref.at[1], tri_ref[1], True)


def _hgrn_sample_kernel(qf_ref, vf_ref, zf_ref, qb_ref, vb_ref, zb_ref, lb_ref, s0_ref, mask_ref, tri_ref,
                        of_ref, ob_ref, state_ref, *, layer):
    tiles = DEC_SEQ // HG_TILE

    @pl.when(pl.program_id(1) % tiles == 0)
    def _():
        state_ref[...] = s0_ref[...]

    for j in range(HG_HEADS_PER_STEP):
        hs = slice(j * HGRN_DK, (j + 1) * HGRN_DK)
        lb_f, lb_b = (_lower_bound(lb_ref.at[:, :, hs], layer, d) for d in range(2))
        of_ref[:, hs], state_ref[0, j] = _hgrn_tile(qf_ref[:, hs], vf_ref[:, hs], zf_ref[:, hs], lb_f,
                                                    state_ref[0, j], mask_ref.at[0], tri_ref[0], False)
        ob_ref[:, hs], state_ref[1, j] = _hgrn_tile(qb_ref[:, hs], vb_ref[:, hs], zb_ref[:, hs], lb_b,
                                                    state_ref[1, j], mask_ref.at[1], tri_ref[1], True)


def hgrn_scan(proj, lb_raw, layer, state_hgrn, layer_j):
    masks, tris = _hgrn_consts()
    hps = HG_HEADS_PER_STEP
    hw = hps * HGRN_DK
    qc, vc, zfc, zbc = 0, 1024 // hw, 2048 // hw, 3072 // hw
    const_specs = [pl.BlockSpec((2, DEPTH, hw), lambda h, i: (0, 0, h))]
    mask_specs = [pl.BlockSpec(masks.shape, lambda h, i: (0, 0, 0, 0)),
                  pl.BlockSpec((2, HG_TILE, HG_TILE), lambda h, i: (0, 0, 0))]
    vmem = 4 * masks.size * 4 + 16 * HG_TILE * hw * 4 + 8 * hw * HGRN_DV * 4 + 24 * hps * HG_TILE * HG_TILE * 4

    def col(cb, row_fn):
        return pl.BlockSpec((HG_TILE, hw), lambda h, i: (row_fn(i), cb + h))

    o_shape = jax.ShapeDtypeStruct((ROWS_P, HGRN_HEADS * HGRN_DV), F32)
    same = lambda i: i
    of_p, ob_p, sfin = pl.pallas_call(
        functools.partial(_hgrn_prompt_kernel, layer=layer),
        out_shape=(o_shape, o_shape, jax.ShapeDtypeStruct((BATCH, 2, HGRN_HEADS, HGRN_DK, HGRN_DV), F32)),
        grid=(HGRN_HEADS // hps, BATCH),
        in_specs=[col(qc, same), col(vc, same), col(zfc, same), col(zbc, same)] + const_specs + mask_specs,
        out_specs=(col(0, same), col(0, same),
                   pl.BlockSpec((None, 2, hps, HGRN_DK, HGRN_DV), lambda h, i: (i, 0, h, 0, 0))),
        compiler_params=_cparams(("arbitrary", "arbitrary"), vmem),
        name="hgrn_scan_prompt",
    )(proj, proj, proj, proj, lb_raw, masks, tris)

    tiles = DEC_SEQ // HG_TILE
    base = ROWS_P // HG_TILE
    bwd = lambda i: (i // tiles) * tiles + (tiles - 1 - i % tiles)
    fwd_in = lambda i: base + i
    bwd_in = lambda i: base + bwd(i)
    o_shape = jax.ShapeDtypeStruct((ROWS_S, HGRN_HEADS * HGRN_DV), F32)
    of_s, ob_s = pl.pallas_call(
        functools.partial(_hgrn_sample_kernel, layer=layer),
        out_shape=(o_shape, o_shape),
        grid=(HGRN_HEADS // hps, DEC_BATCH * tiles),
        in_specs=[col(qc, fwd_in), col(vc, fwd_in), col(zfc, fwd_in), col(qc, bwd_in), col(vc, bwd_in),
                  col(zbc, bwd_in)]
        + const_specs
        + [pl.BlockSpec((None, None, 2, hps, HGRN_DK, HGRN_DV), lambda h, i: (i // tiles, layer_j, 0, h, 0, 0))]
        + mask_specs,
        out_specs=(col(0, same), col(0, bwd)),
        scratch_shapes=[pltpu.VMEM((2, hps, HGRN_DK, HGRN_DV), F32)],
        compiler_params=_cparams(("arbitrary", "arbitrary"), vmem),
        name="hgrn_scan_sample",
    )(proj, proj, proj, proj, proj, proj, lb_raw, state_hgrn, masks, tris)
    return (of_p, ob_p), (of_s, ob_s), sfin


def _hgrn_out_kernel(ofp_ref, obp_ref, ofs_ref, obs_ref, g_ref, gn_ref, w_ref, x_ref, mod_ref, o_ref):
    def run(of_ref, ob_ref):
        gn = gn_ref[...]
        parts = []
        for h in range(HGRN_HEADS):
            hs = slice(h * HGRN_DV, (h + 1) * HGRN_DV)
            o = of_ref[:, hs] + ob_ref[:, hs]
            g = g_ref[:, hs]
            o = o * lax.rsqrt(jnp.mean(o * o, axis=-1, keepdims=True) + NORM_EPS) * gn * _silu(g)
            parts.append(o.astype(BF16))
        y = jnp.dot(jnp.concatenate(parts, axis=-1), w_ref[...], preferred_element_type=F32)
        o_ref[...] = x_ref[...] + mod_ref[2:3, :] * y

    is_prompt = pl.program_id(0) < ROWS_P // HG_OUT_TILE
    pl.when(is_prompt)(lambda: run(ofp_ref, obp_ref))
    pl.when(jnp.logical_not(is_prompt))(lambda: run(ofs_ref, obs_ref))


def hgrn_out(o_p, o_s, proj, g_norm, wo, w_idx, x, layer, mods):
    gcol = 4096 // D
    tile = HG_OUT_TILE
    p_spec, s_spec = _group_specs(D, tile)
    return pl.pallas_call(
        _hgrn_out_kernel,
        out_shape=jax.ShapeDtypeStruct((ROWS, D), F32),
        grid=(ROWS // tile,),
        in_specs=[p_spec, p_spec, s_spec, s_spec,
                  pl.BlockSpec((tile, D), lambda i: (i, gcol)),
                  pl.BlockSpec((1, HGRN_DV), lambda i: (0, 0)),
                  pl.BlockSpec((None, D, D), lambda i: (w_idx, 0, 0), pipeline_mode=pl.Buffered(1)),
                  pl.BlockSpec((tile, D), lambda i: (i, 0)),
                  _mod_spec(layer, 1, tile)],
        out_specs=pl.BlockSpec((tile, D), lambda i: (i, 0)),
        compiler_params=_cparams(("arbitrary",), 22 * tile * D * 4),
        name="hgrn_out",
    )(*o_p, *o_s, proj, g_norm.reshape(1, HGRN_DV), wo, x, mods)


def hgrn_layer(x, layer, layer_j, mods, norm_g, w_in, hgrn_lb, g_norm, wo, state_hgrn):
    proj = norm_mod_matmul(x, layer, norm_g, mods, w_in, layer_j, "hgrn_in")
    o_p, o_s, sfin = hgrn_scan(proj, jnp.transpose(hgrn_lb, (1, 0, 2)), layer, state_hgrn, layer_j)
    x = hgrn_out(o_p, o_s, proj, g_norm[layer_j], wo, layer_j, x, layer, mods)
    return x, sfin


SSM_N = SSM_GROUPS * SSM_STATE
SSM_KT = 8
SSM_ROWS = 256
SSM_PAD = 16


def _ssm_prep_kernel(are_ref, aim_ref, ldt_ref, bre_ref, bim_ref, lre_ref, lim_ref, bbre_ref, bbim_ref):
    a_re = jnp.minimum(are_ref[...], -1e-4)
    a_im = aim_ref[...]
    dt = jnp.exp(ldt_ref[...])
    mag = jnp.exp(a_re * dt)
    l_re = mag * jnp.cos(a_im * dt)
    l_im = mag * jnp.sin(a_im * dt)
    lre_ref[...] = l_re
    lim_ref[...] = l_im
    den = a_re * a_re + a_im * a_im
    c_re = ((l_re - 1.0) * a_re + l_im * a_im) / den
    c_im = (l_im * a_re - (l_re - 1.0) * a_im) / den
    b_re, b_im = bre_ref[...], bim_ref[...]
    bbre_ref[...] = c_re[:, None, :] * b_re - c_im[:, None, :] * b_im
    bbim_ref[...] = c_re[:, None, :] * b_im + c_im[:, None, :] * b_re


def ssm_discretize(a_re, a_im, log_dt, b_re, b_im):
    g2 = 2 * SSM_GROUPS
    sh = jax.ShapeDtypeStruct((g2, SSM_STATE), F32)
    shb = jax.ShapeDtypeStruct((g2, SSM_GROUP, SSM_STATE), F32)
    bt = lambda b: jnp.transpose(b, (0, 1, 3, 2)).reshape(g2, SSM_GROUP, SSM_STATE)
    return pl.pallas_call(_ssm_prep_kernel, out_shape=(sh, sh, shb, shb), name="ssm_discretize")(
        a_re.reshape(g2, SSM_STATE), a_im.reshape(g2, SSM_STATE), log_dt.reshape(g2, 1), bt(b_re), bt(b_im))


def _ssm_block_diag(l_re, l_im, bb_re, bb_im, c_re, c_im):
    nk = SSM_GROUPS // SSM_KT
    eye = jnp.eye(SSM_KT, dtype=F32)

    def bmat(b):
        b = b.reshape(2, nk, SSM_KT, SSM_GROUP, SSM_STATE)
        return jnp.einsum('dkgip,gh->dkgihp', b, eye).reshape(2, nk, SSM_KT * SSM_GROUP, SSM_KT * SSM_STATE)

    def cmat(c):
        c = c.reshape(2, nk, SSM_KT, SSM_GROUP, SSM_STATE)
        return jnp.einsum('dkgip,gh->dkhpgi', c, eye).reshape(2, nk, SSM_KT * SSM_STATE, SSM_KT * SSM_GROUP)

    b_mat = jnp.concatenate([bmat(bb_re), bmat(bb_im)], axis=-1).astype(BF16)
    c_mat = jnp.concatenate([cmat(c_re), cmat(-c_im)], axis=-2).astype(BF16)
    lam = jnp.stack([l_re.reshape(2, SSM_N), l_im.reshape(2, SSM_N)], axis=1)
    return b_mat, c_mat, lam


X4_SHAPE = (ROWS // (4 * SEQ), 4, SEQ, D)


def _tm_geometry(prompt):
    if prompt:
        batch = BATCH
        steps = SSM_ROWS // batch
        return batch, steps, (4, 4, steps, D), (lambda i: (0, 0, i, 0)), SEQ // steps, (0,) * batch
    batch = DEC_BATCH
    steps = SSM_ROWS // batch
    per_q = SEQ // steps
    return (batch, steps, (4, 1, steps, D), (lambda i: (1, i // per_q, i % per_q, 0)), DEC_SEQ // steps,
            tuple(range(1, 1 + batch)))


def _x4_seq(ref, b):
    return ref.at[b // ref.shape[1], b % ref.shape[1]]


LANE_SLABS = D // 128


def _slab_store(s_ref, rows, val):
    for c in range(LANE_SLABS):
        s_ref[c, rows, :] = val[:, c * 128:(c + 1) * 128]


def _slab_load(s_ref, rows):
    return jnp.concatenate([s_ref[c, rows, :] for c in range(LANE_SLABS)], axis=-1)


def _normmod_tm_kernel(x_ref, g_ref, mod_ref, o_ref, s_ref, *, batch, steps, mod_rows):
    g = g_ref[...]
    for b in range(batch):
        m = mod_ref.at[mod_rows[b]]
        _slab_store(s_ref, slice(b * steps, (b + 1) * steps),
                    _norm_mod(_x4_seq(x_ref, b)[...], g, m[0:1, :], m[1:2, :]))
    for t in range(steps):
        o_ref[t * batch:(t + 1) * batch, :] = _slab_load(s_ref, pl.ds(t, batch, stride=steps))


def norm_mod_time_major(x4, layer, norm_g, mods, prompt):
    batch, steps, blk, idx, tiles, mod_rows = _tm_geometry(prompt)
    return pl.pallas_call(
        functools.partial(_normmod_tm_kernel, batch=batch, steps=steps, mod_rows=mod_rows),
        out_shape=jax.ShapeDtypeStruct((tiles * SSM_ROWS, D), F32),
        grid=(tiles,),
        in_specs=[pl.BlockSpec(blk, idx),
                  pl.BlockSpec((None, 1, D), lambda i: (layer, 0, 0)),
                  pl.BlockSpec((None, MOD_ROWS, 6, D), lambda i: (layer, 0, 0, 0))],
        out_specs=pl.BlockSpec((SSM_ROWS, D), lambda i: (i, 0)),
        scratch_shapes=[pltpu.VMEM((LANE_SLABS, SSM_ROWS, 128), F32)],
        compiler_params=_cparams(("arbitrary",), 8 * SSM_ROWS * D * 4),
        name="ssm_norm_time_major",
    )(x4, norm_g.reshape(DEPTH, 1, D), mods)


def _ssm_scan_kernel(xf_ref, xb_ref, bm_ref, cm_ref, lam_ref, h0_ref, yf_ref, yb_ref, hfin_ref,
                     hre_ref, him_ref, st_ref, *, batch):
    i = pl.program_id(0)
    steps = SSM_ROWS // batch
    nk = SSM_GROUPS // SSM_KT
    kw = SSM_KT * SSM_STATE

    @pl.when(i == 0)
    def _():
        st_ref[...] = h0_ref[...]

    def step(d, prev_rows, cur_rows):
        l_re, l_im = lam_ref[d, 0], lam_ref[d, 1]
        p_re, p_im = hre_ref[prev_rows, :], him_ref[prev_rows, :]
        hre_ref[cur_rows, :] = l_re * p_re - l_im * p_im + hre_ref[cur_rows, :]
        him_ref[cur_rows, :] = l_re * p_im + l_im * p_re + him_ref[cur_rows, :]

    def run(d, x_ref, y_ref, reverse):
        base = 0 if reverse else SSM_PAD
        srow = SSM_ROWS if reverse else SSM_PAD - batch
        xb = x_ref[...].astype(BF16)
        for k in range(nk):
            bu = jnp.dot(xb[:, k * 128:(k + 1) * 128], bm_ref[d, k], preferred_element_type=F32)
            hre_ref[base:base + SSM_ROWS, k * kw:(k + 1) * kw] = bu[:, :kw]
            him_ref[base:base + SSM_ROWS, k * kw:(k + 1) * kw] = bu[:, kw:]
        hre_ref[srow:srow + batch, :] = st_ref[d, 0]
        him_ref[srow:srow + batch, :] = st_ref[d, 1]

        if batch % 8 == 0:
            def body(s, carry):
                t = (steps - 1 - s) if reverse else s
                cur = pl.ds(pl.multiple_of(base + t * batch, batch), batch)
                prev = pl.ds(pl.multiple_of(base + (t + 1) * batch if reverse else base + (t - 1) * batch, batch),
                             batch)
                step(d, prev, cur)
                return carry
            lax.fori_loop(0, steps, body, 0)
        else:
            per = 8 // batch

            def body(s, carry):
                g = (steps // per - 1 - s) if reverse else s
                slab = pl.multiple_of(base + g * 8, 8)
                l_re, l_im = lam_ref[d, 0], lam_ref[d, 1]
                cur_re, cur_im = hre_ref[pl.ds(slab, 8), :], him_ref[pl.ds(slab, 8), :]
                nb_slab = pl.multiple_of(slab + 8 if reverse else slab - 8, 8)
                nb_re, nb_im = hre_ref[pl.ds(nb_slab, 8), :], him_ref[pl.ds(nb_slab, 8), :]
                if reverse:
                    p_re, p_im = nb_re[0:batch], nb_im[0:batch]
                    order = range(per - 1, -1, -1)
                else:
                    p_re, p_im = nb_re[8 - batch:8], nb_im[8 - batch:8]
                    order = range(per)
                outs_re, outs_im = [None] * per, [None] * per
                for j in order:
                    b_re, b_im = cur_re[j * batch:(j + 1) * batch], cur_im[j * batch:(j + 1) * batch]
                    p_re, p_im = l_re * p_re - l_im * p_im + b_re, l_re * p_im + l_im * p_re + b_im
                    outs_re[j], outs_im[j] = p_re, p_im
                hre_ref[pl.ds(slab, 8), :] = jnp.concatenate(outs_re, axis=0)
                him_ref[pl.ds(slab, 8), :] = jnp.concatenate(outs_im, axis=0)
                return carry
            lax.fori_loop(0, steps // per, body, 0)

        erow = 0 if reverse else SSM_PAD + SSM_ROWS - batch
        st_ref[d, 0] = hre_ref[erow:erow + batch, :]
        st_ref[d, 1] = him_ref[erow:erow + batch, :]
        for k in range(nk):
            hk = jnp.concatenate([hre_ref[base:base + SSM_ROWS, k * kw:(k + 1) * kw],
                                  him_ref[base:base + SSM_ROWS, k * kw:(k + 1) * kw]], axis=-1).astype(BF16)
            y_ref[:, k * 128:(k + 1) * 128] = jnp.dot(hk, cm_ref[d, k], preferred_element_type=F32)

    run(0, xf_ref, yf_ref, False)
    run(1, xb_ref, yb_ref, True)

    @pl.when(i == pl.num_programs(0) - 1)
    def _():
        hfin_ref[...] = st_ref[...]


def ssm_scan(xn_tm, b_mat, c_mat, lam, h0, batch):
    rows = xn_tm.shape[0]
    n = rows // SSM_ROWS
    lam_b = jnp.broadcast_to(lam[:, :, None, :], (2, 2, batch, SSM_N))
    y_shape = jax.ShapeDtypeStruct((rows, D), F32)
    full = lambda a: pl.BlockSpec(a.shape, lambda i: (0,) * a.ndim)
    vmem = (2 * (SSM_ROWS + 2 * SSM_PAD) * SSM_N * 4 + 8 * SSM_ROWS * D * 4 + 2 * (b_mat.size + c_mat.size) * 2
            + 12 * batch * SSM_N * 4 * 2 + 8 * SSM_ROWS * 1024 * 4)
    return pl.pallas_call(
        functools.partial(_ssm_scan_kernel, batch=batch),
        out_shape=(y_shape, y_shape, jax.ShapeDtypeStruct((2, 2, batch, SSM_N), F32)),
        grid=(n,),
        in_specs=[pl.BlockSpec((SSM_ROWS, D), lambda i: (i, 0)),
                  pl.BlockSpec((SSM_ROWS, D), lambda i: (n - 1 - i, 0)),
                  full(b_mat), full(c_mat), full(lam_b), full(h0)],
        out_specs=(pl.BlockSpec((SSM_ROWS, D), lambda i: (i, 0)),
                   pl.BlockSpec((SSM_ROWS, D), lambda i: (n - 1 - i, 0)),
                   pl.BlockSpec((2, 2, batch, SSM_N), lambda i: (0, 0, 0, 0))),
        scratch_shapes=[pltpu.VMEM((SSM_ROWS + 2 * SSM_PAD, SSM_N), F32),
                        pltpu.VMEM((SSM_ROWS + 2 * SSM_PAD, SSM_N), F32),
                        pltpu.VMEM((2, 2, batch, SSM_N), F32)],
        compiler_params=_cparams(("arbitrary",), vmem),
        name="ssm_scan",
    )(xn_tm, xn_tm, b_mat, c_mat, lam_b, h0)


def _gelu_tanh(x):
    return 0.5 * x * (1.0 + jnp.tanh(math.sqrt(2.0 / math.pi) * (x + 0.044715 * (x * x * x))))


def _ssm_glu_kernel(yf_ref, yb_ref, xn_ref, d_ref, w_ref, x_ref, mod_ref, o_ref, s_ref,
                    *, batch, steps, mod_rows):
    g = _gelu_tanh(yf_ref[...] + yb_ref[...] + d_ref[...] * xn_ref[...])
    u = jnp.dot(g.astype(BF16), w_ref[...], preferred_element_type=F32)
    _slab_store(s_ref, slice(None), u[:, :D] * _sigmoid(u[:, D:]))
    for b in range(batch):
        gate = mod_ref[mod_rows[b], 2:3, :]
        _x4_seq(o_ref, b)[...] = (_x4_seq(x_ref, b)[...]
                                  + gate * _slab_load(s_ref, pl.ds(b, steps, stride=batch)))


def ssm_glu(yf, yb, xn, d, w_glu, w_idx, x4, layer, mods, prompt):
    batch, steps, blk, idx, tiles, mod_rows = _tm_geometry(prompt)
    tm_spec = pl.BlockSpec((SSM_ROWS, D), lambda i: (i, 0))
    out = pl.pallas_call(
        functools.partial(_ssm_glu_kernel, batch=batch, steps=steps, mod_rows=mod_rows),
        out_shape=jax.ShapeDtypeStruct((4,) + X4_SHAPE[1:], F32),
        grid=(tiles,),
        in_specs=[tm_spec, tm_spec, tm_spec,
                  pl.BlockSpec((None, 1, D), lambda i: (w_idx, 0, 0)),
                  pl.BlockSpec((None, D, 2 * D), lambda i: (w_idx, 0, 0), pipeline_mode=pl.Buffered(1)),
                  pl.BlockSpec(blk, idx),
                  pl.BlockSpec((None, MOD_ROWS, 6, D), lambda i: (layer, 0, 0, 0))],
        out_specs=pl.BlockSpec(blk, lambda i: (0,) + idx(i)[1:]),
        scratch_shapes=[pltpu.VMEM((LANE_SLABS, SSM_ROWS, 128), F32)],
        compiler_params=_cparams(("arbitrary",), 24 * SSM_ROWS * D * 4 + D * 2 * D * 2),
        name="ssm_glu",
    )(yf, yb, xn, d.reshape(-1, 1, D), w_glu, x4, mods)
    return out.reshape(-1, D)


def ssm_layer(x, layer, layer_j, mods, norm_g, a_re, a_im, log_dt, b_re, b_im, c_re, c_im, d, w_glu, state_ssm):
    l_re, l_im, bb_re, bb_im = ssm_discretize(a_re[layer_j], a_im[layer_j], log_dt[layer_j], b_re[layer_j],
                                              b_im[layer_j])
    b_mat, c_mat, lam = _ssm_block_diag(l_re, l_im, bb_re, bb_im, c_re[layer_j], c_im[layer_j])
    x4 = x.reshape(X4_SHAPE)
    xn_p = norm_mod_time_major(x4, layer, norm_g, mods, True)
    xn_s = norm_mod_time_major(x4, layer, norm_g, mods, False)
    h0_p = jnp.zeros((2, 2, BATCH, SSM_N), F32)
    h0_s = jnp.transpose(state_ssm[:, layer_j].reshape(DEC_BATCH, 2, SSM_N, 2), (1, 3, 0, 2))
    yfp, ybp, hfin = ssm_scan(xn_p, b_mat, c_mat, lam, h0_p, BATCH)
    yfs, ybs, _ = ssm_scan(xn_s, b_mat, c_mat, lam, h0_s, DEC_BATCH)
    out_p = ssm_glu(yfp, ybp, xn_p, d, w_glu, layer_j, x4, layer, mods, True)
    out_s = ssm_glu(yfs, ybs, xn_s, d, w_glu, layer_j, x4, layer, mods, False)
    new_state = jnp.transpose(hfin, (2, 0, 3, 1)).reshape(BATCH, 2, SSM_GROUPS, SSM_STATE, 2)
    return (out_p, out_s), new_state


def kernel(x_prompt, x_sample, cache_k, cache_v, state_hgrn, state_ssm, c, c_ctx, ada_w, ada_b, norm1_g, norm2_g, attn_wqkv, attn_wo, attn_sink, hgrn_w_in, hgrn_lb, hgrn_g_norm, hgrn_wo, ssm_a_re, ssm_a_im, ssm_log_dt, ssm_b_re, ssm_b_im, ssm_c_re, ssm_c_im, ssm_d, ssm_w_glu, ffn_w_up, ffn_conv_w, ffn_conv_b, ffn_w_down, final_g):
    cond8 = jnp.zeros((MOD_ROWS, D), F32).at[0].set(c_ctx).at[1:1 + DEC_BATCH].set(c)
    mods = ada_modulation(cond8, ada_w, ada_b)
    x = jnp.concatenate([x_prompt.reshape(ROWS_P, D), x_sample.reshape(ROWS_S, D)], axis=0)
    wqkv, wo, w_in, hwo, w_glu, w_up, w_down = (w.astype(BF16) for w in (
        attn_wqkv, attn_wo, hgrn_w_in, hgrn_wo, ssm_w_glu, ffn_w_up, ffn_w_down))
    new_k, new_v, new_hgrn, new_ssm = [], [], [], []
    for l in range(DEPTH):
        kind, j = l % N_MIXERS, l // N_MIXERS
        if kind == 0:
            x, k, v = attention_layer(x, l, j, mods, norm1_g, wqkv, wo, attn_sink[j], cache_k, cache_v)
            new_k.append(k)
            new_v.append(v)
        elif kind == 1:
            x, s = hgrn_layer(x, l, j, mods, norm1_g, w_in, hgrn_lb, hgrn_g_norm, hwo, state_hgrn)
            new_hgrn.append(s)
        else:
            x, s = ssm_layer(x, l, j, mods, norm1_g, ssm_a_re, ssm_a_im, ssm_log_dt, ssm_b_re, ssm_b_im,
                             ssm_c_re, ssm_c_im, ssm_d, w_glu, state_ssm)
            new_ssm.append(s)
        x = conv_ffn_residual(x, l, norm2_g, mods, w_up, ffn_conv_w, ffn_conv_b, w_down)
    y_prompt = final_norm(x, final_g, 0, N_ROW_TILES_P).reshape(BATCH, SEQ, D)
    y_sample = final_norm(x, final_g, N_ROW_TILES_P, N_ROW_TILES - N_ROW_TILES_P).reshape(DEC_BATCH, DEC_SEQ, D)
    return (y_prompt, y_sample, jnp.stack(new_k, axis=1), jnp.stack(new_v, axis=1),
            jnp.stack(new_hgrn, axis=1), jnp.stack(new_ssm, axis=1))
```

```python
import functools
import math

import jax
import jax.numpy as jnp
import numpy as np
from jax import lax
from jax.experimental import pallas as pl
from jax.experimental.pallas import tpu as pltpu

F32 = jnp.float32
BF16 = jnp.bfloat16

D = 1024
BATCH = 16
SEQ = 256
DEPTH = 4
DEC_BATCH = 4
DEC_SEQ = 1024
PAST_LEN = 512
GRID_W = 64
N_MIXERS = 3
ATTN_HEADS = 16
ATTN_KV_HEADS = 4
ATTN_GROUP = ATTN_HEADS // ATTN_KV_HEADS
HEAD_DIM = D // ATTN_HEADS
WINDOW = 128
ROPE_BASE = 10000.0
HGRN_HEADS = 8
HGRN_DK = 128
HGRN_DV = 128
SSM_GROUP = 16
SSM_GROUPS = D // SSM_GROUP
SSM_STATE = 64
D_FF = 2816
NORM_EPS = 1e-6

ROWS_P = BATCH * SEQ
ROWS_S = DEC_BATCH * DEC_SEQ
ROWS = ROWS_P + ROWS_S
ROW_TILE = 1024
N_ROW_TILES = ROWS // ROW_TILE
N_ROW_TILES_P = ROWS_P // ROW_TILE
MOD_ROWS = 8
V7X_VMEM_BYTES = 64 * 1024 * 1024


def _mod_row(i, tile=ROW_TILE):
    return jnp.where(i < ROWS_P // tile, 0, (i - ROWS_P // tile) // (DEC_SEQ // tile) + 1)


def _cparams(semantics, vmem_bytes):
    vmem = int(min(max(vmem_bytes * 5 // 4 + (4 << 20), 16 << 20), V7X_VMEM_BYTES - (6 << 20)))
    return pltpu.CompilerParams(dimension_semantics=semantics, vmem_limit_bytes=vmem)


def _bdot(a, b):
    return jnp.dot(a.astype(BF16), b.astype(BF16), preferred_element_type=F32)


def _norm_mod(x, g, shift, scale):
    y = x * lax.rsqrt(jnp.mean(x * x, axis=-1, keepdims=True) + NORM_EPS) * g
    return y * (1.0 + scale) + shift


def _sigmoid(x):
    return 0.5 + 0.5 * jnp.tanh(0.5 * x)


def _silu(x):
    h = 0.5 * x
    return h + h * jnp.tanh(h)


def _ada_kernel(c_ref, w_ref, b_ref, o_ref):
    c = c_ref[...]
    o_ref[...] = _bdot(_silu(c), w_ref[...]) + b_ref[...]


def ada_modulation(cond8, ada_w, ada_b):
    tn = 1024
    out = pl.pallas_call(
        _ada_kernel,
        out_shape=jax.ShapeDtypeStruct((DEPTH, MOD_ROWS, 6 * D), F32),
        grid=(DEPTH, 6 * D // tn),
        in_specs=[
            pl.BlockSpec((MOD_ROWS, D), lambda l, j: (0, 0)),
            pl.BlockSpec((None, D, tn), lambda l, j: (l, 0, j)),
            pl.BlockSpec((None, 1, tn), lambda l, j: (l, 0, j)),
        ],
        out_specs=pl.BlockSpec((None, MOD_ROWS, tn), lambda l, j: (l, 0, j)),
        compiler_params=_cparams(("arbitrary", "arbitrary"), 2 * D * tn * 4),
        name="ada_modulation",
    )(cond8, ada_w, ada_b.reshape(DEPTH, 1, 6 * D))
    return out.reshape(DEPTH, MOD_ROWS, 6, D)


def _x_operands(x, n_grid=1):
    if isinstance(x, tuple):
        return x, list(_group_specs(D, ROW_TILE, n_grid))
    return (x,), [pl.BlockSpec((ROW_TILE, D), (lambda i: (i, 0)) if n_grid == 1 else (lambda i, j: (i, 0)))]


def _read_rows(x_refs):
    if len(x_refs) == 1:
        return x_refs[0][...]
    return jnp.where(pl.program_id(0) < N_ROW_TILES_P, x_refs[0][...], x_refs[1][...])


def _nmm_kernel(*refs):
    x_refs, (g_ref, mod_ref, w_ref, o_ref, h_ref) = refs[:-5], refs[-5:]

    @pl.when(pl.program_id(1) == 0)
    def _():
        h_ref[...] = _norm_mod(_read_rows(x_refs), g_ref[...], mod_ref[0:1, :], mod_ref[1:2, :]).astype(BF16)

    o_ref[...] = jnp.dot(h_ref[...], w_ref[...], preferred_element_type=F32)


def _mod_spec(layer, n_grid, tile=ROW_TILE):
    if n_grid == 1:
        return pl.BlockSpec((None, None, 6, D), lambda i: (layer, _mod_row(i, tile), 0, 0))
    return pl.BlockSpec((None, None, 6, D), lambda i, j: (layer, _mod_row(i, tile), 0, 0))


def norm_mod_matmul(x, layer, norm_g, mods, w, w_idx, name):
    n = w.shape[-1]
    tn = 1024 if n % 1024 == 0 else 768
    x_ops, x_specs = _x_operands(x, 2)
    return pl.pallas_call(
        _nmm_kernel,
        out_shape=jax.ShapeDtypeStruct((ROWS, n), F32),
        grid=(N_ROW_TILES, n // tn),
        in_specs=x_specs + [
            pl.BlockSpec((None, 1, D), lambda i, j: (layer, 0, 0)),
            _mod_spec(layer, 2),
            pl.BlockSpec((None, D, tn), lambda i, j: (w_idx, 0, j)),
        ],
        out_specs=pl.BlockSpec((ROW_TILE, tn), lambda i, j: (i, j)),
        scratch_shapes=[pltpu.VMEM((ROW_TILE, D), BF16)],
        compiler_params=_cparams(("arbitrary", "arbitrary"),
                                 4 * ROW_TILE * D * 4 + ROW_TILE * D * 2 + 2 * D * tn * 2 + 2 * ROW_TILE * tn * 4),
        name=name,
    )(*x_ops, norm_g.reshape(DEPTH, 1, D), mods, w)


def _mm_res_kernel(ap_ref, as_ref, w_ref, *refs):
    x_refs, (mod_ref, o_ref) = refs[:-2], refs[-2:]

    def run(a_ref, x_ref):
        y = jnp.dot(a_ref[...].astype(BF16), w_ref[...], preferred_element_type=F32)
        o_ref[...] = x_ref[...] + mod_ref[2:3, :] * y

    is_prompt = pl.program_id(0) < N_ROW_TILES_P
    pl.when(is_prompt)(lambda: run(ap_ref, x_refs[0]))
    pl.when(jnp.logical_not(is_prompt))(lambda: run(as_ref, x_refs[-1]))


def _group_specs(k, tile=ROW_TILE, n_grid=1):
    n_p = ROWS_P // tile
    if n_grid == 1:
        return (pl.BlockSpec((tile, k), lambda i: (jnp.minimum(i, n_p - 1), 0)),
                pl.BlockSpec((tile, k), lambda i: (jnp.maximum(i - n_p, 0), 0)))
    return (pl.BlockSpec((tile, k), lambda i, j: (jnp.minimum(i, n_p - 1), 0)),
            pl.BlockSpec((tile, k), lambda i, j: (jnp.maximum(i - n_p, 0), 0)))


def matmul_gated_residual(a_p, a_s, w, w_idx, x, layer, mods, name):
    k = a_p.shape[1]
    x_ops, x_specs = _x_operands(x)
    return pl.pallas_call(
        _mm_res_kernel,
        out_shape=jax.ShapeDtypeStruct((ROWS, D), F32),
        grid=(N_ROW_TILES,),
        in_specs=[
            *_group_specs(k),
            pl.BlockSpec((None, k, D), lambda i: (w_idx, 0, 0), pipeline_mode=pl.Buffered(1)),
            *x_specs,
            _mod_spec(layer, 1),
        ],
        out_specs=pl.BlockSpec((ROW_TILE, D), lambda i: (i, 0)),
        compiler_params=_cparams(("arbitrary",), 4 * ROW_TILE * k * 2 + k * D * 2 + 7 * ROW_TILE * D * 4),
        name=name,
    )(a_p, a_s, w, *x_ops, mods)


FFN_CHUNK = 256
FFN_CHUNKS = D_FF // FFN_CHUNK
CONV_PAD = 8


def _ffn_kernel(*refs, split_x):
    x_refs, (g_ref, mod_ref, wup_ref, cw_ref, cb_ref, wd_ref, o_ref, h_ref, pad_a, pad_b, act_ref) = (
        refs[:1 + split_x], refs[1 + split_x:])
    i = pl.program_id(0)
    h_ref[...] = _norm_mod(_read_rows(x_refs), g_ref[...], mod_ref[3:4, :], mod_ref[4:5, :]).astype(BF16)
    zeros = jnp.zeros((CONV_PAD, 2 * FFN_CHUNK), F32)
    for pad_ref in (pad_a, pad_b):
        pad_ref[0:CONV_PAD, :] = zeros
        pad_ref[CONV_PAD + ROW_TILE:, :] = zeros
    sub = lax.broadcasted_iota(jnp.int32, (8, 1), 0)
    is_prompt = i < N_ROW_TILES_P
    keep_first = jnp.where((sub == 0) & is_prompt, 0.0, 1.0)
    keep_last = jnp.where((sub == 7) & is_prompt, 0.0, 1.0)

    def cut_sequences(v, keep, row):
        parts, at = [], 0
        for b in range(SEQ, ROW_TILE, SEQ):
            lo = b if row == 0 else b - 8
            parts += [v[at:lo], v[lo:lo + 8] * keep]
            at = lo + 8
        return jnp.concatenate(parts + [v[at:]], axis=0)

    def cols(ref, c):
        off = pl.multiple_of(c * FFN_CHUNK, FFN_CHUNK)
        return ref[:, pl.ds(off, FFN_CHUNK)], ref[:, pl.ds(D_FF + off, FFN_CHUNK)]

    def up_proj(c, pad_ref):
        hb = h_ref[...]
        wg, wv = cols(wup_ref, c)
        pad_ref[CONV_PAD:CONV_PAD + ROW_TILE, :FFN_CHUNK] = jnp.dot(hb, wg, preferred_element_type=F32)
        pad_ref[CONV_PAD:CONV_PAD + ROW_TILE, FFN_CHUNK:] = jnp.dot(hb, wv, preferred_element_type=F32)

    def conv_act(c, pad_ref):
        up = pad_ref[CONV_PAD:CONV_PAD + ROW_TILE, :]
        prev = cut_sequences(pad_ref[CONV_PAD - 1:CONV_PAD - 1 + ROW_TILE, :], keep_first, 0)
        nxt = cut_sequences(pad_ref[CONV_PAD + 1:CONV_PAD + 1 + ROW_TILE, :], keep_last, 7)
        cw = jnp.concatenate(cols(cw_ref, c), axis=-1)
        cb = jnp.concatenate(cols(cb_ref, c), axis=-1)
        conv = prev * cw[0:1, :] + up * cw[1:2, :] + nxt * cw[2:3, :] + cb
        gate = conv[:, :FFN_CHUNK]
        act = _silu(gate) * conv[:, FFN_CHUNK:]
        act_ref[:, pl.ds(pl.multiple_of(c * FFN_CHUNK, FFN_CHUNK), FFN_CHUNK)] = act.astype(BF16)

    up_proj(0, pad_a)

    def body(k, carry):
        c = 2 * k
        up_proj(c + 1, pad_b)
        conv_act(c, pad_a)
        up_proj(c + 2, pad_a)
        conv_act(c + 1, pad_b)
        return carry

    lax.fori_loop(0, (FFN_CHUNKS - 1) // 2, body, 0)
    conv_act(FFN_CHUNKS - 1, pad_a)
    y = jnp.dot(act_ref[...], wd_ref[...], preferred_element_type=F32)
    o_ref[...] = _read_rows(x_refs) + mod_ref[5:6, :] * y


def conv_ffn_residual(x, layer, norm_g, mods, w_up, conv_w, conv_b, w_down):
    split_x = isinstance(x, tuple)
    x_ops, x_specs = _x_operands(x)
    once = pl.Buffered(1)
    vmem = (4 * ROW_TILE * D * 4 + ROW_TILE * D * 2 + 2 * (ROW_TILE + 2 * CONV_PAD) * 2 * FFN_CHUNK * 4
            + ROW_TILE * D_FF * 2 + 3 * D * D_FF * 2 + 5 * ROW_TILE * 2 * FFN_CHUNK * 4)
    return pl.pallas_call(
        functools.partial(_ffn_kernel, split_x=split_x),
        out_shape=jax.ShapeDtypeStruct((ROWS, D), F32),
        grid=(N_ROW_TILES,),
        in_specs=x_specs + [
            pl.BlockSpec((None, 1, D), lambda i: (layer, 0, 0)),
            pl.BlockSpec((None, None, 6, D), lambda i: (layer, _mod_row(i), 0, 0)),
            pl.BlockSpec((None, D, 2 * D_FF), lambda i: (layer, 0, 0), pipeline_mode=once),
            pl.BlockSpec((None, 3, 2 * D_FF), lambda i: (layer, 0, 0), pipeline_mode=once),
            pl.BlockSpec((None, 1, 2 * D_FF), lambda i: (layer, 0, 0), pipeline_mode=once),
            pl.BlockSpec((None, D_FF, D), lambda i: (layer, 0, 0), pipeline_mode=once),
        ],
        out_specs=pl.BlockSpec((ROW_TILE, D), lambda i: (i, 0)),
        scratch_shapes=[pltpu.VMEM((ROW_TILE, D), BF16),
                        pltpu.VMEM((ROW_TILE + 2 * CONV_PAD, 2 * FFN_CHUNK), F32),
                        pltpu.VMEM((ROW_TILE + 2 * CONV_PAD, 2 * FFN_CHUNK), F32),
                        pltpu.VMEM((ROW_TILE, D_FF), BF16)],
        compiler_params=_cparams(("arbitrary",), vmem),
        name="conv_ffn",
    )(*x_ops, norm_g.reshape(DEPTH, 1, D), mods, w_up, conv_w, conv_b.reshape(DEPTH, 1, 2 * D_FF), w_down)


def _final_norm_kernel(x_ref, g_ref, o_ref):
    x = x_ref[...]
    o_ref[...] = x * lax.rsqrt(jnp.mean(x * x, axis=-1, keepdims=True) + NORM_EPS) * g_ref[...]


def final_norm(x, g, first_tile, n_tiles):
    return pl.pallas_call(
        _final_norm_kernel,
        out_shape=jax.ShapeDtypeStruct((n_tiles * ROW_TILE, D), F32),
        grid=(n_tiles,),
        in_specs=[pl.BlockSpec((ROW_TILE, D), lambda i: (first_tile + i, 0)),
                  pl.BlockSpec((1, D), lambda i: (0, 0))],
        out_specs=pl.BlockSpec((ROW_TILE, D), lambda i: (i, 0)),
        compiler_params=_cparams(("arbitrary",), 4 * ROW_TILE * D * 4),
        name="final_norm",
    )(x, g.reshape(1, D))


NQ = ATTN_HEADS * HEAD_DIM
NKV = ATTN_KV_HEADS * HEAD_DIM
Q_BLOCK = 128
MASKED = -1e30
LOG2E = 1.0 / math.log(2.0)
Q_PRESCALE = HEAD_DIM ** -0.5 * LOG2E


def _dot_t(a, b):
    return lax.dot_general(a.astype(BF16), b.astype(BF16), (((1,), (1,)), ((), ())),
                           preferred_element_type=F32)


def _group_rows(q):
    return jnp.concatenate([q[:, g * HEAD_DIM:(g + 1) * HEAD_DIM] for g in range(ATTN_GROUP)], axis=0)


def _sink_lanes(sink_ref, h, rows):
    return LOG2E * jnp.concatenate(
        [jnp.broadcast_to(sink_ref[0:1, ATTN_GROUP * h + g:ATTN_GROUP * h + g + 1], (1, rows))
         for g in range(ATTN_GROUP)], axis=-1)


ONES_ROWS = 16


def _values_t(v):
    ones = jnp.ones((ONES_ROWS, v.shape[0]), F32)
    out = []
    for c in range(NKV // 128):
        vt = v[:, c * 128:(c + 1) * 128].T
        out += [jnp.concatenate([vt[j * HEAD_DIM:(j + 1) * HEAD_DIM], ones], axis=0).astype(BF16)
                for j in range(128 // HEAD_DIM)]
    return out


def _softmax_pv(q4, key_sets, sink2):
    scores = []
    for k, _, bias in key_sets:
        s = _dot_t(k, q4)
        if bias is not None:
            s = jnp.concatenate([s[c * 128:(c + 1) * 128] if bc is None else s[c * 128:(c + 1) * 128] + bc
                                 for c, bc in enumerate(bias)], axis=0)
        scores.append(s)
    m = sink2
    for s in scores:
        m = jnp.maximum(m, jnp.max(s, axis=0, keepdims=True))
    acc = None
    for s, (_, v1t, _) in zip(scores, key_sets):
        t = jnp.dot(v1t, jnp.exp2(s - m).astype(BF16), preferred_element_type=F32)
        acc = t if acc is None else acc + t
    denom = acc[HEAD_DIM:HEAD_DIM + 1] + jnp.exp2(sink2 - m)
    return acc[:HEAD_DIM] * (1.0 / denom)


def _heads_to_columns(o_t, rows):
    slabs = []
    for g in range(0, ATTN_GROUP, 128 // HEAD_DIM):
        pair = jnp.concatenate([o_t[:, (g + j) * rows:(g + j + 1) * rows] for j in range(128 // HEAD_DIM)], axis=0)
        slabs.append(pair.T)
    return jnp.concatenate(slabs, axis=-1)


def _ctx_attn_kernel(qkv_ref, sink_ref, o_ref):
    v1t = _values_t(qkv_ref[:, NQ + NKV:])
    outs = []
    for h in range(ATTN_KV_HEADS):
        k = qkv_ref[:, NQ + h * HEAD_DIM:NQ + (h + 1) * HEAD_DIM].astype(BF16)
        q4 = _group_rows(qkv_ref[:, ATTN_GROUP * h * HEAD_DIM:ATTN_GROUP * (h + 1) * HEAD_DIM] * Q_PRESCALE)
        o_t = _softmax_pv(q4.astype(BF16), [(k, v1t[h], None)], _sink_lanes(sink_ref, h, SEQ))
        outs.append(_heads_to_columns(o_t, SEQ))
    o_ref[...] = jnp.concatenate(outs, axis=-1).astype(BF16)


def context_attention(qkv, sink):
    return pl.pallas_call(
        _ctx_attn_kernel,
        out_shape=jax.ShapeDtypeStruct((ROWS_P, NQ), BF16),
        grid=(BATCH,),
        in_specs=[pl.BlockSpec((SEQ, NQ + 2 * NKV), lambda b: (b, 0)),
                  pl.BlockSpec((1, ATTN_HEADS), lambda b: (0, 0))],
        out_specs=pl.BlockSpec((SEQ, NQ), lambda b: (b, 0)),
        compiler_params=_cparams(("arbitrary",), 2 * SEQ * (2 * NQ + 2 * NKV) * 4 + 24 * SEQ * ATTN_GROUP * SEQ * 4),
        name="context_attention",
    )(qkv, sink.reshape(1, ATTN_HEADS))


def _rope(x, cos, sin_a, sin_b):
    outs = []
    for c in range(x.shape[1] // 128):
        s = x[:, c * 128:(c + 1) * 128]
        outs.append(s * cos + pltpu.roll(s, 128 - HEAD_DIM // 4, 1) * sin_a + pltpu.roll(s, HEAD_DIM // 4, 1) * sin_b)
    return jnp.concatenate(outs, axis=-1)


def _lat_attn_kernel(q_ref, kp_ref, kc_ref, kn_ref, vp_ref, vc_ref, vn_ref, ck_ref, cv_ref,
                     cos_ref, sa_ref, sb_ref, sink_ref, o_ref):
    n = pl.program_id(1)
    nb = pl.num_programs(1)

    def tables(blk):
        r = pl.ds(pl.multiple_of(blk * Q_BLOCK, Q_BLOCK), Q_BLOCK)
        return cos_ref[r, :], sa_ref[r, :], sb_ref[r, :]

    cos, sa, sb = tables(n)
    qr = (_rope(q_ref[...], cos, sa, sb) * Q_PRESCALE).astype(BF16)
    k3 = jnp.concatenate([
        _rope(kp_ref[...], *tables(jnp.maximum(n - 1, 0))),
        _rope(kc_ref[...], cos, sa, sb),
        _rope(kn_ref[...], *tables(jnp.minimum(n + 1, nb - 1)))], axis=0).astype(BF16)
    v3 = jnp.concatenate([vp_ref[...], vc_ref[...], vn_ref[...]], axis=0)

    cols = ATTN_GROUP * Q_BLOCK
    koff = lax.broadcasted_iota(jnp.int32, (Q_BLOCK, cols), 0)
    qoff = lax.broadcasted_iota(jnp.int32, (Q_BLOCK, cols), 1) & (Q_BLOCK - 1)
    bias_prev = jnp.where((koff >= qoff) & (n > 0), 0.0, MASKED)
    bias_next = jnp.where((koff <= qoff) & (n < nb - 1), 0.0, MASKED)

    v1t = _values_t(v3)
    cv1t = _values_t(cv_ref[...])
    outs = []
    for h in range(ATTN_KV_HEADS):
        hs = slice(h * HEAD_DIM, (h + 1) * HEAD_DIM)
        q4 = _group_rows(qr[:, ATTN_GROUP * h * HEAD_DIM:ATTN_GROUP * (h + 1) * HEAD_DIM])
        o_t = _softmax_pv(q4, [(k3[:, hs], v1t[h], [bias_prev, None, bias_next]),
                               (ck_ref[:, hs].astype(BF16), cv1t[h], None)],
                          _sink_lanes(sink_ref, h, Q_BLOCK))
        outs.append(_heads_to_columns(o_t, Q_BLOCK))
    o_ref[...] = jnp.concatenate(outs, axis=-1).astype(BF16)


def _rope_tables():
    t = np.arange(DEC_SEQ)
    half = HEAD_DIM // 2
    inv_freq = 1.0 / (ROPE_BASE ** (np.arange(0, half, 2, dtype=np.float32) / half))
    ar = (t // GRID_W).astype(np.float32)[:, None] * inv_freq
    ac = (t % GRID_W).astype(np.float32)[:, None] * inv_freq
    return jnp.concatenate([jnp.asarray(a) for a in (ar, ar, ac, ac)] * 2, axis=-1)


def latent_attention(qkv, cache_k, cache_v, layer_j, sink):
    ang = _rope_tables()
    cos, sin = jnp.cos(ang), jnp.sin(ang)
    first = (lax.broadcasted_iota(jnp.int32, ang.shape, 1) % (HEAD_DIM // 2)) < HEAD_DIM // 4
    sin_a = jnp.where(first, -sin, 0.0)
    sin_b = jnp.where(first, 0.0, sin)
    nb = DEC_SEQ // Q_BLOCK
    base = ROWS_P // Q_BLOCK
    kcol, vcol = NQ // NKV, NQ // NKV + 1
    ck = cache_k.reshape(DEC_BATCH, -1, PAST_LEN, NKV)
    cv = cache_v.reshape(DEC_BATCH, -1, PAST_LEN, NKV)

    def kv_spec(col, off):
        return pl.BlockSpec((Q_BLOCK, NKV),
                            lambda b, n: (base + b * nb + jnp.clip(n + off, 0, nb - 1), col))

    table = pl.BlockSpec((DEC_SEQ, 128), lambda b, n: (0, 0))
    return pl.pallas_call(
        _lat_attn_kernel,
        out_shape=jax.ShapeDtypeStruct((ROWS_S, NQ), BF16),
        grid=(DEC_BATCH, nb),
        in_specs=[pl.BlockSpec((Q_BLOCK, NQ), lambda b, n: (base + b * nb + n, 0)),
                  kv_spec(kcol, -1), kv_spec(kcol, 0), kv_spec(kcol, 1),
                  kv_spec(vcol, -1), kv_spec(vcol, 0), kv_spec(vcol, 1),
                  pl.BlockSpec((None, None, PAST_LEN, NKV), lambda b, n: (b, layer_j, 0, 0)),
                  pl.BlockSpec((None, None, PAST_LEN, NKV), lambda b, n: (b, layer_j, 0, 0)),
                  table, table, table,
                  pl.BlockSpec((1, ATTN_HEADS), lambda b, n: (0, 0))],
        out_specs=pl.BlockSpec((Q_BLOCK, NQ), lambda b, n: (b * nb + n, 0)),
        compiler_params=_cparams(("arbitrary", "arbitrary"),
                                 4 * Q_BLOCK * NQ * 4 + 12 * Q_BLOCK * NKV * 4 + 4 * PAST_LEN * NKV * 4
                                 + 6 * DEC_SEQ * 128 * 4 + 24 * ATTN_GROUP * Q_BLOCK * (3 * Q_BLOCK + PAST_LEN) * 4),
        name="latent_attention",
    )(qkv, qkv, qkv, qkv, qkv, qkv, qkv, ck, cv, cos, sin_a, sin_b, sink.reshape(1, ATTN_HEADS))


def attention_layer(x, layer, layer_j, mods, norm_g, wqkv, wo, sink, cache_k, cache_v):
    qkv = norm_mod_matmul(x, layer, norm_g, mods, wqkv, layer_j, "attn_qkv")
    a_p = context_attention(qkv, sink)
    a_s = latent_attention(qkv, cache_k, cache_v, layer_j, sink)
    x = matmul_gated_residual(a_p, a_s, wo, layer_j, x, layer, mods, "attn_wo")
    new_k = qkv[:ROWS_P, NQ:NQ + NKV].reshape(BATCH, SEQ, ATTN_KV_HEADS, HEAD_DIM)
    new_v = qkv[:ROWS_P, NQ + NKV:].reshape(BATCH, SEQ, ATTN_KV_HEADS, HEAD_DIM)
    return x, new_k, new_v


HG_TILE = 256
HG_LEVELS = 8
HG_IN = 3 * 1024 + 2 * 1024
HG_OUT_TILE = 512
HG_HEADS_PER_STEP = 4


def _hgrn_consts():
    t = np.arange(HG_TILE)
    x = t[:, None] ^ t[None, :]
    hb = np.where(x == 0, -1, np.floor(np.log2(np.maximum(x, 1))).astype(np.int64))
    later = t[:, None] > t[None, :]
    half = HG_TILE // 2
    masks, tris = [], []
    for reverse in (False, True):
        side = ~later & (x != 0) if reverse else later
        lv = [hb == -1] + [(hb == lvl) & side for lvl in range(HG_LEVELS - 1)]
        masks.append(np.stack([m[:half, :half] for m in lv]).astype(np.float32))
        tris.append((t[None, :] >= t[:, None]) if reverse else (t[None, :] <= t[:, None]))
    return jnp.asarray(np.stack(masks)), jnp.asarray(np.stack(tris).astype(np.float32), dtype=BF16)


def _split_bf16(x):
    def top(v):
        bits = lax.bitcast_convert_type(v, jnp.uint32) & jnp.uint32(0xFFFF0000)
        return lax.bitcast_convert_type(bits, F32)

    hi = top(x)
    r = x - hi
    mid = top(r)
    return hi.astype(BF16), mid.astype(BF16), (r - mid).astype(BF16)


def _block_row(x, blk, idx):
    t = x.shape[0]
    x3 = x.reshape(t // blk, blk, x.shape[1])
    return jnp.broadcast_to(x3[:, idx:idx + 1, :], x3.shape).reshape(x.shape)


def _lower_bound(lb_ref, layer, direction):
    x = lb_ref[direction]
    e = jnp.exp(x - jnp.max(x, axis=0, keepdims=True))
    p = e / jnp.sum(e, axis=0, keepdims=True)
    return jnp.sum(p[1:layer + 1, :], axis=0, keepdims=True)


def _hgrn_tile(q, v, z, lb, s_in, mask_ref, tri, reverse):
    t = HG_TILE
    lo, hi = slice(0, t // 2), slice(t // 2, t)
    sg = _sigmoid(z)
    f = lb + (1.0 - lb) * sg
    k = (1.0 - lb) * (1.0 - sg)
    lf3 = _split_bf16(jnp.log(f))
    cum = sum(jnp.dot(tri, p, preferred_element_type=F32) for p in lf3)
    cum2 = cum * LOG2E
    rows = lax.broadcasted_iota(jnp.int32, (t, 1), 0)
    att = [mask_ref[0] * _dot_t(q[r], k[r]) for r in (lo, hi)]
    top = None
    for lvl in range(HG_LEVELS):
        half = 1 << lvl
        bit = (rows & half) != 0
        qside = ~bit if reverse else bit
        if lvl == 0:
            e = jnp.where(qside, f, 1.0)
        else:
            ref = _block_row(cum2, 2 * half, half if reverse else half - 1)
            e = jnp.exp2(-jnp.abs(cum2 - ref))
        w = (jnp.where(qside, q, k) * e).astype(BF16)
        if lvl < HG_LEVELS - 1:
            att = [a + mask_ref[lvl + 1] * _dot_t(w[r], w[r]) for a, r in zip(att, (lo, hi))]
        else:
            top = _dot_t(w[lo], w[hi]) if reverse else _dot_t(w[hi], w[lo])
    vb = v.astype(BF16)
    o_lo, o_hi = _bdot(att[0], vb[lo]), _bdot(att[1], vb[hi])
    if reverse:
        o_lo = o_lo + _bdot(top, vb[hi])
    else:
        o_hi = o_hi + _bdot(top, vb[lo])
    o = jnp.concatenate([o_lo, o_hi], axis=0)
    last = cum[0:1, :] if reverse else cum[t - 1:t, :]
    kd = (k * jnp.exp(last - cum)).astype(BF16)
    s_out = lax.dot_general(kd, vb, (((0,), (0,)), ((), ())), preferred_element_type=F32)
    if s_in is not None:
        o = o + _bdot(q * jnp.exp(cum), s_in)
        ones = jnp.ones((t, HGRN_DV), BF16)
        last_col = sum(lax.dot_general(p, ones, (((0,), (0,)), ((), ())), preferred_element_type=F32) for p in lf3)
        s_out = jnp.exp(last_col) * s_in + s_out
    return o, s_out


def _hgrn_prompt_kernel(q_ref, v_ref, zf_ref, zb_ref, lb_ref, mask_ref, tri_ref, of_ref, ob_ref, sfin_ref, *, layer):
    for j in range(HG_HEADS_PER_STEP):
        hs = slice(j * HGRN_DK, (j + 1) * HGRN_DK)
        q, v = q_ref[:, hs], v_ref[:, hs]
        lb_f, lb_b = (_lower_bound(lb_ref.at[:, :, hs], layer, d) for d in range(2))
        of_ref[:, hs], sfin_ref[0, j] = _hgrn_tile(q, v, zf_ref[:, hs], lb_f, None, mask_ref.at[0], tri_ref[0], False)
        ob_ref[:, hs], sfin_ref[1, j] = _hgrn_tile(q, v, zb_ref[:, hs], lb_b, None, mask_ref.at[1], tri_ref[1], True)


def _hgrn_sample_kernel(qf_ref, vf_ref, zf_ref, qb_ref, vb_ref, zb_ref, lb_ref, s0_ref, mask_ref, tri_ref,
                        of_ref, ob_ref, state_ref, *, layer):
    tiles = DEC_SEQ // HG_TILE

    @pl.when(pl.program_id(1) % tiles == 0)
    def _():
        state_ref[...] = s0_ref[...]

    for j in range(HG_HEADS_PER_STEP):
        hs = slice(j * HGRN_DK, (j + 1) * HGRN_DK)
        lb_f, lb_b = (_lower_bound(lb_ref.at[:, :, hs], layer, d) for d in range(2))
        of_ref[:, hs], state_ref[0, j] = _hgrn_tile(qf_ref[:, hs], vf_ref[:, hs], zf_ref[:, hs], lb_f,
                                                    state_ref[0, j], mask_ref.at[0], tri_ref[0], False)
        ob_ref[:, hs], state_ref[1, j] = _hgrn_tile(qb_ref[:, hs], vb_ref[:, hs], zb_ref[:, hs], lb_b,
                                                    state_ref[1, j], mask_ref.at[1], tri_ref[1], True)


def hgrn_scan(proj, lb_raw, layer, state_hgrn, layer_j):
    masks, tris = _hgrn_consts()
    hps = HG_HEADS_PER_STEP
    hw = hps * HGRN_DK
    qc, vc, zfc, zbc = 0, 1024 // hw, 2048 // hw, 3072 // hw
    const_specs = [pl.BlockSpec((2, DEPTH, hw), lambda h, i: (0, 0, h))]
    mask_specs = [pl.BlockSpec(masks.shape, lambda h, i: (0, 0, 0, 0)),
                  pl.BlockSpec((2, HG_TILE, HG_TILE), lambda h, i: (0, 0, 0))]
    vmem = 4 * masks.size * 4 + 16 * HG_TILE * hw * 4 + 8 * hw * HGRN_DV * 4 + 24 * hps * HG_TILE * HG_TILE * 4

    def col(cb, row_fn):
        return pl.BlockSpec((HG_TILE, hw), lambda h, i: (row_fn(i), cb + h))

    o_shape = jax.ShapeDtypeStruct((ROWS_P, HGRN_HEADS * HGRN_DV), F32)
    same = lambda i: i
    of_p, ob_p, sfin = pl.pallas_call(
        functools.partial(_hgrn_prompt_kernel, layer=layer),
        out_shape=(o_shape, o_shape, jax.ShapeDtypeStruct((BATCH, 2, HGRN_HEADS, HGRN_DK, HGRN_DV), F32)),
        grid=(HGRN_HEADS // hps, BATCH),
        in_specs=[col(qc, same), col(vc, same), col(zfc, same), col(zbc, same)] + const_specs + mask_specs,
        out_specs=(col(0, same), col(0, same),
                   pl.BlockSpec((None, 2, hps, HGRN_DK, HGRN_DV), lambda h, i: (i, 0, h, 0, 0))),
        compiler_params=_cparams(("arbitrary", "arbitrary"), vmem),
        name="hgrn_scan_prompt",
    )(proj, proj, proj, proj, lb_raw, masks, tris)

    tiles = DEC_SEQ // HG_TILE
    base = ROWS_P // HG_TILE
    bwd = lambda i: (i // tiles) * tiles + (tiles - 1 - i % tiles)
    fwd_in = lambda i: base + i
    bwd_in = lambda i: base + bwd(i)
    o_shape = jax.ShapeDtypeStruct((ROWS_S, HGRN_HEADS * HGRN_DV), F32)
    of_s, ob_s = pl.pallas_call(
        functools.partial(_hgrn_sample_kernel, layer=layer),
        out_shape=(o_shape, o_shape),
        grid=(HGRN_HEADS // hps, DEC_BATCH * tiles),
        in_specs=[col(qc, fwd_in), col(vc, fwd_in), col(zfc, fwd_in), col(qc, bwd_in), col(vc, bwd_in),
                  col(zbc, bwd_in)]
        + const_specs
        + [pl.BlockSpec((None, None, 2, hps, HGRN_DK, HGRN_DV), lambda h, i: (i // tiles, layer_j, 0, h, 0, 0))]
        + mask_specs,
        out_specs=(col(0, same), col(0, bwd)),
        scratch_shapes=[pltpu.VMEM((2, hps, HGRN_DK, HGRN_DV), F32)],
        compiler_params=_cparams(("arbitrary", "arbitrary"), vmem),
        name="hgrn_scan_sample",
    )(proj, proj, proj, proj, proj, proj, lb_raw, state_hgrn, masks, tris)
    return (of_p, ob_p), (of_s, ob_s), sfin


def _hgrn_out_kernel(ofp_ref, obp_ref, ofs_ref, obs_ref, g_ref, gn_ref, w_ref, x_ref, mod_ref, o_ref):
    def run(of_ref, ob_ref):
        gn = gn_ref[...]
        parts = []
        for h in range(HGRN_HEADS):
            hs = slice(h * HGRN_DV, (h + 1) * HGRN_DV)
            o = of_ref[:, hs] + ob_ref[:, hs]
            g = g_ref[:, hs]
            o = o * lax.rsqrt(jnp.mean(o * o, axis=-1, keepdims=True) + NORM_EPS) * gn * _silu(g)
            parts.append(o.astype(BF16))
        y = jnp.dot(jnp.concatenate(parts, axis=-1), w_ref[...], preferred_element_type=F32)
        o_ref[...] = x_ref[...] + mod_ref[2:3, :] * y

    is_prompt = pl.program_id(0) < ROWS_P // HG_OUT_TILE
    pl.when(is_prompt)(lambda: run(ofp_ref, obp_ref))
    pl.when(jnp.logical_not(is_prompt))(lambda: run(ofs_ref, obs_ref))


def hgrn_out(o_p, o_s, proj, g_norm, wo, w_idx, x, layer, mods):
    gcol = 4096 // D
    tile = HG_OUT_TILE
    p_spec, s_spec = _group_specs(D, tile)
    return pl.pallas_call(
        _hgrn_out_kernel,
        out_shape=jax.ShapeDtypeStruct((ROWS, D), F32),
        grid=(ROWS // tile,),
        in_specs=[p_spec, p_spec, s_spec, s_spec,
                  pl.BlockSpec((tile, D), lambda i: (i, gcol)),
                  pl.BlockSpec((1, HGRN_DV), lambda i: (0, 0)),
                  pl.BlockSpec((None, D, D), lambda i: (w_idx, 0, 0), pipeline_mode=pl.Buffered(1)),
                  pl.BlockSpec((tile, D), lambda i: (i, 0)),
                  _mod_spec(layer, 1, tile)],
        out_specs=pl.BlockSpec((tile, D), lambda i: (i, 0)),
        compiler_params=_cparams(("arbitrary",), 22 * tile * D * 4),
        name="hgrn_out",
    )(*o_p, *o_s, proj, g_norm.reshape(1, HGRN_DV), wo, x, mods)


def hgrn_layer(x, layer, layer_j, mods, norm_g, w_in, hgrn_lb, g_norm, wo, state_hgrn):
    proj = norm_mod_matmul(x, layer, norm_g, mods, w_in, layer_j, "hgrn_in")
    o_p, o_s, sfin = hgrn_scan(proj, jnp.transpose(hgrn_lb, (1, 0, 2)), layer, state_hgrn, layer_j)
    x = hgrn_out(o_p, o_s, proj, g_norm[layer_j], wo, layer_j, x, layer, mods)
    return x, sfin


SSM_N = SSM_GROUPS * SSM_STATE
SSM_KT = 8
SSM_ROWS = 256
SSM_PAD = 16


def _ssm_prep_kernel(are_ref, aim_ref, ldt_ref, bre_ref, bim_ref, lre_ref, lim_ref, bbre_ref, bbim_ref):
    a_re = jnp.minimum(are_ref[...], -1e-4)
    a_im = aim_ref[...]
    dt = jnp.exp(ldt_ref[...])
    mag = jnp.exp(a_re * dt)
    l_re = mag * jnp.cos(a_im * dt)
    l_im = mag * jnp.sin(a_im * dt)
    lre_ref[...] = l_re
    lim_ref[...] = l_im
    den = a_re * a_re + a_im * a_im
    c_re = ((l_re - 1.0) * a_re + l_im * a_im) / den
    c_im = (l_im * a_re - (l_re - 1.0) * a_im) / den
    b_re, b_im = bre_ref[...], bim_ref[...]
    bbre_ref[...] = c_re[:, None, :] * b_re - c_im[:, None, :] * b_im
    bbim_ref[...] = c_re[:, None, :] * b_im + c_im[:, None, :] * b_re


def ssm_discretize(a_re, a_im, log_dt, b_re, b_im):
    g2 = 2 * SSM_GROUPS
    sh = jax.ShapeDtypeStruct((g2, SSM_STATE), F32)
    shb = jax.ShapeDtypeStruct((g2, SSM_GROUP, SSM_STATE), F32)
    bt = lambda b: jnp.transpose(b, (0, 1, 3, 2)).reshape(g2, SSM_GROUP, SSM_STATE)
    return pl.pallas_call(_ssm_prep_kernel, out_shape=(sh, sh, shb, shb), name="ssm_discretize")(
        a_re.reshape(g2, SSM_STATE), a_im.reshape(g2, SSM_STATE), log_dt.reshape(g2, 1), bt(b_re), bt(b_im))


def _ssm_block_diag(l_re, l_im, bb_re, bb_im, c_re, c_im):
    nk = SSM_GROUPS // SSM_KT
    eye = jnp.eye(SSM_KT, dtype=F32)

    def bmat(b):
        b = b.reshape(2, nk, SSM_KT, SSM_GROUP, SSM_STATE)
        return jnp.einsum('dkgip,gh->dkgihp', b, eye).reshape(2, nk, SSM_KT * SSM_GROUP, SSM_KT * SSM_STATE)

    def cmat(c):
        c = c.reshape(2, nk, SSM_KT, SSM_GROUP, SSM_STATE)
        return jnp.einsum('dkgip,gh->dkhpgi', c, eye).reshape(2, nk, SSM_KT * SSM_STATE, SSM_KT * SSM_GROUP)

    b_mat = jnp.concatenate([bmat(bb_re), bmat(bb_im)], axis=-1).astype(BF16)
    c_mat = jnp.concatenate([cmat(c_re), cmat(-c_im)], axis=-2).astype(BF16)
    lam = jnp.stack([l_re.reshape(2, SSM_N), l_im.reshape(2, SSM_N)], axis=1)
    return b_mat, c_mat, lam


X4_SHAPE = (ROWS // (4 * SEQ), 4, SEQ, D)


def _tm_geometry(prompt):
    if prompt:
        batch = BATCH
        steps = SSM_ROWS // batch
        return batch, steps, (4, 4, steps, D), (lambda i: (0, 0, i, 0)), SEQ // steps, (0,) * batch
    batch = DEC_BATCH
    steps = SSM_ROWS // batch
    per_q = SEQ // steps
    return (batch, steps, (4, 1, steps, D), (lambda i: (1, i // per_q, i % per_q, 0)), DEC_SEQ // steps,
            tuple(range(1, 1 + batch)))


def _x4_seq(ref, b):
    return ref.at[b // ref.shape[1], b % ref.shape[1]]


LANE_SLABS = D // 128


def _slab_store(s_ref, rows, val):
    for c in range(LANE_SLABS):
        s_ref[c, rows, :] = val[:, c * 128:(c + 1) * 128]


def _slab_load(s_ref, rows):
    return jnp.concatenate([s_ref[c, rows, :] for c in range(LANE_SLABS)], axis=-1)


def _normmod_tm_kernel(x_ref, g_ref, mod_ref, o_ref, s_ref, *, batch, steps, mod_rows):
    g = g_ref[...]
    for b in range(batch):
        m = mod_ref.at[mod_rows[b]]
        _slab_store(s_ref, slice(b * steps, (b + 1) * steps),
                    _norm_mod(_x4_seq(x_ref, b)[...], g, m[0:1, :], m[1:2, :]))
    for t in range(steps):
        o_ref[t * batch:(t + 1) * batch, :] = _slab_load(s_ref, pl.ds(t, batch, stride=steps))


def norm_mod_time_major(x4, layer, norm_g, mods, prompt):
    batch, steps, blk, idx, tiles, mod_rows = _tm_geometry(prompt)
    return pl.pallas_call(
        functools.partial(_normmod_tm_kernel, batch=batch, steps=steps, mod_rows=mod_rows),
        out_shape=jax.ShapeDtypeStruct((tiles * SSM_ROWS, D), F32),
        grid=(tiles,),
        in_specs=[pl.BlockSpec(blk, idx),
                  pl.BlockSpec((None, 1, D), lambda i: (layer, 0, 0)),
                  pl.BlockSpec((None, MOD_ROWS, 6, D), lambda i: (layer, 0, 0, 0))],
        out_specs=pl.BlockSpec((SSM_ROWS, D), lambda i: (i, 0)),
        scratch_shapes=[pltpu.VMEM((LANE_SLABS, SSM_ROWS, 128), F32)],
        compiler_params=_cparams(("arbitrary",), 8 * SSM_ROWS * D * 4),
        name="ssm_norm_time_major",
    )(x4, norm_g.reshape(DEPTH, 1, D), mods)


def _ssm_scan_kernel(xf_ref, xb_ref, bm_ref, cm_ref, lam_ref, h0_ref, yf_ref, yb_ref, hfin_ref,
                     hre_ref, him_ref, st_ref, *, batch):
    i = pl.program_id(0)
    steps = SSM_ROWS // batch
    nk = SSM_GROUPS // SSM_KT
    kw = SSM_KT * SSM_STATE

    @pl.when(i == 0)
    def _():
        st_ref[...] = h0_ref[...]

    def step(d, prev_rows, cur_rows):
        l_re, l_im = lam_ref[d, 0], lam_ref[d, 1]
        p_re, p_im = hre_ref[prev_rows, :], him_ref[prev_rows, :]
        hre_ref[cur_rows, :] = l_re * p_re - l_im * p_im + hre_ref[cur_rows, :]
        him_ref[cur_rows, :] = l_re * p_im + l_im * p_re + him_ref[cur_rows, :]

    def run(d, x_ref, y_ref, reverse):
        base = 0 if reverse else SSM_PAD
        srow = SSM_ROWS if reverse else SSM_PAD - batch
        xb = x_ref[...].astype(BF16)
        for k in range(nk):
            bu = jnp.dot(xb[:, k * 128:(k + 1) * 128], bm_ref[d, k], preferred_element_type=F32)
            hre_ref[base:base + SSM_ROWS, k * kw:(k + 1) * kw] = bu[:, :kw]
            him_ref[base:base + SSM_ROWS, k * kw:(k + 1) * kw] = bu[:, kw:]
        hre_ref[srow:srow + batch, :] = st_ref[d, 0]
        him_ref[srow:srow + batch, :] = st_ref[d, 1]

        if batch % 8 == 0:
            def body(s, carry):
                t = (steps - 1 - s) if reverse else s
                cur = pl.ds(pl.multiple_of(base + t * batch, batch), batch)
                prev = pl.ds(pl.multiple_of(base + (t + 1) * batch if reverse else base + (t - 1) * batch, batch),
                             batch)
                step(d, prev, cur)
                return carry
            lax.fori_loop(0, steps, body, 0)
        else:
            per = 8 // batch

            def body(s, carry):
                g = (steps // per - 1 - s) if reverse else s
                slab = pl.multiple_of(base + g * 8, 8)
                l_re, l_im = lam_ref[d, 0], lam_ref[d, 1]
                cur_re, cur_im = hre_ref[pl.ds(slab, 8), :], him_ref[pl.ds(slab, 8), :]
                nb_slab = pl.multiple_of(slab + 8 if reverse else slab - 8, 8)
                nb_re, nb_im = hre_ref[pl.ds(nb_slab, 8), :], him_ref[pl.ds(nb_slab, 8), :]
                if reverse:
                    p_re, p_im = nb_re[0:batch], nb_im[0:batch]
                    order = range(per - 1, -1, -1)
                else:
                    p_re, p_im = nb_re[8 - batch:8], nb_im[8 - batch:8]
                    order = range(per)
                outs_re, outs_im = [None] * per, [None] * per
                for j in order:
                    b_re, b_im = cur_re[j * batch:(j + 1) * batch], cur_im[j * batch:(j + 1) * batch]
                    p_re, p_im = l_re * p_re - l_im * p_im + b_re, l_re * p_im + l_im * p_re + b_im
                    outs_re[j], outs_im[j] = p_re, p_im
                hre_ref[pl.ds(slab, 8), :] = jnp.concatenate(outs_re, axis=0)
                him_ref[pl.ds(slab, 8), :] = jnp.concatenate(outs_im, axis=0)
                return carry
            lax.fori_loop(0, steps // per, body, 0)

        erow = 0 if reverse else SSM_PAD + SSM_ROWS - batch
        st_ref[d, 0] = hre_ref[erow:erow + batch, :]
        st_ref[d, 1] = him_ref[erow:erow + batch, :]
        for k in range(nk):
            hk = jnp.concatenate([hre_ref[base:base + SSM_ROWS, k * kw:(k + 1) * kw],
                                  him_ref[base:base + SSM_ROWS, k * kw:(k + 1) * kw]], axis=-1).astype(BF16)
            y_ref[:, k * 128:(k + 1) * 128] = jnp.dot(hk, cm_ref[d, k], preferred_element_type=F32)

    run(0, xf_ref, yf_ref, False)
    run(1, xb_ref, yb_ref, True)

    @pl.when(i == pl.num_programs(0) - 1)
    def _():
        hfin_ref[...] = st_ref[...]


def ssm_scan(xn_tm, b_mat, c_mat, lam, h0, batch):
    rows = xn_tm.shape[0]
    n = rows // SSM_ROWS
    lam_b = jnp.broadcast_to(lam[:, :, None, :], (2, 2, batch, SSM_N))
    y_shape = jax.ShapeDtypeStruct((rows, D), F32)
    full = lambda a: pl.BlockSpec(a.shape, lambda i: (0,) * a.ndim)
    vmem = (2 * (SSM_ROWS + 2 * SSM_PAD) * SSM_N * 4 + 8 * SSM_ROWS * D * 4 + 2 * (b_mat.size + c_mat.size) * 2
            + 12 * batch * SSM_N * 4 * 2 + 8 * SSM_ROWS * 1024 * 4)
    return pl.pallas_call(
        functools.partial(_ssm_scan_kernel, batch=batch),
        out_shape=(y_shape, y_shape, jax.ShapeDtypeStruct((2, 2, batch, SSM_N), F32)),
        grid=(n,),
        in_specs=[pl.BlockSpec((SSM_ROWS, D), lambda i: (i, 0)),
                  pl.BlockSpec((SSM_ROWS, D), lambda i: (n - 1 - i, 0)),
                  full(b_mat), full(c_mat), full(lam_b), full(h0)],
        out_specs=(pl.BlockSpec((SSM_ROWS, D), lambda i: (i, 0)),
                   pl.BlockSpec((SSM_ROWS, D), lambda i: (n - 1 - i, 0)),
                   pl.BlockSpec((2, 2, batch, SSM_N), lambda i: (0, 0, 0, 0))),
        scratch_shapes=[pltpu.VMEM((SSM_ROWS + 2 * SSM_PAD, SSM_N), F32),
                        pltpu.VMEM((SSM_ROWS + 2 * SSM_PAD, SSM_N), F32),
                        pltpu.VMEM((2, 2, batch, SSM_N), F32)],
        compiler_params=_cparams(("arbitrary",), vmem),
        name="ssm_scan",
    )(xn_tm, xn_tm, b_mat, c_mat, lam_b, h0)


def _gelu_tanh(x):
    return 0.5 * x * (1.0 + jnp.tanh(math.sqrt(2.0 / math.pi) * (x + 0.044715 * (x * x * x))))


def _ssm_glu_kernel(yf_ref, yb_ref, xn_ref, d_ref, w_ref, x_ref, mod_ref, o_ref, s_ref,
                    *, batch, steps, mod_rows):
    g = _gelu_tanh(yf_ref[...] + yb_ref[...] + d_ref[...] * xn_ref[...])
    u = jnp.dot(g.astype(BF16), w_ref[...], preferred_element_type=F32)
    _slab_store(s_ref, slice(None), u[:, :D] * _sigmoid(u[:, D:]))
    for b in range(batch):
        gate = mod_ref[mod_rows[b], 2:3, :]
        _x4_seq(o_ref, b)[...] = (_x4_seq(x_ref, b)[...]
                                  + gate * _slab_load(s_ref, pl.ds(b, steps, stride=batch)))


def ssm_glu(yf, yb, xn, d, w_glu, w_idx, x4, layer, mods, prompt):
    batch, steps, blk, idx, tiles, mod_rows = _tm_geometry(prompt)
    tm_spec = pl.BlockSpec((SSM_ROWS, D), lambda i: (i, 0))
    out = pl.pallas_call(
        functools.partial(_ssm_glu_kernel, batch=batch, steps=steps, mod_rows=mod_rows),
        out_shape=jax.ShapeDtypeStruct((4,) + X4_SHAPE[1:], F32),
        grid=(tiles,),
        in_specs=[tm_spec, tm_spec, tm_spec,
                  pl.BlockSpec((None, 1, D), lambda i: (w_idx, 0, 0)),
                  pl.BlockSpec((None, D, 2 * D), lambda i: (w_idx, 0, 0), pipeline_mode=pl.Buffered(1)),
                  pl.BlockSpec(blk, idx),
                  pl.BlockSpec((None, MOD_ROWS, 6, D), lambda i: (layer, 0, 0, 0))],
        out_specs=pl.BlockSpec(blk, lambda i: (0,) + idx(i)[1:]),
        scratch_shapes=[pltpu.VMEM((LANE_SLABS, SSM_ROWS, 128), F32)],
        compiler_params=_cparams(("arbitrary",), 24 * SSM_ROWS * D * 4 + D * 2 * D * 2),
        name="ssm_glu",
    )(yf, yb, xn, d.reshape(-1, 1, D), w_glu, x4, mods)
    return out.reshape(-1, D)


def ssm_layer(x, layer, layer_j, mods, norm_g, a_re, a_im, log_dt, b_re, b_im, c_re, c_im, d, w_glu, state_ssm):
    l_re, l_im, bb_re, bb_im = ssm_discretize(a_re[layer_j], a_im[layer_j], log_dt[layer_j], b_re[layer_j],
                                              b_im[layer_j])
    b_mat, c_mat, lam = _ssm_block_diag(l_re, l_im, bb_re, bb_im, c_re[layer_j], c_im[layer_j])
    x4 = x.reshape(X4_SHAPE)
    xn_p = norm_mod_time_major(x4, layer, norm_g, mods, True)
    xn_s = norm_mod_time_major(x4, layer, norm_g, mods, False)
    h0_p = jnp.zeros((2, 2, BATCH, SSM_N), F32)
    h0_s = jnp.transpose(state_ssm[:, layer_j].reshape(DEC_BATCH, 2, SSM_N, 2), (1, 3, 0, 2))
    yfp, ybp, hfin = ssm_scan(xn_p, b_mat, c_mat, lam, h0_p, BATCH)
    yfs, ybs, _ = ssm_scan(xn_s, b_mat, c_mat, lam, h0_s, DEC_BATCH)
    out_p = ssm_glu(yfp, ybp, xn_p, d, w_glu, layer_j, x4, layer, mods, True)
    out_s = ssm_glu(yfs, ybs, xn_s, d, w_glu, layer_j, x4, layer, mods, False)
    new_state = jnp.transpose(hfin, (2, 0, 3, 1)).reshape(BATCH, 2, SSM_GROUPS, SSM_STATE, 2)
    return (out_p, out_s), new_state


def kernel(x_prompt, x_sample, cache_k, cache_v, state_hgrn, state_ssm, c, c_ctx, ada_w, ada_b, norm1_g, norm2_g, attn_wqkv, attn_wo, attn_sink, hgrn_w_in, hgrn_lb, hgrn_g_norm, hgrn_wo, ssm_a_re, ssm_a_im, ssm_log_dt, ssm_b_re, ssm_b_im, ssm_c_re, ssm_c_im, ssm_d, ssm_w_glu, ffn_w_up, ffn_conv_w, ffn_conv_b, ffn_w_down, final_g):
    cond8 = jnp.zeros((MOD_ROWS, D), F32).at[0].set(c_ctx).at[1:1 + DEC_BATCH].set(c)
    mods = ada_modulation(cond8, ada_w, ada_b)
    x = (x_prompt.reshape(ROWS_P, D), x_sample.reshape(ROWS_S, D))
    wqkv, wo, w_in, hwo, w_glu, w_up, w_down = (w.astype(BF16) for w in (
        attn_wqkv, attn_wo, hgrn_w_in, hgrn_wo, ssm_w_glu, ffn_w_up, ffn_w_down))
    new_k, new_v, new_hgrn, new_ssm = [], [], [], []
    for l in range(DEPTH):
        kind, j = l % N_MIXERS, l // N_MIXERS
        if kind == 0:
            x, k, v = attention_layer(x, l, j, mods, norm1_g, wqkv, wo, attn_sink[j], cache_k, cache_v)
            new_k.append(k)
            new_v.append(v)
        elif kind == 1:
            x, s = hgrn_layer(x, l, j, mods, norm1_g, w_in, hgrn_lb, hgrn_g_norm, hwo, state_hgrn)
            new_hgrn.append(s)
        else:
            x, s = ssm_layer(x, l, j, mods, norm1_g, ssm_a_re, ssm_a_im, ssm_log_dt, ssm_b_re, ssm_b_im,
                             ssm_c_re, ssm_c_im, ssm_d, w_glu, state_ssm)
            new_ssm.append(s)
        x = conv_ffn_residual(x, l, norm2_g, mods, w_up, ffn_conv_w, ffn_conv_b, w_down)
    y_prompt = final_norm(x, final_g, 0, N_ROW_TILES_P).reshape(BATCH, SEQ, D)
    y_sample = final_norm(x, final_g, N_ROW_TILES_P, N_ROW_TILES - N_ROW_TILES_P).reshape(DEC_BATCH, DEC_SEQ, D)
    return (y_prompt, y_sample, jnp.stack(new_k, axis=1), jnp.stack(new_v, axis=1),
            jnp.stack(new_hgrn, axis=1), jnp.stack(new_ssm, axis=1))
```

```python
import functools
import math

import jax
import jax.numpy as jnp
import numpy as np
from jax import lax
from jax.experimental import pallas as pl
from jax.experimental.pallas import tpu as pltpu

F32 = jnp.float32
BF16 = jnp.bfloat16

D = 1024
BATCH = 16
SEQ = 256
DEPTH = 4
DEC_BATCH = 4
DEC_SEQ = 1024
PAST_LEN = 512
GRID_W = 64
N_MIXERS = 3
ATTN_HEADS = 16
ATTN_KV_HEADS = 4
ATTN_GROUP = ATTN_HEADS // ATTN_KV_HEADS
HEAD_DIM = D // ATTN_HEADS
WINDOW = 128
ROPE_BASE = 10000.0
HGRN_HEADS = 8
HGRN_DK = 128
HGRN_DV = 128
SSM_GROUP = 16
SSM_GROUPS = D // SSM_GROUP
SSM_STATE = 64
D_FF = 2816
NORM_EPS = 1e-6

ROWS_P = BATCH * SEQ
ROWS_S = DEC_BATCH * DEC_SEQ
ROWS = ROWS_P + ROWS_S
ROW_TILE = 1024
N_ROW_TILES = ROWS // ROW_TILE
N_ROW_TILES_P = ROWS_P // ROW_TILE
MOD_ROWS = 8
V7X_VMEM_BYTES = 64 * 1024 * 1024


def _mod_row(i, tile=ROW_TILE):
    return jnp.where(i < ROWS_P // tile, 0, (i - ROWS_P // tile) // (DEC_SEQ // tile) + 1)


def _cparams(semantics, vmem_bytes):
    vmem = int(min(max(vmem_bytes * 5 // 4 + (4 << 20), 16 << 20), V7X_VMEM_BYTES - (6 << 20)))
    return pltpu.CompilerParams(dimension_semantics=semantics, vmem_limit_bytes=vmem)


def _bdot(a, b):
    return jnp.dot(a.astype(BF16), b.astype(BF16), preferred_element_type=F32)


def _norm_mod(x, g, shift, scale):
    y = x * lax.rsqrt(jnp.mean(x * x, axis=-1, keepdims=True) + NORM_EPS) * g
    return y * (1.0 + scale) + shift


def _sigmoid(x):
    return 0.5 + 0.5 * jnp.tanh(0.5 * x)


def _silu(x):
    h = 0.5 * x
    return h + h * jnp.tanh(h)


def _ada_kernel(c_ref, w_ref, b_ref, o_ref):
    c = c_ref[...]
    o_ref[...] = _bdot(_silu(c), w_ref[...]) + b_ref[...]


def ada_modulation(cond8, ada_w, ada_b):
    tn = 1024
    out = pl.pallas_call(
        _ada_kernel,
        out_shape=jax.ShapeDtypeStruct((DEPTH, MOD_ROWS, 6 * D), F32),
        grid=(DEPTH, 6 * D // tn),
        in_specs=[
            pl.BlockSpec((MOD_ROWS, D), lambda l, j: (0, 0)),
            pl.BlockSpec((None, D, tn), lambda l, j: (l, 0, j)),
            pl.BlockSpec((None, 1, tn), lambda l, j: (l, 0, j)),
        ],
        out_specs=pl.BlockSpec((None, MOD_ROWS, tn), lambda l, j: (l, 0, j)),
        compiler_params=_cparams(("arbitrary", "arbitrary"), 2 * D * tn * 4),
        name="ada_modulation",
    )(cond8, ada_w, ada_b.reshape(DEPTH, 1, 6 * D))
    return out.reshape(DEPTH, MOD_ROWS, 6, D)


def _x_operands(x, n_grid=1):
    if isinstance(x, tuple):
        return x, list(_group_specs(D, ROW_TILE, n_grid))
    return (x,), [pl.BlockSpec((ROW_TILE, D), (lambda i: (i, 0)) if n_grid == 1 else (lambda i, j: (i, 0)))]


def _read_rows(x_refs):
    if len(x_refs) == 1:
        return x_refs[0][...]
    return jnp.where(pl.program_id(0) < N_ROW_TILES_P, x_refs[0][...], x_refs[1][...])


def _nmm_kernel(*refs):
    x_refs, (g_ref, mod_ref, w_ref, o_ref, h_ref) = refs[:-5], refs[-5:]

    @pl.when(pl.program_id(1) == 0)
    def _():
        h_ref[...] = _norm_mod(_read_rows(x_refs), g_ref[...], mod_ref[0:1, :], mod_ref[1:2, :]).astype(BF16)

    o_ref[...] = jnp.dot(h_ref[...], w_ref[...], preferred_element_type=F32)


def _mod_spec(layer, n_grid, tile=ROW_TILE):
    if n_grid == 1:
        return pl.BlockSpec((None, None, 6, D), lambda i: (layer, _mod_row(i, tile), 0, 0))
    return pl.BlockSpec((None, None, 6, D), lambda i, j: (layer, _mod_row(i, tile), 0, 0))


def norm_mod_matmul(x, layer, norm_g, mods, w, w_idx, name):
    n = w.shape[-1]
    tn = 1024 if n % 1024 == 0 else 768
    x_ops, x_specs = _x_operands(x, 2)
    return pl.pallas_call(
        _nmm_kernel,
        out_shape=jax.ShapeDtypeStruct((ROWS, n), F32),
        grid=(N_ROW_TILES, n // tn),
        in_specs=x_specs + [
            pl.BlockSpec((None, 1, D), lambda i, j: (layer, 0, 0)),
            _mod_spec(layer, 2),
            pl.BlockSpec((None, D, tn), lambda i, j: (w_idx, 0, j)),
        ],
        out_specs=pl.BlockSpec((ROW_TILE, tn), lambda i, j: (i, j)),
        scratch_shapes=[pltpu.VMEM((ROW_TILE, D), BF16)],
        compiler_params=_cparams(("arbitrary", "arbitrary"),
                                 4 * ROW_TILE * D * 4 + ROW_TILE * D * 2 + 2 * D * tn * 2 + 2 * ROW_TILE * tn * 4),
        name=name,
    )(*x_ops, norm_g.reshape(DEPTH, 1, D), mods, w)


def _mm_res_kernel(ap_ref, as_ref, w_ref, *refs):
    x_refs, (mod_ref, o_ref) = refs[:-2], refs[-2:]

    def run(a_ref, x_ref):
        y = jnp.dot(a_ref[...].astype(BF16), w_ref[...], preferred_element_type=F32)
        o_ref[...] = x_ref[...] + mod_ref[2:3, :] * y

    is_prompt = pl.program_id(0) < N_ROW_TILES_P
    pl.when(is_prompt)(lambda: run(ap_ref, x_refs[0]))
    pl.when(jnp.logical_not(is_prompt))(lambda: run(as_ref, x_refs[-1]))


def _group_specs(k, tile=ROW_TILE, n_grid=1):
    n_p = ROWS_P // tile
    if n_grid == 1:
        return (pl.BlockSpec((tile, k), lambda i: (jnp.minimum(i, n_p - 1), 0)),
                pl.BlockSpec((tile, k), lambda i: (jnp.maximum(i - n_p, 0), 0)))
    return (pl.BlockSpec((tile, k), lambda i, j: (jnp.minimum(i, n_p - 1), 0)),
            pl.BlockSpec((tile, k), lambda i, j: (jnp.maximum(i - n_p, 0), 0)))


def matmul_gated_residual(a_p, a_s, w, w_idx, x, layer, mods, name):
    k = a_p.shape[1]
    x_ops, x_specs = _x_operands(x)
    return pl.pallas_call(
        _mm_res_kernel,
        out_shape=jax.ShapeDtypeStruct((ROWS, D), F32),
        grid=(N_ROW_TILES,),
        in_specs=[
            *_group_specs(k),
            pl.BlockSpec((None, k, D), lambda i: (w_idx, 0, 0), pipeline_mode=pl.Buffered(1)),
            *x_specs,
            _mod_spec(layer, 1),
        ],
        out_specs=pl.BlockSpec((ROW_TILE, D), lambda i: (i, 0)),
        compiler_params=_cparams(("arbitrary",), 4 * ROW_TILE * k * 2 + k * D * 2 + 7 * ROW_TILE * D * 4),
        name=name,
    )(a_p, a_s, w, *x_ops, mods)


FFN_CHUNK = 256
FFN_CHUNKS = D_FF // FFN_CHUNK
CONV_PAD = 8


def _ffn_kernel(*refs, split_x):
    x_refs, (g_ref, mod_ref, wup_ref, cw_ref, cb_ref, wd_ref, o_ref, h_ref, pad_a, pad_b, act_ref) = (
        refs[:1 + split_x], refs[1 + split_x:])
    i = pl.program_id(0)
    h_ref[...] = _norm_mod(_read_rows(x_refs), g_ref[...], mod_ref[3:4, :], mod_ref[4:5, :]).astype(BF16)
    zeros = jnp.zeros((CONV_PAD, 2 * FFN_CHUNK), F32)
    for pad_ref in (pad_a, pad_b):
        pad_ref[0:CONV_PAD, :] = zeros
        pad_ref[CONV_PAD + ROW_TILE:, :] = zeros
    sub = lax.broadcasted_iota(jnp.int32, (8, 1), 0)
    is_prompt = i < N_ROW_TILES_P
    keep_first = jnp.where((sub == 0) & is_prompt, 0.0, 1.0)
    keep_last = jnp.where((sub == 7) & is_prompt, 0.0, 1.0)

    def cut_sequences(v, keep, row):
        parts, at = [], 0
        for b in range(SEQ, ROW_TILE, SEQ):
            lo = b if row == 0 else b - 8
            parts += [v[at:lo], v[lo:lo + 8] * keep]
            at = lo + 8
        return jnp.concatenate(parts + [v[at:]], axis=0)

    def cols(ref, c):
        off = pl.multiple_of(c * FFN_CHUNK, FFN_CHUNK)
        return ref[:, pl.ds(off, FFN_CHUNK)], ref[:, pl.ds(D_FF + off, FFN_CHUNK)]

    def up_proj(c, pad_ref):
        hb = h_ref[...]
        wg, wv = cols(wup_ref, c)
        pad_ref[CONV_PAD:CONV_PAD + ROW_TILE, :FFN_CHUNK] = jnp.dot(hb, wg, preferred_element_type=F32)
        pad_ref[CONV_PAD:CONV_PAD + ROW_TILE, FFN_CHUNK:] = jnp.dot(hb, wv, preferred_element_type=F32)

    def conv_act(c, pad_ref):
        up = pad_ref[CONV_PAD:CONV_PAD + ROW_TILE, :]
        prev = cut_sequences(pad_ref[CONV_PAD - 1:CONV_PAD - 1 + ROW_TILE, :], keep_first, 0)
        nxt = cut_sequences(pad_ref[CONV_PAD + 1:CONV_PAD + 1 + ROW_TILE, :], keep_last, 7)
        cw = jnp.concatenate(cols(cw_ref, c), axis=-1)
        cb = jnp.concatenate(cols(cb_ref, c), axis=-1)
        conv = prev * cw[0:1, :] + up * cw[1:2, :] + nxt * cw[2:3, :] + cb
        gate = conv[:, :FFN_CHUNK]
        act = _silu(gate) * conv[:, FFN_CHUNK:]
        act_ref[:, pl.ds(pl.multiple_of(c * FFN_CHUNK, FFN_CHUNK), FFN_CHUNK)] = act.astype(BF16)

    up_proj(0, pad_a)

    def body(k, carry):
        c = 2 * k
        up_proj(c + 1, pad_b)
        conv_act(c, pad_a)
        up_proj(c + 2, pad_a)
        conv_act(c + 1, pad_b)
        return carry

    lax.fori_loop(0, (FFN_CHUNKS - 1) // 2, body, 0)
    conv_act(FFN_CHUNKS - 1, pad_a)
    y = jnp.dot(act_ref[...], wd_ref[...], preferred_element_type=F32)
    o_ref[...] = _read_rows(x_refs) + mod_ref[5:6, :] * y


def conv_ffn_residual(x, layer, norm_g, mods, w_up, conv_w, conv_b, w_down):
    split_x = isinstance(x, tuple)
    x_ops, x_specs = _x_operands(x)
    once = pl.Buffered(1)
    vmem = (4 * ROW_TILE * D * 4 + ROW_TILE * D * 2 + 2 * (ROW_TILE + 2 * CONV_PAD) * 2 * FFN_CHUNK * 4
            + ROW_TILE * D_FF * 2 + 3 * D * D_FF * 2 + 5 * ROW_TILE * 2 * FFN_CHUNK * 4)
    return pl.pallas_call(
        functools.partial(_ffn_kernel, split_x=split_x),
        out_shape=jax.ShapeDtypeStruct((ROWS, D), F32),
        grid=(N_ROW_TILES,),
        in_specs=x_specs + [
            pl.BlockSpec((None, 1, D), lambda i: (layer, 0, 0)),
            pl.BlockSpec((None, None, 6, D), lambda i: (layer, _mod_row(i), 0, 0)),
            pl.BlockSpec((None, D, 2 * D_FF), lambda i: (layer, 0, 0), pipeline_mode=once),
            pl.BlockSpec((None, 3, 2 * D_FF), lambda i: (layer, 0, 0), pipeline_mode=once),
            pl.BlockSpec((None, 1, 2 * D_FF), lambda i: (layer, 0, 0), pipeline_mode=once),
            pl.BlockSpec((None, D_FF, D), lambda i: (layer, 0, 0), pipeline_mode=once),
        ],
        out_specs=pl.BlockSpec((ROW_TILE, D), lambda i: (i, 0)),
        scratch_shapes=[pltpu.VMEM((ROW_TILE, D), BF16),
                        pltpu.VMEM((ROW_TILE + 2 * CONV_PAD, 2 * FFN_CHUNK), F32),
                        pltpu.VMEM((ROW_TILE + 2 * CONV_PAD, 2 * FFN_CHUNK), F32),
                        pltpu.VMEM((ROW_TILE, D_FF), BF16)],
        compiler_params=_cparams(("arbitrary",), vmem),
        name="conv_ffn",
    )(*x_ops, norm_g.reshape(DEPTH, 1, D), mods, w_up, conv_w, conv_b.reshape(DEPTH, 1, 2 * D_FF), w_down)


def _final_norm_kernel(x_ref, g_ref, o_ref):
    x = x_ref[...]
    o_ref[...] = x * lax.rsqrt(jnp.mean(x * x, axis=-1, keepdims=True) + NORM_EPS) * g_ref[...]


def final_norm(x, g, first_tile, n_tiles):
    return pl.pallas_call(
        _final_norm_kernel,
        out_shape=jax.ShapeDtypeStruct((n_tiles * ROW_TILE, D), F32),
        grid=(n_tiles,),
        in_specs=[pl.BlockSpec((ROW_TILE, D), lambda i: (first_tile + i, 0)),
                  pl.BlockSpec((1, D), lambda i: (0, 0))],
        out_specs=pl.BlockSpec((ROW_TILE, D), lambda i: (i, 0)),
        compiler_params=_cparams(("arbitrary",), 4 * ROW_TILE * D * 4),
        name="final_norm",
    )(x, g.reshape(1, D))


NQ = ATTN_HEADS * HEAD_DIM
NKV = ATTN_KV_HEADS * HEAD_DIM
Q_BLOCK = 128
MASKED = -1e30
LOG2E = 1.0 / math.log(2.0)
Q_PRESCALE = HEAD_DIM ** -0.5 * LOG2E


def _dot_t(a, b):
    return lax.dot_general(a.astype(BF16), b.astype(BF16), (((1,), (1,)), ((), ())),
                           preferred_element_type=F32)


def _group_rows(q):
    return jnp.concatenate([q[:, g * HEAD_DIM:(g + 1) * HEAD_DIM] for g in range(ATTN_GROUP)], axis=0)


def _sink_lanes(sink_ref, h, rows):
    return LOG2E * jnp.concatenate(
        [jnp.broadcast_to(sink_ref[0:1, ATTN_GROUP * h + g:ATTN_GROUP * h + g + 1], (1, rows))
         for g in range(ATTN_GROUP)], axis=-1)


ONES_ROWS = 16


def _values_t(v):
    ones = jnp.ones((ONES_ROWS, v.shape[0]), F32)
    out = []
    for c in range(NKV // 128):
        vt = v[:, c * 128:(c + 1) * 128].T
        out += [jnp.concatenate([vt[j * HEAD_DIM:(j + 1) * HEAD_DIM], ones], axis=0).astype(BF16)
                for j in range(128 // HEAD_DIM)]
    return out


def _softmax_pv(q4, key_sets, sink2):
    scores = []
    for k, _, bias in key_sets:
        s = _dot_t(k, q4)
        if bias is not None:
            s = jnp.concatenate([s[c * 128:(c + 1) * 128] if bc is None else s[c * 128:(c + 1) * 128] + bc
                                 for c, bc in enumerate(bias)], axis=0)
        scores.append(s)
    m = sink2
    for s in scores:
        m = jnp.maximum(m, jnp.max(s, axis=0, keepdims=True))
    acc = None
    for s, (_, v1t, _) in zip(scores, key_sets):
        t = jnp.dot(v1t, jnp.exp2(s - m).astype(BF16), preferred_element_type=F32)
        acc = t if acc is None else acc + t
    denom = acc[HEAD_DIM:HEAD_DIM + 1] + jnp.exp2(sink2 - m)
    return acc[:HEAD_DIM] * (1.0 / denom)


def _heads_to_columns(o_t, rows):
    slabs = []
    for g in range(0, ATTN_GROUP, 128 // HEAD_DIM):
        pair = jnp.concatenate([o_t[:, (g + j) * rows:(g + j + 1) * rows] for j in range(128 // HEAD_DIM)], axis=0)
        slabs.append(pair.T)
    return jnp.concatenate(slabs, axis=-1)


def _ctx_attn_kernel(qkv_ref, sink_ref, o_ref):
    v1t = _values_t(qkv_ref[:, NQ + NKV:])
    outs = []
    for h in range(ATTN_KV_HEADS):
        k = qkv_ref[:, NQ + h * HEAD_DIM:NQ + (h + 1) * HEAD_DIM].astype(BF16)
        q4 = _group_rows(qkv_ref[:, ATTN_GROUP * h * HEAD_DIM:ATTN_GROUP * (h + 1) * HEAD_DIM] * Q_PRESCALE)
        o_t = _softmax_pv(q4.astype(BF16), [(k, v1t[h], None)], _sink_lanes(sink_ref, h, SEQ))
        outs.append(_heads_to_columns(o_t, SEQ))
    o_ref[...] = jnp.concatenate(outs, axis=-1).astype(BF16)


def context_attention(qkv, sink):
    return pl.pallas_call(
        _ctx_attn_kernel,
        out_shape=jax.ShapeDtypeStruct((ROWS_P, NQ), BF16),
        grid=(BATCH,),
        in_specs=[pl.BlockSpec((SEQ, NQ + 2 * NKV), lambda b: (b, 0)),
                  pl.BlockSpec((1, ATTN_HEADS), lambda b: (0, 0))],
        out_specs=pl.BlockSpec((SEQ, NQ), lambda b: (b, 0)),
        compiler_params=_cparams(("arbitrary",), 2 * SEQ * (2 * NQ + 2 * NKV) * 4 + 24 * SEQ * ATTN_GROUP * SEQ * 4),
        name="context_attention",
    )(qkv, sink.reshape(1, ATTN_HEADS))


def _rope(x, cos, sin_a, sin_b):
    outs = []
    for c in range(x.shape[1] // 128):
        s = x[:, c * 128:(c + 1) * 128]
        outs.append(s * cos + pltpu.roll(s, 128 - HEAD_DIM // 4, 1) * sin_a + pltpu.roll(s, HEAD_DIM // 4, 1) * sin_b)
    return jnp.concatenate(outs, axis=-1)


def _lat_attn_kernel(q_ref, kp_ref, kc_ref, kn_ref, vp_ref, vc_ref, vn_ref, ck_ref, cv_ref,
                     cos_ref, sa_ref, sb_ref, sink_ref, o_ref):
    n = pl.program_id(1)
    nb = pl.num_programs(1)

    def tables(blk):
        r = pl.ds(pl.multiple_of(blk * Q_BLOCK, Q_BLOCK), Q_BLOCK)
        return cos_ref[r, :], sa_ref[r, :], sb_ref[r, :]

    cos, sa, sb = tables(n)
    qr = (_rope(q_ref[...], cos, sa, sb) * Q_PRESCALE).astype(BF16)
    k3 = jnp.concatenate([
        _rope(kp_ref[...], *tables(jnp.maximum(n - 1, 0))),
        _rope(kc_ref[...], cos, sa, sb),
        _rope(kn_ref[...], *tables(jnp.minimum(n + 1, nb - 1)))], axis=0).astype(BF16)
    v3 = jnp.concatenate([vp_ref[...], vc_ref[...], vn_ref[...]], axis=0)

    cols = ATTN_GROUP * Q_BLOCK
    koff = lax.broadcasted_iota(jnp.int32, (Q_BLOCK, cols), 0)
    qoff = lax.broadcasted_iota(jnp.int32, (Q_BLOCK, cols), 1) & (Q_BLOCK - 1)
    bias_prev = jnp.where((koff >= qoff) & (n > 0), 0.0, MASKED)
    bias_next = jnp.where((koff <= qoff) & (n < nb - 1), 0.0, MASKED)

    v1t = _values_t(v3)
    cv1t = _values_t(cv_ref[...])
    outs = []
    for h in range(ATTN_KV_HEADS):
        hs = slice(h * HEAD_DIM, (h + 1) * HEAD_DIM)
        q4 = _group_rows(qr[:, ATTN_GROUP * h * HEAD_DIM:ATTN_GROUP * (h + 1) * HEAD_DIM])
        o_t = _softmax_pv(q4, [(k3[:, hs], v1t[h], [bias_prev, None, bias_next]),
                               (ck_ref[:, hs].astype(BF16), cv1t[h], None)],
                          _sink_lanes(sink_ref, h, Q_BLOCK))
        outs.append(_heads_to_columns(o_t, Q_BLOCK))
    o_ref[...] = jnp.concatenate(outs, axis=-1).astype(BF16)


def _rope_tables():
    t = np.arange(DEC_SEQ)
    half = HEAD_DIM // 2
    inv_freq = 1.0 / (ROPE_BASE ** (np.arange(0, half, 2, dtype=np.float32) / half))
    ar = (t // GRID_W).astype(np.float32)[:, None] * inv_freq
    ac = (t % GRID_W).astype(np.float32)[:, None] * inv_freq
    return jnp.concatenate([jnp.asarray(a) for a in (ar, ar, ac, ac)] * 2, axis=-1)


def latent_attention(qkv, cache_k, cache_v, layer_j, sink):
    ang = _rope_tables()
    cos, sin = jnp.cos(ang), jnp.sin(ang)
    first = (lax.broadcasted_iota(jnp.int32, ang.shape, 1) % (HEAD_DIM // 2)) < HEAD_DIM // 4
    sin_a = jnp.where(first, -sin, 0.0)
    sin_b = jnp.where(first, 0.0, sin)
    nb = DEC_SEQ // Q_BLOCK
    base = ROWS_P // Q_BLOCK
    kcol, vcol = NQ // NKV, NQ // NKV + 1
    ck = cache_k.reshape(DEC_BATCH, -1, PAST_LEN, NKV)
    cv = cache_v.reshape(DEC_BATCH, -1, PAST_LEN, NKV)

    def kv_spec(col, off):
        return pl.BlockSpec((Q_BLOCK, NKV),
                            lambda b, n: (base + b * nb + jnp.clip(n + off, 0, nb - 1), col))

    table = pl.BlockSpec((DEC_SEQ, 128), lambda b, n: (0, 0))
    return pl.pallas_call(
        _lat_attn_kernel,
        out_shape=jax.ShapeDtypeStruct((ROWS_S, NQ), BF16),
        grid=(DEC_BATCH, nb),
        in_specs=[pl.BlockSpec((Q_BLOCK, NQ), lambda b, n: (base + b * nb + n, 0)),
                  kv_spec(kcol, -1), kv_spec(kcol, 0), kv_spec(kcol, 1),
                  kv_spec(vcol, -1), kv_spec(vcol, 0), kv_spec(vcol, 1),
                  pl.BlockSpec((None, None, PAST_LEN, NKV), lambda b, n: (b, layer_j, 0, 0)),
                  pl.BlockSpec((None, None, PAST_LEN, NKV), lambda b, n: (b, layer_j, 0, 0)),
                  table, table, table,
                  pl.BlockSpec((1, ATTN_HEADS), lambda b, n: (0, 0))],
        out_specs=pl.BlockSpec((Q_BLOCK, NQ), lambda b, n: (b * nb + n, 0)),
        compiler_params=_cparams(("arbitrary", "arbitrary"),
                                 4 * Q_BLOCK * NQ * 4 + 12 * Q_BLOCK * NKV * 4 + 4 * PAST_LEN * NKV * 4
                                 + 6 * DEC_SEQ * 128 * 4 + 24 * ATTN_GROUP * Q_BLOCK * (3 * Q_BLOCK + PAST_LEN) * 4),
        name="latent_attention",
    )(qkv, qkv, qkv, qkv, qkv, qkv, qkv, ck, cv, cos, sin_a, sin_b, sink.reshape(1, ATTN_HEADS))


def attention_layer(x, layer, layer_j, mods, norm_g, wqkv, wo, sink, cache_k, cache_v):
    qkv = norm_mod_matmul(x, layer, norm_g, mods, wqkv, layer_j, "attn_qkv")
    a_p = context_attention(qkv, sink)
    a_s = latent_attention(qkv, cache_k, cache_v, layer_j, sink)
    x = matmul_gated_residual(a_p, a_s, wo, layer_j, x, layer, mods, "attn_wo")
    new_k = qkv[:ROWS_P, NQ:NQ + NKV].reshape(BATCH, SEQ, ATTN_KV_HEADS, HEAD_DIM)
    new_v = qkv[:ROWS_P, NQ + NKV:].reshape(BATCH, SEQ, ATTN_KV_HEADS, HEAD_DIM)
    return x, new_k, new_v


HG_TILE = 256
HG_LEVELS = 8
HG_IN = 3 * 1024 + 2 * 1024
HG_OUT_TILE = 512
HG_HEADS_PER_STEP = 4


def _hgrn_consts():
    t = np.arange(HG_TILE)
    x = t[:, None] ^ t[None, :]
    hb = np.where(x == 0, -1, np.floor(np.log2(np.maximum(x, 1))).astype(np.int64))
    later = t[:, None] > t[None, :]
    half = HG_TILE // 2
    masks, tris = [], []
    for reverse in (False, True):
        side = ~later & (x != 0) if reverse else later
        lv = [hb == -1] + [(hb == lvl) & side for lvl in range(HG_LEVELS - 1)]
        masks.append(np.stack([m[:half, :half] for m in lv]).astype(np.float32))
        tris.append((t[None, :] >= t[:, None]) if reverse else (t[None, :] <= t[:, None]))
    return jnp.asarray(np.stack(masks)), jnp.asarray(np.stack(tris).astype(np.float32), dtype=BF16)


def _split_bf16(x):
    def top(v):
        bits = lax.bitcast_convert_type(v, jnp.uint32) & jnp.uint32(0xFFFF0000)
        return lax.bitcast_convert_type(bits, F32)

    hi = top(x)
    r = x - hi
    mid = top(r)
    return hi.astype(BF16), mid.astype(BF16), (r - mid).astype(BF16)


def _block_row(x, blk, idx):
    t = x.shape[0]
    x3 = x.reshape(t // blk, blk, x.shape[1])
    return jnp.broadcast_to(x3[:, idx:idx + 1, :], x3.shape).reshape(x.shape)


def _lower_bound(lb_ref, layer, direction):
    x = lb_ref[direction]
    e = jnp.exp(x - jnp.max(x, axis=0, keepdims=True))
    p = e / jnp.sum(e, axis=0, keepdims=True)
    return jnp.sum(p[1:layer + 1, :], axis=0, keepdims=True)


def _hgrn_tile(q, v, z, lb, s_in, mask_ref, tri, reverse):
    t = HG_TILE
    lo, hi = slice(0, t // 2), slice(t // 2, t)
    sg = _sigmoid(z)
    f = lb + (1.0 - lb) * sg
    k = (1.0 - lb) * (1.0 - sg)
    lf3 = _split_bf16(jnp.log(f))
    cum = sum(jnp.dot(tri, p, preferred_element_type=F32) for p in lf3)
    cum2 = cum * LOG2E
    rows = lax.broadcasted_iota(jnp.int32, (t, 1), 0)
    att = [mask_ref[0] * _dot_t(q[r], k[r]) for r in (lo, hi)]
    top = None
    for lvl in range(HG_LEVELS):
        half = 1 << lvl
        bit = (rows & half) != 0
        qside = ~bit if reverse else bit
        if lvl == 0:
            e = jnp.where(qside, f, 1.0)
        else:
            ref = _block_row(cum2, 2 * half, half if reverse else half - 1)
            e = jnp.exp2(-jnp.abs(cum2 - ref))
        w = (jnp.where(qside, q, k) * e).astype(BF16)
        if lvl < HG_LEVELS - 1:
            att = [a + mask_ref[lvl + 1] * _dot_t(w[r], w[r]) for a, r in zip(att, (lo, hi))]
        else:
            top = _dot_t(w[lo], w[hi]) if reverse else _dot_t(w[hi], w[lo])
    vb = v.astype(BF16)
    o_lo, o_hi = _bdot(att[0], vb[lo]), _bdot(att[1], vb[hi])
    if reverse:
        o_lo = o_lo + _bdot(top, vb[hi])
    else:
        o_hi = o_hi + _bdot(top, vb[lo])
    o = jnp.concatenate([o_lo, o_hi], axis=0)
    last = cum[0:1, :] if reverse else cum[t - 1:t, :]
    kd = (k * jnp.exp(last - cum)).astype(BF16)
    s_out = lax.dot_general(kd, vb, (((0,), (0,)), ((), ())), preferred_element_type=F32)
    if s_in is not None:
        o = o + _bdot(q * jnp.exp(cum), s_in)
        ones = jnp.ones((t, HGRN_DV), BF16)
        last_col = sum(lax.dot_general(p, ones, (((0,), (0,)), ((), ())), preferred_element_type=F32) for p in lf3)
        s_out = jnp.exp(last_col) * s_in + s_out
    return o, s_out


def _hgrn_prompt_kernel(q_ref, v_ref, zf_ref, zb_ref, lb_ref, mask_ref, tri_ref, of_ref, ob_ref, sfin_ref, *, layer):
    for j in range(HG_HEADS_PER_STEP):
        hs = slice(j * HGRN_DK, (j + 1) * HGRN_DK)
        q, v = q_ref[:, hs], v_ref[:, hs]
        lb_f, lb_b = (_lower_bound(lb_ref.at[:, :, hs], layer, d) for d in range(2))
        of_ref[:, hs], sfin_ref[0, j] = _hgrn_tile(q, v, zf_ref[:, hs], lb_f, None, mask_ref.at[0], tri_ref[0], False)
        ob_ref[:, hs], sfin_ref[1, j] = _hgrn_tile(q, v, zb_ref[:, hs], lb_b, None, mask_ref.at[1], tri_ref[1], True)


def _hgrn_sample_kernel(qf_ref, vf_ref, zf_ref, qb_ref, vb_ref, zb_ref, lb_ref, s0_ref, mask_ref, tri_ref,
                        of_ref, ob_ref, state_ref, *, layer):
    tiles = DEC_SEQ // HG_TILE

    @pl.when(pl.program_id(1) % tiles == 0)
    def _():
        state_ref[...] = s0_ref[...]

    for j in range(HG_HEADS_PER_STEP):
        hs = slice(j * HGRN_DK, (j + 1) * HGRN_DK)
        lb_f, lb_b = (_lower_bound(lb_ref.at[:, :, hs], layer, d) for d in range(2))
        of_ref[:, hs], state_ref[0, j] = _hgrn_tile(qf_ref[:, hs], vf_ref[:, hs], zf_ref[:, hs], lb_f,
                                                    state_ref[0, j], mask_ref.at[0], tri_ref[0], False)
        ob_ref[:, hs], state_ref[1, j] = _hgrn_tile(qb_ref[:, hs], vb_ref[:, hs], zb_ref[:, hs], lb_b,
                                                    state_ref[1, j], mask_ref.at[1], tri_ref[1], True)


def hgrn_scan(proj, lb_raw, layer, state_hgrn, layer_j):
    masks, tris = _hgrn_consts()
    hps = HG_HEADS_PER_STEP
    hw = hps * HGRN_DK
    qc, vc, zfc, zbc = 0, 1024 // hw, 2048 // hw, 3072 // hw
    const_specs = [pl.BlockSpec((2, DEPTH, hw), lambda h, i: (0, 0, h))]
    mask_specs = [pl.BlockSpec(masks.shape, lambda h, i: (0, 0, 0, 0)),
                  pl.BlockSpec((2, HG_TILE, HG_TILE), lambda h, i: (0, 0, 0))]
    vmem = 4 * masks.size * 4 + 16 * HG_TILE * hw * 4 + 8 * hw * HGRN_DV * 4 + 24 * hps * HG_TILE * HG_TILE * 4

    def col(cb, row_fn):
        return pl.BlockSpec((HG_TILE, hw), lambda h, i: (row_fn(i), cb + h))

    o_shape = jax.ShapeDtypeStruct((ROWS_P, HGRN_HEADS * HGRN_DV), F32)
    same = lambda i: i
    of_p, ob_p, sfin = pl.pallas_call(
        functools.partial(_hgrn_prompt_kernel, layer=layer),
        out_shape=(o_shape, o_shape, jax.ShapeDtypeStruct((BATCH, 2, HGRN_HEADS, HGRN_DK, HGRN_DV), F32)),
        grid=(HGRN_HEADS // hps, BATCH),
        in_specs=[col(qc, same), col(vc, same), col(zfc, same), col(zbc, same)] + const_specs + mask_specs,
        out_specs=(col(0, same), col(0, same),
                   pl.BlockSpec((None, 2, hps, HGRN_DK, HGRN_DV), lambda h, i: (i, 0, h, 0, 0))),
        compiler_params=_cparams(("arbitrary", "arbitrary"), vmem),
        name="hgrn_scan_prompt",
    )(proj, proj, proj, proj, lb_raw, masks, tris)

    tiles = DEC_SEQ // HG_TILE
    base = ROWS_P // HG_TILE
    bwd = lambda i: (i // tiles) * tiles + (tiles - 1 - i % tiles)
    fwd_in = lambda i: base + i
    bwd_in = lambda i: base + bwd(i)
    o_shape = jax.ShapeDtypeStruct((ROWS_S, HGRN_HEADS * HGRN_DV), F32)
    of_s, ob_s = pl.pallas_call(
        functools.partial(_hgrn_sample_kernel, layer=layer),
        out_shape=(o_shape, o_shape),
        grid=(HGRN_HEADS // hps, DEC_BATCH * tiles),
        in_specs=[col(qc, fwd_in), col(vc, fwd_in), col(zfc, fwd_in), col(qc, bwd_in), col(vc, bwd_in),
                  col(zbc, bwd_in)]
        + const_specs
        + [pl.BlockSpec((None, None, 2, hps, HGRN_DK, HGRN_DV), lambda h, i: (i // tiles, layer_j, 0, h, 0, 0))]
        + mask_specs,
        out_specs=(col(0, same), col(0, bwd)),
        scratch_shapes=[pltpu.VMEM((2, hps, HGRN_DK, HGRN_DV), F32)],
        compiler_params=_cparams(("arbitrary", "arbitrary"), vmem),
        name="hgrn_scan_sample",
    )(proj, proj, proj, proj, proj, proj, lb_raw, state_hgrn, masks, tris)
    return (of_p, ob_p), (of_s, ob_s), sfin


def _hgrn_out_kernel(ofp_ref, obp_ref, ofs_ref, obs_ref, g_ref, gn_ref, w_ref, x_ref, mod_ref, o_ref):
    def run(of_ref, ob_ref):
        gn = gn_ref[...]
        parts = []
        for h in range(HGRN_HEADS):
            hs = slice(h * HGRN_DV, (h + 1) * HGRN_DV)
            o = of_ref[:, hs] + ob_ref[:, hs]
            g = g_ref[:, hs]
            o = o * lax.rsqrt(jnp.mean(o * o, axis=-1, keepdims=True) + NORM_EPS) * gn * _silu(g)
            parts.append(o.astype(BF16))
        y = jnp.dot(jnp.concatenate(parts, axis=-1), w_ref[...], preferred_element_type=F32)
        o_ref[...] = x_ref[...] + mod_ref[2:3, :] * y

    is_prompt = pl.program_id(0) < ROWS_P // HG_OUT_TILE
    pl.when(is_prompt)(lambda: run(ofp_ref, obp_ref))
    pl.when(jnp.logical_not(is_prompt))(lambda: run(ofs_ref, obs_ref))


def hgrn_out(o_p, o_s, proj, g_norm, wo, w_idx, x, layer, mods):
    gcol = 4096 // D
    tile = HG_OUT_TILE
    p_spec, s_spec = _group_specs(D, tile)
    return pl.pallas_call(
        _hgrn_out_kernel,
        out_shape=jax.ShapeDtypeStruct((ROWS, D), F32),
        grid=(ROWS // tile,),
        in_specs=[p_spec, p_spec, s_spec, s_spec,
                  pl.BlockSpec((tile, D), lambda i: (i, gcol)),
                  pl.BlockSpec((1, HGRN_DV), lambda i: (0, 0)),
                  pl.BlockSpec((None, D, D), lambda i: (w_idx, 0, 0), pipeline_mode=pl.Buffered(1)),
                  pl.BlockSpec((tile, D), lambda i: (i, 0)),
                  _mod_spec(layer, 1, tile)],
        out_specs=pl.BlockSpec((tile, D), lambda i: (i, 0)),
        compiler_params=_cparams(("arbitrary",), 22 * tile * D * 4),
        name="hgrn_out",
    )(*o_p, *o_s, proj, g_norm.reshape(1, HGRN_DV), wo, x, mods)


def hgrn_layer(x, layer, layer_j, mods, norm_g, w_in, hgrn_lb, g_norm, wo, state_hgrn):
    proj = norm_mod_matmul(x, layer, norm_g, mods, w_in, layer_j, "hgrn_in")
    o_p, o_s, sfin = hgrn_scan(proj, jnp.transpose(hgrn_lb, (1, 0, 2)), layer, state_hgrn, layer_j)
    x = hgrn_out(o_p, o_s, proj, g_norm[layer_j], wo, layer_j, x, layer, mods)
    return x, sfin


SSM_N = SSM_GROUPS * SSM_STATE
SSM_KT = 8
SSM_ROWS = 256


def _ssm_prep_kernel(are_ref, aim_ref, ldt_ref, bre_ref, bim_ref, lre_ref, lim_ref, bbre_ref, bbim_ref):
    a_re = jnp.minimum(are_ref[...], -1e-4)
    a_im = aim_ref[...]
    dt = jnp.exp(ldt_ref[...])
    mag = jnp.exp(a_re * dt)
    l_re = mag * jnp.cos(a_im * dt)
    l_im = mag * jnp.sin(a_im * dt)
    lre_ref[...] = l_re
    lim_ref[...] = l_im
    den = a_re * a_re + a_im * a_im
    c_re = ((l_re - 1.0) * a_re + l_im * a_im) / den
    c_im = (l_im * a_re - (l_re - 1.0) * a_im) / den
    b_re, b_im = bre_ref[...], bim_ref[...]
    bbre_ref[...] = c_re[:, None, :] * b_re - c_im[:, None, :] * b_im
    bbim_ref[...] = c_re[:, None, :] * b_im + c_im[:, None, :] * b_re


def ssm_discretize(a_re, a_im, log_dt, b_re, b_im):
    g2 = 2 * SSM_GROUPS
    sh = jax.ShapeDtypeStruct((g2, SSM_STATE), F32)
    shb = jax.ShapeDtypeStruct((g2, SSM_GROUP, SSM_STATE), F32)
    bt = lambda b: jnp.transpose(b, (0, 1, 3, 2)).reshape(g2, SSM_GROUP, SSM_STATE)
    return pl.pallas_call(_ssm_prep_kernel, out_shape=(sh, sh, shb, shb), name="ssm_discretize")(
        a_re.reshape(g2, SSM_STATE), a_im.reshape(g2, SSM_STATE), log_dt.reshape(g2, 1), bt(b_re), bt(b_im))


def _ssm_block_diag(l_re, l_im, bb_re, bb_im, c_re, c_im):
    nk = SSM_GROUPS // SSM_KT
    eye = jnp.eye(SSM_KT, dtype=F32)

    def bmat(b):
        b = b.reshape(2, nk, SSM_KT, SSM_GROUP, SSM_STATE)
        return jnp.einsum('dkgip,gh->dkgihp', b, eye).reshape(2, nk, SSM_KT * SSM_GROUP, SSM_KT * SSM_STATE)

    def cmat(c):
        c = c.reshape(2, nk, SSM_KT, SSM_GROUP, SSM_STATE)
        return jnp.einsum('dkgip,gh->dkhpgi', c, eye).reshape(2, nk, SSM_KT * SSM_STATE, SSM_KT * SSM_GROUP)

    b_mat = jnp.concatenate([bmat(bb_re), bmat(bb_im)], axis=-1).astype(BF16)
    c_mat = jnp.concatenate([cmat(c_re), cmat(-c_im)], axis=-2).astype(BF16)
    lam = jnp.stack([l_re.reshape(2, SSM_N), l_im.reshape(2, SSM_N)], axis=1)
    return b_mat, c_mat, lam


X4_SHAPE = (ROWS // (4 * SEQ), 4, SEQ, D)


def _tm_geometry(prompt):
    if prompt:
        batch = BATCH
        steps = SSM_ROWS // batch
        return batch, steps, (4, 4, steps, D), (lambda i: (0, 0, i, 0)), SEQ // steps, (0,) * batch
    batch = DEC_BATCH
    steps = SSM_ROWS // batch
    per_q = SEQ // steps
    return (batch, steps, (4, 1, steps, D), (lambda i: (1, i // per_q, i % per_q, 0)), DEC_SEQ // steps,
            tuple(range(1, 1 + batch)))


def _x4_seq(ref, b):
    return ref.at[b // ref.shape[1], b % ref.shape[1]]


LANE_SLABS = D // 128


def _slab_store(s_ref, rows, val):
    for c in range(LANE_SLABS):
        s_ref[c, rows, :] = val[:, c * 128:(c + 1) * 128]


def _slab_load(s_ref, rows):
    return jnp.concatenate([s_ref[c, rows, :] for c in range(LANE_SLABS)], axis=-1)


def _normmod_tm_kernel(x_ref, g_ref, mod_ref, o_ref, s_ref, *, batch, steps, mod_rows):
    g = g_ref[...]
    for b in range(batch):
        m = mod_ref.at[mod_rows[b]]
        _slab_store(s_ref, slice(b * steps, (b + 1) * steps),
                    _norm_mod(_x4_seq(x_ref, b)[...], g, m[0:1, :], m[1:2, :]))
    for t in range(steps):
        o_ref[t * batch:(t + 1) * batch, :] = _slab_load(s_ref, pl.ds(t, batch, stride=steps))


def norm_mod_time_major(x4, layer, norm_g, mods, prompt):
    batch, steps, blk, idx, tiles, mod_rows = _tm_geometry(prompt)
    return pl.pallas_call(
        functools.partial(_normmod_tm_kernel, batch=batch, steps=steps, mod_rows=mod_rows),
        out_shape=jax.ShapeDtypeStruct((tiles * SSM_ROWS, D), F32),
        grid=(tiles,),
        in_specs=[pl.BlockSpec(blk, idx),
                  pl.BlockSpec((None, 1, D), lambda i: (layer, 0, 0)),
                  pl.BlockSpec((None, MOD_ROWS, 6, D), lambda i: (layer, 0, 0, 0))],
        out_specs=pl.BlockSpec((SSM_ROWS, D), lambda i: (i, 0)),
        scratch_shapes=[pltpu.VMEM((LANE_SLABS, SSM_ROWS, 128), F32)],
        compiler_params=_cparams(("arbitrary",), 8 * SSM_ROWS * D * 4),
        name="ssm_norm_time_major",
    )(x4, norm_g.reshape(DEPTH, 1, D), mods)


def _ssm_scan_kernel(xf_ref, xb_ref, bm_ref, cm_ref, lam_ref, h0_ref, yf_ref, yb_ref, hfin_ref,
                     hre_f, him_f, hre_b, him_b, st_ref, *, batch):
    i = pl.program_id(0)
    steps = SSM_ROWS // batch
    nk = SSM_GROUPS // SSM_KT
    kw = SSM_KT * SSM_STATE
    x_refs, y_refs = (xf_ref, xb_ref), (yf_ref, yb_ref)
    h_refs = ((hre_f, him_f), (hre_b, him_b))

    @pl.when(i == 0)
    def _():
        st_ref[...] = h0_ref[...]

    def tile_cols(k):
        return slice(k * kw, (k + 1) * kw)

    def bu_tile(d, k):
        xk = x_refs[d][:, k * 128:(k + 1) * 128].astype(BF16)
        bu = jnp.dot(xk, bm_ref[d, k], preferred_element_type=F32)
        h_refs[d][0][:, tile_cols(k)] = bu[:, :kw]
        h_refs[d][1][:, tile_cols(k)] = bu[:, kw:]

    def c_tile(d, k):
        hk = jnp.concatenate([h_refs[d][0][:, tile_cols(k)], h_refs[d][1][:, tile_cols(k)]], axis=-1)
        y_refs[d][:, k * 128:(k + 1) * 128] = jnp.dot(hk.astype(BF16), cm_ref[d, k], preferred_element_type=F32)

    def scan_tile(d, k):
        hre_ref, him_ref = h_refs[d]
        col = tile_cols(k)
        l_re, l_im = lam_ref[d, 0, :, col], lam_ref[d, 1, :, col]
        h_re, h_im = st_ref[d, 0, :, col], st_ref[d, 1, :, col]
        per = max(8 // batch, 1)
        rows_per = per * batch
        for s in range(steps // per):
            g = (steps // per - 1 - s) if d else s
            rows = slice(g * rows_per, (g + 1) * rows_per)
            cur_re, cur_im = hre_ref[rows, col], him_ref[rows, col]
            outs_re, outs_im = [None] * per, [None] * per
            for r in (range(per - 1, -1, -1) if d else range(per)):
                b_re, b_im = cur_re[r * batch:(r + 1) * batch], cur_im[r * batch:(r + 1) * batch]
                h_re, h_im = l_re * h_re - l_im * h_im + b_re, l_re * h_im + l_im * h_re + b_im
                outs_re[r], outs_im[r] = h_re, h_im
            hre_ref[rows, col] = outs_re[0] if per == 1 else jnp.concatenate(outs_re, axis=0)
            him_ref[rows, col] = outs_im[0] if per == 1 else jnp.concatenate(outs_im, axis=0)
        st_ref[d, 0, :, col] = h_re
        st_ref[d, 1, :, col] = h_im

    for k in range(nk + 2):
        for d in range(2):
            if k < nk:
                bu_tile(d, k)
            if 1 <= k <= nk:
                scan_tile(d, k - 1)
            if k >= 2:
                c_tile(d, k - 2)

    @pl.when(i == pl.num_programs(0) - 1)
    def _():
        hfin_ref[...] = st_ref[...]


def ssm_scan(xn_tm, b_mat, c_mat, lam, h0, batch):
    rows = xn_tm.shape[0]
    n = rows // SSM_ROWS
    lam_b = jnp.broadcast_to(lam[:, :, None, :], (2, 2, batch, SSM_N))
    y_shape = jax.ShapeDtypeStruct((rows, D), F32)
    full = lambda a: pl.BlockSpec(a.shape, lambda i: (0,) * a.ndim)
    vmem = (4 * SSM_ROWS * SSM_N * 4 + 8 * SSM_ROWS * D * 4 + 2 * (b_mat.size + c_mat.size) * 2
            + 12 * batch * SSM_N * 4 * 2 + 8 * SSM_ROWS * 1024 * 4)
    return pl.pallas_call(
        functools.partial(_ssm_scan_kernel, batch=batch),
        out_shape=(y_shape, y_shape, jax.ShapeDtypeStruct((2, 2, batch, SSM_N), F32)),
        grid=(n,),
        in_specs=[pl.BlockSpec((SSM_ROWS, D), lambda i: (i, 0)),
                  pl.BlockSpec((SSM_ROWS, D), lambda i: (n - 1 - i, 0)),
                  full(b_mat), full(c_mat), full(lam_b), full(h0)],
        out_specs=(pl.BlockSpec((SSM_ROWS, D), lambda i: (i, 0)),
                   pl.BlockSpec((SSM_ROWS, D), lambda i: (n - 1 - i, 0)),
                   pl.BlockSpec((2, 2, batch, SSM_N), lambda i: (0, 0, 0, 0))),
        scratch_shapes=[pltpu.VMEM((SSM_ROWS, SSM_N), F32)] * 4
        + [pltpu.VMEM((2, 2, batch, SSM_N), F32)],
        compiler_params=_cparams(("arbitrary",), vmem),
        name="ssm_scan",
    )(xn_tm, xn_tm, b_mat, c_mat, lam_b, h0)


def _gelu_tanh(x):
    return 0.5 * x * (1.0 + jnp.tanh(math.sqrt(2.0 / math.pi) * (x + 0.044715 * (x * x * x))))


def _ssm_glu_kernel(yf_ref, yb_ref, xn_ref, d_ref, w_ref, x_ref, mod_ref, o_ref, s_ref,
                    *, batch, steps, mod_rows):
    g = _gelu_tanh(yf_ref[...] + yb_ref[...] + d_ref[...] * xn_ref[...])
    u = jnp.dot(g.astype(BF16), w_ref[...], preferred_element_type=F32)
    _slab_store(s_ref, slice(None), u[:, :D] * _sigmoid(u[:, D:]))
    for b in range(batch):
        gate = mod_ref[mod_rows[b], 2:3, :]
        _x4_seq(o_ref, b)[...] = (_x4_seq(x_ref, b)[...]
                                  + gate * _slab_load(s_ref, pl.ds(b, steps, stride=batch)))


def ssm_glu(yf, yb, xn, d, w_glu, w_idx, x4, layer, mods, prompt):
    batch, steps, blk, idx, tiles, mod_rows = _tm_geometry(prompt)
    tm_spec = pl.BlockSpec((SSM_ROWS, D), lambda i: (i, 0))
    out = pl.pallas_call(
        functools.partial(_ssm_glu_kernel, batch=batch, steps=steps, mod_rows=mod_rows),
        out_shape=jax.ShapeDtypeStruct((4,) + X4_SHAPE[1:], F32),
        grid=(tiles,),
        in_specs=[tm_spec, tm_spec, tm_spec,
                  pl.BlockSpec((None, 1, D), lambda i: (w_idx, 0, 0)),
                  pl.BlockSpec((None, D, 2 * D), lambda i: (w_idx, 0, 0), pipeline_mode=pl.Buffered(1)),
                  pl.BlockSpec(blk, idx),
                  pl.BlockSpec((None, MOD_ROWS, 6, D), lambda i: (layer, 0, 0, 0))],
        out_specs=pl.BlockSpec(blk, lambda i: (0,) + idx(i)[1:]),
        scratch_shapes=[pltpu.VMEM((LANE_SLABS, SSM_ROWS, 128), F32)],
        compiler_params=_cparams(("arbitrary",), 24 * SSM_ROWS * D * 4 + D * 2 * D * 2),
        name="ssm_glu",
    )(yf, yb, xn, d.reshape(-1, 1, D), w_glu, x4, mods)
    return out.reshape(-1, D)


def ssm_layer(x, layer, layer_j, mods, norm_g, a_re, a_im, log_dt, b_re, b_im, c_re, c_im, d, w_glu, state_ssm):
    l_re, l_im, bb_re, bb_im = ssm_discretize(a_re[layer_j], a_im[layer_j], log_dt[layer_j], b_re[layer_j],
                                              b_im[layer_j])
    b_mat, c_mat, lam = _ssm_block_diag(l_re, l_im, bb_re, bb_im, c_re[layer_j], c_im[layer_j])
    x4 = x.reshape(X4_SHAPE)
    xn_p = norm_mod_time_major(x4, layer, norm_g, mods, True)
    xn_s = norm_mod_time_major(x4, layer, norm_g, mods, False)
    h0_p = jnp.zeros((2, 2, BATCH, SSM_N), F32)
    h0_s = jnp.transpose(state_ssm[:, layer_j].reshape(DEC_BATCH, 2, SSM_N, 2), (1, 3, 0, 2))
    yfp, ybp, hfin = ssm_scan(xn_p, b_mat, c_mat, lam, h0_p, BATCH)
    yfs, ybs, _ = ssm_scan(xn_s, b_mat, c_mat, lam, h0_s, DEC_BATCH)
    out_p = ssm_glu(yfp, ybp, xn_p, d, w_glu, layer_j, x4, layer, mods, True)
    out_s = ssm_glu(yfs, ybs, xn_s, d, w_glu, layer_j, x4, layer, mods, False)
    new_state = jnp.transpose(hfin, (2, 0, 3, 1)).reshape(BATCH, 2, SSM_GROUPS, SSM_STATE, 2)
    return (out_p, out_s), new_state


def kernel(x_prompt, x_sample, cache_k, cache_v, state_hgrn, state_ssm, c, c_ctx, ada_w, ada_b, norm1_g, norm2_g, attn_wqkv, attn_wo, attn_sink, hgrn_w_in, hgrn_lb, hgrn_g_norm, hgrn_wo, ssm_a_re, ssm_a_im, ssm_log_dt, ssm_b_re, ssm_b_im, ssm_c_re, ssm_c_im, ssm_d, ssm_w_glu, ffn_w_up, ffn_conv_w, ffn_conv_b, ffn_w_down, final_g):
    cond8 = jnp.zeros((MOD_ROWS, D), F32).at[0].set(c_ctx).at[1:1 + DEC_BATCH].set(c)
    mods = ada_modulation(cond8, ada_w, ada_b)
    x = (x_prompt.reshape(ROWS_P, D), x_sample.reshape(ROWS_S, D))
    wqkv, wo, w_in, hwo, w_glu, w_up, w_down = (w.astype(BF16) for w in (
        attn_wqkv, attn_wo, hgrn_w_in, hgrn_wo, ssm_w_glu, ffn_w_up, ffn_w_down))
    new_k, new_v, new_hgrn, new_ssm = [], [], [], []
    for l in range(DEPTH):
        kind, j = l % N_MIXERS, l // N_MIXERS
        if kind == 0:
            x, k, v = attention_layer(x, l, j, mods, norm1_g, wqkv, wo, attn_sink[j], cache_k, cache_v)
            new_k.append(k)
            new_v.append(v)
        elif kind == 1:
            x, s = hgrn_layer(x, l, j, mods, norm1_g, w_in, hgrn_lb, hgrn_g_norm, hwo, state_hgrn)
            new_hgrn.append(s)
        else:
            x, s = ssm_layer(x, l, j, mods, norm1_g, ssm_a_re, ssm_a_im, ssm_log_dt, ssm_b_re, ssm_b_im,
                             ssm_c_re, ssm_c_im, ssm_d, w_glu, state_ssm)
            new_ssm.append(s)
        x = conv_ffn_residual(x, l, norm2_g, mods, w_up, ffn_conv_w, ffn_conv_b, w_down)
    y_prompt = final_norm(x, final_g, 0, N_ROW_TILES_P).reshape(BATCH, SEQ, D)
    y_sample = final_norm(x, final_g, N_ROW_TILES_P, N_ROW_TILES - N_ROW_TILES_P).reshape(DEC_BATCH, DEC_SEQ, D)
    return (y_prompt, y_sample, jnp.stack(new_k, axis=1), jnp.stack(new_v, axis=1),
            jnp.stack(new_hgrn, axis=1), jnp.stack(new_ssm, axis=1))
```

```python
import functools
import math

import jax
import jax.numpy as jnp
import numpy as np
from jax import lax
from jax.experimental import pallas as pl
from jax.experimental.pallas import tpu as pltpu

F32 = jnp.float32
BF16 = jnp.bfloat16

D = 1024
BATCH = 16
SEQ = 256
DEPTH = 4
DEC_BATCH = 4
DEC_SEQ = 1024
PAST_LEN = 512
GRID_W = 64
N_MIXERS = 3
ATTN_HEADS = 16
ATTN_KV_HEADS = 4
ATTN_GROUP = ATTN_HEADS // ATTN_KV_HEADS
HEAD_DIM = D // ATTN_HEADS
WINDOW = 128
ROPE_BASE = 10000.0
HGRN_HEADS = 8
HGRN_DK = 128
HGRN_DV = 128
SSM_GROUP = 16
SSM_GROUPS = D // SSM_GROUP
SSM_STATE = 64
D_FF = 2816
NORM_EPS = 1e-6

ROWS_P = BATCH * SEQ
ROWS_S = DEC_BATCH * DEC_SEQ
ROWS = ROWS_P + ROWS_S
ROW_TILE = 1024
N_ROW_TILES = ROWS // ROW_TILE
N_ROW_TILES_P = ROWS_P // ROW_TILE
MOD_ROWS = 8
V7X_VMEM_BYTES = 64 * 1024 * 1024


def _mod_row(i, tile=ROW_TILE):
    return jnp.where(i < ROWS_P // tile, 0, (i - ROWS_P // tile) // (DEC_SEQ // tile) + 1)


def _cparams(semantics, vmem_bytes):
    vmem = int(min(max(vmem_bytes * 5 // 4 + (4 << 20), 16 << 20), V7X_VMEM_BYTES - (6 << 20)))
    return pltpu.CompilerParams(dimension_semantics=semantics, vmem_limit_bytes=vmem)


def _bdot(a, b):
    return jnp.dot(a.astype(BF16), b.astype(BF16), preferred_element_type=F32)


def _norm_mod(x, g, shift, scale):
    y = x * lax.rsqrt(jnp.mean(x * x, axis=-1, keepdims=True) + NORM_EPS) * g
    return y * (1.0 + scale) + shift


def _sigmoid(x):
    return 0.5 + 0.5 * jnp.tanh(0.5 * x)


def _silu(x):
    h = 0.5 * x
    return h + h * jnp.tanh(h)


def _ada_kernel(c_ref, w_ref, b_ref, o_ref):
    c = c_ref[...]
    o_ref[...] = _bdot(_silu(c), w_ref[...]) + b_ref[...]


def ada_modulation(cond8, ada_w, ada_b):
    tn = 1024
    out = pl.pallas_call(
        _ada_kernel,
        out_shape=jax.ShapeDtypeStruct((DEPTH, MOD_ROWS, 6 * D), F32),
        grid=(DEPTH, 6 * D // tn),
        in_specs=[
            pl.BlockSpec((MOD_ROWS, D), lambda l, j: (0, 0)),
            pl.BlockSpec((None, D, tn), lambda l, j: (l, 0, j)),
            pl.BlockSpec((None, 1, tn), lambda l, j: (l, 0, j)),
        ],
        out_specs=pl.BlockSpec((None, MOD_ROWS, tn), lambda l, j: (l, 0, j)),
        compiler_params=_cparams(("arbitrary", "arbitrary"), 2 * D * tn * 4),
        name="ada_modulation",
    )(cond8, ada_w, ada_b.reshape(DEPTH, 1, 6 * D))
    return out.reshape(DEPTH, MOD_ROWS, 6, D)


def _x_operands(x, n_grid=1):
    if isinstance(x, tuple):
        return x, list(_group_specs(D, ROW_TILE, n_grid))
    return (x,), [pl.BlockSpec((ROW_TILE, D), (lambda i: (i, 0)) if n_grid == 1 else (lambda i, j: (i, 0)))]


def _read_rows(x_refs):
    if len(x_refs) == 1:
        return x_refs[0][...]
    return jnp.where(pl.program_id(0) < N_ROW_TILES_P, x_refs[0][...], x_refs[1][...])


def _nmm_kernel(*refs):
    x_refs, (g_ref, mod_ref, w_ref, o_ref, h_ref) = refs[:-5], refs[-5:]

    @pl.when(pl.program_id(1) == 0)
    def _():
        h_ref[...] = _norm_mod(_read_rows(x_refs), g_ref[...], mod_ref[0:1, :], mod_ref[1:2, :]).astype(BF16)

    o_ref[...] = jnp.dot(h_ref[...], w_ref[...], preferred_element_type=F32)


def _mod_spec(layer, n_grid, tile=ROW_TILE):
    if n_grid == 1:
        return pl.BlockSpec((None, None, 6, D), lambda i: (layer, _mod_row(i, tile), 0, 0))
    return pl.BlockSpec((None, None, 6, D), lambda i, j: (layer, _mod_row(i, tile), 0, 0))


def norm_mod_matmul(x, layer, norm_g, mods, w, w_idx, name):
    n = w.shape[-1]
    tn = 1024 if n % 1024 == 0 else 768
    x_ops, x_specs = _x_operands(x, 2)
    return pl.pallas_call(
        _nmm_kernel,
        out_shape=jax.ShapeDtypeStruct((ROWS, n), F32),
        grid=(N_ROW_TILES, n // tn),
        in_specs=x_specs + [
            pl.BlockSpec((None, 1, D), lambda i, j: (layer, 0, 0)),
            _mod_spec(layer, 2),
            pl.BlockSpec((None, D, tn), lambda i, j: (w_idx, 0, j)),
        ],
        out_specs=pl.BlockSpec((ROW_TILE, tn), lambda i, j: (i, j)),
        scratch_shapes=[pltpu.VMEM((ROW_TILE, D), BF16)],
        compiler_params=_cparams(("arbitrary", "arbitrary"),
                                 4 * ROW_TILE * D * 4 + ROW_TILE * D * 2 + 2 * D * tn * 2 + 2 * ROW_TILE * tn * 4),
        name=name,
    )(*x_ops, norm_g.reshape(DEPTH, 1, D), mods, w)


def _mm_res_kernel(ap_ref, as_ref, w_ref, *refs):
    x_refs, (mod_ref, o_ref) = refs[:-2], refs[-2:]

    def run(a_ref, x_ref):
        y = jnp.dot(a_ref[...].astype(BF16), w_ref[...], preferred_element_type=F32)
        o_ref[...] = x_ref[...] + mod_ref[2:3, :] * y

    is_prompt = pl.program_id(0) < N_ROW_TILES_P
    pl.when(is_prompt)(lambda: run(ap_ref, x_refs[0]))
    pl.when(jnp.logical_not(is_prompt))(lambda: run(as_ref, x_refs[-1]))


def _group_specs(k, tile=ROW_TILE, n_grid=1):
    n_p = ROWS_P // tile
    if n_grid == 1:
        return (pl.BlockSpec((tile, k), lambda i: (jnp.minimum(i, n_p - 1), 0)),
                pl.BlockSpec((tile, k), lambda i: (jnp.maximum(i - n_p, 0), 0)))
    return (pl.BlockSpec((tile, k), lambda i, j: (jnp.minimum(i, n_p - 1), 0)),
            pl.BlockSpec((tile, k), lambda i, j: (jnp.maximum(i - n_p, 0), 0)))


def matmul_gated_residual(a_p, a_s, w, w_idx, x, layer, mods, name):
    k = a_p.shape[1]
    x_ops, x_specs = _x_operands(x)
    return pl.pallas_call(
        _mm_res_kernel,
        out_shape=jax.ShapeDtypeStruct((ROWS, D), F32),
        grid=(N_ROW_TILES,),
        in_specs=[
            *_group_specs(k),
            pl.BlockSpec((None, k, D), lambda i: (w_idx, 0, 0), pipeline_mode=pl.Buffered(1)),
            *x_specs,
            _mod_spec(layer, 1),
        ],
        out_specs=pl.BlockSpec((ROW_TILE, D), lambda i: (i, 0)),
        compiler_params=_cparams(("arbitrary",), 4 * ROW_TILE * k * 2 + k * D * 2 + 7 * ROW_TILE * D * 4),
        name=name,
    )(a_p, a_s, w, *x_ops, mods)


FFN_CHUNK = 256
FFN_CHUNKS = D_FF // FFN_CHUNK
FFN_DOWN_GROUP = 4
CONV_PAD = 8


def _ffn_kernel(*refs, split_x):
    x_refs, (g_ref, mod_ref, wup_ref, cw_ref, cb_ref, wd_ref, o_ref, h_ref, pad_a, pad_b, act_ref) = (
        refs[:1 + split_x], refs[1 + split_x:])
    i = pl.program_id(0)
    h_ref[...] = _norm_mod(_read_rows(x_refs), g_ref[...], mod_ref[3:4, :], mod_ref[4:5, :]).astype(BF16)
    zeros = jnp.zeros((CONV_PAD, 2 * FFN_CHUNK), F32)
    for pad_ref in (pad_a, pad_b):
        pad_ref[0:CONV_PAD, :] = zeros
        pad_ref[CONV_PAD + ROW_TILE:, :] = zeros
    sub = lax.broadcasted_iota(jnp.int32, (8, 1), 0)
    is_prompt = i < N_ROW_TILES_P
    keep_first = jnp.where((sub == 0) & is_prompt, 0.0, 1.0)
    keep_last = jnp.where((sub == 7) & is_prompt, 0.0, 1.0)

    def cut_sequences(v, keep, row):
        parts, at = [], 0
        for b in range(SEQ, ROW_TILE, SEQ):
            lo = b if row == 0 else b - 8
            parts += [v[at:lo], v[lo:lo + 8] * keep]
            at = lo + 8
        return jnp.concatenate(parts + [v[at:]], axis=0)

    def cols(ref, c):
        return (ref[:, c * FFN_CHUNK:(c + 1) * FFN_CHUNK],
                ref[:, D_FF + c * FFN_CHUNK:D_FF + (c + 1) * FFN_CHUNK])

    def up_proj(c, pad_ref):
        hb = h_ref[...]
        wg, wv = cols(wup_ref, c)
        pad_ref[CONV_PAD:CONV_PAD + ROW_TILE, :FFN_CHUNK] = jnp.dot(hb, wg, preferred_element_type=F32)
        pad_ref[CONV_PAD:CONV_PAD + ROW_TILE, FFN_CHUNK:] = jnp.dot(hb, wv, preferred_element_type=F32)

    def conv_act(c, pad_ref):
        up = pad_ref[CONV_PAD:CONV_PAD + ROW_TILE, :]
        prev = cut_sequences(pad_ref[CONV_PAD - 1:CONV_PAD - 1 + ROW_TILE, :], keep_first, 0)
        nxt = cut_sequences(pad_ref[CONV_PAD + 1:CONV_PAD + 1 + ROW_TILE, :], keep_last, 7)
        cw = jnp.concatenate(cols(cw_ref, c), axis=-1)
        cb = jnp.concatenate(cols(cb_ref, c), axis=-1)
        conv = prev * cw[0:1, :] + up * cw[1:2, :] + nxt * cw[2:3, :] + cb
        gate = conv[:, :FFN_CHUNK]
        act = _silu(gate) * conv[:, FFN_CHUNK:]
        act_ref[:, c * FFN_CHUNK:(c + 1) * FFN_CHUNK] = act.astype(BF16)

    pads = (pad_a, pad_b)
    done = 0
    up_proj(0, pads[0])
    for c in range(FFN_CHUNKS):
        if c + 1 < FFN_CHUNKS:
            up_proj(c + 1, pads[(c + 1) % 2])
        conv_act(c, pads[c % 2])
        if (c + 1) % FFN_DOWN_GROUP == 0 or c + 1 == FFN_CHUNKS:
            rows = slice(done * FFN_CHUNK, (c + 1) * FFN_CHUNK)
            part = jnp.dot(act_ref[:, rows], wd_ref[rows, :], preferred_element_type=F32)
            if done == 0:
                o_ref[...] = part
            elif c + 1 < FFN_CHUNKS:
                o_ref[...] += part
            else:
                o_ref[...] = _read_rows(x_refs) + mod_ref[5:6, :] * (o_ref[...] + part)
            done = c + 1


def conv_ffn_residual(x, layer, norm_g, mods, w_up, conv_w, conv_b, w_down):
    split_x = isinstance(x, tuple)
    x_ops, x_specs = _x_operands(x)
    once = pl.Buffered(1)
    vmem = (4 * ROW_TILE * D * 4 + ROW_TILE * D * 2 + 2 * (ROW_TILE + 2 * CONV_PAD) * 2 * FFN_CHUNK * 4
            + ROW_TILE * D_FF * 2 + 3 * D * D_FF * 2 + 5 * ROW_TILE * 2 * FFN_CHUNK * 4)
    return pl.pallas_call(
        functools.partial(_ffn_kernel, split_x=split_x),
        out_shape=jax.ShapeDtypeStruct((ROWS, D), F32),
        grid=(N_ROW_TILES,),
        in_specs=x_specs + [
            pl.BlockSpec((None, 1, D), lambda i: (layer, 0, 0)),
            pl.BlockSpec((None, None, 6, D), lambda i: (layer, _mod_row(i), 0, 0)),
            pl.BlockSpec((None, D, 2 * D_FF), lambda i: (layer, 0, 0), pipeline_mode=once),
            pl.BlockSpec((None, 3, 2 * D_FF), lambda i: (layer, 0, 0), pipeline_mode=once),
            pl.BlockSpec((None, 1, 2 * D_FF), lambda i: (layer, 0, 0), pipeline_mode=once),
            pl.BlockSpec((None, D_FF, D), lambda i: (layer, 0, 0), pipeline_mode=once),
        ],
        out_specs=pl.BlockSpec((ROW_TILE, D), lambda i: (i, 0)),
        scratch_shapes=[pltpu.VMEM((ROW_TILE, D), BF16),
                        pltpu.VMEM((ROW_TILE + 2 * CONV_PAD, 2 * FFN_CHUNK), F32),
                        pltpu.VMEM((ROW_TILE + 2 * CONV_PAD, 2 * FFN_CHUNK), F32),
                        pltpu.VMEM((ROW_TILE, D_FF), BF16)],
        compiler_params=_cparams(("arbitrary",), vmem),
        name="conv_ffn",
    )(*x_ops, norm_g.reshape(DEPTH, 1, D), mods, w_up, conv_w, conv_b.reshape(DEPTH, 1, 2 * D_FF), w_down)


def _final_norm_kernel(x_ref, g_ref, o_ref):
    x = x_ref[...]
    o_ref[...] = x * lax.rsqrt(jnp.mean(x * x, axis=-1, keepdims=True) + NORM_EPS) * g_ref[...]


def final_norm(x, g, first_tile, n_tiles):
    return pl.pallas_call(
        _final_norm_kernel,
        out_shape=jax.ShapeDtypeStruct((n_tiles * ROW_TILE, D), F32),
        grid=(n_tiles,),
        in_specs=[pl.BlockSpec((ROW_TILE, D), lambda i: (first_tile + i, 0)),
                  pl.BlockSpec((1, D), lambda i: (0, 0))],
        out_specs=pl.BlockSpec((ROW_TILE, D), lambda i: (i, 0)),
        compiler_params=_cparams(("arbitrary",), 4 * ROW_TILE * D * 4),
        name="final_norm",
    )(x, g.reshape(1, D))


NQ = ATTN_HEADS * HEAD_DIM
NKV = ATTN_KV_HEADS * HEAD_DIM
Q_BLOCK = 128
MASKED = -1e30
LOG2E = 1.0 / math.log(2.0)
Q_PRESCALE = HEAD_DIM ** -0.5 * LOG2E


def _dot_t(a, b):
    return lax.dot_general(a.astype(BF16), b.astype(BF16), (((1,), (1,)), ((), ())),
                           preferred_element_type=F32)


def _group_rows(q):
    return jnp.concatenate([q[:, g * HEAD_DIM:(g + 1) * HEAD_DIM] for g in range(ATTN_GROUP)], axis=0)


def _sink_lanes(sink_ref, h, rows):
    return LOG2E * jnp.concatenate(
        [jnp.broadcast_to(sink_ref[0:1, ATTN_GROUP * h + g:ATTN_GROUP * h + g + 1], (1, rows))
         for g in range(ATTN_GROUP)], axis=-1)


ONES_ROWS = 16


def _values_t(v):
    ones = jnp.ones((ONES_ROWS, v.shape[0]), F32)
    out = []
    for c in range(NKV // 128):
        vt = v[:, c * 128:(c + 1) * 128].T
        out += [jnp.concatenate([vt[j * HEAD_DIM:(j + 1) * HEAD_DIM], ones], axis=0).astype(BF16)
                for j in range(128 // HEAD_DIM)]
    return out


def _softmax_pv(q4, key_sets, sink2):
    scores = []
    for k, _, bias in key_sets:
        s = _dot_t(k, q4)
        if bias is not None:
            s = jnp.concatenate([s[c * 128:(c + 1) * 128] if bc is None else s[c * 128:(c + 1) * 128] + bc
                                 for c, bc in enumerate(bias)], axis=0)
        scores.append(s)
    m = sink2
    for s in scores:
        m = jnp.maximum(m, jnp.max(s, axis=0, keepdims=True))
    acc = None
    for s, (_, v1t, _) in zip(scores, key_sets):
        t = jnp.dot(v1t, jnp.exp2(s - m).astype(BF16), preferred_element_type=F32)
        acc = t if acc is None else acc + t
    denom = acc[HEAD_DIM:HEAD_DIM + 1] + jnp.exp2(sink2 - m)
    return acc[:HEAD_DIM] * (1.0 / denom)


def _heads_to_columns(o_t, rows):
    slabs = []
    for g in range(0, ATTN_GROUP, 128 // HEAD_DIM):
        pair = jnp.concatenate([o_t[:, (g + j) * rows:(g + j + 1) * rows] for j in range(128 // HEAD_DIM)], axis=0)
        slabs.append(pair.T)
    return jnp.concatenate(slabs, axis=-1)


def _ctx_attn_kernel(qkv_ref, sink_ref, o_ref):
    v1t = _values_t(qkv_ref[:, NQ + NKV:])
    outs = []
    for h in range(ATTN_KV_HEADS):
        k = qkv_ref[:, NQ + h * HEAD_DIM:NQ + (h + 1) * HEAD_DIM].astype(BF16)
        q4 = _group_rows(qkv_ref[:, ATTN_GROUP * h * HEAD_DIM:ATTN_GROUP * (h + 1) * HEAD_DIM] * Q_PRESCALE)
        o_t = _softmax_pv(q4.astype(BF16), [(k, v1t[h], None)], _sink_lanes(sink_ref, h, SEQ))
        outs.append(_heads_to_columns(o_t, SEQ))
    o_ref[...] = jnp.concatenate(outs, axis=-1).astype(BF16)


def context_attention(qkv, sink):
    return pl.pallas_call(
        _ctx_attn_kernel,
        out_shape=jax.ShapeDtypeStruct((ROWS_P, NQ), BF16),
        grid=(BATCH,),
        in_specs=[pl.BlockSpec((SEQ, NQ + 2 * NKV), lambda b: (b, 0)),
                  pl.BlockSpec((1, ATTN_HEADS), lambda b: (0, 0))],
        out_specs=pl.BlockSpec((SEQ, NQ), lambda b: (b, 0)),
        compiler_params=_cparams(("arbitrary",), 2 * SEQ * (2 * NQ + 2 * NKV) * 4 + 24 * SEQ * ATTN_GROUP * SEQ * 4),
        name="context_attention",
    )(qkv, sink.reshape(1, ATTN_HEADS))


def _rope(x, cos, sin_a, sin_b):
    outs = []
    for c in range(x.shape[1] // 128):
        s = x[:, c * 128:(c + 1) * 128]
        outs.append(s * cos + pltpu.roll(s, 128 - HEAD_DIM // 4, 1) * sin_a + pltpu.roll(s, HEAD_DIM // 4, 1) * sin_b)
    return jnp.concatenate(outs, axis=-1)


def _lat_attn_kernel(q_ref, kp_ref, kc_ref, kn_ref, vp_ref, vc_ref, vn_ref, ck_ref, cv_ref,
                     cos_ref, sa_ref, sb_ref, sink_ref, o_ref):
    n = pl.program_id(1)
    nb = pl.num_programs(1)

    def tables(blk):
        r = pl.ds(pl.multiple_of(blk * Q_BLOCK, Q_BLOCK), Q_BLOCK)
        return cos_ref[r, :], sa_ref[r, :], sb_ref[r, :]

    cos, sa, sb = tables(n)
    qr = (_rope(q_ref[...], cos, sa, sb) * Q_PRESCALE).astype(BF16)
    k3 = jnp.concatenate([
        _rope(kp_ref[...], *tables(jnp.maximum(n - 1, 0))),
        _rope(kc_ref[...], cos, sa, sb),
        _rope(kn_ref[...], *tables(jnp.minimum(n + 1, nb - 1)))], axis=0).astype(BF16)
    v3 = jnp.concatenate([vp_ref[...], vc_ref[...], vn_ref[...]], axis=0)

    cols = ATTN_GROUP * Q_BLOCK
    koff = lax.broadcasted_iota(jnp.int32, (Q_BLOCK, cols), 0)
    qoff = lax.broadcasted_iota(jnp.int32, (Q_BLOCK, cols), 1) & (Q_BLOCK - 1)
    bias_prev = jnp.where((koff >= qoff) & (n > 0), 0.0, MASKED)
    bias_next = jnp.where((koff <= qoff) & (n < nb - 1), 0.0, MASKED)

    v1t = _values_t(v3)
    cv1t = _values_t(cv_ref[...])
    outs = []
    for h in range(ATTN_KV_HEADS):
        hs = slice(h * HEAD_DIM, (h + 1) * HEAD_DIM)
        q4 = _group_rows(qr[:, ATTN_GROUP * h * HEAD_DIM:ATTN_GROUP * (h + 1) * HEAD_DIM])
        o_t = _softmax_pv(q4, [(k3[:, hs], v1t[h], [bias_prev, None, bias_next]),
                               (ck_ref[:, hs].astype(BF16), cv1t[h], None)],
                          _sink_lanes(sink_ref, h, Q_BLOCK))
        outs.append(_heads_to_columns(o_t, Q_BLOCK))
    o_ref[...] = jnp.concatenate(outs, axis=-1).astype(BF16)


def _rope_tables():
    t = np.arange(DEC_SEQ)
    half = HEAD_DIM // 2
    inv_freq = 1.0 / (ROPE_BASE ** (np.arange(0, half, 2, dtype=np.float32) / half))
    ar = (t // GRID_W).astype(np.float32)[:, None] * inv_freq
    ac = (t % GRID_W).astype(np.float32)[:, None] * inv_freq
    return jnp.concatenate([jnp.asarray(a) for a in (ar, ar, ac, ac)] * 2, axis=-1)


def latent_attention(qkv, cache_k, cache_v, layer_j, sink):
    ang = _rope_tables()
    cos, sin = jnp.cos(ang), jnp.sin(ang)
    first = (lax.broadcasted_iota(jnp.int32, ang.shape, 1) % (HEAD_DIM // 2)) < HEAD_DIM // 4
    sin_a = jnp.where(first, -sin, 0.0)
    sin_b = jnp.where(first, 0.0, sin)
    nb = DEC_SEQ // Q_BLOCK
    base = ROWS_P // Q_BLOCK
    kcol, vcol = NQ // NKV, NQ // NKV + 1
    ck = cache_k.reshape(DEC_BATCH, -1, PAST_LEN, NKV)
    cv = cache_v.reshape(DEC_BATCH, -1, PAST_LEN, NKV)

    def kv_spec(col, off):
        return pl.BlockSpec((Q_BLOCK, NKV),
                            lambda b, n: (base + b * nb + jnp.clip(n + off, 0, nb - 1), col))

    table = pl.BlockSpec((DEC_SEQ, 128), lambda b, n: (0, 0))
    return pl.pallas_call(
        _lat_attn_kernel,
        out_shape=jax.ShapeDtypeStruct((ROWS_S, NQ), BF16),
        grid=(DEC_BATCH, nb),
        in_specs=[pl.BlockSpec((Q_BLOCK, NQ), lambda b, n: (base + b * nb + n, 0)),
                  kv_spec(kcol, -1), kv_spec(kcol, 0), kv_spec(kcol, 1),
                  kv_spec(vcol, -1), kv_spec(vcol, 0), kv_spec(vcol, 1),
                  pl.BlockSpec((None, None, PAST_LEN, NKV), lambda b, n: (b, layer_j, 0, 0)),
                  pl.BlockSpec((None, None, PAST_LEN, NKV), lambda b, n: (b, layer_j, 0, 0)),
                  table, table, table,
                  pl.BlockSpec((1, ATTN_HEADS), lambda b, n: (0, 0))],
        out_specs=pl.BlockSpec((Q_BLOCK, NQ), lambda b, n: (b * nb + n, 0)),
        compiler_params=_cparams(("arbitrary", "arbitrary"),
                                 4 * Q_BLOCK * NQ * 4 + 12 * Q_BLOCK * NKV * 4 + 4 * PAST_LEN * NKV * 4
                                 + 6 * DEC_SEQ * 128 * 4 + 24 * ATTN_GROUP * Q_BLOCK * (3 * Q_BLOCK + PAST_LEN) * 4),
        name="latent_attention",
    )(qkv, qkv, qkv, qkv, qkv, qkv, qkv, ck, cv, cos, sin_a, sin_b, sink.reshape(1, ATTN_HEADS))


def attention_layer(x, layer, layer_j, mods, norm_g, wqkv, wo, sink, cache_k, cache_v):
    qkv = norm_mod_matmul(x, layer, norm_g, mods, wqkv, layer_j, "attn_qkv")
    a_p = context_attention(qkv, sink)
    a_s = latent_attention(qkv, cache_k, cache_v, layer_j, sink)
    x = matmul_gated_residual(a_p, a_s, wo, layer_j, x, layer, mods, "attn_wo")
    new_k = qkv[:ROWS_P, NQ:NQ + NKV].reshape(BATCH, SEQ, ATTN_KV_HEADS, HEAD_DIM)
    new_v = qkv[:ROWS_P, NQ + NKV:].reshape(BATCH, SEQ, ATTN_KV_HEADS, HEAD_DIM)
    return x, new_k, new_v


HG_TILE = 256
HG_LEVELS = 8
HG_IN = 3 * 1024 + 2 * 1024
HG_OUT_TILE = 512
HG_HEADS_PER_STEP = 4


def _hgrn_consts():
    t = np.arange(HG_TILE)
    x = t[:, None] ^ t[None, :]
    hb = np.where(x == 0, -1, np.floor(np.log2(np.maximum(x, 1))).astype(np.int64))
    later = t[:, None] > t[None, :]
    half = HG_TILE // 2
    masks, tris = [], []
    for reverse in (False, True):
        side = ~later & (x != 0) if reverse else later
        lv = [hb == -1] + [(hb == lvl) & side for lvl in range(HG_LEVELS - 1)]
        masks.append(np.stack([m[:half, :half] for m in lv]).astype(np.float32))
        tris.append((t[None, :] >= t[:, None]) if reverse else (t[None, :] <= t[:, None]))
    return jnp.asarray(np.stack(masks)), jnp.asarray(np.stack(tris).astype(np.float32), dtype=BF16)


def _split_bf16(x):
    def top(v):
        bits = lax.bitcast_convert_type(v, jnp.uint32) & jnp.uint32(0xFFFF0000)
        return lax.bitcast_convert_type(bits, F32)

    hi = top(x)
    r = x - hi
    mid = top(r)
    return hi.astype(BF16), mid.astype(BF16), (r - mid).astype(BF16)


def _block_row(x, blk, idx):
    t = x.shape[0]
    x3 = x.reshape(t // blk, blk, x.shape[1])
    return jnp.broadcast_to(x3[:, idx:idx + 1, :], x3.shape).reshape(x.shape)


def _lower_bound(lb_ref, layer, direction):
    x = lb_ref[direction]
    e = jnp.exp(x - jnp.max(x, axis=0, keepdims=True))
    p = e / jnp.sum(e, axis=0, keepdims=True)
    return jnp.sum(p[1:layer + 1, :], axis=0, keepdims=True)


def _hgrn_tile(q, v, z, lb, s_in, mask_ref, tri, reverse):
    t = HG_TILE
    lo, hi = slice(0, t // 2), slice(t // 2, t)
    sg = _sigmoid(z)
    f = lb + (1.0 - lb) * sg
    k = (1.0 - lb) * (1.0 - sg)
    lf3 = _split_bf16(jnp.log(f))
    cum = sum(jnp.dot(tri, p, preferred_element_type=F32) for p in lf3)
    cum2 = cum * LOG2E
    rows = lax.broadcasted_iota(jnp.int32, (t, 1), 0)
    att = [mask_ref[0] * _dot_t(q[r], k[r]) for r in (lo, hi)]
    top = None
    for lvl in range(HG_LEVELS):
        half = 1 << lvl
        bit = (rows & half) != 0
        qside = ~bit if reverse else bit
        if lvl == 0:
            e = jnp.where(qside, f, 1.0)
        else:
            ref = _block_row(cum2, 2 * half, half if reverse else half - 1)
            e = jnp.exp2(-jnp.abs(cum2 - ref))
        w = (jnp.where(qside, q, k) * e).astype(BF16)
        if lvl < HG_LEVELS - 1:
            att = [a + mask_ref[lvl + 1] * _dot_t(w[r], w[r]) for a, r in zip(att, (lo, hi))]
        else:
            top = _dot_t(w[lo], w[hi]) if reverse else _dot_t(w[hi], w[lo])
    vb = v.astype(BF16)
    o_lo, o_hi = _bdot(att[0], vb[lo]), _bdot(att[1], vb[hi])
    if reverse:
        o_lo = o_lo + _bdot(top, vb[hi])
    else:
        o_hi = o_hi + _bdot(top, vb[lo])
    o = jnp.concatenate([o_lo, o_hi], axis=0)
    last = cum[0:1, :] if reverse else cum[t - 1:t, :]
    kd = (k * jnp.exp(last - cum)).astype(BF16)
    s_out = lax.dot_general(kd, vb, (((0,), (0,)), ((), ())), preferred_element_type=F32)
    if s_in is not None:
        o = o + _bdot(q * jnp.exp(cum), s_in)
        ones = jnp.ones((t, HGRN_DV), BF16)
        last_col = sum(lax.dot_general(p, ones, (((0,), (0,)), ((), ())), preferred_element_type=F32) for p in lf3)
        s_out = jnp.exp(last_col) * s_in + s_out
    return o, s_out


def _hgrn_prompt_kernel(q_ref, v_ref, zf_ref, zb_ref, lb_ref, mask_ref, tri_ref, of_ref, ob_ref, sfin_ref, *, layer):
    for j in range(HG_HEADS_PER_STEP):
        hs = slice(j * HGRN_DK, (j + 1) * HGRN_DK)
        q, v = q_ref[:, hs], v_ref[:, hs]
        lb_f, lb_b = (_lower_bound(lb_ref.at[:, :, hs], layer, d) for d in range(2))
        of_ref[:, hs], sfin_ref[0, j] = _hgrn_tile(q, v, zf_ref[:, hs], lb_f, None, mask_ref.at[0], tri_ref[0], False)
        ob_ref[:, hs], sfin_ref[1, j] = _hgrn_tile(q, v, zb_ref[:, hs], lb_b, None, mask_ref.at[1], tri_ref[1], True)


def _hgrn_sample_kernel(qf_ref, vf_ref, zf_ref, qb_ref, vb_ref, zb_ref, lb_ref, s0_ref, mask_ref, tri_ref,
                        of_ref, ob_ref, state_ref, *, layer):
    tiles = DEC_SEQ // HG_TILE

    @pl.when(pl.program_id(1) % tiles == 0)
    def _():
        state_ref[...] = s0_ref[...]

    for j in range(HG_HEADS_PER_STEP):
        hs = slice(j * HGRN_DK, (j + 1) * HGRN_DK)
        lb_f, lb_b = (_lower_bound(lb_ref.at[:, :, hs], layer, d) for d in range(2))
        of_ref[:, hs], state_ref[0, j] = _hgrn_tile(qf_ref[:, hs], vf_ref[:, hs], zf_ref[:, hs], lb_f,
                                                    state_ref[0, j], mask_ref.at[0], tri_ref[0], False)
        ob_ref[:, hs], state_ref[1, j] = _hgrn_tile(qb_ref[:, hs], vb_ref[:, hs], zb_ref[:, hs], lb_b,
                                                    state_ref[1, j], mask_ref.at[1], tri_ref[1], True)


def hgrn_scan(proj, lb_raw, layer, state_hgrn, layer_j):
    masks, tris = _hgrn_consts()
    hps = HG_HEADS_PER_STEP
    hw = hps * HGRN_DK
    qc, vc, zfc, zbc = 0, 1024 // hw, 2048 // hw, 3072 // hw
    const_specs = [pl.BlockSpec((2, DEPTH, hw), lambda h, i: (0, 0, h))]
    mask_specs = [pl.BlockSpec(masks.shape, lambda h, i: (0, 0, 0, 0)),
                  pl.BlockSpec((2, HG_TILE, HG_TILE), lambda h, i: (0, 0, 0))]
    vmem = 4 * masks.size * 4 + 16 * HG_TILE * hw * 4 + 8 * hw * HGRN_DV * 4 + 24 * hps * HG_TILE * HG_TILE * 4

    def col(cb, row_fn):
        return pl.BlockSpec((HG_TILE, hw), lambda h, i: (row_fn(i), cb + h))

    o_shape = jax.ShapeDtypeStruct((ROWS_P, HGRN_HEADS * HGRN_DV), F32)
    same = lambda i: i
    of_p, ob_p, sfin = pl.pallas_call(
        functools.partial(_hgrn_prompt_kernel, layer=layer),
        out_shape=(o_shape, o_shape, jax.ShapeDtypeStruct((BATCH, 2, HGRN_HEADS, HGRN_DK, HGRN_DV), F32)),
        grid=(HGRN_HEADS // hps, BATCH),
        in_specs=[col(qc, same), col(vc, same), col(zfc, same), col(zbc, same)] + const_specs + mask_specs,
        out_specs=(col(0, same), col(0, same),
                   pl.BlockSpec((None, 2, hps, HGRN_DK, HGRN_DV), lambda h, i: (i, 0, h, 0, 0))),
        compiler_params=_cparams(("arbitrary", "arbitrary"), vmem),
        name="hgrn_scan_prompt",
    )(proj, proj, proj, proj, lb_raw, masks, tris)

    tiles = DEC_SEQ // HG_TILE
    base = ROWS_P // HG_TILE
    bwd = lambda i: (i // tiles) * tiles + (tiles - 1 - i % tiles)
    fwd_in = lambda i: base + i
    bwd_in = lambda i: base + bwd(i)
    o_shape = jax.ShapeDtypeStruct((ROWS_S, HGRN_HEADS * HGRN_DV), F32)
    of_s, ob_s = pl.pallas_call(
        functools.partial(_hgrn_sample_kernel, layer=layer),
        out_shape=(o_shape, o_shape),
        grid=(HGRN_HEADS // hps, DEC_BATCH * tiles),
        in_specs=[col(qc, fwd_in), col(vc, fwd_in), col(zfc, fwd_in), col(qc, bwd_in), col(vc, bwd_in),
                  col(zbc, bwd_in)]
        + const_specs
        + [pl.BlockSpec((None, None, 2, hps, HGRN_DK, HGRN_DV), lambda h, i: (i // tiles, layer_j, 0, h, 0, 0))]
        + mask_specs,
        out_specs=(col(0, same), col(0, bwd)),
        scratch_shapes=[pltpu.VMEM((2, hps, HGRN_DK, HGRN_DV), F32)],
        compiler_params=_cparams(("arbitrary", "arbitrary"), vmem),
        name="hgrn_scan_sample",
    )(proj, proj, proj, proj, proj, proj, lb_raw, state_hgrn, masks, tris)
    return (of_p, ob_p), (of_s, ob_s), sfin


def _hgrn_out_kernel(ofp_ref, obp_ref, ofs_ref, obs_ref, g_ref, gn_ref, w_ref, x_ref, mod_ref, o_ref):
    def run(of_ref, ob_ref):
        gn = gn_ref[...]
        parts = []
        for h in range(HGRN_HEADS):
            hs = slice(h * HGRN_DV, (h + 1) * HGRN_DV)
            o = of_ref[:, hs] + ob_ref[:, hs]
            g = g_ref[:, hs]
            o = o * lax.rsqrt(jnp.mean(o * o, axis=-1, keepdims=True) + NORM_EPS) * gn * _silu(g)
            parts.append(o.astype(BF16))
        y = jnp.dot(jnp.concatenate(parts, axis=-1), w_ref[...], preferred_element_type=F32)
        o_ref[...] = x_ref[...] + mod_ref[2:3, :] * y

    is_prompt = pl.program_id(0) < ROWS_P // HG_OUT_TILE
    pl.when(is_prompt)(lambda: run(ofp_ref, obp_ref))
    pl.when(jnp.logical_not(is_prompt))(lambda: run(ofs_ref, obs_ref))


def hgrn_out(o_p, o_s, proj, g_norm, wo, w_idx, x, layer, mods):
    gcol = 4096 // D
    tile = HG_OUT_TILE
    p_spec, s_spec = _group_specs(D, tile)
    return pl.pallas_call(
        _hgrn_out_kernel,
        out_shape=jax.ShapeDtypeStruct((ROWS, D), F32),
        grid=(ROWS // tile,),
        in_specs=[p_spec, p_spec, s_spec, s_spec,
                  pl.BlockSpec((tile, D), lambda i: (i, gcol)),
                  pl.BlockSpec((1, HGRN_DV), lambda i: (0, 0)),
                  pl.BlockSpec((None, D, D), lambda i: (w_idx, 0, 0), pipeline_mode=pl.Buffered(1)),
                  pl.BlockSpec((tile, D), lambda i: (i, 0)),
                  _mod_spec(layer, 1, tile)],
        out_specs=pl.BlockSpec((tile, D), lambda i: (i, 0)),
        compiler_params=_cparams(("arbitrary",), 22 * tile * D * 4),
        name="hgrn_out",
    )(*o_p, *o_s, proj, g_norm.reshape(1, HGRN_DV), wo, x, mods)


def hgrn_layer(x, layer, layer_j, mods, norm_g, w_in, hgrn_lb, g_norm, wo, state_hgrn):
    proj = norm_mod_matmul(x, layer, norm_g, mods, w_in, layer_j, "hgrn_in")
    o_p, o_s, sfin = hgrn_scan(proj, jnp.transpose(hgrn_lb, (1, 0, 2)), layer, state_hgrn, layer_j)
    x = hgrn_out(o_p, o_s, proj, g_norm[layer_j], wo, layer_j, x, layer, mods)
    return x, sfin


SSM_N = SSM_GROUPS * SSM_STATE
SSM_KT = 8
SSM_ROWS = 256


def _ssm_prep_kernel(are_ref, aim_ref, ldt_ref, bre_ref, bim_ref, lre_ref, lim_ref, bbre_ref, bbim_ref):
    a_re = jnp.minimum(are_ref[...], -1e-4)
    a_im = aim_ref[...]
    dt = jnp.exp(ldt_ref[...])
    mag = jnp.exp(a_re * dt)
    l_re = mag * jnp.cos(a_im * dt)
    l_im = mag * jnp.sin(a_im * dt)
    lre_ref[...] = l_re
    lim_ref[...] = l_im
    den = a_re * a_re + a_im * a_im
    c_re = ((l_re - 1.0) * a_re + l_im * a_im) / den
    c_im = (l_im * a_re - (l_re - 1.0) * a_im) / den
    b_re, b_im = bre_ref[...], bim_ref[...]
    bbre_ref[...] = c_re[:, None, :] * b_re - c_im[:, None, :] * b_im
    bbim_ref[...] = c_re[:, None, :] * b_im + c_im[:, None, :] * b_re


def ssm_discretize(a_re, a_im, log_dt, b_re, b_im):
    g2 = 2 * SSM_GROUPS
    sh = jax.ShapeDtypeStruct((g2, SSM_STATE), F32)
    shb = jax.ShapeDtypeStruct((g2, SSM_GROUP, SSM_STATE), F32)
    bt = lambda b: jnp.transpose(b, (0, 1, 3, 2)).reshape(g2, SSM_GROUP, SSM_STATE)
    return pl.pallas_call(_ssm_prep_kernel, out_shape=(sh, sh, shb, shb), name="ssm_discretize")(
        a_re.reshape(g2, SSM_STATE), a_im.reshape(g2, SSM_STATE), log_dt.reshape(g2, 1), bt(b_re), bt(b_im))


def _ssm_block_diag(l_re, l_im, bb_re, bb_im, c_re, c_im):
    nk = SSM_GROUPS // SSM_KT
    eye = jnp.eye(SSM_KT, dtype=F32)

    def bmat(b):
        b = b.reshape(2, nk, SSM_KT, SSM_GROUP, SSM_STATE)
        return jnp.einsum('dkgip,gh->dkgihp', b, eye).reshape(2, nk, SSM_KT * SSM_GROUP, SSM_KT * SSM_STATE)

    def cmat(c):
        c = c.reshape(2, nk, SSM_KT, SSM_GROUP, SSM_STATE)
        return jnp.einsum('dkgip,gh->dkhpgi', c, eye).reshape(2, nk, SSM_KT * SSM_STATE, SSM_KT * SSM_GROUP)

    b_mat = jnp.concatenate([bmat(bb_re), bmat(bb_im)], axis=-1).astype(BF16)
    c_mat = jnp.concatenate([cmat(c_re), cmat(-c_im)], axis=-2).astype(BF16)
    lam = jnp.stack([l_re.reshape(2, SSM_N), l_im.reshape(2, SSM_N)], axis=1)
    return b_mat, c_mat, lam


X4_SHAPE = (ROWS // (4 * SEQ), 4, SEQ, D)


def _tm_geometry(prompt):
    if prompt:
        batch = BATCH
        steps = SSM_ROWS // batch
        return batch, steps, (4, 4, steps, D), (lambda i: (0, 0, i, 0)), SEQ // steps, (0,) * batch
    batch = DEC_BATCH
    steps = SSM_ROWS // batch
    per_q = SEQ // steps
    return (batch, steps, (4, 1, steps, D), (lambda i: (1, i // per_q, i % per_q, 0)), DEC_SEQ // steps,
            tuple(range(1, 1 + batch)))


def _x4_seq(ref, b):
    return ref.at[b // ref.shape[1], b % ref.shape[1]]


LANE_SLABS = D // 128


def _slab_store(s_ref, rows, val):
    for c in range(LANE_SLABS):
        s_ref[c, rows, :] = val[:, c * 128:(c + 1) * 128]


def _slab_load(s_ref, rows):
    return jnp.concatenate([s_ref[c, rows, :] for c in range(LANE_SLABS)], axis=-1)


def _normmod_tm_kernel(x_ref, g_ref, mod_ref, o_ref, s_ref, *, batch, steps, mod_rows):
    g = g_ref[...]
    for b in range(batch):
        m = mod_ref.at[mod_rows[b]]
        _slab_store(s_ref, slice(b * steps, (b + 1) * steps),
                    _norm_mod(_x4_seq(x_ref, b)[...], g, m[0:1, :], m[1:2, :]))
    for t in range(steps):
        o_ref[t * batch:(t + 1) * batch, :] = _slab_load(s_ref, pl.ds(t, batch, stride=steps))


def norm_mod_time_major(x4, layer, norm_g, mods, prompt):
    batch, steps, blk, idx, tiles, mod_rows = _tm_geometry(prompt)
    return pl.pallas_call(
        functools.partial(_normmod_tm_kernel, batch=batch, steps=steps, mod_rows=mod_rows),
        out_shape=jax.ShapeDtypeStruct((tiles * SSM_ROWS, D), F32),
        grid=(tiles,),
        in_specs=[pl.BlockSpec(blk, idx),
                  pl.BlockSpec((None, 1, D), lambda i: (layer, 0, 0)),
                  pl.BlockSpec((None, MOD_ROWS, 6, D), lambda i: (layer, 0, 0, 0))],
        out_specs=pl.BlockSpec((SSM_ROWS, D), lambda i: (i, 0)),
        scratch_shapes=[pltpu.VMEM((LANE_SLABS, SSM_ROWS, 128), F32)],
        compiler_params=_cparams(("arbitrary",), 8 * SSM_ROWS * D * 4),
        name="ssm_norm_time_major",
    )(x4, norm_g.reshape(DEPTH, 1, D), mods)


def _ssm_scan_kernel(xf_ref, xb_ref, bm_ref, cm_ref, lam_ref, h0_ref, yf_ref, yb_ref, hfin_ref,
                     hre_f, him_f, hre_b, him_b, st_ref, *, batch):
    i = pl.program_id(0)
    steps = SSM_ROWS // batch
    nk = SSM_GROUPS // SSM_KT
    kw = SSM_KT * SSM_STATE
    x_refs, y_refs = (xf_ref, xb_ref), (yf_ref, yb_ref)
    h_refs = ((hre_f, him_f), (hre_b, him_b))

    @pl.when(i == 0)
    def _():
        st_ref[...] = h0_ref[...]

    def tile_cols(k):
        return slice(k * kw, (k + 1) * kw)

    def bu_tile(d, k):
        xk = x_refs[d][:, k * 128:(k + 1) * 128].astype(BF16)
        bu = jnp.dot(xk, bm_ref[d, k], preferred_element_type=F32)
        h_refs[d][0][:, tile_cols(k)] = bu[:, :kw]
        h_refs[d][1][:, tile_cols(k)] = bu[:, kw:]

    def c_tile(d, k):
        hk = jnp.concatenate([h_refs[d][0][:, tile_cols(k)], h_refs[d][1][:, tile_cols(k)]], axis=-1)
        y_refs[d][:, k * 128:(k + 1) * 128] = jnp.dot(hk.astype(BF16), cm_ref[d, k], preferred_element_type=F32)

    def scan_tile(d, k):
        hre_ref, him_ref = h_refs[d]
        col = tile_cols(k)
        l_re, l_im = lam_ref[d, 0, :, col], lam_ref[d, 1, :, col]
        h_re, h_im = st_ref[d, 0, :, col], st_ref[d, 1, :, col]
        per = max(8 // batch, 1)
        rows_per = per * batch
        for s in range(steps // per):
            g = (steps // per - 1 - s) if d else s
            rows = slice(g * rows_per, (g + 1) * rows_per)
            cur_re, cur_im = hre_ref[rows, col], him_ref[rows, col]
            outs_re, outs_im = [None] * per, [None] * per
            for r in (range(per - 1, -1, -1) if d else range(per)):
                b_re, b_im = cur_re[r * batch:(r + 1) * batch], cur_im[r * batch:(r + 1) * batch]
                h_re, h_im = l_re * h_re - l_im * h_im + b_re, l_re * h_im + l_im * h_re + b_im
                outs_re[r], outs_im[r] = h_re, h_im
            hre_ref[rows, col] = outs_re[0] if per == 1 else jnp.concatenate(outs_re, axis=0)
            him_ref[rows, col] = outs_im[0] if per == 1 else jnp.concatenate(outs_im, axis=0)
        st_ref[d, 0, :, col] = h_re
        st_ref[d, 1, :, col] = h_im

    for k in range(nk + 2):
        for d in range(2):
            if k < nk:
                bu_tile(d, k)
            if 1 <= k <= nk:
                scan_tile(d, k - 1)
            if k >= 2:
                c_tile(d, k - 2)

    @pl.when(i == pl.num_programs(0) - 1)
    def _():
        hfin_ref[...] = st_ref[...]


def ssm_scan(xn_tm, b_mat, c_mat, lam, h0, batch):
    rows = xn_tm.shape[0]
    n = rows // SSM_ROWS
    lam_b = jnp.broadcast_to(lam[:, :, None, :], (2, 2, batch, SSM_N))
    y_shape = jax.ShapeDtypeStruct((rows, D), F32)
    full = lambda a: pl.BlockSpec(a.shape, lambda i: (0,) * a.ndim)
    vmem = (4 * SSM_ROWS * SSM_N * 4 + 8 * SSM_ROWS * D * 4 + 2 * (b_mat.size + c_mat.size) * 2
            + 12 * batch * SSM_N * 4 * 2 + 8 * SSM_ROWS * 1024 * 4)
    return pl.pallas_call(
        functools.partial(_ssm_scan_kernel, batch=batch),
        out_shape=(y_shape, y_shape, jax.ShapeDtypeStruct((2, 2, batch, SSM_N), F32)),
        grid=(n,),
        in_specs=[pl.BlockSpec((SSM_ROWS, D), lambda i: (i, 0)),
                  pl.BlockSpec((SSM_ROWS, D), lambda i: (n - 1 - i, 0)),
                  full(b_mat), full(c_mat), full(lam_b), full(h0)],
        out_specs=(pl.BlockSpec((SSM_ROWS, D), lambda i: (i, 0)),
                   pl.BlockSpec((SSM_ROWS, D), lambda i: (n - 1 - i, 0)),
                   pl.BlockSpec((2, 2, batch, SSM_N), lambda i: (0, 0, 0, 0))),
        scratch_shapes=[pltpu.VMEM((SSM_ROWS, SSM_N), F32)] * 4
        + [pltpu.VMEM((2, 2, batch, SSM_N), F32)],
        compiler_params=_cparams(("arbitrary",), vmem),
        name="ssm_scan",
    )(xn_tm, xn_tm, b_mat, c_mat, lam_b, h0)


def _gelu_tanh(x):
    return 0.5 * x * (1.0 + jnp.tanh(math.sqrt(2.0 / math.pi) * (x + 0.044715 * (x * x * x))))


def _ssm_glu_kernel(yf_ref, yb_ref, xn_ref, d_ref, w_ref, x_ref, mod_ref, o_ref, s_ref,
                    *, batch, steps, mod_rows):
    g = _gelu_tanh(yf_ref[...] + yb_ref[...] + d_ref[...] * xn_ref[...])
    u = jnp.dot(g.astype(BF16), w_ref[...], preferred_element_type=F32)
    _slab_store(s_ref, slice(None), u[:, :D] * _sigmoid(u[:, D:]))
    for b in range(batch):
        gate = mod_ref[mod_rows[b], 2:3, :]
        _x4_seq(o_ref, b)[...] = (_x4_seq(x_ref, b)[...]
                                  + gate * _slab_load(s_ref, pl.ds(b, steps, stride=batch)))


def ssm_glu(yf, yb, xn, d, w_glu, w_idx, x4, layer, mods, prompt):
    batch, steps, blk, idx, tiles, mod_rows = _tm_geometry(prompt)
    tm_spec = pl.BlockSpec((SSM_ROWS, D), lambda i: (i, 0))
    out = pl.pallas_call(
        functools.partial(_ssm_glu_kernel, batch=batch, steps=steps, mod_rows=mod_rows),
        out_shape=jax.ShapeDtypeStruct((4,) + X4_SHAPE[1:], F32),
        grid=(tiles,),
        in_specs=[tm_spec, tm_spec, tm_spec,
                  pl.BlockSpec((None, 1, D), lambda i: (w_idx, 0, 0)),
                  pl.BlockSpec((None, D, 2 * D), lambda i: (w_idx, 0, 0), pipeline_mode=pl.Buffered(1)),
                  pl.BlockSpec(blk, idx),
                  pl.BlockSpec((None, MOD_ROWS, 6, D), lambda i: (layer, 0, 0, 0))],
        out_specs=pl.BlockSpec(blk, lambda i: (0,) + idx(i)[1:]),
        scratch_shapes=[pltpu.VMEM((LANE_SLABS, SSM_ROWS, 128), F32)],
        compiler_params=_cparams(("arbitrary",), 24 * SSM_ROWS * D * 4 + D * 2 * D * 2),
        name="ssm_glu",
    )(yf, yb, xn, d.reshape(-1, 1, D), w_glu, x4, mods)
    return out.reshape(-1, D)


def ssm_layer(x, layer, layer_j, mods, norm_g, a_re, a_im, log_dt, b_re, b_im, c_re, c_im, d, w_glu, state_ssm):
    l_re, l_im, bb_re, bb_im = ssm_discretize(a_re[layer_j], a_im[layer_j], log_dt[layer_j], b_re[layer_j],
                                              b_im[layer_j])
    b_mat, c_mat, lam = _ssm_block_diag(l_re, l_im, bb_re, bb_im, c_re[layer_j], c_im[layer_j])
    x4 = x.reshape(X4_SHAPE)
    xn_p = norm_mod_time_major(x4, layer, norm_g, mods, True)
    xn_s = norm_mod_time_major(x4, layer, norm_g, mods, False)
    h0_p = jnp.zeros((2, 2, BATCH, SSM_N), F32)
    h0_s = jnp.transpose(state_ssm[:, layer_j].reshape(DEC_BATCH, 2, SSM_N, 2), (1, 3, 0, 2))
    yfp, ybp, hfin = ssm_scan(xn_p, b_mat, c_mat, lam, h0_p, BATCH)
    yfs, ybs, _ = ssm_scan(xn_s, b_mat, c_mat, lam, h0_s, DEC_BATCH)
    out_p = ssm_glu(yfp, ybp, xn_p, d, w_glu, layer_j, x4, layer, mods, True)
    out_s = ssm_glu(yfs, ybs, xn_s, d, w_glu, layer_j, x4, layer, mods, False)
    new_state = jnp.transpose(hfin, (2, 0, 3, 1)).reshape(BATCH, 2, SSM_GROUPS, SSM_STATE, 2)
    return (out_p, out_s), new_state


def kernel(x_prompt, x_sample, cache_k, cache_v, state_hgrn, state_ssm, c, c_ctx, ada_w, ada_b, norm1_g, norm2_g, attn_wqkv, attn_wo, attn_sink, hgrn_w_in, hgrn_lb, hgrn_g_norm, hgrn_wo, ssm_a_re, ssm_a_im, ssm_log_dt, ssm_b_re, ssm_b_im, ssm_c_re, ssm_c_im, ssm_d, ssm_w_glu, ffn_w_up, ffn_conv_w, ffn_conv_b, ffn_w_down, final_g):
    cond8 = jnp.zeros((MOD_ROWS, D), F32).at[0].set(c_ctx).at[1:1 + DEC_BATCH].set(c)
    mods = ada_modulation(cond8, ada_w, ada_b)
    x = (x_prompt.reshape(ROWS_P, D), x_sample.reshape(ROWS_S, D))
    wqkv, wo, w_in, hwo, w_glu, w_up, w_down = (w.astype(BF16) for w in (
        attn_wqkv, attn_wo, hgrn_w_in, hgrn_wo, ssm_w_glu, ffn_w_up, ffn_w_down))
    new_k, new_v, new_hgrn, new_ssm = [], [], [], []
    for l in range(DEPTH):
        kind, j = l % N_MIXERS, l // N_MIXERS
        if kind == 0:
            x, k, v = attention_layer(x, l, j, mods, norm1_g, wqkv, wo, attn_sink[j], cache_k, cache_v)
            new_k.append(k)
            new_v.append(v)
        elif kind == 1:
            x, s = hgrn_layer(x, l, j, mods, norm1_g, w_in, hgrn_lb, hgrn_g_norm, hwo, state_hgrn)
            new_hgrn.append(s)
        else:
            x, s = ssm_layer(x, l, j, mods, norm1_g, ssm_a_re, ssm_a_im, ssm_log_dt, ssm_b_re, ssm_b_im,
                             ssm_c_re, ssm_c_im, ssm_d, w_glu, state_ssm)
            new_ssm.append(s)
        x = conv_ffn_residual(x, l, norm2_g, mods, w_up, ffn_conv_w, ffn_conv_b, w_down)
    y_prompt = final_norm(x, final_g, 0, N_ROW_TILES_P).reshape(BATCH, SEQ, D)
    y_sample = final_norm(x, final_g, N_ROW_TILES_P, N_ROW_TILES - N_ROW_TILES_P).reshape(DEC_BATCH, DEC_SEQ, D)
    return (y_prompt, y_sample, jnp.stack(new_k, axis=1), jnp.stack(new_v, axis=1),
            jnp.stack(new_hgrn, axis=1), jnp.stack(new_ssm, axis=1))
```

```python
import functools
import math

import jax
import jax.numpy as jnp
import numpy as np
from jax import lax
from jax.experimental import pallas as pl
from jax.experimental.pallas import tpu as pltpu

F32 = jnp.float32
BF16 = jnp.bfloat16

D = 1024
BATCH = 16
SEQ = 256
DEPTH = 4
DEC_BATCH = 4
DEC_SEQ = 1024
PAST_LEN = 512
GRID_W = 64
N_MIXERS = 3
ATTN_HEADS = 16
ATTN_KV_HEADS = 4
ATTN_GROUP = ATTN_HEADS // ATTN_KV_HEADS
HEAD_DIM = D // ATTN_HEADS
WINDOW = 128
ROPE_BASE = 10000.0
HGRN_HEADS = 8
HGRN_DK = 128
HGRN_DV = 128
SSM_GROUP = 16
SSM_GROUPS = D // SSM_GROUP
SSM_STATE = 64
D_FF = 2816
NORM_EPS = 1e-6

ROWS_P = BATCH * SEQ
ROWS_S = DEC_BATCH * DEC_SEQ
ROWS = ROWS_P + ROWS_S
ROW_TILE = 1024
N_ROW_TILES = ROWS // ROW_TILE
N_ROW_TILES_P = ROWS_P // ROW_TILE
MOD_ROWS = 8
V7X_VMEM_BYTES = 64 * 1024 * 1024


def _mod_row(i, tile=ROW_TILE):
    return jnp.where(i < ROWS_P // tile, 0, (i - ROWS_P // tile) // (DEC_SEQ // tile) + 1)


def _cparams(semantics, vmem_bytes):
    vmem = int(min(max(vmem_bytes * 5 // 4 + (4 << 20), 16 << 20), V7X_VMEM_BYTES - (6 << 20)))
    return pltpu.CompilerParams(dimension_semantics=semantics, vmem_limit_bytes=vmem)


def _bdot(a, b):
    return jnp.dot(a.astype(BF16), b.astype(BF16), preferred_element_type=F32)


def _norm_mod(x, g, shift, scale):
    y = x * lax.rsqrt(jnp.mean(x * x, axis=-1, keepdims=True) + NORM_EPS) * g
    return y * (1.0 + scale) + shift


def _sigmoid(x):
    return 0.5 + 0.5 * jnp.tanh(0.5 * x)


def _silu(x):
    h = 0.5 * x
    return h + h * jnp.tanh(h)


def _ada_kernel(c_ref, w_ref, b_ref, o_ref):
    c = c_ref[...]
    o_ref[...] = _bdot(_silu(c), w_ref[...]) + b_ref[...]


def ada_modulation(cond8, ada_w, ada_b):
    tn = 1024
    out = pl.pallas_call(
        _ada_kernel,
        out_shape=jax.ShapeDtypeStruct((DEPTH, MOD_ROWS, 6 * D), F32),
        grid=(DEPTH, 6 * D // tn),
        in_specs=[
            pl.BlockSpec((MOD_ROWS, D), lambda l, j: (0, 0)),
            pl.BlockSpec((None, D, tn), lambda l, j: (l, 0, j)),
            pl.BlockSpec((None, 1, tn), lambda l, j: (l, 0, j)),
        ],
        out_specs=pl.BlockSpec((None, MOD_ROWS, tn), lambda l, j: (l, 0, j)),
        compiler_params=_cparams(("arbitrary", "arbitrary"), 2 * D * tn * 4),
        name="ada_modulation",
    )(cond8, ada_w, ada_b.reshape(DEPTH, 1, 6 * D))
    return out.reshape(DEPTH, MOD_ROWS, 6, D)


def _x_operands(x, n_grid=1):
    if isinstance(x, tuple):
        return x, list(_group_specs(D, ROW_TILE, n_grid))
    return (x,), [pl.BlockSpec((ROW_TILE, D), (lambda i: (i, 0)) if n_grid == 1 else (lambda i, j: (i, 0)))]


def _read_rows(x_refs):
    if len(x_refs) == 1:
        return x_refs[0][...]
    return jnp.where(pl.program_id(0) < N_ROW_TILES_P, x_refs[0][...], x_refs[1][...])


def _nmm_kernel(*refs):
    x_refs, (g_ref, mod_ref, w_ref, o_ref, h_ref) = refs[:-5], refs[-5:]

    @pl.when(pl.program_id(1) == 0)
    def _():
        h_ref[...] = _norm_mod(_read_rows(x_refs), g_ref[...], mod_ref[0:1, :], mod_ref[1:2, :]).astype(BF16)

    o_ref[...] = jnp.dot(h_ref[...], w_ref[...], preferred_element_type=F32)


def _mod_spec(layer, n_grid, tile=ROW_TILE):
    if n_grid == 1:
        return pl.BlockSpec((None, None, 6, D), lambda i: (layer, _mod_row(i, tile), 0, 0))
    return pl.BlockSpec((None, None, 6, D), lambda i, j: (layer, _mod_row(i, tile), 0, 0))


def norm_mod_matmul(x, layer, norm_g, mods, w, w_idx, name):
    n = w.shape[-1]
    tn = 1024 if n % 1024 == 0 else 768
    x_ops, x_specs = _x_operands(x, 2)
    return pl.pallas_call(
        _nmm_kernel,
        out_shape=jax.ShapeDtypeStruct((ROWS, n), F32),
        grid=(N_ROW_TILES, n // tn),
        in_specs=x_specs + [
            pl.BlockSpec((None, 1, D), lambda i, j: (layer, 0, 0)),
            _mod_spec(layer, 2),
            pl.BlockSpec((None, D, tn), lambda i, j: (w_idx, 0, j)),
        ],
        out_specs=pl.BlockSpec((ROW_TILE, tn), lambda i, j: (i, j)),
        scratch_shapes=[pltpu.VMEM((ROW_TILE, D), BF16)],
        compiler_params=_cparams(("arbitrary", "arbitrary"),
                                 4 * ROW_TILE * D * 4 + ROW_TILE * D * 2 + 2 * D * tn * 2 + 2 * ROW_TILE * tn * 4),
        name=name,
    )(*x_ops, norm_g.reshape(DEPTH, 1, D), mods, w)


def _mm_res_kernel(ap_ref, as_ref, w_ref, *refs):
    x_refs, (mod_ref, o_ref) = refs[:-2], refs[-2:]

    def run(a_ref, x_ref):
        y = jnp.dot(a_ref[...].astype(BF16), w_ref[...], preferred_element_type=F32)
        o_ref[...] = x_ref[...] + mod_ref[2:3, :] * y

    is_prompt = pl.program_id(0) < N_ROW_TILES_P
    pl.when(is_prompt)(lambda: run(ap_ref, x_refs[0]))
    pl.when(jnp.logical_not(is_prompt))(lambda: run(as_ref, x_refs[-1]))


def _group_specs(k, tile=ROW_TILE, n_grid=1):
    n_p = ROWS_P // tile
    if n_grid == 1:
        return (pl.BlockSpec((tile, k), lambda i: (jnp.minimum(i, n_p - 1), 0)),
                pl.BlockSpec((tile, k), lambda i: (jnp.maximum(i - n_p, 0), 0)))
    return (pl.BlockSpec((tile, k), lambda i, j: (jnp.minimum(i, n_p - 1), 0)),
            pl.BlockSpec((tile, k), lambda i, j: (jnp.maximum(i - n_p, 0), 0)))


def matmul_gated_residual(a_p, a_s, w, w_idx, x, layer, mods, name):
    k = a_p.shape[1]
    x_ops, x_specs = _x_operands(x)
    return pl.pallas_call(
        _mm_res_kernel,
        out_shape=jax.ShapeDtypeStruct((ROWS, D), F32),
        grid=(N_ROW_TILES,),
        in_specs=[
            *_group_specs(k),
            pl.BlockSpec((None, k, D), lambda i: (w_idx, 0, 0), pipeline_mode=pl.Buffered(1)),
            *x_specs,
            _mod_spec(layer, 1),
        ],
        out_specs=pl.BlockSpec((ROW_TILE, D), lambda i: (i, 0)),
        compiler_params=_cparams(("arbitrary",), 4 * ROW_TILE * k * 2 + k * D * 2 + 7 * ROW_TILE * D * 4),
        name=name,
    )(a_p, a_s, w, *x_ops, mods)


FFN_CHUNK = 256
FFN_CHUNKS = D_FF // FFN_CHUNK
FFN_DOWN_GROUP = FFN_CHUNKS
CONV_PAD = 8


def _ffn_kernel(*refs, split_x):
    x_refs, (g_ref, mod_ref, wup_ref, cw_ref, cb_ref, wd_ref, o_ref, h_ref, pad_a, pad_b, act_ref) = (
        refs[:1 + split_x], refs[1 + split_x:])
    i = pl.program_id(0)
    h_ref[...] = _norm_mod(_read_rows(x_refs), g_ref[...], mod_ref[3:4, :], mod_ref[4:5, :]).astype(BF16)
    zeros = jnp.zeros((CONV_PAD, 2 * FFN_CHUNK), F32)
    for pad_ref in (pad_a, pad_b):
        pad_ref[0:CONV_PAD, :] = zeros
        pad_ref[CONV_PAD + ROW_TILE:, :] = zeros
    sub = lax.broadcasted_iota(jnp.int32, (8, 1), 0)
    is_prompt = i < N_ROW_TILES_P
    keep_first = jnp.where((sub == 0) & is_prompt, 0.0, 1.0)
    keep_last = jnp.where((sub == 7) & is_prompt, 0.0, 1.0)

    def cut_sequences(v, keep, row):
        parts, at = [], 0
        for b in range(SEQ, ROW_TILE, SEQ):
            lo = b if row == 0 else b - 8
            parts += [v[at:lo], v[lo:lo + 8] * keep]
            at = lo + 8
        return jnp.concatenate(parts + [v[at:]], axis=0)

    def cols(ref, c):
        return (ref[:, c * FFN_CHUNK:(c + 1) * FFN_CHUNK],
                ref[:, D_FF + c * FFN_CHUNK:D_FF + (c + 1) * FFN_CHUNK])

    def up_proj(c, pad_ref):
        hb = h_ref[...]
        wg, wv = cols(wup_ref, c)
        pad_ref[CONV_PAD:CONV_PAD + ROW_TILE, :FFN_CHUNK] = jnp.dot(hb, wg, preferred_element_type=F32)
        pad_ref[CONV_PAD:CONV_PAD + ROW_TILE, FFN_CHUNK:] = jnp.dot(hb, wv, preferred_element_type=F32)

    def conv_act(c, pad_ref):
        up = pad_ref[CONV_PAD:CONV_PAD + ROW_TILE, :]
        prev = cut_sequences(pad_ref[CONV_PAD - 1:CONV_PAD - 1 + ROW_TILE, :], keep_first, 0)
        nxt = cut_sequences(pad_ref[CONV_PAD + 1:CONV_PAD + 1 + ROW_TILE, :], keep_last, 7)
        cw = jnp.concatenate(cols(cw_ref, c), axis=-1)
        cb = jnp.concatenate(cols(cb_ref, c), axis=-1)
        conv = prev * cw[0:1, :] + up * cw[1:2, :] + nxt * cw[2:3, :] + cb
        gate = conv[:, :FFN_CHUNK]
        act = _silu(gate) * conv[:, FFN_CHUNK:]
        act_ref[:, c * FFN_CHUNK:(c + 1) * FFN_CHUNK] = act.astype(BF16)

    pads = (pad_a, pad_b)
    done = 0
    up_proj(0, pads[0])
    for c in range(FFN_CHUNKS):
        if c + 1 < FFN_CHUNKS:
            up_proj(c + 1, pads[(c + 1) % 2])
        conv_act(c, pads[c % 2])
        if (c + 1) % FFN_DOWN_GROUP == 0 or c + 1 == FFN_CHUNKS:
            rows = slice(done * FFN_CHUNK, (c + 1) * FFN_CHUNK)
            part = jnp.dot(act_ref[:, rows], wd_ref[rows, :], preferred_element_type=F32)
            acc = part if done == 0 else o_ref[...] + part
            o_ref[...] = acc if c + 1 < FFN_CHUNKS else _read_rows(x_refs) + mod_ref[5:6, :] * acc
            done = c + 1


def conv_ffn_residual(x, layer, norm_g, mods, w_up, conv_w, conv_b, w_down):
    split_x = isinstance(x, tuple)
    x_ops, x_specs = _x_operands(x)
    once = pl.Buffered(1)
    vmem = (4 * ROW_TILE * D * 4 + ROW_TILE * D * 2 + 2 * (ROW_TILE + 2 * CONV_PAD) * 2 * FFN_CHUNK * 4
            + ROW_TILE * D_FF * 2 + 3 * D * D_FF * 2 + 5 * ROW_TILE * 2 * FFN_CHUNK * 4)
    return pl.pallas_call(
        functools.partial(_ffn_kernel, split_x=split_x),
        out_shape=jax.ShapeDtypeStruct((ROWS, D), F32),
        grid=(N_ROW_TILES,),
        in_specs=x_specs + [
            pl.BlockSpec((None, 1, D), lambda i: (layer, 0, 0)),
            pl.BlockSpec((None, None, 6, D), lambda i: (layer, _mod_row(i), 0, 0)),
            pl.BlockSpec((None, D, 2 * D_FF), lambda i: (layer, 0, 0), pipeline_mode=once),
            pl.BlockSpec((None, 3, 2 * D_FF), lambda i: (layer, 0, 0), pipeline_mode=once),
            pl.BlockSpec((None, 1, 2 * D_FF), lambda i: (layer, 0, 0), pipeline_mode=once),
            pl.BlockSpec((None, D_FF, D), lambda i: (layer, 0, 0), pipeline_mode=once),
        ],
        out_specs=pl.BlockSpec((ROW_TILE, D), lambda i: (i, 0)),
        scratch_shapes=[pltpu.VMEM((ROW_TILE, D), BF16),
                        pltpu.VMEM((ROW_TILE + 2 * CONV_PAD, 2 * FFN_CHUNK), F32),
                        pltpu.VMEM((ROW_TILE + 2 * CONV_PAD, 2 * FFN_CHUNK), F32),
                        pltpu.VMEM((ROW_TILE, D_FF), BF16)],
        compiler_params=_cparams(("arbitrary",), vmem),
        name="conv_ffn",
    )(*x_ops, norm_g.reshape(DEPTH, 1, D), mods, w_up, conv_w, conv_b.reshape(DEPTH, 1, 2 * D_FF), w_down)


def _final_norm_kernel(x_ref, g_ref, o_ref):
    x = x_ref[...]
    o_ref[...] = x * lax.rsqrt(jnp.mean(x * x, axis=-1, keepdims=True) + NORM_EPS) * g_ref[...]


def final_norm(x, g, first_tile, n_tiles):
    return pl.pallas_call(
        _final_norm_kernel,
        out_shape=jax.ShapeDtypeStruct((n_tiles * ROW_TILE, D), F32),
        grid=(n_tiles,),
        in_specs=[pl.BlockSpec((ROW_TILE, D), lambda i: (first_tile + i, 0)),
                  pl.BlockSpec((1, D), lambda i: (0, 0))],
        out_specs=pl.BlockSpec((ROW_TILE, D), lambda i: (i, 0)),
        compiler_params=_cparams(("arbitrary",), 4 * ROW_TILE * D * 4),
        name="final_norm",
    )(x, g.reshape(1, D))


NQ = ATTN_HEADS * HEAD_DIM
NKV = ATTN_KV_HEADS * HEAD_DIM
Q_BLOCK = 128
MASKED = -1e30
LOG2E = 1.0 / math.log(2.0)
Q_PRESCALE = HEAD_DIM ** -0.5 * LOG2E


def _dot_t(a, b):
    return lax.dot_general(a.astype(BF16), b.astype(BF16), (((1,), (1,)), ((), ())),
                           preferred_element_type=F32)


def _group_rows(q):
    return jnp.concatenate([q[:, g * HEAD_DIM:(g + 1) * HEAD_DIM] for g in range(ATTN_GROUP)], axis=0)


def _sink_lanes(sink_ref, h, rows):
    return LOG2E * jnp.concatenate(
        [jnp.broadcast_to(sink_ref[0:1, ATTN_GROUP * h + g:ATTN_GROUP * h + g + 1], (1, rows))
         for g in range(ATTN_GROUP)], axis=-1)


ONES_ROWS = 16


def _values_t(v):
    ones = jnp.ones((ONES_ROWS, v.shape[0]), F32)
    out = []
    for c in range(NKV // 128):
        vt = v[:, c * 128:(c + 1) * 128].T
        out += [jnp.concatenate([vt[j * HEAD_DIM:(j + 1) * HEAD_DIM], ones], axis=0).astype(BF16)
                for j in range(128 // HEAD_DIM)]
    return out


def _softmax_pv(q4, key_sets, sink2):
    scores = []
    for k, _, bias in key_sets:
        s = _dot_t(k, q4)
        if bias is not None:
            s = jnp.concatenate([s[c * 128:(c + 1) * 128] if bc is None else s[c * 128:(c + 1) * 128] + bc
                                 for c, bc in enumerate(bias)], axis=0)
        scores.append(s)
    m = sink2
    for s in scores:
        m = jnp.maximum(m, jnp.max(s, axis=0, keepdims=True))
    acc = None
    for s, (_, v1t, _) in zip(scores, key_sets):
        t = jnp.dot(v1t, jnp.exp2(s - m).astype(BF16), preferred_element_type=F32)
        acc = t if acc is None else acc + t
    denom = acc[HEAD_DIM:HEAD_DIM + 1] + jnp.exp2(sink2 - m)
    return acc[:HEAD_DIM] * (1.0 / denom)


def _heads_to_columns(o_t, rows):
    slabs = []
    for g in range(0, ATTN_GROUP, 128 // HEAD_DIM):
        pair = jnp.concatenate([o_t[:, (g + j) * rows:(g + j + 1) * rows] for j in range(128 // HEAD_DIM)], axis=0)
        slabs.append(pair.T)
    return jnp.concatenate(slabs, axis=-1)


def _ctx_attn_kernel(qkv_ref, sink_ref, o_ref):
    v1t = _values_t(qkv_ref[:, NQ + NKV:])
    outs = []
    for h in range(ATTN_KV_HEADS):
        k = qkv_ref[:, NQ + h * HEAD_DIM:NQ + (h + 1) * HEAD_DIM].astype(BF16)
        q4 = _group_rows(qkv_ref[:, ATTN_GROUP * h * HEAD_DIM:ATTN_GROUP * (h + 1) * HEAD_DIM] * Q_PRESCALE)
        o_t = _softmax_pv(q4.astype(BF16), [(k, v1t[h], None)], _sink_lanes(sink_ref, h, SEQ))
        outs.append(_heads_to_columns(o_t, SEQ))
    o_ref[...] = jnp.concatenate(outs, axis=-1).astype(BF16)


def context_attention(qkv, sink):
    return pl.pallas_call(
        _ctx_attn_kernel,
        out_shape=jax.ShapeDtypeStruct((ROWS_P, NQ), BF16),
        grid=(BATCH,),
        in_specs=[pl.BlockSpec((SEQ, NQ + 2 * NKV), lambda b: (b, 0)),
                  pl.BlockSpec((1, ATTN_HEADS), lambda b: (0, 0))],
        out_specs=pl.BlockSpec((SEQ, NQ), lambda b: (b, 0)),
        compiler_params=_cparams(("arbitrary",), 2 * SEQ * (2 * NQ + 2 * NKV) * 4 + 24 * SEQ * ATTN_GROUP * SEQ * 4),
        name="context_attention",
    )(qkv, sink.reshape(1, ATTN_HEADS))


def _rope(x, cos, sin_a, sin_b):
    outs = []
    for c in range(x.shape[1] // 128):
        s = x[:, c * 128:(c + 1) * 128]
        outs.append(s * cos + pltpu.roll(s, 128 - HEAD_DIM // 4, 1) * sin_a + pltpu.roll(s, HEAD_DIM // 4, 1) * sin_b)
    return jnp.concatenate(outs, axis=-1)


Q_TILE = 2 * Q_BLOCK


def _lat_attn_kernel(q_ref, kp_ref, km_ref, kn_ref, vp_ref, vm_ref, vn_ref, ck_ref, cv_ref,
                     cos_ref, sa_ref, sb_ref, sink_ref, o_ref):
    m = pl.program_id(1)
    nm = pl.num_programs(1)
    nb = 2 * nm

    def tables(blk, rows):
        r = pl.ds(pl.multiple_of(blk * Q_BLOCK, Q_BLOCK), rows)
        return cos_ref[r, :], sa_ref[r, :], sb_ref[r, :]

    mid = tables(2 * m, Q_TILE)
    qr = (_rope(q_ref[...], *mid) * Q_PRESCALE).astype(BF16)
    k4 = jnp.concatenate([
        _rope(kp_ref[...], *tables(jnp.maximum(2 * m - 1, 0), Q_BLOCK)),
        _rope(km_ref[...], *mid),
        _rope(kn_ref[...], *tables(jnp.minimum(2 * m + 2, nb - 1), Q_BLOCK))], axis=0).astype(BF16)
    v4 = jnp.concatenate([vp_ref[...], vm_ref[...], vn_ref[...]], axis=0)

    cols = ATTN_GROUP * Q_TILE
    koff = lax.broadcasted_iota(jnp.int32, (Q_BLOCK, cols), 0)
    lane = lax.broadcasted_iota(jnp.int32, (Q_BLOCK, cols), 1)
    qoff = lane & (Q_BLOCK - 1)
    second = (lane & Q_BLOCK) != 0
    lower, upper = koff >= qoff, koff <= qoff
    bias = [jnp.where(jnp.logical_not(second) & lower & (m > 0), 0.0, MASKED),
            jnp.where(jnp.logical_not(second) | lower, 0.0, MASKED),
            jnp.where(second | upper, 0.0, MASKED),
            jnp.where(second & upper & (m < nm - 1), 0.0, MASKED)]

    v1t = _values_t(v4)
    cv1t = _values_t(cv_ref[...])
    outs = []
    for h in range(ATTN_KV_HEADS):
        hs = slice(h * HEAD_DIM, (h + 1) * HEAD_DIM)
        q4 = _group_rows(qr[:, ATTN_GROUP * h * HEAD_DIM:ATTN_GROUP * (h + 1) * HEAD_DIM])
        o_t = _softmax_pv(q4, [(k4[:, hs], v1t[h], bias), (ck_ref[:, hs].astype(BF16), cv1t[h], None)],
                          _sink_lanes(sink_ref, h, Q_TILE))
        outs.append(_heads_to_columns(o_t, Q_TILE))
    o_ref[...] = jnp.concatenate(outs, axis=-1).astype(BF16)


def _rope_tables():
    t = np.arange(DEC_SEQ)
    half = HEAD_DIM // 2
    inv_freq = 1.0 / (ROPE_BASE ** (np.arange(0, half, 2, dtype=np.float32) / half))
    ar = (t // GRID_W).astype(np.float32)[:, None] * inv_freq
    ac = (t % GRID_W).astype(np.float32)[:, None] * inv_freq
    return jnp.concatenate([jnp.asarray(a) for a in (ar, ar, ac, ac)] * 2, axis=-1)


def latent_attention(qkv, cache_k, cache_v, layer_j, sink):
    ang = _rope_tables()
    cos, sin = jnp.cos(ang), jnp.sin(ang)
    first = (lax.broadcasted_iota(jnp.int32, ang.shape, 1) % (HEAD_DIM // 2)) < HEAD_DIM // 4
    sin_a = jnp.where(first, -sin, 0.0)
    sin_b = jnp.where(first, 0.0, sin)
    nb = DEC_SEQ // Q_BLOCK
    nm = DEC_SEQ // Q_TILE
    base = ROWS_P // Q_BLOCK
    base_t = ROWS_P // Q_TILE
    kcol, vcol = NQ // NKV, NQ // NKV + 1
    ck = cache_k.reshape(DEC_BATCH, -1, PAST_LEN, NKV)
    cv = cache_v.reshape(DEC_BATCH, -1, PAST_LEN, NKV)

    def edge_spec(col, blk):
        return pl.BlockSpec((Q_BLOCK, NKV), lambda b, m: (base + b * nb + jnp.clip(blk(m), 0, nb - 1), col))

    def mid_spec(col):
        return pl.BlockSpec((Q_TILE, NKV), lambda b, m: (base_t + b * nm + m, col))

    prev_blk, next_blk = (lambda m: 2 * m - 1), (lambda m: 2 * m + 2)
    table = pl.BlockSpec((DEC_SEQ, 128), lambda b, m: (0, 0))
    return pl.pallas_call(
        _lat_attn_kernel,
        out_shape=jax.ShapeDtypeStruct((ROWS_S, NQ), BF16),
        grid=(DEC_BATCH, nm),
        in_specs=[pl.BlockSpec((Q_TILE, NQ), lambda b, m: (base_t + b * nm + m, 0)),
                  edge_spec(kcol, prev_blk), mid_spec(kcol), edge_spec(kcol, next_blk),
                  edge_spec(vcol, prev_blk), mid_spec(vcol), edge_spec(vcol, next_blk),
                  pl.BlockSpec((None, None, PAST_LEN, NKV), lambda b, m: (b, layer_j, 0, 0)),
                  pl.BlockSpec((None, None, PAST_LEN, NKV), lambda b, m: (b, layer_j, 0, 0)),
                  table, table, table,
                  pl.BlockSpec((1, ATTN_HEADS), lambda b, m: (0, 0))],
        out_specs=pl.BlockSpec((Q_TILE, NQ), lambda b, m: (b * nm + m, 0)),
        compiler_params=_cparams(("arbitrary", "arbitrary"),
                                 4 * Q_TILE * NQ * 4 + 12 * Q_TILE * NKV * 4 + 4 * PAST_LEN * NKV * 4
                                 + 6 * DEC_SEQ * 128 * 4 + 16 * ATTN_GROUP * Q_TILE * (4 * Q_BLOCK + PAST_LEN) * 4),
        name="latent_attention",
    )(qkv, qkv, qkv, qkv, qkv, qkv, qkv, ck, cv, cos, sin_a, sin_b, sink.reshape(1, ATTN_HEADS))


def attention_layer(x, layer, layer_j, mods, norm_g, wqkv, wo, sink, cache_k, cache_v):
    qkv = norm_mod_matmul(x, layer, norm_g, mods, wqkv, layer_j, "attn_qkv")
    a_p = context_attention(qkv, sink)
    a_s = latent_attention(qkv, cache_k, cache_v, layer_j, sink)
    x = matmul_gated_residual(a_p, a_s, wo, layer_j, x, layer, mods, "attn_wo")
    new_k = qkv[:ROWS_P, NQ:NQ + NKV].reshape(BATCH, SEQ, ATTN_KV_HEADS, HEAD_DIM)
    new_v = qkv[:ROWS_P, NQ + NKV:].reshape(BATCH, SEQ, ATTN_KV_HEADS, HEAD_DIM)
    return x, new_k, new_v


HG_TILE = 256
HG_LEVELS = 8
HG_IN = 3 * 1024 + 2 * 1024
HG_OUT_TILE = 512
HG_HEADS_PER_STEP = 8


def _hgrn_consts():
    t = np.arange(HG_TILE)
    x = t[:, None] ^ t[None, :]
    hb = np.where(x == 0, -1, np.floor(np.log2(np.maximum(x, 1))).astype(np.int64))
    later = t[:, None] > t[None, :]
    half = HG_TILE // 2
    masks, tris = [], []
    for reverse in (False, True):
        side = ~later & (x != 0) if reverse else later
        lv = [hb == -1] + [(hb == lvl) & side for lvl in range(HG_LEVELS - 1)]
        masks.append(np.stack([m[:half, :half] for m in lv]).astype(np.float32))
        tris.append((t[None, :] >= t[:, None]) if reverse else (t[None, :] <= t[:, None]))
    return jnp.asarray(np.stack(masks)), jnp.asarray(np.stack(tris).astype(np.float32), dtype=BF16)


def _split_bf16(x):
    def top(v):
        bits = lax.bitcast_convert_type(v, jnp.uint32) & jnp.uint32(0xFFFF0000)
        return lax.bitcast_convert_type(bits, F32)

    hi = top(x)
    r = x - hi
    mid = top(r)
    return hi.astype(BF16), mid.astype(BF16), (r - mid).astype(BF16)


def _block_row(x, blk, idx):
    t = x.shape[0]
    x3 = x.reshape(t // blk, blk, x.shape[1])
    return jnp.broadcast_to(x3[:, idx:idx + 1, :], x3.shape).reshape(x.shape)


def _lower_bound(lb_ref, layer, direction):
    x = lb_ref[direction]
    e = jnp.exp(x - jnp.max(x, axis=0, keepdims=True))
    p = e / jnp.sum(e, axis=0, keepdims=True)
    return jnp.sum(p[1:layer + 1, :], axis=0, keepdims=True)


def _hgrn_tile(q, v, z, lb, s_in, mask_ref, tri, reverse):
    t = HG_TILE
    lo, hi = slice(0, t // 2), slice(t // 2, t)
    sg = _sigmoid(z)
    f = lb + (1.0 - lb) * sg
    k = (1.0 - lb) * (1.0 - sg)
    lf3 = _split_bf16(jnp.log(f))
    cum = sum(jnp.dot(tri, p, preferred_element_type=F32) for p in lf3)
    cum2 = cum * LOG2E
    rows = lax.broadcasted_iota(jnp.int32, (t, 1), 0)
    att = [mask_ref[0] * _dot_t(q[r], k[r]) for r in (lo, hi)]
    top = None
    for lvl in range(HG_LEVELS):
        half = 1 << lvl
        bit = (rows & half) != 0
        qside = ~bit if reverse else bit
        if lvl == 0:
            e = jnp.where(qside, f, 1.0)
        else:
            ref = _block_row(cum2, 2 * half, half if reverse else half - 1)
            e = jnp.exp2(-jnp.abs(cum2 - ref))
        w = (jnp.where(qside, q, k) * e).astype(BF16)
        if lvl < HG_LEVELS - 1:
            att = [a + mask_ref[lvl + 1] * _dot_t(w[r], w[r]) for a, r in zip(att, (lo, hi))]
        else:
            top = _dot_t(w[lo], w[hi]) if reverse else _dot_t(w[hi], w[lo])
    vb = v.astype(BF16)
    o_lo, o_hi = _bdot(att[0], vb[lo]), _bdot(att[1], vb[hi])
    if reverse:
        o_lo = o_lo + _bdot(top, vb[hi])
    else:
        o_hi = o_hi + _bdot(top, vb[lo])
    o = jnp.concatenate([o_lo, o_hi], axis=0)
    last = cum[0:1, :] if reverse else cum[t - 1:t, :]
    kd = (k * jnp.exp(last - cum)).astype(BF16)
    s_out = lax.dot_general(kd, vb, (((0,), (0,)), ((), ())), preferred_element_type=F32)
    if s_in is not None:
        o = o + _bdot(q * jnp.exp(cum), s_in)
        ones = jnp.ones((t, HGRN_DV), BF16)
        last_col = sum(lax.dot_general(p, ones, (((0,), (0,)), ((), ())), preferred_element_type=F32) for p in lf3)
        s_out = jnp.exp(last_col) * s_in + s_out
    return o, s_out


def _hgrn_prompt_kernel(q_ref, v_ref, zf_ref, zb_ref, lb_ref, mask_ref, tri_ref, of_ref, ob_ref, sfin_ref, *, layer):
    for j in range(HG_HEADS_PER_STEP):
        hs = slice(j * HGRN_DK, (j + 1) * HGRN_DK)
        q, v = q_ref[:, hs], v_ref[:, hs]
        lb_f, lb_b = (_lower_bound(lb_ref.at[:, :, hs], layer, d) for d in range(2))
        of_ref[:, hs], sfin_ref[0, j] = _hgrn_tile(q, v, zf_ref[:, hs], lb_f, None, mask_ref.at[0], tri_ref[0], False)
        ob_ref[:, hs], sfin_ref[1, j] = _hgrn_tile(q, v, zb_ref[:, hs], lb_b, None, mask_ref.at[1], tri_ref[1], True)


def _hgrn_sample_kernel(qf_ref, vf_ref, zf_ref, qb_ref, vb_ref, zb_ref, lb_ref, s0_ref, mask_ref, tri_ref,
                        of_ref, ob_ref, state_ref, *, layer):
    tiles = DEC_SEQ // HG_TILE

    @pl.when(pl.program_id(1) % tiles == 0)
    def _():
        state_ref[...] = s0_ref[...]

    for j in range(HG_HEADS_PER_STEP):
        hs = slice(j * HGRN_DK, (j + 1) * HGRN_DK)
        lb_f, lb_b = (_lower_bound(lb_ref.at[:, :, hs], layer, d) for d in range(2))
        of_ref[:, hs], state_ref[0, j] = _hgrn_tile(qf_ref[:, hs], vf_ref[:, hs], zf_ref[:, hs], lb_f,
                                                    state_ref[0, j], mask_ref.at[0], tri_ref[0], False)
        ob_ref[:, hs], state_ref[1, j] = _hgrn_tile(qb_ref[:, hs], vb_ref[:, hs], zb_ref[:, hs], lb_b,
                                                    state_ref[1, j], mask_ref.at[1], tri_ref[1], True)


def hgrn_scan(proj, lb_raw, layer, state_hgrn, layer_j):
    masks, tris = _hgrn_consts()
    hps = HG_HEADS_PER_STEP
    hw = hps * HGRN_DK
    qc, vc, zfc, zbc = 0, 1024 // hw, 2048 // hw, 3072 // hw
    const_specs = [pl.BlockSpec((2, DEPTH, hw), lambda h, i: (0, 0, h))]
    mask_specs = [pl.BlockSpec(masks.shape, lambda h, i: (0, 0, 0, 0)),
                  pl.BlockSpec((2, HG_TILE, HG_TILE), lambda h, i: (0, 0, 0))]
    vmem = 4 * masks.size * 4 + 16 * HG_TILE * hw * 4 + 8 * hw * HGRN_DV * 4 + 24 * hps * HG_TILE * HG_TILE * 4

    def col(cb, row_fn):
        return pl.BlockSpec((HG_TILE, hw), lambda h, i: (row_fn(i), cb + h))

    o_shape = jax.ShapeDtypeStruct((ROWS_P, HGRN_HEADS * HGRN_DV), F32)
    same = lambda i: i
    of_p, ob_p, sfin = pl.pallas_call(
        functools.partial(_hgrn_prompt_kernel, layer=layer),
        out_shape=(o_shape, o_shape, jax.ShapeDtypeStruct((BATCH, 2, HGRN_HEADS, HGRN_DK, HGRN_DV), F32)),
        grid=(HGRN_HEADS // hps, BATCH),
        in_specs=[col(qc, same), col(vc, same), col(zfc, same), col(zbc, same)] + const_specs + mask_specs,
        out_specs=(col(0, same), col(0, same),
                   pl.BlockSpec((None, 2, hps, HGRN_DK, HGRN_DV), lambda h, i: (i, 0, h, 0, 0))),
        compiler_params=_cparams(("arbitrary", "arbitrary"), vmem),
        name="hgrn_scan_prompt",
    )(proj, proj, proj, proj, lb_raw, masks, tris)

    tiles = DEC_SEQ // HG_TILE
    base = ROWS_P // HG_TILE
    bwd = lambda i: (i // tiles) * tiles + (tiles - 1 - i % tiles)
    fwd_in = lambda i: base + i
    bwd_in = lambda i: base + bwd(i)
    o_shape = jax.ShapeDtypeStruct((ROWS_S, HGRN_HEADS * HGRN_DV), F32)
    of_s, ob_s = pl.pallas_call(
        functools.partial(_hgrn_sample_kernel, layer=layer),
        out_shape=(o_shape, o_shape),
        grid=(HGRN_HEADS // hps, DEC_BATCH * tiles),
        in_specs=[col(qc, fwd_in), col(vc, fwd_in), col(zfc, fwd_in), col(qc, bwd_in), col(vc, bwd_in),
                  col(zbc, bwd_in)]
        + const_specs
        + [pl.BlockSpec((None, None, 2, hps, HGRN_DK, HGRN_DV), lambda h, i: (i // tiles, layer_j, 0, h, 0, 0))]
        + mask_specs,
        out_specs=(col(0, same), col(0, bwd)),
        scratch_shapes=[pltpu.VMEM((2, hps, HGRN_DK, HGRN_DV), F32)],
        compiler_params=_cparams(("arbitrary", "arbitrary"), vmem),
        name="hgrn_scan_sample",
    )(proj, proj, proj, proj, proj, proj, lb_raw, state_hgrn, masks, tris)
    return (of_p, ob_p), (of_s, ob_s), sfin


def _hgrn_out_kernel(ofp_ref, obp_ref, ofs_ref, obs_ref, g_ref, gn_ref, w_ref, x_ref, mod_ref, o_ref):
    def run(of_ref, ob_ref):
        gn = gn_ref[...]
        parts = []
        for h in range(HGRN_HEADS):
            hs = slice(h * HGRN_DV, (h + 1) * HGRN_DV)
            o = of_ref[:, hs] + ob_ref[:, hs]
            g = g_ref[:, hs]
            o = o * lax.rsqrt(jnp.mean(o * o, axis=-1, keepdims=True) + NORM_EPS) * gn * _silu(g)
            parts.append(o.astype(BF16))
        y = jnp.dot(jnp.concatenate(parts, axis=-1), w_ref[...], preferred_element_type=F32)
        o_ref[...] = x_ref[...] + mod_ref[2:3, :] * y

    is_prompt = pl.program_id(0) < ROWS_P // HG_OUT_TILE
    pl.when(is_prompt)(lambda: run(ofp_ref, obp_ref))
    pl.when(jnp.logical_not(is_prompt))(lambda: run(ofs_ref, obs_ref))


def hgrn_out(o_p, o_s, proj, g_norm, wo, w_idx, x, layer, mods):
    gcol = 4096 // D
    tile = HG_OUT_TILE
    p_spec, s_spec = _group_specs(D, tile)
    return pl.pallas_call(
        _hgrn_out_kernel,
        out_shape=jax.ShapeDtypeStruct((ROWS, D), F32),
        grid=(ROWS // tile,),
        in_specs=[p_spec, p_spec, s_spec, s_spec,
                  pl.BlockSpec((tile, D), lambda i: (i, gcol)),
                  pl.BlockSpec((1, HGRN_DV), lambda i: (0, 0)),
                  pl.BlockSpec((None, D, D), lambda i: (w_idx, 0, 0), pipeline_mode=pl.Buffered(1)),
                  pl.BlockSpec((tile, D), lambda i: (i, 0)),
                  _mod_spec(layer, 1, tile)],
        out_specs=pl.BlockSpec((tile, D), lambda i: (i, 0)),
        compiler_params=_cparams(("arbitrary",), 22 * tile * D * 4),
        name="hgrn_out",
    )(*o_p, *o_s, proj, g_norm.reshape(1, HGRN_DV), wo, x, mods)


def hgrn_layer(x, layer, layer_j, mods, norm_g, w_in, hgrn_lb, g_norm, wo, state_hgrn):
    proj = norm_mod_matmul(x, layer, norm_g, mods, w_in, layer_j, "hgrn_in")
    o_p, o_s, sfin = hgrn_scan(proj, jnp.transpose(hgrn_lb, (1, 0, 2)), layer, state_hgrn, layer_j)
    x = hgrn_out(o_p, o_s, proj, g_norm[layer_j], wo, layer_j, x, layer, mods)
    return x, sfin


SSM_N = SSM_GROUPS * SSM_STATE
SSM_KT = 8
SSM_ROWS = 256


def _ssm_prep_kernel(are_ref, aim_ref, ldt_ref, bre_ref, bim_ref, lre_ref, lim_ref, bbre_ref, bbim_ref):
    a_re = jnp.minimum(are_ref[...], -1e-4)
    a_im = aim_ref[...]
    dt = jnp.exp(ldt_ref[...])
    mag = jnp.exp(a_re * dt)
    l_re = mag * jnp.cos(a_im * dt)
    l_im = mag * jnp.sin(a_im * dt)
    lre_ref[...] = l_re
    lim_ref[...] = l_im
    den = a_re * a_re + a_im * a_im
    c_re = ((l_re - 1.0) * a_re + l_im * a_im) / den
    c_im = (l_im * a_re - (l_re - 1.0) * a_im) / den
    b_re, b_im = bre_ref[...], bim_ref[...]
    bbre_ref[...] = c_re[:, None, :] * b_re - c_im[:, None, :] * b_im
    bbim_ref[...] = c_re[:, None, :] * b_im + c_im[:, None, :] * b_re


def ssm_discretize(a_re, a_im, log_dt, b_re, b_im):
    g2 = 2 * SSM_GROUPS
    sh = jax.ShapeDtypeStruct((g2, SSM_STATE), F32)
    shb = jax.ShapeDtypeStruct((g2, SSM_GROUP, SSM_STATE), F32)
    bt = lambda b: jnp.transpose(b, (0, 1, 3, 2)).reshape(g2, SSM_GROUP, SSM_STATE)
    return pl.pallas_call(_ssm_prep_kernel, out_shape=(sh, sh, shb, shb), name="ssm_discretize")(
        a_re.reshape(g2, SSM_STATE), a_im.reshape(g2, SSM_STATE), log_dt.reshape(g2, 1), bt(b_re), bt(b_im))


def _ssm_block_diag(l_re, l_im, bb_re, bb_im, c_re, c_im):
    nk = SSM_GROUPS // SSM_KT
    eye = jnp.eye(SSM_KT, dtype=F32)

    def bmat(b):
        b = b.reshape(2, nk, SSM_KT, SSM_GROUP, SSM_STATE)
        return jnp.einsum('dkgip,gh->dkgihp', b, eye).reshape(2, nk, SSM_KT * SSM_GROUP, SSM_KT * SSM_STATE)

    def cmat(c):
        c = c.reshape(2, nk, SSM_KT, SSM_GROUP, SSM_STATE)
        return jnp.einsum('dkgip,gh->dkhpgi', c, eye).reshape(2, nk, SSM_KT * SSM_STATE, SSM_KT * SSM_GROUP)

    b_mat = jnp.concatenate([bmat(bb_re), bmat(bb_im)], axis=-1).astype(BF16)
    c_mat = jnp.concatenate([cmat(c_re), cmat(-c_im)], axis=-2).astype(BF16)
    lam = jnp.stack([l_re.reshape(2, SSM_N), l_im.reshape(2, SSM_N)], axis=1)
    return b_mat, c_mat, lam


X4_SHAPE = (ROWS // (4 * SEQ), 4, SEQ, D)


def _tm_geometry(prompt):
    if prompt:
        batch = BATCH
        steps = SSM_ROWS // batch
        return batch, steps, (4, 4, steps, D), (lambda i: (0, 0, i, 0)), SEQ // steps, (0,) * batch
    batch = DEC_BATCH
    steps = SSM_ROWS // batch
    per_q = SEQ // steps
    return (batch, steps, (4, 1, steps, D), (lambda i: (1, i // per_q, i % per_q, 0)), DEC_SEQ // steps,
            tuple(range(1, 1 + batch)))


def _x4_seq(ref, b):
    return ref.at[b // ref.shape[1], b % ref.shape[1]]


LANE_SLABS = D // 128


def _slab_store(s_ref, rows, val):
    for c in range(LANE_SLABS):
        s_ref[c, rows, :] = val[:, c * 128:(c + 1) * 128]


def _slab_load(s_ref, rows):
    return jnp.concatenate([s_ref[c, rows, :] for c in range(LANE_SLABS)], axis=-1)


def _normmod_tm_kernel(x_ref, g_ref, mod_ref, o_ref, s_ref, *, batch, steps, mod_rows):
    g = g_ref[...]
    for b in range(batch):
        m = mod_ref.at[mod_rows[b]]
        _slab_store(s_ref, slice(b * steps, (b + 1) * steps),
                    _norm_mod(_x4_seq(x_ref, b)[...], g, m[0:1, :], m[1:2, :]))
    for t in range(steps):
        o_ref[t * batch:(t + 1) * batch, :] = _slab_load(s_ref, pl.ds(t, batch, stride=steps))


def norm_mod_time_major(x4, layer, norm_g, mods, prompt):
    batch, steps, blk, idx, tiles, mod_rows = _tm_geometry(prompt)
    return pl.pallas_call(
        functools.partial(_normmod_tm_kernel, batch=batch, steps=steps, mod_rows=mod_rows),
        out_shape=jax.ShapeDtypeStruct((tiles * SSM_ROWS, D), F32),
        grid=(tiles,),
        in_specs=[pl.BlockSpec(blk, idx),
                  pl.BlockSpec((None, 1, D), lambda i: (layer, 0, 0)),
                  pl.BlockSpec((None, MOD_ROWS, 6, D), lambda i: (layer, 0, 0, 0))],
        out_specs=pl.BlockSpec((SSM_ROWS, D), lambda i: (i, 0)),
        scratch_shapes=[pltpu.VMEM((LANE_SLABS, SSM_ROWS, 128), F32)],
        compiler_params=_cparams(("arbitrary",), 8 * SSM_ROWS * D * 4),
        name="ssm_norm_time_major",
    )(x4, norm_g.reshape(DEPTH, 1, D), mods)


def _ssm_scan_kernel(xf_ref, xb_ref, bm_ref, cm_ref, lam_ref, h0_ref, yf_ref, yb_ref, hfin_ref,
                     hre_f, him_f, hre_b, him_b, st_ref, *, batch):
    i = pl.program_id(0)
    steps = SSM_ROWS // batch
    nk = SSM_GROUPS // SSM_KT
    kw = SSM_KT * SSM_STATE
    x_refs, y_refs = (xf_ref, xb_ref), (yf_ref, yb_ref)
    h_refs = ((hre_f, him_f), (hre_b, him_b))

    @pl.when(i == 0)
    def _():
        st_ref[...] = h0_ref[...]

    def tile_cols(k):
        return slice(k * kw, (k + 1) * kw)

    def bu_tile(d, k):
        xk = x_refs[d][:, k * 128:(k + 1) * 128].astype(BF16)
        bu = jnp.dot(xk, bm_ref[d, k], preferred_element_type=F32)
        h_refs[d][0][:, tile_cols(k)] = bu[:, :kw]
        h_refs[d][1][:, tile_cols(k)] = bu[:, kw:]

    def c_tile(d, k):
        hk = jnp.concatenate([h_refs[d][0][:, tile_cols(k)], h_refs[d][1][:, tile_cols(k)]], axis=-1)
        y_refs[d][:, k * 128:(k + 1) * 128] = jnp.dot(hk.astype(BF16), cm_ref[d, k], preferred_element_type=F32)

    def scan_tile(d, k):
        hre_ref, him_ref = h_refs[d]
        col = tile_cols(k)
        l_re, l_im = lam_ref[d, 0, :, col], lam_ref[d, 1, :, col]
        h_re, h_im = st_ref[d, 0, :, col], st_ref[d, 1, :, col]
        per = max(8 // batch, 1)
        rows_per = per * batch
        for s in range(steps // per):
            g = (steps // per - 1 - s) if d else s
            rows = slice(g * rows_per, (g + 1) * rows_per)
            cur_re, cur_im = hre_ref[rows, col], him_ref[rows, col]
            outs_re, outs_im = [None] * per, [None] * per
            for r in (range(per - 1, -1, -1) if d else range(per)):
                b_re, b_im = cur_re[r * batch:(r + 1) * batch], cur_im[r * batch:(r + 1) * batch]
                h_re, h_im = l_re * h_re - l_im * h_im + b_re, l_re * h_im + l_im * h_re + b_im
                outs_re[r], outs_im[r] = h_re, h_im
            hre_ref[rows, col] = outs_re[0] if per == 1 else jnp.concatenate(outs_re, axis=0)
            him_ref[rows, col] = outs_im[0] if per == 1 else jnp.concatenate(outs_im, axis=0)
        st_ref[d, 0, :, col] = h_re
        st_ref[d, 1, :, col] = h_im

    for k in range(nk + 2):
        for d in range(2):
            if k < nk:
                bu_tile(d, k)
            if 1 <= k <= nk:
                scan_tile(d, k - 1)
            if k >= 2:
                c_tile(d, k - 2)

    @pl.when(i == pl.num_programs(0) - 1)
    def _():
        hfin_ref[...] = st_ref[...]


def ssm_scan(xn_tm, b_mat, c_mat, lam, h0, batch):
    rows = xn_tm.shape[0]
    n = rows // SSM_ROWS
    lam_b = jnp.broadcast_to(lam[:, :, None, :], (2, 2, batch, SSM_N))
    y_shape = jax.ShapeDtypeStruct((rows, D), F32)
    full = lambda a: pl.BlockSpec(a.shape, lambda i: (0,) * a.ndim)
    vmem = (4 * SSM_ROWS * SSM_N * 4 + 8 * SSM_ROWS * D * 4 + 2 * (b_mat.size + c_mat.size) * 2
            + 12 * batch * SSM_N * 4 * 2 + 8 * SSM_ROWS * 1024 * 4)
    return pl.pallas_call(
        functools.partial(_ssm_scan_kernel, batch=batch),
        out_shape=(y_shape, y_shape, jax.ShapeDtypeStruct((2, 2, batch, SSM_N), F32)),
        grid=(n,),
        in_specs=[pl.BlockSpec((SSM_ROWS, D), lambda i: (i, 0)),
                  pl.BlockSpec((SSM_ROWS, D), lambda i: (n - 1 - i, 0)),
                  full(b_mat), full(c_mat), full(lam_b), full(h0)],
        out_specs=(pl.BlockSpec((SSM_ROWS, D), lambda i: (i, 0)),
                   pl.BlockSpec((SSM_ROWS, D), lambda i: (n - 1 - i, 0)),
                   pl.BlockSpec((2, 2, batch, SSM_N), lambda i: (0, 0, 0, 0))),
        scratch_shapes=[pltpu.VMEM((SSM_ROWS, SSM_N), F32)] * 4
        + [pltpu.VMEM((2, 2, batch, SSM_N), F32)],
        compiler_params=_cparams(("arbitrary",), vmem),
        name="ssm_scan",
    )(xn_tm, xn_tm, b_mat, c_mat, lam_b, h0)


def _gelu_tanh(x):
    return 0.5 * x * (1.0 + jnp.tanh(math.sqrt(2.0 / math.pi) * (x + 0.044715 * (x * x * x))))


def _ssm_glu_kernel(yf_ref, yb_ref, xn_ref, d_ref, w_ref, x_ref, mod_ref, o_ref, s_ref,
                    *, batch, steps, mod_rows):
    g = _gelu_tanh(yf_ref[...] + yb_ref[...] + d_ref[...] * xn_ref[...])
    u = jnp.dot(g.astype(BF16), w_ref[...], preferred_element_type=F32)
    _slab_store(s_ref, slice(None), u[:, :D] * _sigmoid(u[:, D:]))
    for b in range(batch):
        gate = mod_ref[mod_rows[b], 2:3, :]
        _x4_seq(o_ref, b)[...] = (_x4_seq(x_ref, b)[...]
                                  + gate * _slab_load(s_ref, pl.ds(b, steps, stride=batch)))


def ssm_glu(yf, yb, xn, d, w_glu, w_idx, x4, layer, mods, prompt):
    batch, steps, blk, idx, tiles, mod_rows = _tm_geometry(prompt)
    tm_spec = pl.BlockSpec((SSM_ROWS, D), lambda i: (i, 0))
    out = pl.pallas_call(
        functools.partial(_ssm_glu_kernel, batch=batch, steps=steps, mod_rows=mod_rows),
        out_shape=jax.ShapeDtypeStruct((4,) + X4_SHAPE[1:], F32),
        grid=(tiles,),
        in_specs=[tm_spec, tm_spec, tm_spec,
                  pl.BlockSpec((None, 1, D), lambda i: (w_idx, 0, 0)),
                  pl.BlockSpec((None, D, 2 * D), lambda i: (w_idx, 0, 0), pipeline_mode=pl.Buffered(1)),
                  pl.BlockSpec(blk, idx),
                  pl.BlockSpec((None, MOD_ROWS, 6, D), lambda i: (layer, 0, 0, 0))],
        out_specs=pl.BlockSpec(blk, lambda i: (0,) + idx(i)[1:]),
        scratch_shapes=[pltpu.VMEM((LANE_SLABS, SSM_ROWS, 128), F32)],
        compiler_params=_cparams(("arbitrary",), 24 * SSM_ROWS * D * 4 + D * 2 * D * 2),
        name="ssm_glu",
    )(yf, yb, xn, d.reshape(-1, 1, D), w_glu, x4, mods)
    return out.reshape(-1, D)


def ssm_layer(x, layer, layer_j, mods, norm_g, a_re, a_im, log_dt, b_re, b_im, c_re, c_im, d, w_glu, state_ssm):
    l_re, l_im, bb_re, bb_im = ssm_discretize(a_re[layer_j], a_im[layer_j], log_dt[layer_j], b_re[layer_j],
                                              b_im[layer_j])
    b_mat, c_mat, lam = _ssm_block_diag(l_re, l_im, bb_re, bb_im, c_re[layer_j], c_im[layer_j])
    x4 = x.reshape(X4_SHAPE)
    xn_p = norm_mod_time_major(x4, layer, norm_g, mods, True)
    xn_s = norm_mod_time_major(x4, layer, norm_g, mods, False)
    h0_p = jnp.zeros((2, 2, BATCH, SSM_N), F32)
    h0_s = jnp.transpose(state_ssm[:, layer_j].reshape(DEC_BATCH, 2, SSM_N, 2), (1, 3, 0, 2))
    yfp, ybp, hfin = ssm_scan(xn_p, b_mat, c_mat, lam, h0_p, BATCH)
    yfs, ybs, _ = ssm_scan(xn_s, b_mat, c_mat, lam, h0_s, DEC_BATCH)
    out_p = ssm_glu(yfp, ybp, xn_p, d, w_glu, layer_j, x4, layer, mods, True)
    out_s = ssm_glu(yfs, ybs, xn_s, d, w_glu, layer_j, x4, layer, mods, False)
    new_state = jnp.transpose(hfin, (2, 0, 3, 1)).reshape(BATCH, 2, SSM_GROUPS, SSM_STATE, 2)
    return (out_p, out_s), new_state


def kernel(x_prompt, x_sample, cache_k, cache_v, state_hgrn, state_ssm, c, c_ctx, ada_w, ada_b, norm1_g, norm2_g, attn_wqkv, attn_wo, attn_sink, hgrn_w_in, hgrn_lb, hgrn_g_norm, hgrn_wo, ssm_a_re, ssm_a_im, ssm_log_dt, ssm_b_re, ssm_b_im, ssm_c_re, ssm_c_im, ssm_d, ssm_w_glu, ffn_w_up, ffn_conv_w, ffn_conv_b, ffn_w_down, final_g):
    cond8 = jnp.zeros((MOD_ROWS, D), F32).at[0].set(c_ctx).at[1:1 + DEC_BATCH].set(c)
    mods = ada_modulation(cond8, ada_w, ada_b)
    x = (x_prompt.reshape(ROWS_P, D), x_sample.reshape(ROWS_S, D))
    wqkv, wo, w_in, hwo, w_glu, w_up, w_down = (w.astype(BF16) for w in (
        attn_wqkv, attn_wo, hgrn_w_in, hgrn_wo, ssm_w_glu, ffn_w_up, ffn_w_down))
    new_k, new_v, new_hgrn, new_ssm = [], [], [], []
    for l in range(DEPTH):
        kind, j = l % N_MIXERS, l // N_MIXERS
        if kind == 0:
            x, k, v = attention_layer(x, l, j, mods, norm1_g, wqkv, wo, attn_sink[j], cache_k, cache_v)
            new_k.append(k)
            new_v.append(v)
        elif kind == 1:
            x, s = hgrn_layer(x, l, j, mods, norm1_g, w_in, hgrn_lb, hgrn_g_norm, hwo, state_hgrn)
            new_hgrn.append(s)
        else:
            x, s = ssm_layer(x, l, j, mods, norm1_g, ssm_a_re, ssm_a_im, ssm_log_dt, ssm_b_re, ssm_b_im,
                             ssm_c_re, ssm_c_im, ssm_d, w_glu, state_ssm)
            new_ssm.append(s)
        x = conv_ffn_residual(x, l, norm2_g, mods, w_up, ffn_conv_w, ffn_conv_b, w_down)
    y_prompt = final_norm(x, final_g, 0, N_ROW_TILES_P).reshape(BATCH, SEQ, D)
    y_sample = final_norm(x, final_g, N_ROW_TILES_P, N_ROW_TILES - N_ROW_TILES_P).reshape(DEC_BATCH, DEC_SEQ, D)
    return (y_prompt, y_sample, jnp.stack(new_k, axis=1), jnp.stack(new_v, axis=1),
            jnp.stack(new_hgrn, axis=1), jnp.stack(new_ssm, axis=1))
```

```python
import functools
import math

import jax
import jax.numpy as jnp
import numpy as np
from jax import lax
from jax.experimental import pallas as pl
from jax.experimental.pallas import tpu as pltpu

F32 = jnp.float32
BF16 = jnp.bfloat16

D = 1024
BATCH = 16
SEQ = 256
DEPTH = 4
DEC_BATCH = 4
DEC_SEQ = 1024
PAST_LEN = 512
GRID_W = 64
N_MIXERS = 3
ATTN_HEADS = 16
ATTN_KV_HEADS = 4
ATTN_GROUP = ATTN_HEADS // ATTN_KV_HEADS
HEAD_DIM = D // ATTN_HEADS
WINDOW = 128
ROPE_BASE = 10000.0
HGRN_HEADS = 8
HGRN_DK = 128
HGRN_DV = 128
SSM_GROUP = 16
SSM_GROUPS = D // SSM_GROUP
SSM_STATE = 64
D_FF = 2816
NORM_EPS = 1e-6

ROWS_P = BATCH * SEQ
ROWS_S = DEC_BATCH * DEC_SEQ
ROWS = ROWS_P + ROWS_S
ROW_TILE = 1024
N_ROW_TILES = ROWS // ROW_TILE
N_ROW_TILES_P = ROWS_P // ROW_TILE
MOD_ROWS = 8
V7X_VMEM_BYTES = 64 * 1024 * 1024


def _mod_row(i, tile=ROW_TILE):
    return jnp.where(i < ROWS_P // tile, 0, (i - ROWS_P // tile) // (DEC_SEQ // tile) + 1)


def _cparams(semantics, vmem_bytes):
    vmem = int(min(max(vmem_bytes * 5 // 4 + (4 << 20), 16 << 20), V7X_VMEM_BYTES - (6 << 20)))
    return pltpu.CompilerParams(dimension_semantics=semantics, vmem_limit_bytes=vmem)


def _bdot(a, b):
    return jnp.dot(a.astype(BF16), b.astype(BF16), preferred_element_type=F32)


def _norm_mod(x, g, shift, scale):
    y = x * lax.rsqrt(jnp.mean(x * x, axis=-1, keepdims=True) + NORM_EPS) * g
    return y * (1.0 + scale) + shift


def _sigmoid(x):
    return 0.5 + 0.5 * jnp.tanh(0.5 * x)


def _silu(x):
    h = 0.5 * x
    return h + h * jnp.tanh(h)


def _ada_kernel(c_ref, w_ref, b_ref, o_ref):
    c = c_ref[...]
    o_ref[...] = _bdot(_silu(c), w_ref[...]) + b_ref[...]


def ada_modulation(cond8, ada_w, ada_b):
    tn = 1024
    out = pl.pallas_call(
        _ada_kernel,
        out_shape=jax.ShapeDtypeStruct((DEPTH, MOD_ROWS, 6 * D), F32),
        grid=(DEPTH, 6 * D // tn),
        in_specs=[
            pl.BlockSpec((MOD_ROWS, D), lambda l, j: (0, 0)),
            pl.BlockSpec((None, D, tn), lambda l, j: (l, 0, j)),
            pl.BlockSpec((None, 1, tn), lambda l, j: (l, 0, j)),
        ],
        out_specs=pl.BlockSpec((None, MOD_ROWS, tn), lambda l, j: (l, 0, j)),
        compiler_params=_cparams(("arbitrary", "arbitrary"), 2 * D * tn * 4),
        name="ada_modulation",
    )(cond8, ada_w, ada_b.reshape(DEPTH, 1, 6 * D))
    return out.reshape(DEPTH, MOD_ROWS, 6, D)


def _x_operands(x, n_grid=1):
    if isinstance(x, tuple):
        return x, list(_group_specs(D, ROW_TILE, n_grid))
    return (x,), [pl.BlockSpec((ROW_TILE, D), (lambda i: (i, 0)) if n_grid == 1 else (lambda i, j: (i, 0)))]


def _read_rows(x_refs):
    if len(x_refs) == 1:
        return x_refs[0][...]
    return jnp.where(pl.program_id(0) < N_ROW_TILES_P, x_refs[0][...], x_refs[1][...])


def _nmm_kernel(*refs):
    x_refs, (g_ref, mod_ref, w_ref, o_ref, h_ref) = refs[:-5], refs[-5:]

    @pl.when(pl.program_id(1) == 0)
    def _():
        h_ref[...] = _norm_mod(_read_rows(x_refs), g_ref[...], mod_ref[0:1, :], mod_ref[1:2, :]).astype(BF16)

    o_ref[...] = jnp.dot(h_ref[...], w_ref[...], preferred_element_type=F32)


def _mod_spec(layer, n_grid, tile=ROW_TILE):
    if n_grid == 1:
        return pl.BlockSpec((None, None, 6, D), lambda i: (layer, _mod_row(i, tile), 0, 0))
    return pl.BlockSpec((None, None, 6, D), lambda i, j: (layer, _mod_row(i, tile), 0, 0))


def norm_mod_matmul(x, layer, norm_g, mods, w, w_idx, name):
    n = w.shape[-1]
    tn = 1024 if n % 1024 == 0 else 768
    x_ops, x_specs = _x_operands(x, 2)
    return pl.pallas_call(
        _nmm_kernel,
        out_shape=jax.ShapeDtypeStruct((ROWS, n), F32),
        grid=(N_ROW_TILES, n // tn),
        in_specs=x_specs + [
            pl.BlockSpec((None, 1, D), lambda i, j: (layer, 0, 0)),
            _mod_spec(layer, 2),
            pl.BlockSpec((None, D, tn), lambda i, j: (w_idx, 0, j)),
        ],
        out_specs=pl.BlockSpec((ROW_TILE, tn), lambda i, j: (i, j)),
        scratch_shapes=[pltpu.VMEM((ROW_TILE, D), BF16)],
        compiler_params=_cparams(("arbitrary", "arbitrary"),
                                 4 * ROW_TILE * D * 4 + ROW_TILE * D * 2 + 2 * D * tn * 2 + 2 * ROW_TILE * tn * 4),
        name=name,
    )(*x_ops, norm_g.reshape(DEPTH, 1, D), mods, w)


def _mm_res_kernel(ap_ref, as_ref, w_ref, *refs):
    x_refs, (mod_ref, o_ref) = refs[:-2], refs[-2:]

    def run(a_ref, x_ref):
        y = jnp.dot(a_ref[...].astype(BF16), w_ref[...], preferred_element_type=F32)
        o_ref[...] = x_ref[...] + mod_ref[2:3, :] * y

    is_prompt = pl.program_id(0) < N_ROW_TILES_P
    pl.when(is_prompt)(lambda: run(ap_ref, x_refs[0]))
    pl.when(jnp.logical_not(is_prompt))(lambda: run(as_ref, x_refs[-1]))


def _group_specs(k, tile=ROW_TILE, n_grid=1):
    n_p = ROWS_P // tile
    if n_grid == 1:
        return (pl.BlockSpec((tile, k), lambda i: (jnp.minimum(i, n_p - 1), 0)),
                pl.BlockSpec((tile, k), lambda i: (jnp.maximum(i - n_p, 0), 0)))
    return (pl.BlockSpec((tile, k), lambda i, j: (jnp.minimum(i, n_p - 1), 0)),
            pl.BlockSpec((tile, k), lambda i, j: (jnp.maximum(i - n_p, 0), 0)))


def matmul_gated_residual(a_p, a_s, w, w_idx, x, layer, mods, name):
    k = a_p.shape[1]
    x_ops, x_specs = _x_operands(x)
    return pl.pallas_call(
        _mm_res_kernel,
        out_shape=jax.ShapeDtypeStruct((ROWS, D), F32),
        grid=(N_ROW_TILES,),
        in_specs=[
            *_group_specs(k),
            pl.BlockSpec((None, k, D), lambda i: (w_idx, 0, 0), pipeline_mode=pl.Buffered(1)),
            *x_specs,
            _mod_spec(layer, 1),
        ],
        out_specs=pl.BlockSpec((ROW_TILE, D), lambda i: (i, 0)),
        compiler_params=_cparams(("arbitrary",), 4 * ROW_TILE * k * 2 + k * D * 2 + 7 * ROW_TILE * D * 4),
        name=name,
    )(a_p, a_s, w, *x_ops, mods)


FFN_CHUNK = 256
FFN_CHUNKS = D_FF // FFN_CHUNK
FFN_DOWN_GROUP = FFN_CHUNKS
CONV_PAD = 8


def _ffn_kernel(*refs, split_x, first_tile, final_norm):
    n_x = 1 + split_x
    x_refs = refs[:n_x]
    g_ref, mod_ref, wup_ref, cw_ref, cb_ref, wd_ref = refs[n_x:n_x + 6]
    fg_ref = refs[n_x + 6] if final_norm else None
    o_ref, h_ref, pad_a, pad_b, act_ref = refs[n_x + 6 + final_norm:]
    i = pl.program_id(0) + first_tile
    h_ref[...] = _norm_mod(_read_rows(x_refs), g_ref[...], mod_ref[3:4, :], mod_ref[4:5, :]).astype(BF16)
    zeros = jnp.zeros((CONV_PAD, 2 * FFN_CHUNK), F32)
    for pad_ref in (pad_a, pad_b):
        pad_ref[0:CONV_PAD, :] = zeros
        pad_ref[CONV_PAD + ROW_TILE:, :] = zeros
    sub = lax.broadcasted_iota(jnp.int32, (8, 1), 0)
    is_prompt = i < N_ROW_TILES_P
    keep_first = jnp.where((sub == 0) & is_prompt, 0.0, 1.0)
    keep_last = jnp.where((sub == 7) & is_prompt, 0.0, 1.0)

    def cut_sequences(v, keep, row):
        parts, at = [], 0
        for b in range(SEQ, ROW_TILE, SEQ):
            lo = b if row == 0 else b - 8
            parts += [v[at:lo], v[lo:lo + 8] * keep]
            at = lo + 8
        return jnp.concatenate(parts + [v[at:]], axis=0)

    def cols(ref, c):
        return (ref[:, c * FFN_CHUNK:(c + 1) * FFN_CHUNK],
                ref[:, D_FF + c * FFN_CHUNK:D_FF + (c + 1) * FFN_CHUNK])

    def up_proj(c, pad_ref):
        hb = h_ref[...]
        wg, wv = cols(wup_ref, c)
        pad_ref[CONV_PAD:CONV_PAD + ROW_TILE, :FFN_CHUNK] = jnp.dot(hb, wg, preferred_element_type=F32)
        pad_ref[CONV_PAD:CONV_PAD + ROW_TILE, FFN_CHUNK:] = jnp.dot(hb, wv, preferred_element_type=F32)

    def conv_act(c, pad_ref):
        up = pad_ref[CONV_PAD:CONV_PAD + ROW_TILE, :]
        prev = cut_sequences(pad_ref[CONV_PAD - 1:CONV_PAD - 1 + ROW_TILE, :], keep_first, 0)
        nxt = cut_sequences(pad_ref[CONV_PAD + 1:CONV_PAD + 1 + ROW_TILE, :], keep_last, 7)
        cw = jnp.concatenate(cols(cw_ref, c), axis=-1)
        cb = jnp.concatenate(cols(cb_ref, c), axis=-1)
        conv = prev * cw[0:1, :] + up * cw[1:2, :] + nxt * cw[2:3, :] + cb
        gate = conv[:, :FFN_CHUNK]
        act = _silu(gate) * conv[:, FFN_CHUNK:]
        act_ref[:, c * FFN_CHUNK:(c + 1) * FFN_CHUNK] = act.astype(BF16)

    pads = (pad_a, pad_b)
    done = 0
    up_proj(0, pads[0])
    for c in range(FFN_CHUNKS):
        if c + 1 < FFN_CHUNKS:
            up_proj(c + 1, pads[(c + 1) % 2])
        conv_act(c, pads[c % 2])
        if (c + 1) % FFN_DOWN_GROUP == 0 or c + 1 == FFN_CHUNKS:
            rows = slice(done * FFN_CHUNK, (c + 1) * FFN_CHUNK)
            part = jnp.dot(act_ref[:, rows], wd_ref[rows, :], preferred_element_type=F32)
            acc = part if done == 0 else o_ref[...] + part
            if c + 1 == FFN_CHUNKS:
                acc = _read_rows(x_refs) + mod_ref[5:6, :] * acc
                if final_norm:
                    acc = acc * lax.rsqrt(jnp.mean(acc * acc, axis=-1, keepdims=True) + NORM_EPS) * fg_ref[...]
            o_ref[...] = acc
            done = c + 1


def conv_ffn_residual(x, layer, norm_g, mods, w_up, conv_w, conv_b, w_down, tiles=(0, N_ROW_TILES), final_g=None):
    split_x = isinstance(x, tuple)
    first, count = tiles
    if split_x:
        assert tiles == (0, N_ROW_TILES)
        x_ops, x_specs = _x_operands(x)
    else:
        x_ops, x_specs = (x,), [pl.BlockSpec((ROW_TILE, D), lambda i: (first + i, 0))]
    final_ops = () if final_g is None else (final_g.reshape(1, D),)
    final_specs = [] if final_g is None else [pl.BlockSpec((1, D), lambda i: (0, 0))]
    once = pl.Buffered(1)
    vmem = (4 * ROW_TILE * D * 4 + ROW_TILE * D * 2 + 2 * (ROW_TILE + 2 * CONV_PAD) * 2 * FFN_CHUNK * 4
            + ROW_TILE * D_FF * 2 + 3 * D * D_FF * 2 + 5 * ROW_TILE * 2 * FFN_CHUNK * 4)
    return pl.pallas_call(
        functools.partial(_ffn_kernel, split_x=split_x, first_tile=first, final_norm=final_g is not None),
        out_shape=jax.ShapeDtypeStruct((count * ROW_TILE, D), F32),
        grid=(count,),
        in_specs=x_specs + [
            pl.BlockSpec((None, 1, D), lambda i: (layer, 0, 0)),
            pl.BlockSpec((None, None, 6, D), lambda i: (layer, _mod_row(first + i), 0, 0)),
            pl.BlockSpec((None, D, 2 * D_FF), lambda i: (layer, 0, 0), pipeline_mode=once),
            pl.BlockSpec((None, 3, 2 * D_FF), lambda i: (layer, 0, 0), pipeline_mode=once),
            pl.BlockSpec((None, 1, 2 * D_FF), lambda i: (layer, 0, 0), pipeline_mode=once),
            pl.BlockSpec((None, D_FF, D), lambda i: (layer, 0, 0), pipeline_mode=once),
        ] + final_specs,
        out_specs=pl.BlockSpec((ROW_TILE, D), lambda i: (i, 0)),
        scratch_shapes=[pltpu.VMEM((ROW_TILE, D), BF16),
                        pltpu.VMEM((ROW_TILE + 2 * CONV_PAD, 2 * FFN_CHUNK), F32),
                        pltpu.VMEM((ROW_TILE + 2 * CONV_PAD, 2 * FFN_CHUNK), F32),
                        pltpu.VMEM((ROW_TILE, D_FF), BF16)],
        compiler_params=_cparams(("arbitrary",), vmem),
        name="conv_ffn",
    )(*x_ops, norm_g.reshape(DEPTH, 1, D), mods, w_up, conv_w, conv_b.reshape(DEPTH, 1, 2 * D_FF), w_down,
      *final_ops)


NQ = ATTN_HEADS * HEAD_DIM
NKV = ATTN_KV_HEADS * HEAD_DIM
Q_BLOCK = 128
MASKED = -1e30
LOG2E = 1.0 / math.log(2.0)
Q_PRESCALE = HEAD_DIM ** -0.5 * LOG2E


def _dot_t(a, b):
    return lax.dot_general(a.astype(BF16), b.astype(BF16), (((1,), (1,)), ((), ())),
                           preferred_element_type=F32)


def _group_rows(q):
    return jnp.concatenate([q[:, g * HEAD_DIM:(g + 1) * HEAD_DIM] for g in range(ATTN_GROUP)], axis=0)


def _sink_lanes(sink_ref, h, rows):
    return LOG2E * jnp.concatenate(
        [jnp.broadcast_to(sink_ref[0:1, ATTN_GROUP * h + g:ATTN_GROUP * h + g + 1], (1, rows))
         for g in range(ATTN_GROUP)], axis=-1)


ONES_ROWS = 16


def _values_t(v):
    ones = jnp.ones((ONES_ROWS, v.shape[0]), F32)
    out = []
    for c in range(NKV // 128):
        vt = v[:, c * 128:(c + 1) * 128].T
        out += [jnp.concatenate([vt[j * HEAD_DIM:(j + 1) * HEAD_DIM], ones], axis=0).astype(BF16)
                for j in range(128 // HEAD_DIM)]
    return out


def _softmax_pv(q4, key_sets, sink2):
    scores = []
    for k, _, bias in key_sets:
        s = _dot_t(k, q4)
        if bias is not None:
            s = jnp.concatenate([s[c * 128:(c + 1) * 128] if bc is None else s[c * 128:(c + 1) * 128] + bc
                                 for c, bc in enumerate(bias)], axis=0)
        scores.append(s)
    m = sink2
    for s in scores:
        m = jnp.maximum(m, jnp.max(s, axis=0, keepdims=True))
    acc = None
    for s, (_, v1t, _) in zip(scores, key_sets):
        t = jnp.dot(v1t, jnp.exp2(s - m).astype(BF16), preferred_element_type=F32)
        acc = t if acc is None else acc + t
    denom = acc[HEAD_DIM:HEAD_DIM + 1] + jnp.exp2(sink2 - m)
    return acc[:HEAD_DIM] * (1.0 / denom)


def _heads_to_columns(o_t, rows):
    slabs = []
    for g in range(0, ATTN_GROUP, 128 // HEAD_DIM):
        pair = jnp.concatenate([o_t[:, (g + j) * rows:(g + j + 1) * rows] for j in range(128 // HEAD_DIM)], axis=0)
        slabs.append(pair.T)
    return jnp.concatenate(slabs, axis=-1)


def _ctx_attn_kernel(qkv_ref, sink_ref, o_ref):
    v1t = _values_t(qkv_ref[:, NQ + NKV:])
    outs = []
    for h in range(ATTN_KV_HEADS):
        k = qkv_ref[:, NQ + h * HEAD_DIM:NQ + (h + 1) * HEAD_DIM].astype(BF16)
        q4 = _group_rows(qkv_ref[:, ATTN_GROUP * h * HEAD_DIM:ATTN_GROUP * (h + 1) * HEAD_DIM] * Q_PRESCALE)
        o_t = _softmax_pv(q4.astype(BF16), [(k, v1t[h], None)], _sink_lanes(sink_ref, h, SEQ))
        outs.append(_heads_to_columns(o_t, SEQ))
    o_ref[...] = jnp.concatenate(outs, axis=-1).astype(BF16)


def context_attention(qkv, sink):
    return pl.pallas_call(
        _ctx_attn_kernel,
        out_shape=jax.ShapeDtypeStruct((ROWS_P, NQ), BF16),
        grid=(BATCH,),
        in_specs=[pl.BlockSpec((SEQ, NQ + 2 * NKV), lambda b: (b, 0)),
                  pl.BlockSpec((1, ATTN_HEADS), lambda b: (0, 0))],
        out_specs=pl.BlockSpec((SEQ, NQ), lambda b: (b, 0)),
        compiler_params=_cparams(("arbitrary",), 2 * SEQ * (2 * NQ + 2 * NKV) * 4 + 24 * SEQ * ATTN_GROUP * SEQ * 4),
        name="context_attention",
    )(qkv, sink.reshape(1, ATTN_HEADS))


def _rope(x, cos, sin_a, sin_b):
    outs = []
    for c in range(x.shape[1] // 128):
        s = x[:, c * 128:(c + 1) * 128]
        outs.append(s * cos + pltpu.roll(s, 128 - HEAD_DIM // 4, 1) * sin_a + pltpu.roll(s, HEAD_DIM // 4, 1) * sin_b)
    return jnp.concatenate(outs, axis=-1)


Q_TILE = 2 * Q_BLOCK


def _lat_attn_kernel(q_ref, kp_ref, km_ref, kn_ref, vp_ref, vm_ref, vn_ref, ck_ref, cv_ref,
                     cos_ref, sa_ref, sb_ref, sink_ref, o_ref):
    m = pl.program_id(1)
    nm = pl.num_programs(1)
    nb = 2 * nm

    def tables(blk, rows):
        r = pl.ds(pl.multiple_of(blk * Q_BLOCK, Q_BLOCK), rows)
        return cos_ref[r, :], sa_ref[r, :], sb_ref[r, :]

    mid = tables(2 * m, Q_TILE)
    qr = (_rope(q_ref[...], *mid) * Q_PRESCALE).astype(BF16)
    k4 = jnp.concatenate([
        _rope(kp_ref[...], *tables(jnp.maximum(2 * m - 1, 0), Q_BLOCK)),
        _rope(km_ref[...], *mid),
        _rope(kn_ref[...], *tables(jnp.minimum(2 * m + 2, nb - 1), Q_BLOCK))], axis=0).astype(BF16)
    v4 = jnp.concatenate([vp_ref[...], vm_ref[...], vn_ref[...]], axis=0)

    cols = ATTN_GROUP * Q_TILE
    koff = lax.broadcasted_iota(jnp.int32, (Q_BLOCK, cols), 0)
    lane = lax.broadcasted_iota(jnp.int32, (Q_BLOCK, cols), 1)
    qoff = lane & (Q_BLOCK - 1)
    second = (lane & Q_BLOCK) != 0
    lower, upper = koff >= qoff, koff <= qoff
    bias = [jnp.where(jnp.logical_not(second) & lower & (m > 0), 0.0, MASKED),
            jnp.where(jnp.logical_not(second) | lower, 0.0, MASKED),
            jnp.where(second | upper, 0.0, MASKED),
            jnp.where(second & upper & (m < nm - 1), 0.0, MASKED)]

    v1t = _values_t(v4)
    cv1t = _values_t(cv_ref[...])
    outs = []
    for h in range(ATTN_KV_HEADS):
        hs = slice(h * HEAD_DIM, (h + 1) * HEAD_DIM)
        q4 = _group_rows(qr[:, ATTN_GROUP * h * HEAD_DIM:ATTN_GROUP * (h + 1) * HEAD_DIM])
        o_t = _softmax_pv(q4, [(k4[:, hs], v1t[h], bias), (ck_ref[:, hs].astype(BF16), cv1t[h], None)],
                          _sink_lanes(sink_ref, h, Q_TILE))
        outs.append(_heads_to_columns(o_t, Q_TILE))
    o_ref[...] = jnp.concatenate(outs, axis=-1).astype(BF16)


def _rope_tables():
    t = np.arange(DEC_SEQ)
    half = HEAD_DIM // 2
    inv_freq = 1.0 / (ROPE_BASE ** (np.arange(0, half, 2, dtype=np.float32) / half))
    ar = (t // GRID_W).astype(np.float32)[:, None] * inv_freq
    ac = (t % GRID_W).astype(np.float32)[:, None] * inv_freq
    return jnp.concatenate([jnp.asarray(a) for a in (ar, ar, ac, ac)] * 2, axis=-1)


def latent_attention(qkv, cache_k, cache_v, layer_j, sink):
    ang = _rope_tables()
    cos, sin = jnp.cos(ang), jnp.sin(ang)
    first = (lax.broadcasted_iota(jnp.int32, ang.shape, 1) % (HEAD_DIM // 2)) < HEAD_DIM // 4
    sin_a = jnp.where(first, -sin, 0.0)
    sin_b = jnp.where(first, 0.0, sin)
    nb = DEC_SEQ // Q_BLOCK
    nm = DEC_SEQ // Q_TILE
    base = ROWS_P // Q_BLOCK
    base_t = ROWS_P // Q_TILE
    kcol, vcol = NQ // NKV, NQ // NKV + 1
    ck = cache_k.reshape(DEC_BATCH, -1, PAST_LEN, NKV)
    cv = cache_v.reshape(DEC_BATCH, -1, PAST_LEN, NKV)

    def edge_spec(col, blk):
        return pl.BlockSpec((Q_BLOCK, NKV), lambda b, m: (base + b * nb + jnp.clip(blk(m), 0, nb - 1), col))

    def mid_spec(col):
        return pl.BlockSpec((Q_TILE, NKV), lambda b, m: (base_t + b * nm + m, col))

    prev_blk, next_blk = (lambda m: 2 * m - 1), (lambda m: 2 * m + 2)
    table = pl.BlockSpec((DEC_SEQ, 128), lambda b, m: (0, 0))
    return pl.pallas_call(
        _lat_attn_kernel,
        out_shape=jax.ShapeDtypeStruct((ROWS_S, NQ), BF16),
        grid=(DEC_BATCH, nm),
        in_specs=[pl.BlockSpec((Q_TILE, NQ), lambda b, m: (base_t + b * nm + m, 0)),
                  edge_spec(kcol, prev_blk), mid_spec(kcol), edge_spec(kcol, next_blk),
                  edge_spec(vcol, prev_blk), mid_spec(vcol), edge_spec(vcol, next_blk),
                  pl.BlockSpec((None, None, PAST_LEN, NKV), lambda b, m: (b, layer_j, 0, 0)),
                  pl.BlockSpec((None, None, PAST_LEN, NKV), lambda b, m: (b, layer_j, 0, 0)),
                  table, table, table,
                  pl.BlockSpec((1, ATTN_HEADS), lambda b, m: (0, 0))],
        out_specs=pl.BlockSpec((Q_TILE, NQ), lambda b, m: (b * nm + m, 0)),
        compiler_params=_cparams(("arbitrary", "arbitrary"),
                                 4 * Q_TILE * NQ * 4 + 12 * Q_TILE * NKV * 4 + 4 * PAST_LEN * NKV * 4
                                 + 6 * DEC_SEQ * 128 * 4 + 16 * ATTN_GROUP * Q_TILE * (4 * Q_BLOCK + PAST_LEN) * 4),
        name="latent_attention",
    )(qkv, qkv, qkv, qkv, qkv, qkv, qkv, ck, cv, cos, sin_a, sin_b, sink.reshape(1, ATTN_HEADS))


def attention_layer(x, layer, layer_j, mods, norm_g, wqkv, wo, sink, cache_k, cache_v):
    qkv = norm_mod_matmul(x, layer, norm_g, mods, wqkv, layer_j, "attn_qkv")
    a_p = context_attention(qkv, sink)
    a_s = latent_attention(qkv, cache_k, cache_v, layer_j, sink)
    x = matmul_gated_residual(a_p, a_s, wo, layer_j, x, layer, mods, "attn_wo")
    new_k = qkv[:ROWS_P, NQ:NQ + NKV].reshape(BATCH, SEQ, ATTN_KV_HEADS, HEAD_DIM)
    new_v = qkv[:ROWS_P, NQ + NKV:].reshape(BATCH, SEQ, ATTN_KV_HEADS, HEAD_DIM)
    return x, new_k, new_v


HG_TILE = 256
HG_LEVELS = 8
HG_IN = 3 * 1024 + 2 * 1024
HG_OUT_TILE = 512
HG_HEADS_PER_STEP = 8


def _hgrn_consts():
    t = np.arange(HG_TILE)
    x = t[:, None] ^ t[None, :]
    hb = np.where(x == 0, -1, np.floor(np.log2(np.maximum(x, 1))).astype(np.int64))
    later = t[:, None] > t[None, :]
    half = HG_TILE // 2
    masks, tris = [], []
    for reverse in (False, True):
        side = ~later & (x != 0) if reverse else later
        lv = [hb == -1] + [(hb == lvl) & side for lvl in range(HG_LEVELS - 1)]
        masks.append(np.stack([m[:half, :half] for m in lv]).astype(np.float32))
        tris.append((t[None, :] >= t[:, None]) if reverse else (t[None, :] <= t[:, None]))
    return jnp.asarray(np.stack(masks)), jnp.asarray(np.stack(tris).astype(np.float32), dtype=BF16)


def _split_bf16(x):
    def top(v):
        bits = lax.bitcast_convert_type(v, jnp.uint32) & jnp.uint32(0xFFFF0000)
        return lax.bitcast_convert_type(bits, F32)

    hi = top(x)
    r = x - hi
    mid = top(r)
    return hi.astype(BF16), mid.astype(BF16), (r - mid).astype(BF16)


def _block_row(x, blk, idx):
    t = x.shape[0]
    x3 = x.reshape(t // blk, blk, x.shape[1])
    return jnp.broadcast_to(x3[:, idx:idx + 1, :], x3.shape).reshape(x.shape)


def _lower_bound(lb_ref, layer, direction):
    x = lb_ref[direction]
    e = jnp.exp(x - jnp.max(x, axis=0, keepdims=True))
    p = e / jnp.sum(e, axis=0, keepdims=True)
    return jnp.sum(p[1:layer + 1, :], axis=0, keepdims=True)


def _hgrn_tile(q, v, z, lb, s_in, mask_ref, tri, reverse):
    t = HG_TILE
    lo, hi = slice(0, t // 2), slice(t // 2, t)
    sg = _sigmoid(z)
    f = lb + (1.0 - lb) * sg
    k = (1.0 - lb) * (1.0 - sg)
    lf3 = _split_bf16(jnp.log(f))
    cum = sum(jnp.dot(tri, p, preferred_element_type=F32) for p in lf3)
    cum2 = cum * LOG2E
    rows = lax.broadcasted_iota(jnp.int32, (t, 1), 0)
    att = [mask_ref[0] * _dot_t(q[r], k[r]) for r in (lo, hi)]
    top = None
    for lvl in range(HG_LEVELS):
        half = 1 << lvl
        bit = (rows & half) != 0
        qside = ~bit if reverse else bit
        if lvl == 0:
            e = jnp.where(qside, f, 1.0)
        else:
            ref = _block_row(cum2, 2 * half, half if reverse else half - 1)
            e = jnp.exp2(-jnp.abs(cum2 - ref))
        w = (jnp.where(qside, q, k) * e).astype(BF16)
        if lvl < HG_LEVELS - 1:
            att = [a + mask_ref[lvl + 1] * _dot_t(w[r], w[r]) for a, r in zip(att, (lo, hi))]
        else:
            top = _dot_t(w[lo], w[hi]) if reverse else _dot_t(w[hi], w[lo])
    vb = v.astype(BF16)
    o_lo, o_hi = _bdot(att[0], vb[lo]), _bdot(att[1], vb[hi])
    if reverse:
        o_lo = o_lo + _bdot(top, vb[hi])
    else:
        o_hi = o_hi + _bdot(top, vb[lo])
    o = jnp.concatenate([o_lo, o_hi], axis=0)
    last = cum[0:1, :] if reverse else cum[t - 1:t, :]
    kd = (k * jnp.exp(last - cum)).astype(BF16)
    s_out = lax.dot_general(kd, vb, (((0,), (0,)), ((), ())), preferred_element_type=F32)
    if s_in is not None:
        o = o + _bdot(q * jnp.exp(cum), s_in)
        ones = jnp.ones((t, HGRN_DV), BF16)
        last_col = sum(lax.dot_general(p, ones, (((0,), (0,)), ((), ())), preferred_element_type=F32) for p in lf3)
        s_out = jnp.exp(last_col) * s_in + s_out
    return o, s_out


def _hgrn_prompt_kernel(q_ref, v_ref, zf_ref, zb_ref, lb_ref, mask_ref, tri_ref, of_ref, ob_ref, sfin_ref, *, layer):
    for j in range(HG_HEADS_PER_STEP):
        hs = slice(j * HGRN_DK, (j + 1) * HGRN_DK)
        q, v = q_ref[:, hs], v_ref[:, hs]
        lb_f, lb_b = (_lower_bound(lb_ref.at[:, :, hs], layer, d) for d in range(2))
        of_ref[:, hs], sfin_ref[0, j] = _hgrn_tile(q, v, zf_ref[:, hs], lb_f, None, mask_ref.at[0], tri_ref[0], False)
        ob_ref[:, hs], sfin_ref[1, j] = _hgrn_tile(q, v, zb_ref[:, hs], lb_b, None, mask_ref.at[1], tri_ref[1], True)


def _hgrn_sample_kernel(qf_ref, vf_ref, zf_ref, qb_ref, vb_ref, zb_ref, lb_ref, s0_ref, mask_ref, tri_ref,
                        of_ref, ob_ref, state_ref, *, layer):
    tiles = DEC_SEQ // HG_TILE

    @pl.when(pl.program_id(1) % tiles == 0)
    def _():
        state_ref[...] = s0_ref[...]

    for j in range(HG_HEADS_PER_STEP):
        hs = slice(j * HGRN_DK, (j + 1) * HGRN_DK)
        lb_f, lb_b = (_lower_bound(lb_ref.at[:, :, hs], layer, d) for d in range(2))
        of_ref[:, hs], state_ref[0, j] = _hgrn_tile(qf_ref[:, hs], vf_ref[:, hs], zf_ref[:, hs], lb_f,
                                                    state_ref[0, j], mask_ref.at[0], tri_ref[0], False)
        ob_ref[:, hs], state_ref[1, j] = _hgrn_tile(qb_ref[:, hs], vb_ref[:, hs], zb_ref[:, hs], lb_b,
                                                    state_ref[1, j], mask_ref.at[1], tri_ref[1], True)


def hgrn_scan(proj, lb_raw, layer, state_hgrn, layer_j):
    masks, tris = _hgrn_consts()
    hps = HG_HEADS_PER_STEP
    hw = hps * HGRN_DK
    qc, vc, zfc, zbc = 0, 1024 // hw, 2048 // hw, 3072 // hw
    const_specs = [pl.BlockSpec((2, DEPTH, hw), lambda h, i: (0, 0, h))]
    mask_specs = [pl.BlockSpec(masks.shape, lambda h, i: (0, 0, 0, 0)),
                  pl.BlockSpec((2, HG_TILE, HG_TILE), lambda h, i: (0, 0, 0))]
    vmem = 4 * masks.size * 4 + 16 * HG_TILE * hw * 4 + 8 * hw * HGRN_DV * 4 + 24 * hps * HG_TILE * HG_TILE * 4

    def col(cb, row_fn):
        return pl.BlockSpec((HG_TILE, hw), lambda h, i: (row_fn(i), cb + h))

    o_shape = jax.ShapeDtypeStruct((ROWS_P, HGRN_HEADS * HGRN_DV), F32)
    same = lambda i: i
    of_p, ob_p, sfin = pl.pallas_call(
        functools.partial(_hgrn_prompt_kernel, layer=layer),
        out_shape=(o_shape, o_shape, jax.ShapeDtypeStruct((BATCH, 2, HGRN_HEADS, HGRN_DK, HGRN_DV), F32)),
        grid=(HGRN_HEADS // hps, BATCH),
        in_specs=[col(qc, same), col(vc, same), col(zfc, same), col(zbc, same)] + const_specs + mask_specs,
        out_specs=(col(0, same), col(0, same),
                   pl.BlockSpec((None, 2, hps, HGRN_DK, HGRN_DV), lambda h, i: (i, 0, h, 0, 0))),
        compiler_params=_cparams(("arbitrary", "arbitrary"), vmem),
        name="hgrn_scan_prompt",
    )(proj, proj, proj, proj, lb_raw, masks, tris)

    tiles = DEC_SEQ // HG_TILE
    base = ROWS_P // HG_TILE
    bwd = lambda i: (i // tiles) * tiles + (tiles - 1 - i % tiles)
    fwd_in = lambda i: base + i
    bwd_in = lambda i: base + bwd(i)
    o_shape = jax.ShapeDtypeStruct((ROWS_S, HGRN_HEADS * HGRN_DV), F32)
    of_s, ob_s = pl.pallas_call(
        functools.partial(_hgrn_sample_kernel, layer=layer),
        out_shape=(o_shape, o_shape),
        grid=(HGRN_HEADS // hps, DEC_BATCH * tiles),
        in_specs=[col(qc, fwd_in), col(vc, fwd_in), col(zfc, fwd_in), col(qc, bwd_in), col(vc, bwd_in),
                  col(zbc, bwd_in)]
        + const_specs
        + [pl.BlockSpec((None, None, 2, hps, HGRN_DK, HGRN_DV), lambda h, i: (i // tiles, layer_j, 0, h, 0, 0))]
        + mask_specs,
        out_specs=(col(0, same), col(0, bwd)),
        scratch_shapes=[pltpu.VMEM((2, hps, HGRN_DK, HGRN_DV), F32)],
        compiler_params=_cparams(("arbitrary", "arbitrary"), vmem),
        name="hgrn_scan_sample",
    )(proj, proj, proj, proj, proj, proj, lb_raw, state_hgrn, masks, tris)
    return (of_p, ob_p), (of_s, ob_s), sfin


def _hgrn_out_kernel(ofp_ref, obp_ref, ofs_ref, obs_ref, g_ref, gn_ref, w_ref, x_ref, mod_ref, o_ref):
    def run(of_ref, ob_ref):
        gn = gn_ref[...]
        parts = []
        for h in range(HGRN_HEADS):
            hs = slice(h * HGRN_DV, (h + 1) * HGRN_DV)
            o = of_ref[:, hs] + ob_ref[:, hs]
            g = g_ref[:, hs]
            o = o * lax.rsqrt(jnp.mean(o * o, axis=-1, keepdims=True) + NORM_EPS) * gn * _silu(g)
            parts.append(o.astype(BF16))
        y = jnp.dot(jnp.concatenate(parts, axis=-1), w_ref[...], preferred_element_type=F32)
        o_ref[...] = x_ref[...] + mod_ref[2:3, :] * y

    is_prompt = pl.program_id(0) < ROWS_P // HG_OUT_TILE
    pl.when(is_prompt)(lambda: run(ofp_ref, obp_ref))
    pl.when(jnp.logical_not(is_prompt))(lambda: run(ofs_ref, obs_ref))


def hgrn_out(o_p, o_s, proj, g_norm, wo, w_idx, x, layer, mods):
    gcol = 4096 // D
    tile = HG_OUT_TILE
    p_spec, s_spec = _group_specs(D, tile)
    return pl.pallas_call(
        _hgrn_out_kernel,
        out_shape=jax.ShapeDtypeStruct((ROWS, D), F32),
        grid=(ROWS // tile,),
        in_specs=[p_spec, p_spec, s_spec, s_spec,
                  pl.BlockSpec((tile, D), lambda i: (i, gcol)),
                  pl.BlockSpec((1, HGRN_DV), lambda i: (0, 0)),
                  pl.BlockSpec((None, D, D), lambda i: (w_idx, 0, 0), pipeline_mode=pl.Buffered(1)),
                  pl.BlockSpec((tile, D), lambda i: (i, 0)),
                  _mod_spec(layer, 1, tile)],
        out_specs=pl.BlockSpec((tile, D), lambda i: (i, 0)),
        compiler_params=_cparams(("arbitrary",), 22 * tile * D * 4),
        name="hgrn_out",
    )(*o_p, *o_s, proj, g_norm.reshape(1, HGRN_DV), wo, x, mods)


def hgrn_layer(x, layer, layer_j, mods, norm_g, w_in, hgrn_lb, g_norm, wo, state_hgrn):
    proj = norm_mod_matmul(x, layer, norm_g, mods, w_in, layer_j, "hgrn_in")
    o_p, o_s, sfin = hgrn_scan(proj, jnp.transpose(hgrn_lb, (1, 0, 2)), layer, state_hgrn, layer_j)
    x = hgrn_out(o_p, o_s, proj, g_norm[layer_j], wo, layer_j, x, layer, mods)
    return x, sfin


SSM_N = SSM_GROUPS * SSM_STATE
SSM_KT = 8
SSM_ROWS = 256


def _ssm_prep_kernel(are_ref, aim_ref, ldt_ref, bre_ref, bim_ref, cre_ref, cim_ref,
                     lre_ref, lim_ref, bm_ref, cm_ref, bm_acc, cm_acc):
    a_re = jnp.minimum(are_ref[...], -1e-4)
    a_im = aim_ref[...]
    dt = jnp.exp(ldt_ref[...])
    mag = jnp.exp(a_re * dt)
    l_re = mag * jnp.cos(a_im * dt)
    l_im = mag * jnp.sin(a_im * dt)
    lre_ref[...] = l_re
    lim_ref[...] = l_im
    den = a_re * a_re + a_im * a_im
    c_re = ((l_re - 1.0) * a_re + l_im * a_im) / den
    c_im = (l_im * a_re - (l_re - 1.0) * a_im) / den
    b_re, b_im = bre_ref[...], bim_ref[...]
    bb_re = c_re[:, None, :] * b_re - c_im[:, None, :] * b_im
    bb_im = c_re[:, None, :] * b_im + c_im[:, None, :] * b_re
    bm_acc[...] = jnp.zeros_like(bm_acc)
    cm_acc[...] = jnp.zeros_like(cm_acc)
    kw = SSM_KT * SSM_STATE
    for dg in range(2 * SSM_GROUPS):
        d, g = divmod(dg, SSM_GROUPS)
        k, gl = divmod(g, SSM_KT)
        ch = slice(gl * SSM_GROUP, (gl + 1) * SSM_GROUP)
        st = slice(gl * SSM_STATE, (gl + 1) * SSM_STATE)
        st_im = slice(kw + gl * SSM_STATE, kw + (gl + 1) * SSM_STATE)
        bm_acc[d, k, ch, st] = bb_re[dg]
        bm_acc[d, k, ch, st_im] = bb_im[dg]
        cm_acc[d, k, st, ch] = cre_ref[dg]
        cm_acc[d, k, st_im, ch] = -cim_ref[dg]
    bm_ref[...] = bm_acc[...].astype(BF16)
    cm_ref[...] = cm_acc[...].astype(BF16)


def ssm_discretize(a_re, a_im, log_dt, b_re, b_im, c_re, c_im):
    g2 = 2 * SSM_GROUPS
    nk = SSM_GROUPS // SSM_KT
    kw = SSM_KT * SSM_STATE
    sh = jax.ShapeDtypeStruct((g2, SSM_STATE), F32)
    bm_shape = (2, nk, SSM_KT * SSM_GROUP, 2 * kw)
    cm_shape = (2, nk, 2 * kw, SSM_KT * SSM_GROUP)
    bt = lambda b: jnp.transpose(b, (0, 1, 3, 2)).reshape(g2, SSM_GROUP, SSM_STATE)
    ct = lambda c: jnp.transpose(c, (0, 1, 3, 2)).reshape(g2, SSM_STATE, SSM_GROUP)
    l_re, l_im, b_mat, c_mat = pl.pallas_call(
        _ssm_prep_kernel,
        out_shape=(sh, sh, jax.ShapeDtypeStruct(bm_shape, BF16), jax.ShapeDtypeStruct(cm_shape, BF16)),
        scratch_shapes=[pltpu.VMEM(bm_shape, F32), pltpu.VMEM(cm_shape, F32)],
        compiler_params=pltpu.CompilerParams(vmem_limit_bytes=40 << 20),
        name="ssm_discretize",
    )(a_re.reshape(g2, SSM_STATE), a_im.reshape(g2, SSM_STATE), log_dt.reshape(g2, 1), bt(b_re), bt(b_im),
      ct(c_re), ct(c_im))
    lam = jnp.stack([l_re.reshape(2, SSM_N), l_im.reshape(2, SSM_N)], axis=1)
    return b_mat, c_mat, lam


X4_SHAPE = (ROWS // (4 * SEQ), 4, SEQ, D)


def _tm_geometry(prompt):
    if prompt:
        batch = BATCH
        steps = SSM_ROWS // batch
        return batch, steps, (4, 4, steps, D), (lambda i: (0, 0, i, 0)), SEQ // steps, (0,) * batch
    batch = DEC_BATCH
    steps = SSM_ROWS // batch
    per_q = SEQ // steps
    return (batch, steps, (4, 1, steps, D), (lambda i: (1, i // per_q, i % per_q, 0)), DEC_SEQ // steps,
            tuple(range(1, 1 + batch)))


def _x4_seq(ref, b):
    return ref.at[b // ref.shape[1], b % ref.shape[1]]


LANE_SLABS = D // 128


def _slab_store(s_ref, rows, val):
    for c in range(LANE_SLABS):
        s_ref[c, rows, :] = val[:, c * 128:(c + 1) * 128]


def _slab_load(s_ref, rows):
    return jnp.concatenate([s_ref[c, rows, :] for c in range(LANE_SLABS)], axis=-1)


def _normmod_tm_kernel(x_ref, g_ref, mod_ref, o_ref, s_ref, *, batch, steps, mod_rows):
    g = g_ref[...]
    for b in range(batch):
        m = mod_ref.at[mod_rows[b]]
        _slab_store(s_ref, slice(b * steps, (b + 1) * steps),
                    _norm_mod(_x4_seq(x_ref, b)[...], g, m[0:1, :], m[1:2, :]))
    for t in range(steps):
        o_ref[t * batch:(t + 1) * batch, :] = _slab_load(s_ref, pl.ds(t, batch, stride=steps))


def norm_mod_time_major(x4, layer, norm_g, mods, prompt):
    batch, steps, blk, idx, tiles, mod_rows = _tm_geometry(prompt)
    return pl.pallas_call(
        functools.partial(_normmod_tm_kernel, batch=batch, steps=steps, mod_rows=mod_rows),
        out_shape=jax.ShapeDtypeStruct((tiles * SSM_ROWS, D), F32),
        grid=(tiles,),
        in_specs=[pl.BlockSpec(blk, idx),
                  pl.BlockSpec((None, 1, D), lambda i: (layer, 0, 0)),
                  pl.BlockSpec((None, MOD_ROWS, 6, D), lambda i: (layer, 0, 0, 0))],
        out_specs=pl.BlockSpec((SSM_ROWS, D), lambda i: (i, 0)),
        scratch_shapes=[pltpu.VMEM((LANE_SLABS, SSM_ROWS, 128), F32)],
        compiler_params=_cparams(("arbitrary",), 8 * SSM_ROWS * D * 4),
        name="ssm_norm_time_major",
    )(x4, norm_g.reshape(DEPTH, 1, D), mods)


def _ssm_scan_kernel(xf_ref, xb_ref, bm_ref, cm_ref, lam_ref, h0_ref, yf_ref, yb_ref, hfin_ref,
                     hre_f, him_f, hre_b, him_b, st_ref, *, batch):
    i = pl.program_id(0)
    steps = SSM_ROWS // batch
    nk = SSM_GROUPS // SSM_KT
    kw = SSM_KT * SSM_STATE
    x_refs, y_refs = (xf_ref, xb_ref), (yf_ref, yb_ref)
    h_refs = ((hre_f, him_f), (hre_b, him_b))

    @pl.when(i == 0)
    def _():
        st_ref[...] = h0_ref[...]

    def tile_cols(k):
        return slice(k * kw, (k + 1) * kw)

    def bu_tile(d, k):
        xk = x_refs[d][:, k * 128:(k + 1) * 128].astype(BF16)
        bu = jnp.dot(xk, bm_ref[d, k], preferred_element_type=F32)
        h_refs[d][0][:, tile_cols(k)] = bu[:, :kw]
        h_refs[d][1][:, tile_cols(k)] = bu[:, kw:]

    def c_tile(d, k):
        hk = jnp.concatenate([h_refs[d][0][:, tile_cols(k)], h_refs[d][1][:, tile_cols(k)]], axis=-1)
        y_refs[d][:, k * 128:(k + 1) * 128] = jnp.dot(hk.astype(BF16), cm_ref[d, k], preferred_element_type=F32)

    def scan_tile(d, k):
        hre_ref, him_ref = h_refs[d]
        col = tile_cols(k)
        l_re, l_im = lam_ref[d, 0, :, col], lam_ref[d, 1, :, col]
        h_re, h_im = st_ref[d, 0, :, col], st_ref[d, 1, :, col]
        per = max(8 // batch, 1)
        rows_per = per * batch
        for s in range(steps // per):
            g = (steps // per - 1 - s) if d else s
            rows = slice(g * rows_per, (g + 1) * rows_per)
            cur_re, cur_im = hre_ref[rows, col], him_ref[rows, col]
            outs_re, outs_im = [None] * per, [None] * per
            for r in (range(per - 1, -1, -1) if d else range(per)):
                b_re, b_im = cur_re[r * batch:(r + 1) * batch], cur_im[r * batch:(r + 1) * batch]
                h_re, h_im = l_re * h_re - l_im * h_im + b_re, l_re * h_im + l_im * h_re + b_im
                outs_re[r], outs_im[r] = h_re, h_im
            hre_ref[rows, col] = outs_re[0] if per == 1 else jnp.concatenate(outs_re, axis=0)
            him_ref[rows, col] = outs_im[0] if per == 1 else jnp.concatenate(outs_im, axis=0)
        st_ref[d, 0, :, col] = h_re
        st_ref[d, 1, :, col] = h_im

    for k in range(nk + 2):
        for d in range(2):
            if k < nk:
                bu_tile(d, k)
            if 1 <= k <= nk:
                scan_tile(d, k - 1)
            if k >= 2:
                c_tile(d, k - 2)

    @pl.when(i == pl.num_programs(0) - 1)
    def _():
        hfin_ref[...] = st_ref[...]


def ssm_scan(xn_tm, b_mat, c_mat, lam, h0, batch):
    rows = xn_tm.shape[0]
    n = rows // SSM_ROWS
    lam_b = jnp.broadcast_to(lam[:, :, None, :], (2, 2, batch, SSM_N))
    y_shape = jax.ShapeDtypeStruct((rows, D), F32)
    full = lambda a: pl.BlockSpec(a.shape, lambda i: (0,) * a.ndim)
    vmem = (4 * SSM_ROWS * SSM_N * 4 + 8 * SSM_ROWS * D * 4 + 2 * (b_mat.size + c_mat.size) * 2
            + 12 * batch * SSM_N * 4 * 2 + 8 * SSM_ROWS * 1024 * 4)
    return pl.pallas_call(
        functools.partial(_ssm_scan_kernel, batch=batch),
        out_shape=(y_shape, y_shape, jax.ShapeDtypeStruct((2, 2, batch, SSM_N), F32)),
        grid=(n,),
        in_specs=[pl.BlockSpec((SSM_ROWS, D), lambda i: (i, 0)),
                  pl.BlockSpec((SSM_ROWS, D), lambda i: (n - 1 - i, 0)),
                  full(b_mat), full(c_mat), full(lam_b), full(h0)],
        out_specs=(pl.BlockSpec((SSM_ROWS, D), lambda i: (i, 0)),
                   pl.BlockSpec((SSM_ROWS, D), lambda i: (n - 1 - i, 0)),
                   pl.BlockSpec((2, 2, batch, SSM_N), lambda i: (0, 0, 0, 0))),
        scratch_shapes=[pltpu.VMEM((SSM_ROWS, SSM_N), F32)] * 4
        + [pltpu.VMEM((2, 2, batch, SSM_N), F32)],
        compiler_params=_cparams(("arbitrary",), vmem),
        name="ssm_scan",
    )(xn_tm, xn_tm, b_mat, c_mat, lam_b, h0)


def _gelu_tanh(x):
    return 0.5 * x * (1.0 + jnp.tanh(math.sqrt(2.0 / math.pi) * (x + 0.044715 * (x * x * x))))


def _ssm_glu_kernel(yf_ref, yb_ref, xn_ref, d_ref, w_ref, x_ref, mod_ref, o_ref, s_ref,
                    *, batch, steps, mod_rows):
    g = _gelu_tanh(yf_ref[...] + yb_ref[...] + d_ref[...] * xn_ref[...])
    u = jnp.dot(g.astype(BF16), w_ref[...], preferred_element_type=F32)
    _slab_store(s_ref, slice(None), u[:, :D] * _sigmoid(u[:, D:]))
    for b in range(batch):
        gate = mod_ref[mod_rows[b], 2:3, :]
        _x4_seq(o_ref, b)[...] = (_x4_seq(x_ref, b)[...]
                                  + gate * _slab_load(s_ref, pl.ds(b, steps, stride=batch)))


def ssm_glu(yf, yb, xn, d, w_glu, w_idx, x4, layer, mods, prompt):
    batch, steps, blk, idx, tiles, mod_rows = _tm_geometry(prompt)
    tm_spec = pl.BlockSpec((SSM_ROWS, D), lambda i: (i, 0))
    out = pl.pallas_call(
        functools.partial(_ssm_glu_kernel, batch=batch, steps=steps, mod_rows=mod_rows),
        out_shape=jax.ShapeDtypeStruct((4,) + X4_SHAPE[1:], F32),
        grid=(tiles,),
        in_specs=[tm_spec, tm_spec, tm_spec,
                  pl.BlockSpec((None, 1, D), lambda i: (w_idx, 0, 0)),
                  pl.BlockSpec((None, D, 2 * D), lambda i: (w_idx, 0, 0), pipeline_mode=pl.Buffered(1)),
                  pl.BlockSpec(blk, idx),
                  pl.BlockSpec((None, MOD_ROWS, 6, D), lambda i: (layer, 0, 0, 0))],
        out_specs=pl.BlockSpec(blk, lambda i: (0,) + idx(i)[1:]),
        scratch_shapes=[pltpu.VMEM((LANE_SLABS, SSM_ROWS, 128), F32)],
        compiler_params=_cparams(("arbitrary",), 24 * SSM_ROWS * D * 4 + D * 2 * D * 2),
        name="ssm_glu",
    )(yf, yb, xn, d.reshape(-1, 1, D), w_glu, x4, mods)
    return out.reshape(-1, D)


def ssm_layer(x, layer, layer_j, mods, norm_g, a_re, a_im, log_dt, b_re, b_im, c_re, c_im, d, w_glu, state_ssm):
    b_mat, c_mat, lam = ssm_discretize(a_re[layer_j], a_im[layer_j], log_dt[layer_j], b_re[layer_j],
                                       b_im[layer_j], c_re[layer_j], c_im[layer_j])
    x4 = x.reshape(X4_SHAPE)
    xn_p = norm_mod_time_major(x4, layer, norm_g, mods, True)
    xn_s = norm_mod_time_major(x4, layer, norm_g, mods, False)
    h0_p = jnp.zeros((2, 2, BATCH, SSM_N), F32)
    h0_s = jnp.transpose(state_ssm[:, layer_j].reshape(DEC_BATCH, 2, SSM_N, 2), (1, 3, 0, 2))
    yfp, ybp, hfin = ssm_scan(xn_p, b_mat, c_mat, lam, h0_p, BATCH)
    yfs, ybs, _ = ssm_scan(xn_s, b_mat, c_mat, lam, h0_s, DEC_BATCH)
    out_p = ssm_glu(yfp, ybp, xn_p, d, w_glu, layer_j, x4, layer, mods, True)
    out_s = ssm_glu(yfs, ybs, xn_s, d, w_glu, layer_j, x4, layer, mods, False)
    new_state = jnp.transpose(hfin, (2, 0, 3, 1)).reshape(BATCH, 2, SSM_GROUPS, SSM_STATE, 2)
    return (out_p, out_s), new_state


def kernel(x_prompt, x_sample, cache_k, cache_v, state_hgrn, state_ssm, c, c_ctx, ada_w, ada_b, norm1_g, norm2_g, attn_wqkv, attn_wo, attn_sink, hgrn_w_in, hgrn_lb, hgrn_g_norm, hgrn_wo, ssm_a_re, ssm_a_im, ssm_log_dt, ssm_b_re, ssm_b_im, ssm_c_re, ssm_c_im, ssm_d, ssm_w_glu, ffn_w_up, ffn_conv_w, ffn_conv_b, ffn_w_down, final_g):
    cond8 = jnp.zeros((MOD_ROWS, D), F32).at[0].set(c_ctx).at[1:1 + DEC_BATCH].set(c)
    mods = ada_modulation(cond8, ada_w, ada_b)
    x = (x_prompt.reshape(ROWS_P, D), x_sample.reshape(ROWS_S, D))
    wqkv, wo, w_in, hwo, w_glu, w_up, w_down = (w.astype(BF16) for w in (
        attn_wqkv, attn_wo, hgrn_w_in, hgrn_wo, ssm_w_glu, ffn_w_up, ffn_w_down))
    new_k, new_v, new_hgrn, new_ssm = [], [], [], []
    for l in range(DEPTH):
        kind, j = l % N_MIXERS, l // N_MIXERS
        if kind == 0:
            x, k, v = attention_layer(x, l, j, mods, norm1_g, wqkv, wo, attn_sink[j], cache_k, cache_v)
            new_k.append(k)
            new_v.append(v)
        elif kind == 1:
            x, s = hgrn_layer(x, l, j, mods, norm1_g, w_in, hgrn_lb, hgrn_g_norm, hwo, state_hgrn)
            new_hgrn.append(s)
        else:
            x, s = ssm_layer(x, l, j, mods, norm1_g, ssm_a_re, ssm_a_im, ssm_log_dt, ssm_b_re, ssm_b_im,
                             ssm_c_re, ssm_c_im, ssm_d, w_glu, state_ssm)
            new_ssm.append(s)
        ffn = functools.partial(conv_ffn_residual, x, l, norm2_g, mods, w_up, ffn_conv_w, ffn_conv_b, w_down)
        if l + 1 < DEPTH:
            x = ffn()
    y_prompt = ffn(tiles=(0, N_ROW_TILES_P), final_g=final_g).reshape(BATCH, SEQ, D)
    y_sample = ffn(tiles=(N_ROW_TILES_P, N_ROW_TILES - N_ROW_TILES_P), final_g=final_g).reshape(DEC_BATCH, DEC_SEQ, D)
    return (y_prompt, y_sample, jnp.stack(new_k, axis=1), jnp.stack(new_v, axis=1),
            jnp.stack(new_hgrn, axis=1), jnp.stack(new_ssm, axis=1))
```

```python
import functools
import math

import jax
import jax.numpy as jnp
import numpy as np
from jax import lax
from jax.experimental import pallas as pl
from jax.experimental.pallas import tpu as pltpu

F32 = jnp.float32
BF16 = jnp.bfloat16

D = 1024
BATCH = 16
SEQ = 256
DEPTH = 4
DEC_BATCH = 4
DEC_SEQ = 1024
PAST_LEN = 512
GRID_W = 64
N_MIXERS = 3
ATTN_HEADS = 16
ATTN_KV_HEADS = 4
ATTN_GROUP = ATTN_HEADS // ATTN_KV_HEADS
HEAD_DIM = D // ATTN_HEADS
WINDOW = 128
ROPE_BASE = 10000.0
HGRN_HEADS = 8
HGRN_DK = 128
HGRN_DV = 128
SSM_GROUP = 16
SSM_GROUPS = D // SSM_GROUP
SSM_STATE = 64
D_FF = 2816
NORM_EPS = 1e-6

ROWS_P = BATCH * SEQ
ROWS_S = DEC_BATCH * DEC_SEQ
ROWS = ROWS_P + ROWS_S
ROW_TILE = 1024
N_ROW_TILES = ROWS // ROW_TILE
N_ROW_TILES_P = ROWS_P // ROW_TILE
MOD_ROWS = 8
V7X_VMEM_BYTES = 64 * 1024 * 1024


def _mod_row(i, tile=ROW_TILE):
    return jnp.where(i < ROWS_P // tile, 0, (i - ROWS_P // tile) // (DEC_SEQ // tile) + 1)


def _cparams(semantics, vmem_bytes):
    vmem = int(min(max(vmem_bytes * 5 // 4 + (4 << 20), 16 << 20), V7X_VMEM_BYTES - (6 << 20)))
    return pltpu.CompilerParams(dimension_semantics=semantics, vmem_limit_bytes=vmem)


def _bdot(a, b):
    return jnp.dot(a.astype(BF16), b.astype(BF16), preferred_element_type=F32)


def _norm_mod(x, g, shift, scale):
    y = x * lax.rsqrt(jnp.mean(x * x, axis=-1, keepdims=True) + NORM_EPS) * g
    return y * (1.0 + scale) + shift


def _sigmoid(x):
    return 0.5 + 0.5 * jnp.tanh(0.5 * x)


def _silu(x):
    h = 0.5 * x
    return h + h * jnp.tanh(h)


def _ada_kernel(c_ref, w_ref, b_ref, o_ref):
    c = c_ref[...]
    o_ref[...] = _bdot(_silu(c), w_ref[...]) + b_ref[...]


def ada_modulation(cond8, ada_w, ada_b):
    tn = 1024
    out = pl.pallas_call(
        _ada_kernel,
        out_shape=jax.ShapeDtypeStruct((DEPTH, MOD_ROWS, 6 * D), F32),
        grid=(DEPTH, 6 * D // tn),
        in_specs=[
            pl.BlockSpec((MOD_ROWS, D), lambda l, j: (0, 0)),
            pl.BlockSpec((None, D, tn), lambda l, j: (l, 0, j)),
            pl.BlockSpec((None, 1, tn), lambda l, j: (l, 0, j)),
        ],
        out_specs=pl.BlockSpec((None, MOD_ROWS, tn), lambda l, j: (l, 0, j)),
        compiler_params=_cparams(("arbitrary", "arbitrary"), 2 * D * tn * 4),
        name="ada_modulation",
    )(cond8, ada_w, ada_b.reshape(DEPTH, 1, 6 * D))
    return out.reshape(DEPTH, MOD_ROWS, 6, D)


def _x_operands(x, n_grid=1):
    if isinstance(x, tuple):
        return x, list(_group_specs(D, ROW_TILE, n_grid))
    return (x,), [pl.BlockSpec((ROW_TILE, D), (lambda i: (i, 0)) if n_grid == 1 else (lambda i, j: (i, 0)))]


def _read_rows(x_refs):
    if len(x_refs) == 1:
        return x_refs[0][...]
    return jnp.where(pl.program_id(0) < N_ROW_TILES_P, x_refs[0][...], x_refs[1][...])


def _nmm_kernel(*refs):
    x_refs, (g_ref, mod_ref, w_ref, o_ref, h_ref) = refs[:-5], refs[-5:]

    @pl.when(pl.program_id(1) == 0)
    def _():
        h_ref[...] = _norm_mod(_read_rows(x_refs), g_ref[...], mod_ref[0:1, :], mod_ref[1:2, :]).astype(BF16)

    o_ref[...] = jnp.dot(h_ref[...], w_ref[...], preferred_element_type=F32)


def _mod_spec(layer, n_grid, tile=ROW_TILE):
    if n_grid == 1:
        return pl.BlockSpec((None, None, 6, D), lambda i: (layer, _mod_row(i, tile), 0, 0))
    return pl.BlockSpec((None, None, 6, D), lambda i, j: (layer, _mod_row(i, tile), 0, 0))


def norm_mod_matmul(x, layer, norm_g, mods, w, w_idx, name):
    n = w.shape[-1]
    tn = 1024 if n % 1024 == 0 else n
    x_ops, x_specs = _x_operands(x, 2)
    return pl.pallas_call(
        _nmm_kernel,
        out_shape=jax.ShapeDtypeStruct((ROWS, n), F32),
        grid=(N_ROW_TILES, n // tn),
        in_specs=x_specs + [
            pl.BlockSpec((None, 1, D), lambda i, j: (layer, 0, 0)),
            _mod_spec(layer, 2),
            pl.BlockSpec((None, D, tn), lambda i, j: (w_idx, 0, j)),
        ],
        out_specs=pl.BlockSpec((ROW_TILE, tn), lambda i, j: (i, j)),
        scratch_shapes=[pltpu.VMEM((ROW_TILE, D), BF16)],
        compiler_params=_cparams(("arbitrary", "arbitrary"),
                                 4 * ROW_TILE * D * 4 + ROW_TILE * D * 2 + 2 * D * tn * 2 + 2 * ROW_TILE * tn * 4),
        name=name,
    )(*x_ops, norm_g.reshape(DEPTH, 1, D), mods, w)


def _mm_res_kernel(ap_ref, as_ref, w_ref, *refs):
    x_refs, (mod_ref, o_ref) = refs[:-2], refs[-2:]

    def run(a_ref, x_ref):
        y = jnp.dot(a_ref[...].astype(BF16), w_ref[...], preferred_element_type=F32)
        o_ref[...] = x_ref[...] + mod_ref[2:3, :] * y

    is_prompt = pl.program_id(0) < N_ROW_TILES_P
    pl.when(is_prompt)(lambda: run(ap_ref, x_refs[0]))
    pl.when(jnp.logical_not(is_prompt))(lambda: run(as_ref, x_refs[-1]))


def _group_specs(k, tile=ROW_TILE, n_grid=1):
    n_p = ROWS_P // tile
    if n_grid == 1:
        return (pl.BlockSpec((tile, k), lambda i: (jnp.minimum(i, n_p - 1), 0)),
                pl.BlockSpec((tile, k), lambda i: (jnp.maximum(i - n_p, 0), 0)))
    return (pl.BlockSpec((tile, k), lambda i, j: (jnp.minimum(i, n_p - 1), 0)),
            pl.BlockSpec((tile, k), lambda i, j: (jnp.maximum(i - n_p, 0), 0)))


def matmul_gated_residual(a_p, a_s, w, w_idx, x, layer, mods, name):
    k = a_p.shape[1]
    x_ops, x_specs = _x_operands(x)
    return pl.pallas_call(
        _mm_res_kernel,
        out_shape=jax.ShapeDtypeStruct((ROWS, D), F32),
        grid=(N_ROW_TILES,),
        in_specs=[
            *_group_specs(k),
            pl.BlockSpec((None, k, D), lambda i: (w_idx, 0, 0), pipeline_mode=pl.Buffered(1)),
            *x_specs,
            _mod_spec(layer, 1),
        ],
        out_specs=pl.BlockSpec((ROW_TILE, D), lambda i: (i, 0)),
        compiler_params=_cparams(("arbitrary",), 4 * ROW_TILE * k * 2 + k * D * 2 + 7 * ROW_TILE * D * 4),
        name=name,
    )(a_p, a_s, w, *x_ops, mods)


FFN_CHUNK = 256
FFN_CHUNKS = D_FF // FFN_CHUNK
FFN_DOWN_GROUP = FFN_CHUNKS
CONV_PAD = 8


def _ffn_kernel(*refs, split_x, first_tile, final_norm):
    n_x = 1 + split_x
    x_refs = refs[:n_x]
    g_ref, mod_ref, wup_ref, cw_ref, cb_ref, wd_ref = refs[n_x:n_x + 6]
    fg_ref = refs[n_x + 6] if final_norm else None
    o_ref, h_ref, pad_a, pad_b, act_ref = refs[n_x + 6 + final_norm:]
    i = pl.program_id(0) + first_tile
    h_ref[...] = _norm_mod(_read_rows(x_refs), g_ref[...], mod_ref[3:4, :], mod_ref[4:5, :]).astype(BF16)
    zeros = jnp.zeros((CONV_PAD, 2 * FFN_CHUNK), F32)
    for pad_ref in (pad_a, pad_b):
        pad_ref[0:CONV_PAD, :] = zeros
        pad_ref[CONV_PAD + ROW_TILE:, :] = zeros
    sub = lax.broadcasted_iota(jnp.int32, (8, 1), 0)
    is_prompt = i < N_ROW_TILES_P
    keep_first = jnp.where((sub == 0) & is_prompt, 0.0, 1.0)
    keep_last = jnp.where((sub == 7) & is_prompt, 0.0, 1.0)

    def cut_sequences(v, keep, row):
        parts, at = [], 0
        for b in range(SEQ, ROW_TILE, SEQ):
            lo = b if row == 0 else b - 8
            parts += [v[at:lo], v[lo:lo + 8] * keep]
            at = lo + 8
        return jnp.concatenate(parts + [v[at:]], axis=0)

    def cols(ref, c):
        return (ref[:, c * FFN_CHUNK:(c + 1) * FFN_CHUNK],
                ref[:, D_FF + c * FFN_CHUNK:D_FF + (c + 1) * FFN_CHUNK])

    def up_proj(c, pad_ref):
        hb = h_ref[...]
        wg, wv = cols(wup_ref, c)
        pad_ref[CONV_PAD:CONV_PAD + ROW_TILE, :FFN_CHUNK] = jnp.dot(hb, wg, preferred_element_type=F32)
        pad_ref[CONV_PAD:CONV_PAD + ROW_TILE, FFN_CHUNK:] = jnp.dot(hb, wv, preferred_element_type=F32)

    def conv_act(c, pad_ref):
        up = pad_ref[CONV_PAD:CONV_PAD + ROW_TILE, :]
        prev = cut_sequences(pad_ref[CONV_PAD - 1:CONV_PAD - 1 + ROW_TILE, :], keep_first, 0)
        nxt = cut_sequences(pad_ref[CONV_PAD + 1:CONV_PAD + 1 + ROW_TILE, :], keep_last, 7)
        cw = jnp.concatenate(cols(cw_ref, c), axis=-1)
        cb = jnp.concatenate(cols(cb_ref, c), axis=-1)
        conv = prev * cw[0:1, :] + up * cw[1:2, :] + nxt * cw[2:3, :] + cb
        gate = conv[:, :FFN_CHUNK]
        act = _silu(gate) * conv[:, FFN_CHUNK:]
        act_ref[:, c * FFN_CHUNK:(c + 1) * FFN_CHUNK] = act.astype(BF16)

    pads = (pad_a, pad_b)
    done = 0
    up_proj(0, pads[0])
    for c in range(FFN_CHUNKS):
        if c + 1 < FFN_CHUNKS:
            up_proj(c + 1, pads[(c + 1) % 2])
        conv_act(c, pads[c % 2])
        if (c + 1) % FFN_DOWN_GROUP == 0 or c + 1 == FFN_CHUNKS:
            rows = slice(done * FFN_CHUNK, (c + 1) * FFN_CHUNK)
            part = jnp.dot(act_ref[:, rows], wd_ref[rows, :], preferred_element_type=F32)
            acc = part if done == 0 else o_ref[...] + part
            if c + 1 == FFN_CHUNKS:
                acc = _read_rows(x_refs) + mod_ref[5:6, :] * acc
                if final_norm:
                    acc = acc * lax.rsqrt(jnp.mean(acc * acc, axis=-1, keepdims=True) + NORM_EPS) * fg_ref[...]
            o_ref[...] = acc
            done = c + 1


def conv_ffn_residual(x, layer, norm_g, mods, w_up, conv_w, conv_b, w_down, tiles=(0, N_ROW_TILES), final_g=None):
    split_x = isinstance(x, tuple)
    first, count = tiles
    if split_x:
        assert tiles == (0, N_ROW_TILES)
        x_ops, x_specs = _x_operands(x)
    else:
        x_ops, x_specs = (x,), [pl.BlockSpec((ROW_TILE, D), lambda i: (first + i, 0))]
    final_ops = () if final_g is None else (final_g.reshape(1, D),)
    final_specs = [] if final_g is None else [pl.BlockSpec((1, D), lambda i: (0, 0))]
    once = pl.Buffered(1)
    vmem = (4 * ROW_TILE * D * 4 + ROW_TILE * D * 2 + 2 * (ROW_TILE + 2 * CONV_PAD) * 2 * FFN_CHUNK * 4
            + ROW_TILE * D_FF * 2 + 3 * D * D_FF * 2 + 5 * ROW_TILE * 2 * FFN_CHUNK * 4)
    return pl.pallas_call(
        functools.partial(_ffn_kernel, split_x=split_x, first_tile=first, final_norm=final_g is not None),
        out_shape=jax.ShapeDtypeStruct((count * ROW_TILE, D), F32),
        grid=(count,),
        in_specs=x_specs + [
            pl.BlockSpec((None, 1, D), lambda i: (layer, 0, 0)),
            pl.BlockSpec((None, None, 6, D), lambda i: (layer, _mod_row(first + i), 0, 0)),
            pl.BlockSpec((None, D, 2 * D_FF), lambda i: (layer, 0, 0), pipeline_mode=once),
            pl.BlockSpec((None, 3, 2 * D_FF), lambda i: (layer, 0, 0), pipeline_mode=once),
            pl.BlockSpec((None, 1, 2 * D_FF), lambda i: (layer, 0, 0), pipeline_mode=once),
            pl.BlockSpec((None, D_FF, D), lambda i: (layer, 0, 0), pipeline_mode=once),
        ] + final_specs,
        out_specs=pl.BlockSpec((ROW_TILE, D), lambda i: (i, 0)),
        scratch_shapes=[pltpu.VMEM((ROW_TILE, D), BF16),
                        pltpu.VMEM((ROW_TILE + 2 * CONV_PAD, 2 * FFN_CHUNK), F32),
                        pltpu.VMEM((ROW_TILE + 2 * CONV_PAD, 2 * FFN_CHUNK), F32),
                        pltpu.VMEM((ROW_TILE, D_FF), BF16)],
        compiler_params=_cparams(("arbitrary",), vmem),
        name="conv_ffn",
    )(*x_ops, norm_g.reshape(DEPTH, 1, D), mods, w_up, conv_w, conv_b.reshape(DEPTH, 1, 2 * D_FF), w_down,
      *final_ops)


NQ = ATTN_HEADS * HEAD_DIM
NKV = ATTN_KV_HEADS * HEAD_DIM
Q_BLOCK = 128
MASKED = -1e30
LOG2E = 1.0 / math.log(2.0)
Q_PRESCALE = HEAD_DIM ** -0.5 * LOG2E


def _dot_t(a, b):
    return lax.dot_general(a.astype(BF16), b.astype(BF16), (((1,), (1,)), ((), ())),
                           preferred_element_type=F32)


def _group_rows(q):
    return jnp.concatenate([q[:, g * HEAD_DIM:(g + 1) * HEAD_DIM] for g in range(ATTN_GROUP)], axis=0)


def _sink_lanes(sink_ref, h, rows):
    return LOG2E * jnp.concatenate(
        [jnp.broadcast_to(sink_ref[0:1, ATTN_GROUP * h + g:ATTN_GROUP * h + g + 1], (1, rows))
         for g in range(ATTN_GROUP)], axis=-1)


ONES_ROWS = 16


def _values_t(v):
    ones = jnp.ones((ONES_ROWS, v.shape[0]), F32)
    out = []
    for c in range(NKV // 128):
        vt = v[:, c * 128:(c + 1) * 128].T
        out += [jnp.concatenate([vt[j * HEAD_DIM:(j + 1) * HEAD_DIM], ones], axis=0).astype(BF16)
                for j in range(128 // HEAD_DIM)]
    return out


def _softmax_pv(q4, key_sets, sink2):
    scores = []
    for k, _, bias in key_sets:
        s = _dot_t(k, q4)
        if bias is not None:
            s = jnp.concatenate([s[c * 128:(c + 1) * 128] if bc is None else s[c * 128:(c + 1) * 128] + bc
                                 for c, bc in enumerate(bias)], axis=0)
        scores.append(s)
    m = sink2
    for s in scores:
        m = jnp.maximum(m, jnp.max(s, axis=0, keepdims=True))
    acc = None
    for s, (_, v1t, _) in zip(scores, key_sets):
        t = jnp.dot(v1t, jnp.exp2(s - m).astype(BF16), preferred_element_type=F32)
        acc = t if acc is None else acc + t
    denom = acc[HEAD_DIM:HEAD_DIM + 1] + jnp.exp2(sink2 - m)
    return acc[:HEAD_DIM] * (1.0 / denom)


def _heads_to_columns(o_t, rows):
    slabs = []
    for g in range(0, ATTN_GROUP, 128 // HEAD_DIM):
        pair = jnp.concatenate([o_t[:, (g + j) * rows:(g + j + 1) * rows] for j in range(128 // HEAD_DIM)], axis=0)
        slabs.append(pair.T)
    return jnp.concatenate(slabs, axis=-1)


def _ctx_attn_kernel(qkv_ref, sink_ref, o_ref):
    v1t = _values_t(qkv_ref[:, NQ + NKV:])
    outs = []
    for h in range(ATTN_KV_HEADS):
        k = qkv_ref[:, NQ + h * HEAD_DIM:NQ + (h + 1) * HEAD_DIM].astype(BF16)
        q4 = _group_rows(qkv_ref[:, ATTN_GROUP * h * HEAD_DIM:ATTN_GROUP * (h + 1) * HEAD_DIM] * Q_PRESCALE)
        o_t = _softmax_pv(q4.astype(BF16), [(k, v1t[h], None)], _sink_lanes(sink_ref, h, SEQ))
        outs.append(_heads_to_columns(o_t, SEQ))
    o_ref[...] = jnp.concatenate(outs, axis=-1).astype(BF16)


def context_attention(qkv, sink):
    return pl.pallas_call(
        _ctx_attn_kernel,
        out_shape=jax.ShapeDtypeStruct((ROWS_P, NQ), BF16),
        grid=(BATCH,),
        in_specs=[pl.BlockSpec((SEQ, NQ + 2 * NKV), lambda b: (b, 0)),
                  pl.BlockSpec((1, ATTN_HEADS), lambda b: (0, 0))],
        out_specs=pl.BlockSpec((SEQ, NQ), lambda b: (b, 0)),
        compiler_params=_cparams(("arbitrary",), 2 * SEQ * (2 * NQ + 2 * NKV) * 4 + 24 * SEQ * ATTN_GROUP * SEQ * 4),
        name="context_attention",
    )(qkv, sink.reshape(1, ATTN_HEADS))


def _rope(x, cos, sin_a, sin_b):
    outs = []
    for c in range(x.shape[1] // 128):
        s = x[:, c * 128:(c + 1) * 128]
        outs.append(s * cos + pltpu.roll(s, 128 - HEAD_DIM // 4, 1) * sin_a + pltpu.roll(s, HEAD_DIM // 4, 1) * sin_b)
    return jnp.concatenate(outs, axis=-1)


Q_TILE = 2 * Q_BLOCK


def _lat_attn_kernel(q_ref, kp_ref, km_ref, kn_ref, vp_ref, vm_ref, vn_ref, ck_ref, cv_ref,
                     cos_ref, sa_ref, sb_ref, sink_ref, o_ref):
    m = pl.program_id(1)
    nm = pl.num_programs(1)
    nb = 2 * nm

    def tables(blk, rows):
        r = pl.ds(pl.multiple_of(blk * Q_BLOCK, Q_BLOCK), rows)
        return cos_ref[r, :], sa_ref[r, :], sb_ref[r, :]

    mid = tables(2 * m, Q_TILE)
    qr = (_rope(q_ref[...], *mid) * Q_PRESCALE).astype(BF16)
    k4 = jnp.concatenate([
        _rope(kp_ref[...], *tables(jnp.maximum(2 * m - 1, 0), Q_BLOCK)),
        _rope(km_ref[...], *mid),
        _rope(kn_ref[...], *tables(jnp.minimum(2 * m + 2, nb - 1), Q_BLOCK))], axis=0).astype(BF16)
    v4 = jnp.concatenate([vp_ref[...], vm_ref[...], vn_ref[...]], axis=0)

    cols = ATTN_GROUP * Q_TILE
    koff = lax.broadcasted_iota(jnp.int32, (Q_BLOCK, cols), 0)
    lane = lax.broadcasted_iota(jnp.int32, (Q_BLOCK, cols), 1)
    qoff = lane & (Q_BLOCK - 1)
    second = (lane & Q_BLOCK) != 0
    lower, upper = koff >= qoff, koff <= qoff
    bias = [jnp.where(jnp.logical_not(second) & lower & (m > 0), 0.0, MASKED),
            jnp.where(jnp.logical_not(second) | lower, 0.0, MASKED),
            jnp.where(second | upper, 0.0, MASKED),
            jnp.where(second & upper & (m < nm - 1), 0.0, MASKED)]

    v1t = _values_t(v4)
    cv1t = _values_t(cv_ref[...])
    outs = []
    for h in range(ATTN_KV_HEADS):
        hs = slice(h * HEAD_DIM, (h + 1) * HEAD_DIM)
        q4 = _group_rows(qr[:, ATTN_GROUP * h * HEAD_DIM:ATTN_GROUP * (h + 1) * HEAD_DIM])
        o_t = _softmax_pv(q4, [(k4[:, hs], v1t[h], bias), (ck_ref[:, hs].astype(BF16), cv1t[h], None)],
                          _sink_lanes(sink_ref, h, Q_TILE))
        outs.append(_heads_to_columns(o_t, Q_TILE))
    o_ref[...] = jnp.concatenate(outs, axis=-1).astype(BF16)


def _rope_tables():
    t = np.arange(DEC_SEQ)
    half = HEAD_DIM // 2
    inv_freq = 1.0 / (ROPE_BASE ** (np.arange(0, half, 2, dtype=np.float32) / half))
    ar = (t // GRID_W).astype(np.float32)[:, None] * inv_freq
    ac = (t % GRID_W).astype(np.float32)[:, None] * inv_freq
    return jnp.concatenate([jnp.asarray(a) for a in (ar, ar, ac, ac)] * 2, axis=-1)


def latent_attention(qkv, cache_k, cache_v, layer_j, sink):
    ang = _rope_tables()
    cos, sin = jnp.cos(ang), jnp.sin(ang)
    first = (lax.broadcasted_iota(jnp.int32, ang.shape, 1) % (HEAD_DIM // 2)) < HEAD_DIM // 4
    sin_a = jnp.where(first, -sin, 0.0)
    sin_b = jnp.where(first, 0.0, sin)
    nb = DEC_SEQ // Q_BLOCK
    nm = DEC_SEQ // Q_TILE
    base = ROWS_P // Q_BLOCK
    base_t = ROWS_P // Q_TILE
    kcol, vcol = NQ // NKV, NQ // NKV + 1
    ck = cache_k.reshape(DEC_BATCH, -1, PAST_LEN, NKV)
    cv = cache_v.reshape(DEC_BATCH, -1, PAST_LEN, NKV)

    def edge_spec(col, blk):
        return pl.BlockSpec((Q_BLOCK, NKV), lambda b, m: (base + b * nb + jnp.clip(blk(m), 0, nb - 1), col))

    def mid_spec(col):
        return pl.BlockSpec((Q_TILE, NKV), lambda b, m: (base_t + b * nm + m, col))

    prev_blk, next_blk = (lambda m: 2 * m - 1), (lambda m: 2 * m + 2)
    table = pl.BlockSpec((DEC_SEQ, 128), lambda b, m: (0, 0))
    return pl.pallas_call(
        _lat_attn_kernel,
        out_shape=jax.ShapeDtypeStruct((ROWS_S, NQ), BF16),
        grid=(DEC_BATCH, nm),
        in_specs=[pl.BlockSpec((Q_TILE, NQ), lambda b, m: (base_t + b * nm + m, 0)),
                  edge_spec(kcol, prev_blk), mid_spec(kcol), edge_spec(kcol, next_blk),
                  edge_spec(vcol, prev_blk), mid_spec(vcol), edge_spec(vcol, next_blk),
                  pl.BlockSpec((None, None, PAST_LEN, NKV), lambda b, m: (b, layer_j, 0, 0)),
                  pl.BlockSpec((None, None, PAST_LEN, NKV), lambda b, m: (b, layer_j, 0, 0)),
                  table, table, table,
                  pl.BlockSpec((1, ATTN_HEADS), lambda b, m: (0, 0))],
        out_specs=pl.BlockSpec((Q_TILE, NQ), lambda b, m: (b * nm + m, 0)),
        compiler_params=_cparams(("arbitrary", "arbitrary"),
                                 4 * Q_TILE * NQ * 4 + 12 * Q_TILE * NKV * 4 + 4 * PAST_LEN * NKV * 4
                                 + 6 * DEC_SEQ * 128 * 4 + 16 * ATTN_GROUP * Q_TILE * (4 * Q_BLOCK + PAST_LEN) * 4),
        name="latent_attention",
    )(qkv, qkv, qkv, qkv, qkv, qkv, qkv, ck, cv, cos, sin_a, sin_b, sink.reshape(1, ATTN_HEADS))


def attention_layer(x, layer, layer_j, mods, norm_g, wqkv, wo, sink, cache_k, cache_v):
    qkv = norm_mod_matmul(x, layer, norm_g, mods, wqkv, layer_j, "attn_qkv")
    a_p = context_attention(qkv, sink)
    a_s = latent_attention(qkv, cache_k, cache_v, layer_j, sink)
    x = matmul_gated_residual(a_p, a_s, wo, layer_j, x, layer, mods, "attn_wo")
    return x, qkv


def _cache_out_kernel(*refs):
    n = (len(refs) - 2) // 2
    k_refs, v_refs, (ok_ref, ov_ref) = refs[:n], refs[n:2 * n], refs[2 * n:]
    for j in range(n):
        @pl.when(pl.program_id(1) == j)
        def _(j=j):
            for h in range(ATTN_KV_HEADS):
                hs = slice(h * HEAD_DIM, (h + 1) * HEAD_DIM)
                ok_ref[:, h, :] = k_refs[j][:, hs]
                ov_ref[:, h, :] = v_refs[j][:, hs]


def new_context_cache(qkvs):
    n = len(qkvs)
    kcol, vcol = NQ // NKV, NQ // NKV + 1
    spec = lambda col: pl.BlockSpec((SEQ, NKV), lambda b, j: (b, col))
    out = jax.ShapeDtypeStruct((BATCH, n, SEQ, ATTN_KV_HEADS, HEAD_DIM), F32)
    out_spec = pl.BlockSpec((None, None, SEQ, ATTN_KV_HEADS, HEAD_DIM), lambda b, j: (b, j, 0, 0, 0))
    return pl.pallas_call(
        _cache_out_kernel,
        out_shape=(out, out),
        grid=(BATCH, n),
        in_specs=[spec(kcol)] * n + [spec(vcol)] * n,
        out_specs=(out_spec, out_spec),
        compiler_params=_cparams(("arbitrary", "arbitrary"), 8 * n * SEQ * NKV * 4 + 4 * SEQ * 8 * 128 * 4),
        name="new_context_cache",
    )(*qkvs, *qkvs)


HG_TILE = 256
HG_LEVELS = 8
HG_IN = 3 * 1024 + 2 * 1024
HG_OUT_TILE = 512
HG_HEADS_PER_STEP = 8


def _hgrn_consts():
    t = np.arange(HG_TILE)
    x = t[:, None] ^ t[None, :]
    hb = np.where(x == 0, -1, np.floor(np.log2(np.maximum(x, 1))).astype(np.int64))
    later = t[:, None] > t[None, :]
    half = HG_TILE // 2
    masks, tris = [], []
    for reverse in (False, True):
        side = ~later & (x != 0) if reverse else later
        lv = [hb == -1] + [(hb == lvl) & side for lvl in range(HG_LEVELS - 1)]
        masks.append(np.stack([m[:half, :half] for m in lv]).astype(np.float32))
        tris.append((t[None, :] >= t[:, None]) if reverse else (t[None, :] <= t[:, None]))
    return jnp.asarray(np.stack(masks)), jnp.asarray(np.stack(tris).astype(np.float32), dtype=BF16)


def _split_bf16(x):
    def top(v):
        bits = lax.bitcast_convert_type(v, jnp.uint32) & jnp.uint32(0xFFFF0000)
        return lax.bitcast_convert_type(bits, F32)

    hi = top(x)
    r = x - hi
    mid = top(r)
    return hi.astype(BF16), mid.astype(BF16), (r - mid).astype(BF16)


def _block_row(x, blk, idx):
    t = x.shape[0]
    x3 = x.reshape(t // blk, blk, x.shape[1])
    return jnp.broadcast_to(x3[:, idx:idx + 1, :], x3.shape).reshape(x.shape)


def _lower_bound(lb_ref, layer, direction):
    x = lb_ref[direction]
    e = jnp.exp(x - jnp.max(x, axis=0, keepdims=True))
    p = e / jnp.sum(e, axis=0, keepdims=True)
    return jnp.sum(p[1:layer + 1, :], axis=0, keepdims=True)


def _hgrn_tile(q, v, z, lb, s_in, mask_ref, tri, reverse):
    t = HG_TILE
    lo, hi = slice(0, t // 2), slice(t // 2, t)
    sg = _sigmoid(z)
    f = lb + (1.0 - lb) * sg
    k = (1.0 - lb) * (1.0 - sg)
    lf3 = _split_bf16(jnp.log(f))
    cum = sum(jnp.dot(tri, p, preferred_element_type=F32) for p in lf3)
    cum2 = cum * LOG2E
    rows = lax.broadcasted_iota(jnp.int32, (t, 1), 0)
    att = [mask_ref[0] * _dot_t(q[r], k[r]) for r in (lo, hi)]
    top = None
    for lvl in range(HG_LEVELS):
        half = 1 << lvl
        bit = (rows & half) != 0
        qside = ~bit if reverse else bit
        if lvl == 0:
            e = jnp.where(qside, f, 1.0)
        else:
            ref = _block_row(cum2, 2 * half, half if reverse else half - 1)
            e = jnp.exp2(-jnp.abs(cum2 - ref))
        w = (jnp.where(qside, q, k) * e).astype(BF16)
        if lvl < HG_LEVELS - 1:
            att = [a + mask_ref[lvl + 1] * _dot_t(w[r], w[r]) for a, r in zip(att, (lo, hi))]
        else:
            top = _dot_t(w[lo], w[hi]) if reverse else _dot_t(w[hi], w[lo])
    vb = v.astype(BF16)
    o_lo, o_hi = _bdot(att[0], vb[lo]), _bdot(att[1], vb[hi])
    if reverse:
        o_lo = o_lo + _bdot(top, vb[hi])
    else:
        o_hi = o_hi + _bdot(top, vb[lo])
    o = jnp.concatenate([o_lo, o_hi], axis=0)
    last = cum[0:1, :] if reverse else cum[t - 1:t, :]
    kd = (k * jnp.exp(last - cum)).astype(BF16)
    s_out = lax.dot_general(kd, vb, (((0,), (0,)), ((), ())), preferred_element_type=F32)
    if s_in is not None:
        o = o + _bdot(q * jnp.exp(cum), s_in)
        ones = jnp.ones((t, HGRN_DV), BF16)
        last_col = sum(lax.dot_general(p, ones, (((0,), (0,)), ((), ())), preferred_element_type=F32) for p in lf3)
        s_out = jnp.exp(last_col) * s_in + s_out
    return o, s_out


def _hgrn_prompt_kernel(q_ref, v_ref, zf_ref, zb_ref, lb_ref, mask_ref, tri_ref, of_ref, ob_ref, sfin_ref, *, layer):
    for j in range(HG_HEADS_PER_STEP):
        hs = slice(j * HGRN_DK, (j + 1) * HGRN_DK)
        q, v = q_ref[:, hs], v_ref[:, hs]
        lb_f, lb_b = (_lower_bound(lb_ref.at[:, :, hs], layer, d) for d in range(2))
        of_ref[:, hs], sfin_ref[0, j] = _hgrn_tile(q, v, zf_ref[:, hs], lb_f, None, mask_ref.at[0], tri_ref[0], False)
        ob_ref[:, hs], sfin_ref[1, j] = _hgrn_tile(q, v, zb_ref[:, hs], lb_b, None, mask_ref.at[1], tri_ref[1], True)


def _hgrn_sample_kernel(qf_ref, vf_ref, zf_ref, qb_ref, vb_ref, zb_ref, lb_ref, s0_ref, mask_ref, tri_ref,
                        of_ref, ob_ref, state_ref, *, layer):
    tiles = DEC_SEQ // HG_TILE

    @pl.when(pl.program_id(1) % tiles == 0)
    def _():
        state_ref[...] = s0_ref[...]

    for j in range(HG_HEADS_PER_STEP):
        hs = slice(j * HGRN_DK, (j + 1) * HGRN_DK)
        lb_f, lb_b = (_lower_bound(lb_ref.at[:, :, hs], layer, d) for d in range(2))
        of_ref[:, hs], state_ref[0, j] = _hgrn_tile(qf_ref[:, hs], vf_ref[:, hs], zf_ref[:, hs], lb_f,
                                                    state_ref[0, j], mask_ref.at[0], tri_ref[0], False)
        ob_ref[:, hs], state_ref[1, j] = _hgrn_tile(qb_ref[:, hs], vb_ref[:, hs], zb_ref[:, hs], lb_b,
                                                    state_ref[1, j], mask_ref.at[1], tri_ref[1], True)


def hgrn_scan(proj, lb_raw, layer, state_hgrn, layer_j):
    masks, tris = _hgrn_consts()
    hps = HG_HEADS_PER_STEP
    hw = hps * HGRN_DK
    qc, vc, zfc, zbc = 0, 1024 // hw, 2048 // hw, 3072 // hw
    const_specs = [pl.BlockSpec((2, DEPTH, hw), lambda h, i: (0, 0, h))]
    mask_specs = [pl.BlockSpec(masks.shape, lambda h, i: (0, 0, 0, 0)),
                  pl.BlockSpec((2, HG_TILE, HG_TILE), lambda h, i: (0, 0, 0))]
    vmem = 4 * masks.size * 4 + 16 * HG_TILE * hw * 4 + 8 * hw * HGRN_DV * 4 + 24 * hps * HG_TILE * HG_TILE * 4

    def col(cb, row_fn):
        return pl.BlockSpec((HG_TILE, hw), lambda h, i: (row_fn(i), cb + h))

    o_shape = jax.ShapeDtypeStruct((ROWS_P, HGRN_HEADS * HGRN_DV), F32)
    same = lambda i: i
    of_p, ob_p, sfin = pl.pallas_call(
        functools.partial(_hgrn_prompt_kernel, layer=layer),
        out_shape=(o_shape, o_shape, jax.ShapeDtypeStruct((BATCH, 2, HGRN_HEADS, HGRN_DK, HGRN_DV), F32)),
        grid=(HGRN_HEADS // hps, BATCH),
        in_specs=[col(qc, same), col(vc, same), col(zfc, same), col(zbc, same)] + const_specs + mask_specs,
        out_specs=(col(0, same), col(0, same),
                   pl.BlockSpec((None, 2, hps, HGRN_DK, HGRN_DV), lambda h, i: (i, 0, h, 0, 0))),
        compiler_params=_cparams(("arbitrary", "arbitrary"), vmem),
        name="hgrn_scan_prompt",
    )(proj, proj, proj, proj, lb_raw, masks, tris)

    tiles = DEC_SEQ // HG_TILE
    base = ROWS_P // HG_TILE
    bwd = lambda i: (i // tiles) * tiles + (tiles - 1 - i % tiles)
    fwd_in = lambda i: base + i
    bwd_in = lambda i: base + bwd(i)
    o_shape = jax.ShapeDtypeStruct((ROWS_S, HGRN_HEADS * HGRN_DV), F32)
    of_s, ob_s = pl.pallas_call(
        functools.partial(_hgrn_sample_kernel, layer=layer),
        out_shape=(o_shape, o_shape),
        grid=(HGRN_HEADS // hps, DEC_BATCH * tiles),
        in_specs=[col(qc, fwd_in), col(vc, fwd_in), col(zfc, fwd_in), col(qc, bwd_in), col(vc, bwd_in),
                  col(zbc, bwd_in)]
        + const_specs
        + [pl.BlockSpec((None, None, 2, hps, HGRN_DK, HGRN_DV), lambda h, i: (i // tiles, layer_j, 0, h, 0, 0))]
        + mask_specs,
        out_specs=(col(0, same), col(0, bwd)),
        scratch_shapes=[pltpu.VMEM((2, hps, HGRN_DK, HGRN_DV), F32)],
        compiler_params=_cparams(("arbitrary", "arbitrary"), vmem),
        name="hgrn_scan_sample",
    )(proj, proj, proj, proj, proj, proj, lb_raw, state_hgrn, masks, tris)
    return (of_p, ob_p), (of_s, ob_s), sfin


def _hgrn_out_kernel(ofp_ref, obp_ref, ofs_ref, obs_ref, g_ref, gn_ref, w_ref, x_ref, mod_ref, o_ref):
    def run(of_ref, ob_ref):
        gn = gn_ref[...]
        parts = []
        for h in range(HGRN_HEADS):
            hs = slice(h * HGRN_DV, (h + 1) * HGRN_DV)
            o = of_ref[:, hs] + ob_ref[:, hs]
            g = g_ref[:, hs]
            o = o * lax.rsqrt(jnp.mean(o * o, axis=-1, keepdims=True) + NORM_EPS) * gn * _silu(g)
            parts.append(o.astype(BF16))
        y = jnp.dot(jnp.concatenate(parts, axis=-1), w_ref[...], preferred_element_type=F32)
        o_ref[...] = x_ref[...] + mod_ref[2:3, :] * y

    is_prompt = pl.program_id(0) < ROWS_P // HG_OUT_TILE
    pl.when(is_prompt)(lambda: run(ofp_ref, obp_ref))
    pl.when(jnp.logical_not(is_prompt))(lambda: run(ofs_ref, obs_ref))


def hgrn_out(o_p, o_s, proj, g_norm, wo, w_idx, x, layer, mods):
    gcol = 4096 // D
    tile = HG_OUT_TILE
    p_spec, s_spec = _group_specs(D, tile)
    return pl.pallas_call(
        _hgrn_out_kernel,
        out_shape=jax.ShapeDtypeStruct((ROWS, D), F32),
        grid=(ROWS // tile,),
        in_specs=[p_spec, p_spec, s_spec, s_spec,
                  pl.BlockSpec((tile, D), lambda i: (i, gcol)),
                  pl.BlockSpec((1, HGRN_DV), lambda i: (0, 0)),
                  pl.BlockSpec((None, D, D), lambda i: (w_idx, 0, 0), pipeline_mode=pl.Buffered(1)),
                  pl.BlockSpec((tile, D), lambda i: (i, 0)),
                  _mod_spec(layer, 1, tile)],
        out_specs=pl.BlockSpec((tile, D), lambda i: (i, 0)),
        compiler_params=_cparams(("arbitrary",), 22 * tile * D * 4),
        name="hgrn_out",
    )(*o_p, *o_s, proj, g_norm.reshape(1, HGRN_DV), wo, x, mods)


def hgrn_layer(x, layer, layer_j, mods, norm_g, w_in, hgrn_lb, g_norm, wo, state_hgrn):
    proj = norm_mod_matmul(x, layer, norm_g, mods, w_in, layer_j, "hgrn_in")
    o_p, o_s, sfin = hgrn_scan(proj, jnp.transpose(hgrn_lb, (1, 0, 2)), layer, state_hgrn, layer_j)
    x = hgrn_out(o_p, o_s, proj, g_norm[layer_j], wo, layer_j, x, layer, mods)
    return x, sfin


SSM_N = SSM_GROUPS * SSM_STATE
SSM_KT = 8
SSM_ROWS = 256


def _ssm_prep_kernel(are_ref, aim_ref, ldt_ref, bre_ref, bim_ref, cre_ref, cim_ref,
                     lre_ref, lim_ref, bm_ref, cm_ref, bm_acc, cm_acc):
    a_re = jnp.minimum(are_ref[...], -1e-4)
    a_im = aim_ref[...]
    dt = jnp.exp(ldt_ref[...])
    mag = jnp.exp(a_re * dt)
    l_re = mag * jnp.cos(a_im * dt)
    l_im = mag * jnp.sin(a_im * dt)
    lre_ref[...] = l_re
    lim_ref[...] = l_im
    den = a_re * a_re + a_im * a_im
    c_re = ((l_re - 1.0) * a_re + l_im * a_im) / den
    c_im = (l_im * a_re - (l_re - 1.0) * a_im) / den
    b_re, b_im = bre_ref[...], bim_ref[...]
    bb_re = c_re[:, None, :] * b_re - c_im[:, None, :] * b_im
    bb_im = c_re[:, None, :] * b_im + c_im[:, None, :] * b_re
    bm_acc[...] = jnp.zeros_like(bm_acc)
    cm_acc[...] = jnp.zeros_like(cm_acc)
    kw = SSM_KT * SSM_STATE
    for dg in range(2 * SSM_GROUPS):
        d, g = divmod(dg, SSM_GROUPS)
        k, gl = divmod(g, SSM_KT)
        ch = slice(gl * SSM_GROUP, (gl + 1) * SSM_GROUP)
        st = slice(gl * SSM_STATE, (gl + 1) * SSM_STATE)
        st_im = slice(kw + gl * SSM_STATE, kw + (gl + 1) * SSM_STATE)
        bm_acc[d, k, ch, st] = bb_re[dg]
        bm_acc[d, k, ch, st_im] = bb_im[dg]
        cm_acc[d, k, st, ch] = cre_ref[dg]
        cm_acc[d, k, st_im, ch] = -cim_ref[dg]
    bm_ref[...] = bm_acc[...].astype(BF16)
    cm_ref[...] = cm_acc[...].astype(BF16)


def ssm_discretize(a_re, a_im, log_dt, b_re, b_im, c_re, c_im):
    g2 = 2 * SSM_GROUPS
    nk = SSM_GROUPS // SSM_KT
    kw = SSM_KT * SSM_STATE
    sh = jax.ShapeDtypeStruct((g2, SSM_STATE), F32)
    bm_shape = (2, nk, SSM_KT * SSM_GROUP, 2 * kw)
    cm_shape = (2, nk, 2 * kw, SSM_KT * SSM_GROUP)
    bt = lambda b: jnp.transpose(b, (0, 1, 3, 2)).reshape(g2, SSM_GROUP, SSM_STATE)
    ct = lambda c: jnp.transpose(c, (0, 1, 3, 2)).reshape(g2, SSM_STATE, SSM_GROUP)
    l_re, l_im, b_mat, c_mat = pl.pallas_call(
        _ssm_prep_kernel,
        out_shape=(sh, sh, jax.ShapeDtypeStruct(bm_shape, BF16), jax.ShapeDtypeStruct(cm_shape, BF16)),
        scratch_shapes=[pltpu.VMEM(bm_shape, F32), pltpu.VMEM(cm_shape, F32)],
        compiler_params=pltpu.CompilerParams(vmem_limit_bytes=40 << 20),
        name="ssm_discretize",
    )(a_re.reshape(g2, SSM_STATE), a_im.reshape(g2, SSM_STATE), log_dt.reshape(g2, 1), bt(b_re), bt(b_im),
      ct(c_re), ct(c_im))
    lam = jnp.stack([l_re.reshape(2, SSM_N), l_im.reshape(2, SSM_N)], axis=1)
    return b_mat, c_mat, lam


X4_SHAPE = (ROWS // (4 * SEQ), 4, SEQ, D)


def _tm_geometry(prompt):
    if prompt:
        batch = BATCH
        steps = SSM_ROWS // batch
        return batch, steps, (4, 4, steps, D), (lambda i: (0, 0, i, 0)), SEQ // steps, (0,) * batch
    batch = DEC_BATCH
    steps = SSM_ROWS // batch
    per_q = SEQ // steps
    return (batch, steps, (4, 1, steps, D), (lambda i: (1, i // per_q, i % per_q, 0)), DEC_SEQ // steps,
            tuple(range(1, 1 + batch)))


def _x4_seq(ref, b):
    return ref.at[b // ref.shape[1], b % ref.shape[1]]


LANE_SLABS = D // 128


def _slab_store(s_ref, rows, val):
    for c in range(LANE_SLABS):
        s_ref[c, rows, :] = val[:, c * 128:(c + 1) * 128]


def _slab_load(s_ref, rows):
    return jnp.concatenate([s_ref[c, rows, :] for c in range(LANE_SLABS)], axis=-1)


def _normmod_tm_kernel(x_ref, g_ref, mod_ref, o_ref, s_ref, *, batch, steps, mod_rows):
    g = g_ref[...]
    for b in range(batch):
        m = mod_ref.at[mod_rows[b]]
        _slab_store(s_ref, slice(b * steps, (b + 1) * steps),
                    _norm_mod(_x4_seq(x_ref, b)[...], g, m[0:1, :], m[1:2, :]))
    for t in range(steps):
        o_ref[t * batch:(t + 1) * batch, :] = _slab_load(s_ref, pl.ds(t, batch, stride=steps))


def norm_mod_time_major(x4, layer, norm_g, mods, prompt):
    batch, steps, blk, idx, tiles, mod_rows = _tm_geometry(prompt)
    return pl.pallas_call(
        functools.partial(_normmod_tm_kernel, batch=batch, steps=steps, mod_rows=mod_rows),
        out_shape=jax.ShapeDtypeStruct((tiles * SSM_ROWS, D), F32),
        grid=(tiles,),
        in_specs=[pl.BlockSpec(blk, idx),
                  pl.BlockSpec((None, 1, D), lambda i: (layer, 0, 0)),
                  pl.BlockSpec((None, MOD_ROWS, 6, D), lambda i: (layer, 0, 0, 0))],
        out_specs=pl.BlockSpec((SSM_ROWS, D), lambda i: (i, 0)),
        scratch_shapes=[pltpu.VMEM((LANE_SLABS, SSM_ROWS, 128), F32)],
        compiler_params=_cparams(("arbitrary",), 8 * SSM_ROWS * D * 4),
        name="ssm_norm_time_major",
    )(x4, norm_g.reshape(DEPTH, 1, D), mods)


def _ssm_scan_kernel(xf_ref, xb_ref, bm_ref, cm_ref, lam_ref, h0_ref, yf_ref, yb_ref, hfin_ref,
                     hre_f, him_f, hre_b, him_b, st_ref, *, batch):
    i = pl.program_id(0)
    steps = SSM_ROWS // batch
    nk = SSM_GROUPS // SSM_KT
    kw = SSM_KT * SSM_STATE
    x_refs, y_refs = (xf_ref, xb_ref), (yf_ref, yb_ref)
    h_refs = ((hre_f, him_f), (hre_b, him_b))

    @pl.when(i == 0)
    def _():
        st_ref[...] = h0_ref[...]

    def tile_cols(k):
        return slice(k * kw, (k + 1) * kw)

    def bu_tile(d, k):
        xk = x_refs[d][:, k * 128:(k + 1) * 128].astype(BF16)
        bu = jnp.dot(xk, bm_ref[d, k], preferred_element_type=F32)
        h_refs[d][0][:, tile_cols(k)] = bu[:, :kw]
        h_refs[d][1][:, tile_cols(k)] = bu[:, kw:]

    def c_tile(d, k):
        hk = jnp.concatenate([h_refs[d][0][:, tile_cols(k)], h_refs[d][1][:, tile_cols(k)]], axis=-1)
        y_refs[d][:, k * 128:(k + 1) * 128] = jnp.dot(hk.astype(BF16), cm_ref[d, k], preferred_element_type=F32)

    def scan_tile(d, k):
        hre_ref, him_ref = h_refs[d]
        col = tile_cols(k)
        l_re, l_im = lam_ref[d, 0, :, col], lam_ref[d, 1, :, col]
        h_re, h_im = st_ref[d, 0, :, col], st_ref[d, 1, :, col]
        per = max(8 // batch, 1)
        rows_per = per * batch
        for s in range(steps // per):
            g = (steps // per - 1 - s) if d else s
            rows = slice(g * rows_per, (g + 1) * rows_per)
            cur_re, cur_im = hre_ref[rows, col], him_ref[rows, col]
            outs_re, outs_im = [None] * per, [None] * per
            for r in (range(per - 1, -1, -1) if d else range(per)):
                b_re, b_im = cur_re[r * batch:(r + 1) * batch], cur_im[r * batch:(r + 1) * batch]
                h_re, h_im = l_re * h_re - l_im * h_im + b_re, l_re * h_im + l_im * h_re + b_im
                outs_re[r], outs_im[r] = h_re, h_im
            hre_ref[rows, col] = outs_re[0] if per == 1 else jnp.concatenate(outs_re, axis=0)
            him_ref[rows, col] = outs_im[0] if per == 1 else jnp.concatenate(outs_im, axis=0)
        st_ref[d, 0, :, col] = h_re
        st_ref[d, 1, :, col] = h_im

    for k in range(nk + 2):
        for d in range(2):
            if k < nk:
                bu_tile(d, k)
            if 1 <= k <= nk:
                scan_tile(d, k - 1)
            if k >= 2:
                c_tile(d, k - 2)

    @pl.when(i == pl.num_programs(0) - 1)
    def _():
        hfin_ref[...] = st_ref[...]


def ssm_scan(xn_tm, b_mat, c_mat, lam, h0, batch):
    rows = xn_tm.shape[0]
    n = rows // SSM_ROWS
    lam_b = jnp.broadcast_to(lam[:, :, None, :], (2, 2, batch, SSM_N))
    y_shape = jax.ShapeDtypeStruct((rows, D), F32)
    full = lambda a: pl.BlockSpec(a.shape, lambda i: (0,) * a.ndim)
    vmem = (4 * SSM_ROWS * SSM_N * 4 + 8 * SSM_ROWS * D * 4 + 2 * (b_mat.size + c_mat.size) * 2
            + 12 * batch * SSM_N * 4 * 2 + 8 * SSM_ROWS * 1024 * 4)
    return pl.pallas_call(
        functools.partial(_ssm_scan_kernel, batch=batch),
        out_shape=(y_shape, y_shape, jax.ShapeDtypeStruct((2, 2, batch, SSM_N), F32)),
        grid=(n,),
        in_specs=[pl.BlockSpec((SSM_ROWS, D), lambda i: (i, 0)),
                  pl.BlockSpec((SSM_ROWS, D), lambda i: (n - 1 - i, 0)),
                  full(b_mat), full(c_mat), full(lam_b), full(h0)],
        out_specs=(pl.BlockSpec((SSM_ROWS, D), lambda i: (i, 0)),
                   pl.BlockSpec((SSM_ROWS, D), lambda i: (n - 1 - i, 0)),
                   pl.BlockSpec((2, 2, batch, SSM_N), lambda i: (0, 0, 0, 0))),
        scratch_shapes=[pltpu.VMEM((SSM_ROWS, SSM_N), F32)] * 4
        + [pltpu.VMEM((2, 2, batch, SSM_N), F32)],
        compiler_params=_cparams(("arbitrary",), vmem),
        name="ssm_scan",
    )(xn_tm, xn_tm, b_mat, c_mat, lam_b, h0)


def _gelu_tanh(x):
    return 0.5 * x * (1.0 + jnp.tanh(math.sqrt(2.0 / math.pi) * (x + 0.044715 * (x * x * x))))


def _ssm_glu_kernel(yf_ref, yb_ref, xn_ref, d_ref, w_ref, x_ref, mod_ref, o_ref, s_ref,
                    *, batch, steps, mod_rows):
    g = _gelu_tanh(yf_ref[...] + yb_ref[...] + d_ref[...] * xn_ref[...])
    u = jnp.dot(g.astype(BF16), w_ref[...], preferred_element_type=F32)
    _slab_store(s_ref, slice(None), u[:, :D] * _sigmoid(u[:, D:]))
    for b in range(batch):
        gate = mod_ref[mod_rows[b], 2:3, :]
        _x4_seq(o_ref, b)[...] = (_x4_seq(x_ref, b)[...]
                                  + gate * _slab_load(s_ref, pl.ds(b, steps, stride=batch)))


def ssm_glu(yf, yb, xn, d, w_glu, w_idx, x4, layer, mods, prompt):
    batch, steps, blk, idx, tiles, mod_rows = _tm_geometry(prompt)
    tm_spec = pl.BlockSpec((SSM_ROWS, D), lambda i: (i, 0))
    out = pl.pallas_call(
        functools.partial(_ssm_glu_kernel, batch=batch, steps=steps, mod_rows=mod_rows),
        out_shape=jax.ShapeDtypeStruct((4,) + X4_SHAPE[1:], F32),
        grid=(tiles,),
        in_specs=[tm_spec, tm_spec, tm_spec,
                  pl.BlockSpec((None, 1, D), lambda i: (w_idx, 0, 0)),
                  pl.BlockSpec((None, D, 2 * D), lambda i: (w_idx, 0, 0), pipeline_mode=pl.Buffered(1)),
                  pl.BlockSpec(blk, idx),
                  pl.BlockSpec((None, MOD_ROWS, 6, D), lambda i: (layer, 0, 0, 0))],
        out_specs=pl.BlockSpec(blk, lambda i: (0,) + idx(i)[1:]),
        scratch_shapes=[pltpu.VMEM((LANE_SLABS, SSM_ROWS, 128), F32)],
        compiler_params=_cparams(("arbitrary",), 24 * SSM_ROWS * D * 4 + D * 2 * D * 2),
        name="ssm_glu",
    )(yf, yb, xn, d.reshape(-1, 1, D), w_glu, x4, mods)
    return out.reshape(-1, D)


def ssm_layer(x, layer, layer_j, mods, norm_g, a_re, a_im, log_dt, b_re, b_im, c_re, c_im, d, w_glu, state_ssm):
    b_mat, c_mat, lam = ssm_discretize(a_re[layer_j], a_im[layer_j], log_dt[layer_j], b_re[layer_j],
                                       b_im[layer_j], c_re[layer_j], c_im[layer_j])
    x4 = x.reshape(X4_SHAPE)
    xn_p = norm_mod_time_major(x4, layer, norm_g, mods, True)
    xn_s = norm_mod_time_major(x4, layer, norm_g, mods, False)
    h0_p = jnp.zeros((2, 2, BATCH, SSM_N), F32)
    h0_s = jnp.transpose(state_ssm[:, layer_j].reshape(DEC_BATCH, 2, SSM_N, 2), (1, 3, 0, 2))
    yfp, ybp, hfin = ssm_scan(xn_p, b_mat, c_mat, lam, h0_p, BATCH)
    yfs, ybs, _ = ssm_scan(xn_s, b_mat, c_mat, lam, h0_s, DEC_BATCH)
    out_p = ssm_glu(yfp, ybp, xn_p, d, w_glu, layer_j, x4, layer, mods, True)
    out_s = ssm_glu(yfs, ybs, xn_s, d, w_glu, layer_j, x4, layer, mods, False)
    new_state = jnp.transpose(hfin, (2, 0, 3, 1)).reshape(BATCH, 2, SSM_GROUPS, SSM_STATE, 2)
    return (out_p, out_s), new_state


def kernel(x_prompt, x_sample, cache_k, cache_v, state_hgrn, state_ssm, c, c_ctx, ada_w, ada_b, norm1_g, norm2_g, attn_wqkv, attn_wo, attn_sink, hgrn_w_in, hgrn_lb, hgrn_g_norm, hgrn_wo, ssm_a_re, ssm_a_im, ssm_log_dt, ssm_b_re, ssm_b_im, ssm_c_re, ssm_c_im, ssm_d, ssm_w_glu, ffn_w_up, ffn_conv_w, ffn_conv_b, ffn_w_down, final_g):
    cond8 = jnp.zeros((MOD_ROWS, D), F32).at[0].set(c_ctx).at[1:1 + DEC_BATCH].set(c)
    mods = ada_modulation(cond8, ada_w, ada_b)
    x = (x_prompt.reshape(ROWS_P, D), x_sample.reshape(ROWS_S, D))
    wqkv, wo, w_in, hwo, w_glu, w_up, w_down = (w.astype(BF16) for w in (
        attn_wqkv, attn_wo, hgrn_w_in, hgrn_wo, ssm_w_glu, ffn_w_up, ffn_w_down))
    qkvs, new_hgrn, new_ssm = [], [], []
    for l in range(DEPTH):
        kind, j = l % N_MIXERS, l // N_MIXERS
        if kind == 0:
            x, qkv = attention_layer(x, l, j, mods, norm1_g, wqkv, wo, attn_sink[j], cache_k, cache_v)
            qkvs.append(qkv)
        elif kind == 1:
            x, s = hgrn_layer(x, l, j, mods, norm1_g, w_in, hgrn_lb, hgrn_g_norm, hwo, state_hgrn)
            new_hgrn.append(s)
        else:
            x, s = ssm_layer(x, l, j, mods, norm1_g, ssm_a_re, ssm_a_im, ssm_log_dt, ssm_b_re, ssm_b_im,
                             ssm_c_re, ssm_c_im, ssm_d, w_glu, state_ssm)
            new_ssm.append(s)
        ffn = functools.partial(conv_ffn_residual, x, l, norm2_g, mods, w_up, ffn_conv_w, ffn_conv_b, w_down)
        if l + 1 < DEPTH:
            x = ffn()
    y_prompt = ffn(tiles=(0, N_ROW_TILES_P), final_g=final_g).reshape(BATCH, SEQ, D)
    y_sample = ffn(tiles=(N_ROW_TILES_P, N_ROW_TILES - N_ROW_TILES_P), final_g=final_g).reshape(DEC_BATCH, DEC_SEQ, D)
    new_k, new_v = new_context_cache(qkvs)
    return (y_prompt, y_sample, new_k, new_v, jnp.stack(new_hgrn, axis=1), jnp.stack(new_ssm, axis=1))
```

```python
import functools
import math

import jax
import jax.numpy as jnp
import numpy as np
from jax import lax
from jax.experimental import pallas as pl
from jax.experimental.pallas import tpu as pltpu

F32 = jnp.float32
BF16 = jnp.bfloat16

D = 1024
BATCH = 16
SEQ = 256
DEPTH = 4
DEC_BATCH = 4
DEC_SEQ = 1024
PAST_LEN = 512
GRID_W = 64
N_MIXERS = 3
ATTN_HEADS = 16
ATTN_KV_HEADS = 4
ATTN_GROUP = ATTN_HEADS // ATTN_KV_HEADS
HEAD_DIM = D // ATTN_HEADS
WINDOW = 128
ROPE_BASE = 10000.0
HGRN_HEADS = 8
HGRN_DK = 128
HGRN_DV = 128
SSM_GROUP = 16
SSM_GROUPS = D // SSM_GROUP
SSM_STATE = 64
D_FF = 2816
NORM_EPS = 1e-6

ROWS_P = BATCH * SEQ
ROWS_S = DEC_BATCH * DEC_SEQ
ROWS = ROWS_P + ROWS_S
ROW_TILE = 1024
N_ROW_TILES = ROWS // ROW_TILE
N_ROW_TILES_P = ROWS_P // ROW_TILE
MOD_ROWS = 8
V7X_VMEM_BYTES = 64 * 1024 * 1024


def _mod_row(i, tile=ROW_TILE):
    return jnp.where(i < ROWS_P // tile, 0, (i - ROWS_P // tile) // (DEC_SEQ // tile) + 1)


def _cparams(semantics, vmem_bytes):
    vmem = int(min(max(vmem_bytes * 5 // 4 + (4 << 20), 16 << 20), V7X_VMEM_BYTES - (6 << 20)))
    return pltpu.CompilerParams(dimension_semantics=semantics, vmem_limit_bytes=vmem)


def _bdot(a, b):
    return jnp.dot(a.astype(BF16), b.astype(BF16), preferred_element_type=F32)


def _norm_mod(x, g, shift, scale):
    y = x * lax.rsqrt(jnp.mean(x * x, axis=-1, keepdims=True) + NORM_EPS) * g
    return y * (1.0 + scale) + shift


def _sigmoid(x):
    return 0.5 + 0.5 * jnp.tanh(0.5 * x)


def _silu(x):
    h = 0.5 * x
    return h + h * jnp.tanh(h)


def _ada_kernel(c_ref, w_ref, b_ref, o_ref):
    c = c_ref[...]
    o_ref[...] = _bdot(_silu(c), w_ref[...]) + b_ref[...]


def ada_modulation(cond8, ada_w, ada_b):
    tn = 1024
    out = pl.pallas_call(
        _ada_kernel,
        out_shape=jax.ShapeDtypeStruct((DEPTH, MOD_ROWS, 6 * D), F32),
        grid=(DEPTH, 6 * D // tn),
        in_specs=[
            pl.BlockSpec((MOD_ROWS, D), lambda l, j: (0, 0)),
            pl.BlockSpec((None, D, tn), lambda l, j: (l, 0, j)),
            pl.BlockSpec((None, 1, tn), lambda l, j: (l, 0, j)),
        ],
        out_specs=pl.BlockSpec((None, MOD_ROWS, tn), lambda l, j: (l, 0, j)),
        compiler_params=_cparams(("arbitrary", "arbitrary"), 2 * D * tn * 4),
        name="ada_modulation",
    )(cond8, ada_w, ada_b.reshape(DEPTH, 1, 6 * D))
    return out.reshape(DEPTH, MOD_ROWS, 6, D)


def _x_operands(x, n_grid=1):
    if isinstance(x, tuple):
        return x, list(_group_specs(D, ROW_TILE, n_grid))
    return (x,), [pl.BlockSpec((ROW_TILE, D), (lambda i: (i, 0)) if n_grid == 1 else (lambda i, j: (i, 0)))]


def _read_rows(x_refs):
    if len(x_refs) == 1:
        return x_refs[0][...]
    return jnp.where(pl.program_id(0) < N_ROW_TILES_P, x_refs[0][...], x_refs[1][...])


def _nmm_kernel(*refs):
    x_refs, (g_ref, mod_ref, w_ref, o_ref, h_ref) = refs[:-5], refs[-5:]

    @pl.when(pl.program_id(1) == 0)
    def _():
        h_ref[...] = _norm_mod(_read_rows(x_refs), g_ref[...], mod_ref[0:1, :], mod_ref[1:2, :]).astype(BF16)

    o_ref[...] = jnp.dot(h_ref[...], w_ref[...], preferred_element_type=F32)


NMM_MAX_COLS = 2560


def _mod_spec(layer, n_grid, tile=ROW_TILE):
    if n_grid == 1:
        return pl.BlockSpec((None, None, 6, D), lambda i: (layer, _mod_row(i, tile), 0, 0))
    return pl.BlockSpec((None, None, 6, D), lambda i, j: (layer, _mod_row(i, tile), 0, 0))


def norm_mod_matmul(x, layer, norm_g, mods, w, w_idx, name):
    n = w.shape[-1]
    tn = n if n <= NMM_MAX_COLS else NMM_MAX_COLS
    assert n % tn == 0
    x_ops, x_specs = _x_operands(x, 2)
    return pl.pallas_call(
        _nmm_kernel,
        out_shape=jax.ShapeDtypeStruct((ROWS, n), F32),
        grid=(N_ROW_TILES, n // tn),
        in_specs=x_specs + [
            pl.BlockSpec((None, 1, D), lambda i, j: (layer, 0, 0)),
            _mod_spec(layer, 2),
            pl.BlockSpec((None, D, tn), lambda i, j: (w_idx, 0, j)),
        ],
        out_specs=pl.BlockSpec((ROW_TILE, tn), lambda i, j: (i, j)),
        scratch_shapes=[pltpu.VMEM((ROW_TILE, D), BF16)],
        compiler_params=_cparams(("arbitrary", "arbitrary"),
                                 4 * ROW_TILE * D * 4 + ROW_TILE * D * 2 + 2 * D * tn * 2 + 2 * ROW_TILE * tn * 4),
        name=name,
    )(*x_ops, norm_g.reshape(DEPTH, 1, D), mods, w)


def _mm_res_kernel(ap_ref, as_ref, w_ref, *refs):
    x_refs, (mod_ref, o_ref) = refs[:-2], refs[-2:]

    def run(a_ref, x_ref):
        y = jnp.dot(a_ref[...].astype(BF16), w_ref[...], preferred_element_type=F32)
        o_ref[...] = x_ref[...] + mod_ref[2:3, :] * y

    is_prompt = pl.program_id(0) < N_ROW_TILES_P
    pl.when(is_prompt)(lambda: run(ap_ref, x_refs[0]))
    pl.when(jnp.logical_not(is_prompt))(lambda: run(as_ref, x_refs[-1]))


def _group_specs(k, tile=ROW_TILE, n_grid=1):
    n_p = ROWS_P // tile
    if n_grid == 1:
        return (pl.BlockSpec((tile, k), lambda i: (jnp.minimum(i, n_p - 1), 0)),
                pl.BlockSpec((tile, k), lambda i: (jnp.maximum(i - n_p, 0), 0)))
    return (pl.BlockSpec((tile, k), lambda i, j: (jnp.minimum(i, n_p - 1), 0)),
            pl.BlockSpec((tile, k), lambda i, j: (jnp.maximum(i - n_p, 0), 0)))


def matmul_gated_residual(a_p, a_s, w, w_idx, x, layer, mods, name):
    k = a_p.shape[1]
    x_ops, x_specs = _x_operands(x)
    return pl.pallas_call(
        _mm_res_kernel,
        out_shape=jax.ShapeDtypeStruct((ROWS, D), F32),
        grid=(N_ROW_TILES,),
        in_specs=[
            *_group_specs(k),
            pl.BlockSpec((None, k, D), lambda i: (w_idx, 0, 0), pipeline_mode=pl.Buffered(1)),
            *x_specs,
            _mod_spec(layer, 1),
        ],
        out_specs=pl.BlockSpec((ROW_TILE, D), lambda i: (i, 0)),
        compiler_params=_cparams(("arbitrary",), 4 * ROW_TILE * k * 2 + k * D * 2 + 7 * ROW_TILE * D * 4),
        name=name,
    )(a_p, a_s, w, *x_ops, mods)


FFN_CHUNK = 256
FFN_CHUNKS = D_FF // FFN_CHUNK
FFN_DOWN_GROUP = FFN_CHUNKS
CONV_PAD = 8


def _ffn_kernel(*refs, split_x, first_tile, final_norm):
    n_x = 1 + split_x
    x_refs = refs[:n_x]
    g_ref, mod_ref, wup_ref, cw_ref, cb_ref, wd_ref = refs[n_x:n_x + 6]
    fg_ref = refs[n_x + 6] if final_norm else None
    o_ref, h_ref, pad_a, pad_b, act_ref = refs[n_x + 6 + final_norm:]
    i = pl.program_id(0) + first_tile
    h_ref[...] = _norm_mod(_read_rows(x_refs), g_ref[...], mod_ref[3:4, :], mod_ref[4:5, :]).astype(BF16)
    zeros = jnp.zeros((CONV_PAD, 2 * FFN_CHUNK), F32)
    for pad_ref in (pad_a, pad_b):
        pad_ref[0:CONV_PAD, :] = zeros
        pad_ref[CONV_PAD + ROW_TILE:, :] = zeros
    sub = lax.broadcasted_iota(jnp.int32, (8, 1), 0)
    is_prompt = i < N_ROW_TILES_P
    keep_first = jnp.where((sub == 0) & is_prompt, 0.0, 1.0)
    keep_last = jnp.where((sub == 7) & is_prompt, 0.0, 1.0)

    def cut_sequences(v, keep, row):
        parts, at = [], 0
        for b in range(SEQ, ROW_TILE, SEQ):
            lo = b if row == 0 else b - 8
            parts += [v[at:lo], v[lo:lo + 8] * keep]
            at = lo + 8
        return jnp.concatenate(parts + [v[at:]], axis=0)

    def cols(ref, c):
        return (ref[:, c * FFN_CHUNK:(c + 1) * FFN_CHUNK],
                ref[:, D_FF + c * FFN_CHUNK:D_FF + (c + 1) * FFN_CHUNK])

    def up_proj(c, pad_ref):
        hb = h_ref[...]
        wg, wv = cols(wup_ref, c)
        pad_ref[CONV_PAD:CONV_PAD + ROW_TILE, :FFN_CHUNK] = jnp.dot(hb, wg, preferred_element_type=F32)
        pad_ref[CONV_PAD:CONV_PAD + ROW_TILE, FFN_CHUNK:] = jnp.dot(hb, wv, preferred_element_type=F32)

    def conv_act(c, pad_ref):
        up = pad_ref[CONV_PAD:CONV_PAD + ROW_TILE, :]
        prev = cut_sequences(pad_ref[CONV_PAD - 1:CONV_PAD - 1 + ROW_TILE, :], keep_first, 0)
        nxt = cut_sequences(pad_ref[CONV_PAD + 1:CONV_PAD + 1 + ROW_TILE, :], keep_last, 7)
        cw = jnp.concatenate(cols(cw_ref, c), axis=-1)
        cb = jnp.concatenate(cols(cb_ref, c), axis=-1)
        conv = prev * cw[0:1, :] + up * cw[1:2, :] + nxt * cw[2:3, :] + cb
        gate = conv[:, :FFN_CHUNK]
        act = _silu(gate) * conv[:, FFN_CHUNK:]
        act_ref[:, c * FFN_CHUNK:(c + 1) * FFN_CHUNK] = act.astype(BF16)

    pads = (pad_a, pad_b)
    done = 0
    up_proj(0, pads[0])
    for c in range(FFN_CHUNKS):
        if c + 1 < FFN_CHUNKS:
            up_proj(c + 1, pads[(c + 1) % 2])
        conv_act(c, pads[c % 2])
        if (c + 1) % FFN_DOWN_GROUP == 0 or c + 1 == FFN_CHUNKS:
            rows = slice(done * FFN_CHUNK, (c + 1) * FFN_CHUNK)
            part = jnp.dot(act_ref[:, rows], wd_ref[rows, :], preferred_element_type=F32)
            acc = part if done == 0 else o_ref[...] + part
            if c + 1 == FFN_CHUNKS:
                acc = _read_rows(x_refs) + mod_ref[5:6, :] * acc
                if final_norm:
                    acc = acc * lax.rsqrt(jnp.mean(acc * acc, axis=-1, keepdims=True) + NORM_EPS) * fg_ref[...]
            o_ref[...] = acc
            done = c + 1


def conv_ffn_residual(x, layer, norm_g, mods, w_up, conv_w, conv_b, w_down, tiles=(0, N_ROW_TILES), final_g=None):
    split_x = isinstance(x, tuple)
    first, count = tiles
    if split_x:
        assert tiles == (0, N_ROW_TILES)
        x_ops, x_specs = _x_operands(x)
    else:
        x_ops, x_specs = (x,), [pl.BlockSpec((ROW_TILE, D), lambda i: (first + i, 0))]
    final_ops = () if final_g is None else (final_g.reshape(1, D),)
    final_specs = [] if final_g is None else [pl.BlockSpec((1, D), lambda i: (0, 0))]
    once = pl.Buffered(1)
    vmem = (4 * ROW_TILE * D * 4 + ROW_TILE * D * 2 + 2 * (ROW_TILE + 2 * CONV_PAD) * 2 * FFN_CHUNK * 4
            + ROW_TILE * D_FF * 2 + 3 * D * D_FF * 2 + 5 * ROW_TILE * 2 * FFN_CHUNK * 4)
    return pl.pallas_call(
        functools.partial(_ffn_kernel, split_x=split_x, first_tile=first, final_norm=final_g is not None),
        out_shape=jax.ShapeDtypeStruct((count * ROW_TILE, D), F32),
        grid=(count,),
        in_specs=x_specs + [
            pl.BlockSpec((None, 1, D), lambda i: (layer, 0, 0)),
            pl.BlockSpec((None, None, 6, D), lambda i: (layer, _mod_row(first + i), 0, 0)),
            pl.BlockSpec((None, D, 2 * D_FF), lambda i: (layer, 0, 0), pipeline_mode=once),
            pl.BlockSpec((None, 3, 2 * D_FF), lambda i: (layer, 0, 0), pipeline_mode=once),
            pl.BlockSpec((None, 1, 2 * D_FF), lambda i: (layer, 0, 0), pipeline_mode=once),
            pl.BlockSpec((None, D_FF, D), lambda i: (layer, 0, 0), pipeline_mode=once),
        ] + final_specs,
        out_specs=pl.BlockSpec((ROW_TILE, D), lambda i: (i, 0)),
        scratch_shapes=[pltpu.VMEM((ROW_TILE, D), BF16),
                        pltpu.VMEM((ROW_TILE + 2 * CONV_PAD, 2 * FFN_CHUNK), F32),
                        pltpu.VMEM((ROW_TILE + 2 * CONV_PAD, 2 * FFN_CHUNK), F32),
                        pltpu.VMEM((ROW_TILE, D_FF), BF16)],
        compiler_params=_cparams(("arbitrary",), vmem),
        name="conv_ffn",
    )(*x_ops, norm_g.reshape(DEPTH, 1, D), mods, w_up, conv_w, conv_b.reshape(DEPTH, 1, 2 * D_FF), w_down,
      *final_ops)


NQ = ATTN_HEADS * HEAD_DIM
NKV = ATTN_KV_HEADS * HEAD_DIM
Q_BLOCK = 128
MASKED = -1e30
LOG2E = 1.0 / math.log(2.0)
Q_PRESCALE = HEAD_DIM ** -0.5 * LOG2E


def _dot_t(a, b):
    return lax.dot_general(a.astype(BF16), b.astype(BF16), (((1,), (1,)), ((), ())),
                           preferred_element_type=F32)


def _group_rows(q):
    return jnp.concatenate([q[:, g * HEAD_DIM:(g + 1) * HEAD_DIM] for g in range(ATTN_GROUP)], axis=0)


def _sink_lanes(sink_ref, h, rows):
    return LOG2E * jnp.concatenate(
        [jnp.broadcast_to(sink_ref[0:1, ATTN_GROUP * h + g:ATTN_GROUP * h + g + 1], (1, rows))
         for g in range(ATTN_GROUP)], axis=-1)


ONES_ROWS = 16


def _values_t(v):
    ones = jnp.ones((ONES_ROWS, v.shape[0]), F32)
    out = []
    for c in range(NKV // 128):
        vt = v[:, c * 128:(c + 1) * 128].T
        out += [jnp.concatenate([vt[j * HEAD_DIM:(j + 1) * HEAD_DIM], ones], axis=0).astype(BF16)
                for j in range(128 // HEAD_DIM)]
    return out


def _softmax_pv(q4, key_sets, sink2):
    scores = []
    for k, _, bias in key_sets:
        s = _dot_t(k, q4)
        if bias is not None:
            s = jnp.concatenate([s[c * 128:(c + 1) * 128] if bc is None else s[c * 128:(c + 1) * 128] + bc
                                 for c, bc in enumerate(bias)], axis=0)
        scores.append(s)
    m = sink2
    for s in scores:
        m = jnp.maximum(m, jnp.max(s, axis=0, keepdims=True))
    acc = None
    for s, (_, v1t, _) in zip(scores, key_sets):
        t = jnp.dot(v1t, jnp.exp2(s - m).astype(BF16), preferred_element_type=F32)
        acc = t if acc is None else acc + t
    denom = acc[HEAD_DIM:HEAD_DIM + 1] + jnp.exp2(sink2 - m)
    return acc[:HEAD_DIM] * (1.0 / denom)


def _heads_to_columns(o_t, rows):
    slabs = []
    for g in range(0, ATTN_GROUP, 128 // HEAD_DIM):
        pair = jnp.concatenate([o_t[:, (g + j) * rows:(g + j + 1) * rows] for j in range(128 // HEAD_DIM)], axis=0)
        slabs.append(pair.T)
    return jnp.concatenate(slabs, axis=-1)


def _ctx_attn_kernel(qkv_ref, sink_ref, o_ref):
    v1t = _values_t(qkv_ref[:, NQ + NKV:])
    outs = []
    for h in range(ATTN_KV_HEADS):
        k = qkv_ref[:, NQ + h * HEAD_DIM:NQ + (h + 1) * HEAD_DIM].astype(BF16)
        q4 = _group_rows(qkv_ref[:, ATTN_GROUP * h * HEAD_DIM:ATTN_GROUP * (h + 1) * HEAD_DIM] * Q_PRESCALE)
        o_t = _softmax_pv(q4.astype(BF16), [(k, v1t[h], None)], _sink_lanes(sink_ref, h, SEQ))
        outs.append(_heads_to_columns(o_t, SEQ))
    o_ref[...] = jnp.concatenate(outs, axis=-1).astype(BF16)


def context_attention(qkv, sink):
    return pl.pallas_call(
        _ctx_attn_kernel,
        out_shape=jax.ShapeDtypeStruct((ROWS_P, NQ), BF16),
        grid=(BATCH,),
        in_specs=[pl.BlockSpec((SEQ, NQ + 2 * NKV), lambda b: (b, 0)),
                  pl.BlockSpec((1, ATTN_HEADS), lambda b: (0, 0))],
        out_specs=pl.BlockSpec((SEQ, NQ), lambda b: (b, 0)),
        compiler_params=_cparams(("arbitrary",), 2 * SEQ * (2 * NQ + 2 * NKV) * 4 + 24 * SEQ * ATTN_GROUP * SEQ * 4),
        name="context_attention",
    )(qkv, sink.reshape(1, ATTN_HEADS))


def _rope(x, cos, sin_a, sin_b):
    outs = []
    for c in range(x.shape[1] // 128):
        s = x[:, c * 128:(c + 1) * 128]
        outs.append(s * cos + pltpu.roll(s, 128 - HEAD_DIM // 4, 1) * sin_a + pltpu.roll(s, HEAD_DIM // 4, 1) * sin_b)
    return jnp.concatenate(outs, axis=-1)


Q_TILE = 2 * Q_BLOCK


def _lat_attn_kernel(q_ref, kp_ref, km_ref, kn_ref, vp_ref, vm_ref, vn_ref, ck_ref, cv_ref,
                     cos_ref, sa_ref, sb_ref, sink_ref, o_ref):
    m = pl.program_id(1)
    nm = pl.num_programs(1)
    nb = 2 * nm

    def tables(blk, rows):
        r = pl.ds(pl.multiple_of(blk * Q_BLOCK, Q_BLOCK), rows)
        return cos_ref[r, :], sa_ref[r, :], sb_ref[r, :]

    mid = tables(2 * m, Q_TILE)
    qr = (_rope(q_ref[...], *mid) * Q_PRESCALE).astype(BF16)
    k4 = jnp.concatenate([
        _rope(kp_ref[...], *tables(jnp.maximum(2 * m - 1, 0), Q_BLOCK)),
        _rope(km_ref[...], *mid),
        _rope(kn_ref[...], *tables(jnp.minimum(2 * m + 2, nb - 1), Q_BLOCK))], axis=0).astype(BF16)
    v4 = jnp.concatenate([vp_ref[...], vm_ref[...], vn_ref[...]], axis=0)

    cols = ATTN_GROUP * Q_TILE
    koff = lax.broadcasted_iota(jnp.int32, (Q_BLOCK, cols), 0)
    lane = lax.broadcasted_iota(jnp.int32, (Q_BLOCK, cols), 1)
    qoff = lane & (Q_BLOCK - 1)
    second = (lane & Q_BLOCK) != 0
    lower, upper = koff >= qoff, koff <= qoff
    bias = [jnp.where(jnp.logical_not(second) & lower & (m > 0), 0.0, MASKED),
            jnp.where(jnp.logical_not(second) | lower, 0.0, MASKED),
            jnp.where(second | upper, 0.0, MASKED),
            jnp.where(second & upper & (m < nm - 1), 0.0, MASKED)]

    v1t = _values_t(v4)
    cv1t = _values_t(cv_ref[...])
    outs = []
    for h in range(ATTN_KV_HEADS):
        hs = slice(h * HEAD_DIM, (h + 1) * HEAD_DIM)
        q4 = _group_rows(qr[:, ATTN_GROUP * h * HEAD_DIM:ATTN_GROUP * (h + 1) * HEAD_DIM])
        o_t = _softmax_pv(q4, [(k4[:, hs], v1t[h], bias), (ck_ref[:, hs].astype(BF16), cv1t[h], None)],
                          _sink_lanes(sink_ref, h, Q_TILE))
        outs.append(_heads_to_columns(o_t, Q_TILE))
    o_ref[...] = jnp.concatenate(outs, axis=-1).astype(BF16)


def _rope_tables():
    t = np.arange(DEC_SEQ)
    half = HEAD_DIM // 2
    inv_freq = 1.0 / (ROPE_BASE ** (np.arange(0, half, 2, dtype=np.float32) / half))
    ar = (t // GRID_W).astype(np.float32)[:, None] * inv_freq
    ac = (t % GRID_W).astype(np.float32)[:, None] * inv_freq
    return jnp.concatenate([jnp.asarray(a) for a in (ar, ar, ac, ac)] * 2, axis=-1)


def latent_attention(qkv, cache_k, cache_v, layer_j, sink):
    ang = _rope_tables()
    cos, sin = jnp.cos(ang), jnp.sin(ang)
    first = (lax.broadcasted_iota(jnp.int32, ang.shape, 1) % (HEAD_DIM // 2)) < HEAD_DIM // 4
    sin_a = jnp.where(first, -sin, 0.0)
    sin_b = jnp.where(first, 0.0, sin)
    nb = DEC_SEQ // Q_BLOCK
    nm = DEC_SEQ // Q_TILE
    base = ROWS_P // Q_BLOCK
    base_t = ROWS_P // Q_TILE
    kcol, vcol = NQ // NKV, NQ // NKV + 1
    ck = cache_k.reshape(DEC_BATCH, -1, PAST_LEN, NKV)
    cv = cache_v.reshape(DEC_BATCH, -1, PAST_LEN, NKV)

    def edge_spec(col, blk):
        return pl.BlockSpec((Q_BLOCK, NKV), lambda b, m: (base + b * nb + jnp.clip(blk(m), 0, nb - 1), col))

    def mid_spec(col):
        return pl.BlockSpec((Q_TILE, NKV), lambda b, m: (base_t + b * nm + m, col))

    prev_blk, next_blk = (lambda m: 2 * m - 1), (lambda m: 2 * m + 2)
    table = pl.BlockSpec((DEC_SEQ, 128), lambda b, m: (0, 0))
    return pl.pallas_call(
        _lat_attn_kernel,
        out_shape=jax.ShapeDtypeStruct((ROWS_S, NQ), BF16),
        grid=(DEC_BATCH, nm),
        in_specs=[pl.BlockSpec((Q_TILE, NQ), lambda b, m: (base_t + b * nm + m, 0)),
                  edge_spec(kcol, prev_blk), mid_spec(kcol), edge_spec(kcol, next_blk),
                  edge_spec(vcol, prev_blk), mid_spec(vcol), edge_spec(vcol, next_blk),
                  pl.BlockSpec((None, None, PAST_LEN, NKV), lambda b, m: (b, layer_j, 0, 0)),
                  pl.BlockSpec((None, None, PAST_LEN, NKV), lambda b, m: (b, layer_j, 0, 0)),
                  table, table, table,
                  pl.BlockSpec((1, ATTN_HEADS), lambda b, m: (0, 0))],
        out_specs=pl.BlockSpec((Q_TILE, NQ), lambda b, m: (b * nm + m, 0)),
        compiler_params=_cparams(("arbitrary", "arbitrary"),
                                 4 * Q_TILE * NQ * 4 + 12 * Q_TILE * NKV * 4 + 4 * PAST_LEN * NKV * 4
                                 + 6 * DEC_SEQ * 128 * 4 + 16 * ATTN_GROUP * Q_TILE * (4 * Q_BLOCK + PAST_LEN) * 4),
        name="latent_attention",
    )(qkv, qkv, qkv, qkv, qkv, qkv, qkv, ck, cv, cos, sin_a, sin_b, sink.reshape(1, ATTN_HEADS))


def attention_layer(x, layer, layer_j, mods, norm_g, wqkv, wo, sink, cache_k, cache_v):
    qkv = norm_mod_matmul(x, layer, norm_g, mods, wqkv, layer_j, "attn_qkv")
    a_p = context_attention(qkv, sink)
    a_s = latent_attention(qkv, cache_k, cache_v, layer_j, sink)
    x = matmul_gated_residual(a_p, a_s, wo, layer_j, x, layer, mods, "attn_wo")
    return x, qkv


def _cache_out_kernel(*refs):
    n = (len(refs) - 2) // 2
    k_refs, v_refs, (ok_ref, ov_ref) = refs[:n], refs[n:2 * n], refs[2 * n:]
    for j in range(n):
        @pl.when(pl.program_id(1) == j)
        def _(j=j):
            for h in range(ATTN_KV_HEADS):
                hs = slice(h * HEAD_DIM, (h + 1) * HEAD_DIM)
                ok_ref[:, h, :] = k_refs[j][:, hs]
                ov_ref[:, h, :] = v_refs[j][:, hs]


def new_context_cache(qkvs):
    n = len(qkvs)
    kcol, vcol = NQ // NKV, NQ // NKV + 1
    spec = lambda col: pl.BlockSpec((SEQ, NKV), lambda b, j: (b, col))
    out = jax.ShapeDtypeStruct((BATCH, n, SEQ, ATTN_KV_HEADS, HEAD_DIM), F32)
    out_spec = pl.BlockSpec((None, None, SEQ, ATTN_KV_HEADS, HEAD_DIM), lambda b, j: (b, j, 0, 0, 0))
    return pl.pallas_call(
        _cache_out_kernel,
        out_shape=(out, out),
        grid=(BATCH, n),
        in_specs=[spec(kcol)] * n + [spec(vcol)] * n,
        out_specs=(out_spec, out_spec),
        compiler_params=_cparams(("arbitrary", "arbitrary"), 8 * n * SEQ * NKV * 4 + 4 * SEQ * 8 * 128 * 4),
        name="new_context_cache",
    )(*qkvs, *qkvs)


HG_TILE = 256
HG_LEVELS = 8
HG_IN = 3 * 1024 + 2 * 1024
HG_OUT_TILE = 512
HG_HEADS_PER_STEP = 8


def _hgrn_consts():
    t = np.arange(HG_TILE)
    x = t[:, None] ^ t[None, :]
    hb = np.where(x == 0, -1, np.floor(np.log2(np.maximum(x, 1))).astype(np.int64))
    later = t[:, None] > t[None, :]
    half = HG_TILE // 2
    masks, tris = [], []
    for reverse in (False, True):
        side = ~later & (x != 0) if reverse else later
        lv = [hb == -1] + [(hb == lvl) & side for lvl in range(HG_LEVELS - 1)]
        masks.append(np.stack([m[:half, :half] for m in lv]).astype(np.float32))
        tris.append((t[None, :] >= t[:, None]) if reverse else (t[None, :] <= t[:, None]))
    return jnp.asarray(np.stack(masks)), jnp.asarray(np.stack(tris).astype(np.float32), dtype=BF16)


def _split_bf16(x):
    def top(v):
        bits = lax.bitcast_convert_type(v, jnp.uint32) & jnp.uint32(0xFFFF0000)
        return lax.bitcast_convert_type(bits, F32)

    hi = top(x)
    r = x - hi
    mid = top(r)
    return hi.astype(BF16), mid.astype(BF16), (r - mid).astype(BF16)


def _block_row(x, blk, idx):
    t = x.shape[0]
    x3 = x.reshape(t // blk, blk, x.shape[1])
    return jnp.broadcast_to(x3[:, idx:idx + 1, :], x3.shape).reshape(x.shape)


def _lower_bound(lb_ref, layer, direction):
    x = lb_ref[direction]
    e = jnp.exp(x - jnp.max(x, axis=0, keepdims=True))
    p = e / jnp.sum(e, axis=0, keepdims=True)
    return jnp.sum(p[1:layer + 1, :], axis=0, keepdims=True)


def _hgrn_tile(q, v, z, lb, s_in, mask_ref, tri, reverse):
    t = HG_TILE
    lo, hi = slice(0, t // 2), slice(t // 2, t)
    sg = _sigmoid(z)
    f = lb + (1.0 - lb) * sg
    k = (1.0 - lb) * (1.0 - sg)
    lf3 = _split_bf16(jnp.log(f))
    cum = sum(jnp.dot(tri, p, preferred_element_type=F32) for p in lf3)
    cum2 = cum * LOG2E
    rows = lax.broadcasted_iota(jnp.int32, (t, 1), 0)
    att = [mask_ref[0] * _dot_t(q[r], k[r]) for r in (lo, hi)]
    top = None
    for lvl in range(HG_LEVELS):
        half = 1 << lvl
        bit = (rows & half) != 0
        qside = ~bit if reverse else bit
        if lvl == 0:
            e = jnp.where(qside, f, 1.0)
        else:
            ref = _block_row(cum2, 2 * half, half if reverse else half - 1)
            e = jnp.exp2(-jnp.abs(cum2 - ref))
        w = (jnp.where(qside, q, k) * e).astype(BF16)
        if lvl < HG_LEVELS - 1:
            att = [a + mask_ref[lvl + 1] * _dot_t(w[r], w[r]) for a, r in zip(att, (lo, hi))]
        else:
            top = _dot_t(w[lo], w[hi]) if reverse else _dot_t(w[hi], w[lo])
    vb = v.astype(BF16)
    o_lo, o_hi = _bdot(att[0], vb[lo]), _bdot(att[1], vb[hi])
    if reverse:
        o_lo = o_lo + _bdot(top, vb[hi])
    else:
        o_hi = o_hi + _bdot(top, vb[lo])
    o = jnp.concatenate([o_lo, o_hi], axis=0)
    last = cum[0:1, :] if reverse else cum[t - 1:t, :]
    kd = (k * jnp.exp(last - cum)).astype(BF16)
    s_out = lax.dot_general(kd, vb, (((0,), (0,)), ((), ())), preferred_element_type=F32)
    if s_in is not None:
        o = o + _bdot(q * jnp.exp(cum), s_in)
        ones = jnp.ones((t, HGRN_DV), BF16)
        last_col = sum(lax.dot_general(p, ones, (((0,), (0,)), ((), ())), preferred_element_type=F32) for p in lf3)
        s_out = jnp.exp(last_col) * s_in + s_out
    return o, s_out


def _hgrn_prompt_kernel(q_ref, v_ref, zf_ref, zb_ref, lb_ref, mask_ref, tri_ref, of_ref, ob_ref, sfin_ref, *, layer):
    for j in range(HG_HEADS_PER_STEP):
        hs = slice(j * HGRN_DK, (j + 1) * HGRN_DK)
        q, v = q_ref[:, hs], v_ref[:, hs]
        lb_f, lb_b = (_lower_bound(lb_ref.at[:, :, hs], layer, d) for d in range(2))
        of_ref[:, hs], sfin_ref[0, j] = _hgrn_tile(q, v, zf_ref[:, hs], lb_f, None, mask_ref.at[0], tri_ref[0], False)
        ob_ref[:, hs], sfin_ref[1, j] = _hgrn_tile(q, v, zb_ref[:, hs], lb_b, None, mask_ref.at[1], tri_ref[1], True)


def _hgrn_sample_kernel(qf_ref, vf_ref, zf_ref, qb_ref, vb_ref, zb_ref, lb_ref, s0_ref, mask_ref, tri_ref,
                        of_ref, ob_ref, state_ref, *, layer):
    tiles = DEC_SEQ // HG_TILE

    @pl.when(pl.program_id(1) % tiles == 0)
    def _():
        state_ref[...] = s0_ref[...]

    for j in range(HG_HEADS_PER_STEP):
        hs = slice(j * HGRN_DK, (j + 1) * HGRN_DK)
        lb_f, lb_b = (_lower_bound(lb_ref.at[:, :, hs], layer, d) for d in range(2))
        of_ref[:, hs], state_ref[0, j] = _hgrn_tile(qf_ref[:, hs], vf_ref[:, hs], zf_ref[:, hs], lb_f,
                                                    state_ref[0, j], mask_ref.at[0], tri_ref[0], False)
        ob_ref[:, hs], state_ref[1, j] = _hgrn_tile(qb_ref[:, hs], vb_ref[:, hs], zb_ref[:, hs], lb_b,
                                                    state_ref[1, j], mask_ref.at[1], tri_ref[1], True)


def hgrn_scan(proj, lb_raw, layer, state_hgrn, layer_j):
    masks, tris = _hgrn_consts()
    hps = HG_HEADS_PER_STEP
    hw = hps * HGRN_DK
    qc, vc, zfc, zbc = 0, 1024 // hw, 2048 // hw, 3072 // hw
    const_specs = [pl.BlockSpec((2, DEPTH, hw), lambda h, i: (0, 0, h))]
    mask_specs = [pl.BlockSpec(masks.shape, lambda h, i: (0, 0, 0, 0)),
                  pl.BlockSpec((2, HG_TILE, HG_TILE), lambda h, i: (0, 0, 0))]
    vmem = 4 * masks.size * 4 + 16 * HG_TILE * hw * 4 + 8 * hw * HGRN_DV * 4 + 24 * hps * HG_TILE * HG_TILE * 4

    def col(cb, row_fn):
        return pl.BlockSpec((HG_TILE, hw), lambda h, i: (row_fn(i), cb + h))

    o_shape = jax.ShapeDtypeStruct((ROWS_P, HGRN_HEADS * HGRN_DV), F32)
    same = lambda i: i
    of_p, ob_p, sfin = pl.pallas_call(
        functools.partial(_hgrn_prompt_kernel, layer=layer),
        out_shape=(o_shape, o_shape, jax.ShapeDtypeStruct((BATCH, 2, HGRN_HEADS, HGRN_DK, HGRN_DV), F32)),
        grid=(HGRN_HEADS // hps, BATCH),
        in_specs=[col(qc, same), col(vc, same), col(zfc, same), col(zbc, same)] + const_specs + mask_specs,
        out_specs=(col(0, same), col(0, same),
                   pl.BlockSpec((None, 2, hps, HGRN_DK, HGRN_DV), lambda h, i: (i, 0, h, 0, 0))),
        compiler_params=_cparams(("arbitrary", "arbitrary"), vmem),
        name="hgrn_scan_prompt",
    )(proj, proj, proj, proj, lb_raw, masks, tris)

    tiles = DEC_SEQ // HG_TILE
    base = ROWS_P // HG_TILE
    bwd = lambda i: (i // tiles) * tiles + (tiles - 1 - i % tiles)
    fwd_in = lambda i: base + i
    bwd_in = lambda i: base + bwd(i)
    o_shape = jax.ShapeDtypeStruct((ROWS_S, HGRN_HEADS * HGRN_DV), F32)
    of_s, ob_s = pl.pallas_call(
        functools.partial(_hgrn_sample_kernel, layer=layer),
        out_shape=(o_shape, o_shape),
        grid=(HGRN_HEADS // hps, DEC_BATCH * tiles),
        in_specs=[col(qc, fwd_in), col(vc, fwd_in), col(zfc, fwd_in), col(qc, bwd_in), col(vc, bwd_in),
                  col(zbc, bwd_in)]
        + const_specs
        + [pl.BlockSpec((None, None, 2, hps, HGRN_DK, HGRN_DV), lambda h, i: (i // tiles, layer_j, 0, h, 0, 0))]
        + mask_specs,
        out_specs=(col(0, same), col(0, bwd)),
        scratch_shapes=[pltpu.VMEM((2, hps, HGRN_DK, HGRN_DV), F32)],
        compiler_params=_cparams(("arbitrary", "arbitrary"), vmem),
        name="hgrn_scan_sample",
    )(proj, proj, proj, proj, proj, proj, lb_raw, state_hgrn, masks, tris)
    return (of_p, ob_p), (of_s, ob_s), sfin


def _hgrn_out_kernel(ofp_ref, obp_ref, ofs_ref, obs_ref, g_ref, gn_ref, w_ref, x_ref, mod_ref, o_ref):
    def run(of_ref, ob_ref):
        gn = gn_ref[...]
        parts = []
        for h in range(HGRN_HEADS):
            hs = slice(h * HGRN_DV, (h + 1) * HGRN_DV)
            o = of_ref[:, hs] + ob_ref[:, hs]
            g = g_ref[:, hs]
            o = o * lax.rsqrt(jnp.mean(o * o, axis=-1, keepdims=True) + NORM_EPS) * gn * _silu(g)
            parts.append(o.astype(BF16))
        y = jnp.dot(jnp.concatenate(parts, axis=-1), w_ref[...], preferred_element_type=F32)
        o_ref[...] = x_ref[...] + mod_ref[2:3, :] * y

    is_prompt = pl.program_id(0) < ROWS_P // HG_OUT_TILE
    pl.when(is_prompt)(lambda: run(ofp_ref, obp_ref))
    pl.when(jnp.logical_not(is_prompt))(lambda: run(ofs_ref, obs_ref))


def hgrn_out(o_p, o_s, proj, g_norm, wo, w_idx, x, layer, mods):
    gcol = 4096 // D
    tile = HG_OUT_TILE
    p_spec, s_spec = _group_specs(D, tile)
    return pl.pallas_call(
        _hgrn_out_kernel,
        out_shape=jax.ShapeDtypeStruct((ROWS, D), F32),
        grid=(ROWS // tile,),
        in_specs=[p_spec, p_spec, s_spec, s_spec,
                  pl.BlockSpec((tile, D), lambda i: (i, gcol)),
                  pl.BlockSpec((1, HGRN_DV), lambda i: (0, 0)),
                  pl.BlockSpec((None, D, D), lambda i: (w_idx, 0, 0), pipeline_mode=pl.Buffered(1)),
                  pl.BlockSpec((tile, D), lambda i: (i, 0)),
                  _mod_spec(layer, 1, tile)],
        out_specs=pl.BlockSpec((tile, D), lambda i: (i, 0)),
        compiler_params=_cparams(("arbitrary",), 22 * tile * D * 4),
        name="hgrn_out",
    )(*o_p, *o_s, proj, g_norm.reshape(1, HGRN_DV), wo, x, mods)


def hgrn_layer(x, layer, layer_j, mods, norm_g, w_in, hgrn_lb, g_norm, wo, state_hgrn):
    proj = norm_mod_matmul(x, layer, norm_g, mods, w_in, layer_j, "hgrn_in")
    o_p, o_s, sfin = hgrn_scan(proj, jnp.transpose(hgrn_lb, (1, 0, 2)), layer, state_hgrn, layer_j)
    x = hgrn_out(o_p, o_s, proj, g_norm[layer_j], wo, layer_j, x, layer, mods)
    return x, sfin


SSM_N = SSM_GROUPS * SSM_STATE
SSM_KT = 8
SSM_ROWS = 256


def _ssm_prep_kernel(are_ref, aim_ref, ldt_ref, bre_ref, bim_ref, cre_ref, cim_ref,
                     lre_ref, lim_ref, bm_ref, cm_ref, bm_acc, cm_acc):
    a_re = jnp.minimum(are_ref[...], -1e-4)
    a_im = aim_ref[...]
    dt = jnp.exp(ldt_ref[...])
    mag = jnp.exp(a_re * dt)
    l_re = mag * jnp.cos(a_im * dt)
    l_im = mag * jnp.sin(a_im * dt)
    lre_ref[...] = l_re
    lim_ref[...] = l_im
    den = a_re * a_re + a_im * a_im
    c_re = ((l_re - 1.0) * a_re + l_im * a_im) / den
    c_im = (l_im * a_re - (l_re - 1.0) * a_im) / den
    b_re, b_im = bre_ref[...], bim_ref[...]
    bb_re = c_re[:, None, :] * b_re - c_im[:, None, :] * b_im
    bb_im = c_re[:, None, :] * b_im + c_im[:, None, :] * b_re
    bm_acc[...] = jnp.zeros_like(bm_acc)
    cm_acc[...] = jnp.zeros_like(cm_acc)
    kw = SSM_KT * SSM_STATE
    for dg in range(2 * SSM_GROUPS):
        d, g = divmod(dg, SSM_GROUPS)
        k, gl = divmod(g, SSM_KT)
        ch = slice(gl * SSM_GROUP, (gl + 1) * SSM_GROUP)
        st = slice(gl * SSM_STATE, (gl + 1) * SSM_STATE)
        st_im = slice(kw + gl * SSM_STATE, kw + (gl + 1) * SSM_STATE)
        bm_acc[d, k, ch, st] = bb_re[dg]
        bm_acc[d, k, ch, st_im] = bb_im[dg]
        cm_acc[d, k, st, ch] = cre_ref[dg]
        cm_acc[d, k, st_im, ch] = -cim_ref[dg]
    bm_ref[...] = bm_acc[...].astype(BF16)
    cm_ref[...] = cm_acc[...].astype(BF16)


def ssm_discretize(a_re, a_im, log_dt, b_re, b_im, c_re, c_im):
    g2 = 2 * SSM_GROUPS
    nk = SSM_GROUPS // SSM_KT
    kw = SSM_KT * SSM_STATE
    sh = jax.ShapeDtypeStruct((g2, SSM_STATE), F32)
    bm_shape = (2, nk, SSM_KT * SSM_GROUP, 2 * kw)
    cm_shape = (2, nk, 2 * kw, SSM_KT * SSM_GROUP)
    bt = lambda b: jnp.transpose(b, (0, 1, 3, 2)).reshape(g2, SSM_GROUP, SSM_STATE)
    ct = lambda c: jnp.transpose(c, (0, 1, 3, 2)).reshape(g2, SSM_STATE, SSM_GROUP)
    l_re, l_im, b_mat, c_mat = pl.pallas_call(
        _ssm_prep_kernel,
        out_shape=(sh, sh, jax.ShapeDtypeStruct(bm_shape, BF16), jax.ShapeDtypeStruct(cm_shape, BF16)),
        scratch_shapes=[pltpu.VMEM(bm_shape, F32), pltpu.VMEM(cm_shape, F32)],
        compiler_params=pltpu.CompilerParams(vmem_limit_bytes=40 << 20),
        name="ssm_discretize",
    )(a_re.reshape(g2, SSM_STATE), a_im.reshape(g2, SSM_STATE), log_dt.reshape(g2, 1), bt(b_re), bt(b_im),
      ct(c_re), ct(c_im))
    lam = jnp.stack([l_re.reshape(2, SSM_N), l_im.reshape(2, SSM_N)], axis=1)
    return b_mat, c_mat, lam


X4_SHAPE = (ROWS // (4 * SEQ), 4, SEQ, D)


def _tm_geometry(prompt):
    if prompt:
        batch = BATCH
        steps = SSM_ROWS // batch
        return batch, steps, (4, 4, steps, D), (lambda i: (0, 0, i, 0)), SEQ // steps, (0,) * batch
    batch = DEC_BATCH
    steps = SSM_ROWS // batch
    per_q = SEQ // steps
    return (batch, steps, (4, 1, steps, D), (lambda i: (1, i // per_q, i % per_q, 0)), DEC_SEQ // steps,
            tuple(range(1, 1 + batch)))


def _x4_seq(ref, b):
    return ref.at[b // ref.shape[1], b % ref.shape[1]]


LANE_SLABS = D // 128


def _slab_store(s_ref, rows, val):
    for c in range(LANE_SLABS):
        s_ref[c, rows, :] = val[:, c * 128:(c + 1) * 128]


def _slab_load(s_ref, rows):
    return jnp.concatenate([s_ref[c, rows, :] for c in range(LANE_SLABS)], axis=-1)


def _normmod_tm_kernel(x_ref, g_ref, mod_ref, o_ref, s_ref, *, batch, steps, mod_rows):
    g = g_ref[...]
    for b in range(batch):
        m = mod_ref.at[mod_rows[b]]
        _slab_store(s_ref, slice(b * steps, (b + 1) * steps),
                    _norm_mod(_x4_seq(x_ref, b)[...], g, m[0:1, :], m[1:2, :]))
    for t in range(steps):
        o_ref[t * batch:(t + 1) * batch, :] = _slab_load(s_ref, pl.ds(t, batch, stride=steps))


def norm_mod_time_major(x4, layer, norm_g, mods, prompt):
    batch, steps, blk, idx, tiles, mod_rows = _tm_geometry(prompt)
    return pl.pallas_call(
        functools.partial(_normmod_tm_kernel, batch=batch, steps=steps, mod_rows=mod_rows),
        out_shape=jax.ShapeDtypeStruct((tiles * SSM_ROWS, D), F32),
        grid=(tiles,),
        in_specs=[pl.BlockSpec(blk, idx),
                  pl.BlockSpec((None, 1, D), lambda i: (layer, 0, 0)),
                  pl.BlockSpec((None, MOD_ROWS, 6, D), lambda i: (layer, 0, 0, 0))],
        out_specs=pl.BlockSpec((SSM_ROWS, D), lambda i: (i, 0)),
        scratch_shapes=[pltpu.VMEM((LANE_SLABS, SSM_ROWS, 128), F32)],
        compiler_params=_cparams(("arbitrary",), 8 * SSM_ROWS * D * 4),
        name="ssm_norm_time_major",
    )(x4, norm_g.reshape(DEPTH, 1, D), mods)


def _ssm_scan_kernel(xf_ref, xb_ref, bm_ref, cm_ref, lam_ref, h0_ref, yf_ref, yb_ref, hfin_ref,
                     hre_f, him_f, hre_b, him_b, st_ref, *, batch):
    i = pl.program_id(0)
    steps = SSM_ROWS // batch
    nk = SSM_GROUPS // SSM_KT
    kw = SSM_KT * SSM_STATE
    x_refs, y_refs = (xf_ref, xb_ref), (yf_ref, yb_ref)
    h_refs = ((hre_f, him_f), (hre_b, him_b))

    @pl.when(i == 0)
    def _():
        st_ref[...] = h0_ref[...]

    def tile_cols(k):
        return slice(k * kw, (k + 1) * kw)

    def bu_tile(d, k):
        xk = x_refs[d][:, k * 128:(k + 1) * 128].astype(BF16)
        bu = jnp.dot(xk, bm_ref[d, k], preferred_element_type=F32)
        h_refs[d][0][:, tile_cols(k)] = bu[:, :kw]
        h_refs[d][1][:, tile_cols(k)] = bu[:, kw:]

    def c_tile(d, k):
        hk = jnp.concatenate([h_refs[d][0][:, tile_cols(k)], h_refs[d][1][:, tile_cols(k)]], axis=-1)
        y_refs[d][:, k * 128:(k + 1) * 128] = jnp.dot(hk.astype(BF16), cm_ref[d, k], preferred_element_type=F32)

    def scan_tile(d, k):
        hre_ref, him_ref = h_refs[d]
        col = tile_cols(k)
        l_re, l_im = lam_ref[d, 0, :, col], lam_ref[d, 1, :, col]
        h_re, h_im = st_ref[d, 0, :, col], st_ref[d, 1, :, col]
        per = max(8 // batch, 1)
        rows_per = per * batch
        for s in range(steps // per):
            g = (steps // per - 1 - s) if d else s
            rows = slice(g * rows_per, (g + 1) * rows_per)
            cur_re, cur_im = hre_ref[rows, col], him_ref[rows, col]
            outs_re, outs_im = [None] * per, [None] * per
            for r in (range(per - 1, -1, -1) if d else range(per)):
                b_re, b_im = cur_re[r * batch:(r + 1) * batch], cur_im[r * batch:(r + 1) * batch]
                h_re, h_im = l_re * h_re - l_im * h_im + b_re, l_re * h_im + l_im * h_re + b_im
                outs_re[r], outs_im[r] = h_re, h_im
            hre_ref[rows, col] = outs_re[0] if per == 1 else jnp.concatenate(outs_re, axis=0)
            him_ref[rows, col] = outs_im[0] if per == 1 else jnp.concatenate(outs_im, axis=0)
        st_ref[d, 0, :, col] = h_re
        st_ref[d, 1, :, col] = h_im

    for k in range(nk + 2):
        for d in range(2):
            if k < nk:
                bu_tile(d, k)
            if 1 <= k <= nk:
                scan_tile(d, k - 1)
            if k >= 2:
                c_tile(d, k - 2)

    @pl.when(i == pl.num_programs(0) - 1)
    def _():
        hfin_ref[...] = st_ref[...]


def ssm_scan(xn_tm, b_mat, c_mat, lam, h0, batch):
    rows = xn_tm.shape[0]
    n = rows // SSM_ROWS
    lam_b = jnp.broadcast_to(lam[:, :, None, :], (2, 2, batch, SSM_N))
    y_shape = jax.ShapeDtypeStruct((rows, D), F32)
    full = lambda a: pl.BlockSpec(a.shape, lambda i: (0,) * a.ndim)
    vmem = (4 * SSM_ROWS * SSM_N * 4 + 8 * SSM_ROWS * D * 4 + 2 * (b_mat.size + c_mat.size) * 2
            + 12 * batch * SSM_N * 4 * 2 + 8 * SSM_ROWS * 1024 * 4)
    return pl.pallas_call(
        functools.partial(_ssm_scan_kernel, batch=batch),
        out_shape=(y_shape, y_shape, jax.ShapeDtypeStruct((2, 2, batch, SSM_N), F32)),
        grid=(n,),
        in_specs=[pl.BlockSpec((SSM_ROWS, D), lambda i: (i, 0)),
                  pl.BlockSpec((SSM_ROWS, D), lambda i: (n - 1 - i, 0)),
                  full(b_mat), full(c_mat), full(lam_b), full(h0)],
        out_specs=(pl.BlockSpec((SSM_ROWS, D), lambda i: (i, 0)),
                   pl.BlockSpec((SSM_ROWS, D), lambda i: (n - 1 - i, 0)),
                   pl.BlockSpec((2, 2, batch, SSM_N), lambda i: (0, 0, 0, 0))),
        scratch_shapes=[pltpu.VMEM((SSM_ROWS, SSM_N), F32)] * 4
        + [pltpu.VMEM((2, 2, batch, SSM_N), F32)],
        compiler_params=_cparams(("arbitrary",), vmem),
        name="ssm_scan",
    )(xn_tm, xn_tm, b_mat, c_mat, lam_b, h0)


def _gelu_tanh(x):
    return 0.5 * x * (1.0 + jnp.tanh(math.sqrt(2.0 / math.pi) * (x + 0.044715 * (x * x * x))))


def _ssm_glu_kernel(yf_ref, yb_ref, xn_ref, d_ref, w_ref, x_ref, mod_ref, o_ref, s_ref,
                    *, batch, steps, mod_rows):
    g = _gelu_tanh(yf_ref[...] + yb_ref[...] + d_ref[...] * xn_ref[...])
    u = jnp.dot(g.astype(BF16), w_ref[...], preferred_element_type=F32)
    _slab_store(s_ref, slice(None), u[:, :D] * _sigmoid(u[:, D:]))
    for b in range(batch):
        gate = mod_ref[mod_rows[b], 2:3, :]
        _x4_seq(o_ref, b)[...] = (_x4_seq(x_ref, b)[...]
                                  + gate * _slab_load(s_ref, pl.ds(b, steps, stride=batch)))


def ssm_glu(yf, yb, xn, d, w_glu, w_idx, x4, layer, mods, prompt):
    batch, steps, blk, idx, tiles, mod_rows = _tm_geometry(prompt)
    tm_spec = pl.BlockSpec((SSM_ROWS, D), lambda i: (i, 0))
    out = pl.pallas_call(
        functools.partial(_ssm_glu_kernel, batch=batch, steps=steps, mod_rows=mod_rows),
        out_shape=jax.ShapeDtypeStruct((4,) + X4_SHAPE[1:], F32),
        grid=(tiles,),
        in_specs=[tm_spec, tm_spec, tm_spec,
                  pl.BlockSpec((None, 1, D), lambda i: (w_idx, 0, 0)),
                  pl.BlockSpec((None, D, 2 * D), lambda i: (w_idx, 0, 0), pipeline_mode=pl.Buffered(1)),
                  pl.BlockSpec(blk, idx),
                  pl.BlockSpec((None, MOD_ROWS, 6, D), lambda i: (layer, 0, 0, 0))],
        out_specs=pl.BlockSpec(blk, lambda i: (0,) + idx(i)[1:]),
        scratch_shapes=[pltpu.VMEM((LANE_SLABS, SSM_ROWS, 128), F32)],
        compiler_params=_cparams(("arbitrary",), 24 * SSM_ROWS * D * 4 + D * 2 * D * 2),
        name="ssm_glu",
    )(yf, yb, xn, d.reshape(-1, 1, D), w_glu, x4, mods)
    return out.reshape(-1, D)


def ssm_layer(x, layer, layer_j, mods, norm_g, a_re, a_im, log_dt, b_re, b_im, c_re, c_im, d, w_glu, state_ssm):
    b_mat, c_mat, lam = ssm_discretize(a_re[layer_j], a_im[layer_j], log_dt[layer_j], b_re[layer_j],
                                       b_im[layer_j], c_re[layer_j], c_im[layer_j])
    x4 = x.reshape(X4_SHAPE)
    xn_p = norm_mod_time_major(x4, layer, norm_g, mods, True)
    xn_s = norm_mod_time_major(x4, layer, norm_g, mods, False)
    h0_p = jnp.zeros((2, 2, BATCH, SSM_N), F32)
    h0_s = jnp.transpose(state_ssm[:, layer_j].reshape(DEC_BATCH, 2, SSM_N, 2), (1, 3, 0, 2))
    yfp, ybp, hfin = ssm_scan(xn_p, b_mat, c_mat, lam, h0_p, BATCH)
    yfs, ybs, _ = ssm_scan(xn_s, b_mat, c_mat, lam, h0_s, DEC_BATCH)
    out_p = ssm_glu(yfp, ybp, xn_p, d, w_glu, layer_j, x4, layer, mods, True)
    out_s = ssm_glu(yfs, ybs, xn_s, d, w_glu, layer_j, x4, layer, mods, False)
    new_state = jnp.transpose(hfin, (2, 0, 3, 1)).reshape(BATCH, 2, SSM_GROUPS, SSM_STATE, 2)
    return (out_p, out_s), new_state


def kernel(x_prompt, x_sample, cache_k, cache_v, state_hgrn, state_ssm, c, c_ctx, ada_w, ada_b, norm1_g, norm2_g, attn_wqkv, attn_wo, attn_sink, hgrn_w_in, hgrn_lb, hgrn_g_norm, hgrn_wo, ssm_a_re, ssm_a_im, ssm_log_dt, ssm_b_re, ssm_b_im, ssm_c_re, ssm_c_im, ssm_d, ssm_w_glu, ffn_w_up, ffn_conv_w, ffn_conv_b, ffn_w_down, final_g):
    cond8 = jnp.zeros((MOD_ROWS, D), F32).at[0].set(c_ctx).at[1:1 + DEC_BATCH].set(c)
    mods = ada_modulation(cond8, ada_w, ada_b)
    x = (x_prompt.reshape(ROWS_P, D), x_sample.reshape(ROWS_S, D))
    wqkv, wo, w_in, hwo, w_glu, w_up, w_down = (w.astype(BF16) for w in (
        attn_wqkv, attn_wo, hgrn_w_in, hgrn_wo, ssm_w_glu, ffn_w_up, ffn_w_down))
    qkvs, new_hgrn, new_ssm = [], [], []
    for l in range(DEPTH):
        kind, j = l % N_MIXERS, l // N_MIXERS
        if kind == 0:
            x, qkv = attention_layer(x, l, j, mods, norm1_g, wqkv, wo, attn_sink[j], cache_k, cache_v)
            qkvs.append(qkv)
        elif kind == 1:
            x, s = hgrn_layer(x, l, j, mods, norm1_g, w_in, hgrn_lb, hgrn_g_norm, hwo, state_hgrn)
            new_hgrn.append(s)
        else:
            x, s = ssm_layer(x, l, j, mods, norm1_g, ssm_a_re, ssm_a_im, ssm_log_dt, ssm_b_re, ssm_b_im,
                             ssm_c_re, ssm_c_im, ssm_d, w_glu, state_ssm)
            new_ssm.append(s)
        ffn = functools.partial(conv_ffn_residual, x, l, norm2_g, mods, w_up, ffn_conv_w, ffn_conv_b, w_down)
        if l + 1 < DEPTH:
            x = ffn()
    y_prompt = ffn(tiles=(0, N_ROW_TILES_P), final_g=final_g).reshape(BATCH, SEQ, D)
    y_sample = ffn(tiles=(N_ROW_TILES_P, N_ROW_TILES - N_ROW_TILES_P), final_g=final_g).reshape(DEC_BATCH, DEC_SEQ, D)
    new_k, new_v = new_context_cache(qkvs)
    return (y_prompt, y_sample, new_k, new_v, jnp.stack(new_hgrn, axis=1), jnp.stack(new_ssm, axis=1))
```

```python
import functools
import math

import jax
import jax.numpy as jnp
import numpy as np
from jax import lax
from jax.experimental import pallas as pl
from jax.experimental.pallas import tpu as pltpu

F32 = jnp.float32
BF16 = jnp.bfloat16

D = 1024
BATCH = 16
SEQ = 256
DEPTH = 4
DEC_BATCH = 4
DEC_SEQ = 1024
PAST_LEN = 512
GRID_W = 64
N_MIXERS = 3
ATTN_HEADS = 16
ATTN_KV_HEADS = 4
ATTN_GROUP = ATTN_HEADS // ATTN_KV_HEADS
HEAD_DIM = D // ATTN_HEADS
WINDOW = 128
ROPE_BASE = 10000.0
HGRN_HEADS = 8
HGRN_DK = 128
HGRN_DV = 128
SSM_GROUP = 16
SSM_GROUPS = D // SSM_GROUP
SSM_STATE = 64
D_FF = 2816
NORM_EPS = 1e-6

ROWS_P = BATCH * SEQ
ROWS_S = DEC_BATCH * DEC_SEQ
ROWS = ROWS_P + ROWS_S
ROW_TILE = 1024
N_ROW_TILES = ROWS // ROW_TILE
N_ROW_TILES_P = ROWS_P // ROW_TILE
MOD_ROWS = 8
V7X_VMEM_BYTES = 64 * 1024 * 1024


def _mod_row(i, tile=ROW_TILE):
    return jnp.where(i < ROWS_P // tile, 0, (i - ROWS_P // tile) // (DEC_SEQ // tile) + 1)


def _cparams(semantics, vmem_bytes):
    vmem = int(min(max(vmem_bytes * 5 // 4 + (4 << 20), 16 << 20), V7X_VMEM_BYTES - (6 << 20)))
    return pltpu.CompilerParams(dimension_semantics=semantics, vmem_limit_bytes=vmem)


def _bdot(a, b):
    return jnp.dot(a.astype(BF16), b.astype(BF16), preferred_element_type=F32)


def _norm_mod(x, g, shift, scale):
    y = x * lax.rsqrt(jnp.mean(x * x, axis=-1, keepdims=True) + NORM_EPS) * g
    return y * (1.0 + scale) + shift


def _sigmoid(x):
    return 0.5 + 0.5 * jnp.tanh(0.5 * x)


def _silu(x):
    h = 0.5 * x
    return h + h * jnp.tanh(h)


def _ada_kernel(c_ref, w_ref, b_ref, o_ref):
    c = c_ref[...]
    o_ref[...] = _bdot(_silu(c), w_ref[...]) + b_ref[...]


def ada_modulation(cond8, ada_w, ada_b):
    tn = 1024
    out = pl.pallas_call(
        _ada_kernel,
        out_shape=jax.ShapeDtypeStruct((DEPTH, MOD_ROWS, 6 * D), F32),
        grid=(DEPTH, 6 * D // tn),
        in_specs=[
            pl.BlockSpec((MOD_ROWS, D), lambda l, j: (0, 0)),
            pl.BlockSpec((None, D, tn), lambda l, j: (l, 0, j)),
            pl.BlockSpec((None, 1, tn), lambda l, j: (l, 0, j)),
        ],
        out_specs=pl.BlockSpec((None, MOD_ROWS, tn), lambda l, j: (l, 0, j)),
        compiler_params=_cparams(("arbitrary", "arbitrary"), 2 * D * tn * 4),
        name="ada_modulation",
    )(cond8, ada_w, ada_b.reshape(DEPTH, 1, 6 * D))
    return out.reshape(DEPTH, MOD_ROWS, 6, D)


def _x_operands(x, n_grid=1):
    if isinstance(x, tuple):
        return x, list(_group_specs(D, ROW_TILE, n_grid))
    return (x,), [pl.BlockSpec((ROW_TILE, D), (lambda i: (i, 0)) if n_grid == 1 else (lambda i, j: (i, 0)))]


def _read_rows(x_refs):
    if len(x_refs) == 1:
        return x_refs[0][...]
    return jnp.where(pl.program_id(0) < N_ROW_TILES_P, x_refs[0][...], x_refs[1][...])


def _nmm_kernel(*refs):
    x_refs, (g_ref, mod_ref, w_ref, o_ref, h_ref) = refs[:-5], refs[-5:]

    @pl.when(pl.program_id(1) == 0)
    def _():
        h_ref[...] = _norm_mod(_read_rows(x_refs), g_ref[...], mod_ref[0:1, :], mod_ref[1:2, :]).astype(BF16)

    o_ref[...] = jnp.dot(h_ref[...], w_ref[...], preferred_element_type=F32)


NMM_MAX_COLS = 2560


def _mod_spec(layer, n_grid, tile=ROW_TILE):
    if n_grid == 1:
        return pl.BlockSpec((None, None, 6, D), lambda i: (layer, _mod_row(i, tile), 0, 0))
    return pl.BlockSpec((None, None, 6, D), lambda i, j: (layer, _mod_row(i, tile), 0, 0))


def norm_mod_matmul(x, layer, norm_g, mods, w, w_idx, name):
    n = w.shape[-1]
    tn = n if n <= NMM_MAX_COLS else NMM_MAX_COLS
    assert n % tn == 0
    x_ops, x_specs = _x_operands(x, 2)
    return pl.pallas_call(
        _nmm_kernel,
        out_shape=jax.ShapeDtypeStruct((ROWS, n), F32),
        grid=(N_ROW_TILES, n // tn),
        in_specs=x_specs + [
            pl.BlockSpec((None, 1, D), lambda i, j: (layer, 0, 0)),
            _mod_spec(layer, 2),
            pl.BlockSpec((None, D, tn), lambda i, j: (w_idx, 0, j)),
        ],
        out_specs=pl.BlockSpec((ROW_TILE, tn), lambda i, j: (i, j)),
        scratch_shapes=[pltpu.VMEM((ROW_TILE, D), BF16)],
        compiler_params=_cparams(("arbitrary", "arbitrary"),
                                 4 * ROW_TILE * D * 4 + ROW_TILE * D * 2 + 2 * D * tn * 2 + 2 * ROW_TILE * tn * 4),
        name=name,
    )(*x_ops, norm_g.reshape(DEPTH, 1, D), mods, w)


def _mm_res_kernel(ap_ref, as_ref, w_ref, *refs):
    x_refs, (mod_ref, o_ref) = refs[:-2], refs[-2:]

    def run(a_ref, x_ref):
        y = jnp.dot(a_ref[...].astype(BF16), w_ref[...], preferred_element_type=F32)
        o_ref[...] = x_ref[...] + mod_ref[2:3, :] * y

    is_prompt = pl.program_id(0) < N_ROW_TILES_P
    pl.when(is_prompt)(lambda: run(ap_ref, x_refs[0]))
    pl.when(jnp.logical_not(is_prompt))(lambda: run(as_ref, x_refs[-1]))


def _group_specs(k, tile=ROW_TILE, n_grid=1):
    n_p = ROWS_P // tile
    if n_grid == 1:
        return (pl.BlockSpec((tile, k), lambda i: (jnp.minimum(i, n_p - 1), 0)),
                pl.BlockSpec((tile, k), lambda i: (jnp.maximum(i - n_p, 0), 0)))
    return (pl.BlockSpec((tile, k), lambda i, j: (jnp.minimum(i, n_p - 1), 0)),
            pl.BlockSpec((tile, k), lambda i, j: (jnp.maximum(i - n_p, 0), 0)))


def matmul_gated_residual(a_p, a_s, w, w_idx, x, layer, mods, name):
    k = a_p.shape[1]
    x_ops, x_specs = _x_operands(x)
    return pl.pallas_call(
        _mm_res_kernel,
        out_shape=jax.ShapeDtypeStruct((ROWS, D), F32),
        grid=(N_ROW_TILES,),
        in_specs=[
            *_group_specs(k),
            pl.BlockSpec((None, k, D), lambda i: (w_idx, 0, 0), pipeline_mode=pl.Buffered(1)),
            *x_specs,
            _mod_spec(layer, 1),
        ],
        out_specs=pl.BlockSpec((ROW_TILE, D), lambda i: (i, 0)),
        compiler_params=_cparams(("arbitrary",), 4 * ROW_TILE * k * 2 + k * D * 2 + 7 * ROW_TILE * D * 4),
        name=name,
    )(a_p, a_s, w, *x_ops, mods)


FFN_CHUNK = 256
FFN_CHUNKS = D_FF // FFN_CHUNK
FFN_DOWN_GROUP = FFN_CHUNKS
CONV_PAD = 8


def _ffn_kernel(*refs, split_x, first_tile, final_norm):
    n_x = 1 + split_x
    x_refs = refs[:n_x]
    g_ref, mod_ref, wup_ref, cw_ref, cb_ref, wd_ref = refs[n_x:n_x + 6]
    fg_ref = refs[n_x + 6] if final_norm else None
    o_ref, h_ref, pad_a, pad_b, act_ref = refs[n_x + 6 + final_norm:]
    i = pl.program_id(0) + first_tile
    h_ref[...] = _norm_mod(_read_rows(x_refs), g_ref[...], mod_ref[3:4, :], mod_ref[4:5, :]).astype(BF16)
    zeros = jnp.zeros((CONV_PAD, 2 * FFN_CHUNK), F32)
    for pad_ref in (pad_a, pad_b):
        pad_ref[0:CONV_PAD, :] = zeros
        pad_ref[CONV_PAD + ROW_TILE:, :] = zeros
    sub = lax.broadcasted_iota(jnp.int32, (8, 1), 0)
    is_prompt = i < N_ROW_TILES_P
    keep_first = jnp.where((sub == 0) & is_prompt, 0.0, 1.0)
    keep_last = jnp.where((sub == 7) & is_prompt, 0.0, 1.0)

    def cut_sequences(v, keep, row):
        parts, at = [], 0
        for b in range(SEQ, ROW_TILE, SEQ):
            lo = b if row == 0 else b - 8
            parts += [v[at:lo], v[lo:lo + 8] * keep]
            at = lo + 8
        return jnp.concatenate(parts + [v[at:]], axis=0)

    def cols(ref, c):
        return (ref[:, c * FFN_CHUNK:(c + 1) * FFN_CHUNK],
                ref[:, D_FF + c * FFN_CHUNK:D_FF + (c + 1) * FFN_CHUNK])

    def up_proj(c, pad_ref):
        hb = h_ref[...]
        wg, wv = cols(wup_ref, c)
        pad_ref[CONV_PAD:CONV_PAD + ROW_TILE, :FFN_CHUNK] = jnp.dot(hb, wg, preferred_element_type=F32)
        pad_ref[CONV_PAD:CONV_PAD + ROW_TILE, FFN_CHUNK:] = jnp.dot(hb, wv, preferred_element_type=F32)

    def conv_act(c, pad_ref):
        up = pad_ref[CONV_PAD:CONV_PAD + ROW_TILE, :]
        prev = cut_sequences(pad_ref[CONV_PAD - 1:CONV_PAD - 1 + ROW_TILE, :], keep_first, 0)
        nxt = cut_sequences(pad_ref[CONV_PAD + 1:CONV_PAD + 1 + ROW_TILE, :], keep_last, 7)
        cw = jnp.concatenate(cols(cw_ref, c), axis=-1)
        cb = jnp.concatenate(cols(cb_ref, c), axis=-1)
        conv = prev * cw[0:1, :] + up * cw[1:2, :] + nxt * cw[2:3, :] + cb
        gate = conv[:, :FFN_CHUNK]
        act = _silu(gate) * conv[:, FFN_CHUNK:]
        act_ref[:, c * FFN_CHUNK:(c + 1) * FFN_CHUNK] = act.astype(BF16)

    pads = (pad_a, pad_b)
    done = 0
    up_proj(0, pads[0])
    for c in range(FFN_CHUNKS):
        if c + 1 < FFN_CHUNKS:
            up_proj(c + 1, pads[(c + 1) % 2])
        conv_act(c, pads[c % 2])
        if (c + 1) % FFN_DOWN_GROUP == 0 or c + 1 == FFN_CHUNKS:
            rows = slice(done * FFN_CHUNK, (c + 1) * FFN_CHUNK)
            part = jnp.dot(act_ref[:, rows], wd_ref[rows, :], preferred_element_type=F32)
            acc = part if done == 0 else o_ref[...] + part
            if c + 1 == FFN_CHUNKS:
                acc = _read_rows(x_refs) + mod_ref[5:6, :] * acc
                if final_norm:
                    acc = acc * lax.rsqrt(jnp.mean(acc * acc, axis=-1, keepdims=True) + NORM_EPS) * fg_ref[...]
            o_ref[...] = acc
            done = c + 1


def conv_ffn_residual(x, layer, norm_g, mods, w_up, conv_w, conv_b, w_down, tiles=(0, N_ROW_TILES), final_g=None):
    split_x = isinstance(x, tuple)
    first, count = tiles
    if split_x:
        assert tiles == (0, N_ROW_TILES)
        x_ops, x_specs = _x_operands(x)
    else:
        x_ops, x_specs = (x,), [pl.BlockSpec((ROW_TILE, D), lambda i: (first + i, 0))]
    final_ops = () if final_g is None else (final_g.reshape(1, D),)
    final_specs = [] if final_g is None else [pl.BlockSpec((1, D), lambda i: (0, 0))]
    once = pl.Buffered(1)
    vmem = (4 * ROW_TILE * D * 4 + ROW_TILE * D * 2 + 2 * (ROW_TILE + 2 * CONV_PAD) * 2 * FFN_CHUNK * 4
            + ROW_TILE * D_FF * 2 + 3 * D * D_FF * 2 + 5 * ROW_TILE * 2 * FFN_CHUNK * 4)
    return pl.pallas_call(
        functools.partial(_ffn_kernel, split_x=split_x, first_tile=first, final_norm=final_g is not None),
        out_shape=jax.ShapeDtypeStruct((count * ROW_TILE, D), F32),
        grid=(count,),
        in_specs=x_specs + [
            pl.BlockSpec((None, 1, D), lambda i: (layer, 0, 0)),
            pl.BlockSpec((None, None, 6, D), lambda i: (layer, _mod_row(first + i), 0, 0)),
            pl.BlockSpec((None, D, 2 * D_FF), lambda i: (layer, 0, 0), pipeline_mode=once),
            pl.BlockSpec((None, 3, 2 * D_FF), lambda i: (layer, 0, 0), pipeline_mode=once),
            pl.BlockSpec((None, 1, 2 * D_FF), lambda i: (layer, 0, 0), pipeline_mode=once),
            pl.BlockSpec((None, D_FF, D), lambda i: (layer, 0, 0), pipeline_mode=once),
        ] + final_specs,
        out_specs=pl.BlockSpec((ROW_TILE, D), lambda i: (i, 0)),
        scratch_shapes=[pltpu.VMEM((ROW_TILE, D), BF16),
                        pltpu.VMEM((ROW_TILE + 2 * CONV_PAD, 2 * FFN_CHUNK), F32),
                        pltpu.VMEM((ROW_TILE + 2 * CONV_PAD, 2 * FFN_CHUNK), F32),
                        pltpu.VMEM((ROW_TILE, D_FF), BF16)],
        compiler_params=_cparams(("arbitrary",), vmem),
        name="conv_ffn",
    )(*x_ops, norm_g.reshape(DEPTH, 1, D), mods, w_up, conv_w, conv_b.reshape(DEPTH, 1, 2 * D_FF), w_down,
      *final_ops)


NQ = ATTN_HEADS * HEAD_DIM
NKV = ATTN_KV_HEADS * HEAD_DIM
Q_BLOCK = 128
MASKED = -1e30
LOG2E = 1.0 / math.log(2.0)
Q_PRESCALE = HEAD_DIM ** -0.5 * LOG2E


def _dot_t(a, b):
    return lax.dot_general(a.astype(BF16), b.astype(BF16), (((1,), (1,)), ((), ())),
                           preferred_element_type=F32)


def _group_rows(q):
    return jnp.concatenate([q[:, g * HEAD_DIM:(g + 1) * HEAD_DIM] for g in range(ATTN_GROUP)], axis=0)


def _sink_lanes(sink_ref, h, rows):
    return LOG2E * jnp.concatenate(
        [jnp.broadcast_to(sink_ref[0:1, ATTN_GROUP * h + g:ATTN_GROUP * h + g + 1], (1, rows))
         for g in range(ATTN_GROUP)], axis=-1)


ONES_ROWS = 16


def _values_t(v):
    ones = jnp.ones((ONES_ROWS, v.shape[0]), F32)
    out = []
    for c in range(NKV // 128):
        vt = v[:, c * 128:(c + 1) * 128].T
        out += [jnp.concatenate([vt[j * HEAD_DIM:(j + 1) * HEAD_DIM], ones], axis=0).astype(BF16)
                for j in range(128 // HEAD_DIM)]
    return out


def _softmax_pv(q4, key_sets, sink2):
    scores = []
    for k, _, bias in key_sets:
        s = _dot_t(k, q4)
        if bias is not None:
            s = jnp.concatenate([s[c * 128:(c + 1) * 128] if bc is None else s[c * 128:(c + 1) * 128] + bc
                                 for c, bc in enumerate(bias)], axis=0)
        scores.append(s)
    m = sink2
    for s in scores:
        m = jnp.maximum(m, jnp.max(s, axis=0, keepdims=True))
    acc = None
    for s, (_, v1t, _) in zip(scores, key_sets):
        t = jnp.dot(v1t, jnp.exp2(s - m).astype(BF16), preferred_element_type=F32)
        acc = t if acc is None else acc + t
    denom = acc[HEAD_DIM:HEAD_DIM + 1] + jnp.exp2(sink2 - m)
    return acc[:HEAD_DIM] * (1.0 / denom)


def _heads_to_columns(o_t, rows):
    slabs = []
    for g in range(0, ATTN_GROUP, 128 // HEAD_DIM):
        pair = jnp.concatenate([o_t[:, (g + j) * rows:(g + j + 1) * rows] for j in range(128 // HEAD_DIM)], axis=0)
        slabs.append(pair.T)
    return jnp.concatenate(slabs, axis=-1)


def _ctx_attn_kernel(qkv_ref, sink_ref, o_ref):
    v1t = _values_t(qkv_ref[:, NQ + NKV:])
    outs = []
    for h in range(ATTN_KV_HEADS):
        k = qkv_ref[:, NQ + h * HEAD_DIM:NQ + (h + 1) * HEAD_DIM].astype(BF16)
        q4 = _group_rows(qkv_ref[:, ATTN_GROUP * h * HEAD_DIM:ATTN_GROUP * (h + 1) * HEAD_DIM] * Q_PRESCALE)
        o_t = _softmax_pv(q4.astype(BF16), [(k, v1t[h], None)], _sink_lanes(sink_ref, h, SEQ))
        outs.append(_heads_to_columns(o_t, SEQ))
    o_ref[...] = jnp.concatenate(outs, axis=-1).astype(BF16)


def context_attention(qkv, sink):
    return pl.pallas_call(
        _ctx_attn_kernel,
        out_shape=jax.ShapeDtypeStruct((ROWS_P, NQ), BF16),
        grid=(BATCH,),
        in_specs=[pl.BlockSpec((SEQ, NQ + 2 * NKV), lambda b: (b, 0)),
                  pl.BlockSpec((1, ATTN_HEADS), lambda b: (0, 0))],
        out_specs=pl.BlockSpec((SEQ, NQ), lambda b: (b, 0)),
        compiler_params=_cparams(("arbitrary",), 2 * SEQ * (2 * NQ + 2 * NKV) * 4 + 24 * SEQ * ATTN_GROUP * SEQ * 4),
        name="context_attention",
    )(qkv, sink.reshape(1, ATTN_HEADS))


def _rope(x, cos, sin_a, sin_b):
    outs = []
    for c in range(x.shape[1] // 128):
        s = x[:, c * 128:(c + 1) * 128]
        outs.append(s * cos + pltpu.roll(s, 128 - HEAD_DIM // 4, 1) * sin_a + pltpu.roll(s, HEAD_DIM // 4, 1) * sin_b)
    return jnp.concatenate(outs, axis=-1)


Q_TILE = 2 * Q_BLOCK


def _lat_attn_kernel(q_ref, kp_ref, km_ref, kn_ref, vp_ref, vm_ref, vn_ref, ck_ref, cv_ref,
                     cos_ref, sa_ref, sb_ref, sink_ref, o_ref):
    m = pl.program_id(1)
    nm = pl.num_programs(1)
    nb = 2 * nm

    def tables(blk, rows):
        r = pl.ds(pl.multiple_of(blk * Q_BLOCK, Q_BLOCK), rows)
        return cos_ref[r, :], sa_ref[r, :], sb_ref[r, :]

    mid = tables(2 * m, Q_TILE)
    qr = (_rope(q_ref[...], *mid) * Q_PRESCALE).astype(BF16)
    k4 = jnp.concatenate([
        _rope(kp_ref[...], *tables(jnp.maximum(2 * m - 1, 0), Q_BLOCK)),
        _rope(km_ref[...], *mid),
        _rope(kn_ref[...], *tables(jnp.minimum(2 * m + 2, nb - 1), Q_BLOCK))], axis=0).astype(BF16)
    v4 = jnp.concatenate([vp_ref[...], vm_ref[...], vn_ref[...]], axis=0)

    cols = ATTN_GROUP * Q_TILE
    koff = lax.broadcasted_iota(jnp.int32, (Q_BLOCK, cols), 0)
    lane = lax.broadcasted_iota(jnp.int32, (Q_BLOCK, cols), 1)
    qoff = lane & (Q_BLOCK - 1)
    second = (lane & Q_BLOCK) != 0
    lower, upper = koff >= qoff, koff <= qoff
    bias = [jnp.where(jnp.logical_not(second) & lower & (m > 0), 0.0, MASKED),
            jnp.where(jnp.logical_not(second) | lower, 0.0, MASKED),
            jnp.where(second | upper, 0.0, MASKED),
            jnp.where(second & upper & (m < nm - 1), 0.0, MASKED)]

    v1t = _values_t(v4)
    cv1t = _values_t(cv_ref[...])
    outs = []
    for h in range(ATTN_KV_HEADS):
        hs = slice(h * HEAD_DIM, (h + 1) * HEAD_DIM)
        q4 = _group_rows(qr[:, ATTN_GROUP * h * HEAD_DIM:ATTN_GROUP * (h + 1) * HEAD_DIM])
        o_t = _softmax_pv(q4, [(k4[:, hs], v1t[h], bias), (ck_ref[:, hs].astype(BF16), cv1t[h], None)],
                          _sink_lanes(sink_ref, h, Q_TILE))
        outs.append(_heads_to_columns(o_t, Q_TILE))
    o_ref[...] = jnp.concatenate(outs, axis=-1).astype(BF16)


def _rope_tables():
    t = np.arange(DEC_SEQ)
    half = HEAD_DIM // 2
    inv_freq = 1.0 / (ROPE_BASE ** (np.arange(0, half, 2, dtype=np.float32) / half))
    ar = (t // GRID_W).astype(np.float32)[:, None] * inv_freq
    ac = (t % GRID_W).astype(np.float32)[:, None] * inv_freq
    return jnp.concatenate([jnp.asarray(a) for a in (ar, ar, ac, ac)] * 2, axis=-1)


def latent_attention(qkv, cache_k, cache_v, layer_j, sink):
    ang = _rope_tables()
    cos, sin = jnp.cos(ang), jnp.sin(ang)
    first = (lax.broadcasted_iota(jnp.int32, ang.shape, 1) % (HEAD_DIM // 2)) < HEAD_DIM // 4
    sin_a = jnp.where(first, -sin, 0.0)
    sin_b = jnp.where(first, 0.0, sin)
    nb = DEC_SEQ // Q_BLOCK
    nm = DEC_SEQ // Q_TILE
    base = ROWS_P // Q_BLOCK
    base_t = ROWS_P // Q_TILE
    kcol, vcol = NQ // NKV, NQ // NKV + 1
    ck = cache_k.reshape(DEC_BATCH, -1, PAST_LEN, NKV)
    cv = cache_v.reshape(DEC_BATCH, -1, PAST_LEN, NKV)

    def edge_spec(col, blk):
        return pl.BlockSpec((Q_BLOCK, NKV), lambda b, m: (base + b * nb + jnp.clip(blk(m), 0, nb - 1), col))

    def mid_spec(col):
        return pl.BlockSpec((Q_TILE, NKV), lambda b, m: (base_t + b * nm + m, col))

    prev_blk, next_blk = (lambda m: 2 * m - 1), (lambda m: 2 * m + 2)
    table = pl.BlockSpec((DEC_SEQ, 128), lambda b, m: (0, 0))
    return pl.pallas_call(
        _lat_attn_kernel,
        out_shape=jax.ShapeDtypeStruct((ROWS_S, NQ), BF16),
        grid=(DEC_BATCH, nm),
        in_specs=[pl.BlockSpec((Q_TILE, NQ), lambda b, m: (base_t + b * nm + m, 0)),
                  edge_spec(kcol, prev_blk), mid_spec(kcol), edge_spec(kcol, next_blk),
                  edge_spec(vcol, prev_blk), mid_spec(vcol), edge_spec(vcol, next_blk),
                  pl.BlockSpec((None, None, PAST_LEN, NKV), lambda b, m: (b, layer_j, 0, 0)),
                  pl.BlockSpec((None, None, PAST_LEN, NKV), lambda b, m: (b, layer_j, 0, 0)),
                  table, table, table,
                  pl.BlockSpec((1, ATTN_HEADS), lambda b, m: (0, 0))],
        out_specs=pl.BlockSpec((Q_TILE, NQ), lambda b, m: (b * nm + m, 0)),
        compiler_params=_cparams(("arbitrary", "arbitrary"),
                                 4 * Q_TILE * NQ * 4 + 12 * Q_TILE * NKV * 4 + 4 * PAST_LEN * NKV * 4
                                 + 6 * DEC_SEQ * 128 * 4 + 16 * ATTN_GROUP * Q_TILE * (4 * Q_BLOCK + PAST_LEN) * 4),
        name="latent_attention",
    )(qkv, qkv, qkv, qkv, qkv, qkv, qkv, ck, cv, cos, sin_a, sin_b, sink.reshape(1, ATTN_HEADS))


def attention_layer(x, layer, layer_j, mods, norm_g, wqkv, wo, sink, cache_k, cache_v):
    qkv = norm_mod_matmul(x, layer, norm_g, mods, wqkv, layer_j, "attn_qkv")
    a_p = context_attention(qkv, sink)
    a_s = latent_attention(qkv, cache_k, cache_v, layer_j, sink)
    x = matmul_gated_residual(a_p, a_s, wo, layer_j, x, layer, mods, "attn_wo")
    return x, qkv


def _cache_out_kernel(*refs):
    n = (len(refs) - 2) // 2
    k_refs, v_refs, (ok_ref, ov_ref) = refs[:n], refs[n:2 * n], refs[2 * n:]
    for j in range(n):
        @pl.when(pl.program_id(1) == j)
        def _(j=j):
            for h in range(ATTN_KV_HEADS):
                hs = slice(h * HEAD_DIM, (h + 1) * HEAD_DIM)
                ok_ref[:, h, :] = k_refs[j][:, hs]
                ov_ref[:, h, :] = v_refs[j][:, hs]


def new_context_cache(qkvs):
    n = len(qkvs)
    kcol, vcol = NQ // NKV, NQ // NKV + 1
    spec = lambda col: pl.BlockSpec((SEQ, NKV), lambda b, j: (b, col))
    out = jax.ShapeDtypeStruct((BATCH, n, SEQ, ATTN_KV_HEADS, HEAD_DIM), F32)
    out_spec = pl.BlockSpec((None, None, SEQ, ATTN_KV_HEADS, HEAD_DIM), lambda b, j: (b, j, 0, 0, 0))
    return pl.pallas_call(
        _cache_out_kernel,
        out_shape=(out, out),
        grid=(BATCH, n),
        in_specs=[spec(kcol)] * n + [spec(vcol)] * n,
        out_specs=(out_spec, out_spec),
        compiler_params=_cparams(("arbitrary", "arbitrary"), 8 * n * SEQ * NKV * 4 + 4 * SEQ * 8 * 128 * 4),
        name="new_context_cache",
    )(*qkvs, *qkvs)


HG_TILE = 256
HG_LEVELS = 8
HG_IN = 3 * 1024 + 2 * 1024
HG_OUT_TILE = 512
HG_HEADS_PER_STEP = 8


def _hgrn_consts():
    t = np.arange(HG_TILE)
    x = t[:, None] ^ t[None, :]
    hb = np.where(x == 0, -1, np.floor(np.log2(np.maximum(x, 1))).astype(np.int64))
    later = t[:, None] > t[None, :]
    half = HG_TILE // 2
    masks, tris = [], []
    for reverse in (False, True):
        side = ~later & (x != 0) if reverse else later
        lv = [hb == -1] + [(hb == lvl) & side for lvl in range(HG_LEVELS - 1)]
        masks.append(np.stack([m[:half, :half] for m in lv]).astype(np.float32))
        tris.append((t[None, :] >= t[:, None]) if reverse else (t[None, :] <= t[:, None]))
    return jnp.asarray(np.stack(masks)), jnp.asarray(np.stack(tris).astype(np.float32), dtype=BF16)


def _split_bf16(x):
    def top(v):
        bits = lax.bitcast_convert_type(v, jnp.uint32) & jnp.uint32(0xFFFF0000)
        return lax.bitcast_convert_type(bits, F32)

    hi = top(x)
    r = x - hi
    mid = top(r)
    return hi.astype(BF16), mid.astype(BF16), (r - mid).astype(BF16)


def _block_row(x, blk, idx):
    t = x.shape[0]
    x3 = x.reshape(t // blk, blk, x.shape[1])
    return jnp.broadcast_to(x3[:, idx:idx + 1, :], x3.shape).reshape(x.shape)


def _lower_bound(lb_ref, layer, direction):
    x = lb_ref[direction]
    e = jnp.exp(x - jnp.max(x, axis=0, keepdims=True))
    p = e / jnp.sum(e, axis=0, keepdims=True)
    return jnp.sum(p[1:layer + 1, :], axis=0, keepdims=True)


def _hgrn_tile(q, v, z, lb, s_in, mask_ref, tri, reverse):
    t = HG_TILE
    lo, hi = slice(0, t // 2), slice(t // 2, t)
    sg = _sigmoid(z)
    f = lb + (1.0 - lb) * sg
    k = (1.0 - lb) * (1.0 - sg)
    lf3 = _split_bf16(jnp.log(f))
    cum = sum(jnp.dot(tri, p, preferred_element_type=F32) for p in lf3)
    cum2 = cum * LOG2E
    rows = lax.broadcasted_iota(jnp.int32, (t, 1), 0)
    att = [mask_ref[0] * _dot_t(q[r], k[r]) for r in (lo, hi)]
    top = None
    for lvl in range(HG_LEVELS):
        half = 1 << lvl
        bit = (rows & half) != 0
        qside = ~bit if reverse else bit
        if lvl == 0:
            e = jnp.where(qside, f, 1.0)
        else:
            ref = _block_row(cum2, 2 * half, half if reverse else half - 1)
            e = jnp.exp2(-jnp.abs(cum2 - ref))
        w = (jnp.where(qside, q, k) * e).astype(BF16)
        if lvl < HG_LEVELS - 1:
            att = [a + mask_ref[lvl + 1] * _dot_t(w[r], w[r]) for a, r in zip(att, (lo, hi))]
        else:
            top = _dot_t(w[lo], w[hi]) if reverse else _dot_t(w[hi], w[lo])
    vb = v.astype(BF16)
    o_lo, o_hi = _bdot(att[0], vb[lo]), _bdot(att[1], vb[hi])
    if reverse:
        o_lo = o_lo + _bdot(top, vb[hi])
    else:
        o_hi = o_hi + _bdot(top, vb[lo])
    o = jnp.concatenate([o_lo, o_hi], axis=0)
    last = cum[0:1, :] if reverse else cum[t - 1:t, :]
    kd = (k * jnp.exp(last - cum)).astype(BF16)
    s_out = lax.dot_general(kd, vb, (((0,), (0,)), ((), ())), preferred_element_type=F32)
    if s_in is not None:
        o = o + _bdot(q * jnp.exp(cum), s_in)
        last_col = jnp.broadcast_to(last, (HGRN_DK, HGRN_DK)).T
        s_out = jnp.exp(last_col) * s_in + s_out
    return o, s_out


def _hgrn_prompt_kernel(q_ref, v_ref, zf_ref, zb_ref, lb_ref, mask_ref, tri_ref, of_ref, ob_ref, sfin_ref, *, layer):
    for j in range(HG_HEADS_PER_STEP):
        hs = slice(j * HGRN_DK, (j + 1) * HGRN_DK)
        q, v = q_ref[:, hs], v_ref[:, hs]
        lb_f, lb_b = (_lower_bound(lb_ref.at[:, :, hs], layer, d) for d in range(2))
        of_ref[:, hs], sfin_ref[0, j] = _hgrn_tile(q, v, zf_ref[:, hs], lb_f, None, mask_ref.at[0], tri_ref[0], False)
        ob_ref[:, hs], sfin_ref[1, j] = _hgrn_tile(q, v, zb_ref[:, hs], lb_b, None, mask_ref.at[1], tri_ref[1], True)


def _hgrn_sample_kernel(qf_ref, vf_ref, zf_ref, qb_ref, vb_ref, zb_ref, lb_ref, s0_ref, mask_ref, tri_ref,
                        of_ref, ob_ref, state_ref, *, layer):
    tiles = DEC_SEQ // HG_TILE

    @pl.when(pl.program_id(1) % tiles == 0)
    def _():
        state_ref[...] = s0_ref[...]

    for j in range(HG_HEADS_PER_STEP):
        hs = slice(j * HGRN_DK, (j + 1) * HGRN_DK)
        lb_f, lb_b = (_lower_bound(lb_ref.at[:, :, hs], layer, d) for d in range(2))
        of_ref[:, hs], state_ref[0, j] = _hgrn_tile(qf_ref[:, hs], vf_ref[:, hs], zf_ref[:, hs], lb_f,
                                                    state_ref[0, j], mask_ref.at[0], tri_ref[0], False)
        ob_ref[:, hs], state_ref[1, j] = _hgrn_tile(qb_ref[:, hs], vb_ref[:, hs], zb_ref[:, hs], lb_b,
                                                    state_ref[1, j], mask_ref.at[1], tri_ref[1], True)


def hgrn_scan(proj, lb_raw, layer, state_hgrn, layer_j):
    masks, tris = _hgrn_consts()
    hps = HG_HEADS_PER_STEP
    hw = hps * HGRN_DK
    qc, vc, zfc, zbc = 0, 1024 // hw, 2048 // hw, 3072 // hw
    const_specs = [pl.BlockSpec((2, DEPTH, hw), lambda h, i: (0, 0, h))]
    mask_specs = [pl.BlockSpec(masks.shape, lambda h, i: (0, 0, 0, 0)),
                  pl.BlockSpec((2, HG_TILE, HG_TILE), lambda h, i: (0, 0, 0))]
    vmem = 4 * masks.size * 4 + 16 * HG_TILE * hw * 4 + 8 * hw * HGRN_DV * 4 + 24 * hps * HG_TILE * HG_TILE * 4

    def col(cb, row_fn):
        return pl.BlockSpec((HG_TILE, hw), lambda h, i: (row_fn(i), cb + h))

    o_shape = jax.ShapeDtypeStruct((ROWS_P, HGRN_HEADS * HGRN_DV), F32)
    same = lambda i: i
    of_p, ob_p, sfin = pl.pallas_call(
        functools.partial(_hgrn_prompt_kernel, layer=layer),
        out_shape=(o_shape, o_shape, jax.ShapeDtypeStruct((BATCH, 2, HGRN_HEADS, HGRN_DK, HGRN_DV), F32)),
        grid=(HGRN_HEADS // hps, BATCH),
        in_specs=[col(qc, same), col(vc, same), col(zfc, same), col(zbc, same)] + const_specs + mask_specs,
        out_specs=(col(0, same), col(0, same),
                   pl.BlockSpec((None, 2, hps, HGRN_DK, HGRN_DV), lambda h, i: (i, 0, h, 0, 0))),
        compiler_params=_cparams(("arbitrary", "arbitrary"), vmem),
        name="hgrn_scan_prompt",
    )(proj, proj, proj, proj, lb_raw, masks, tris)

    tiles = DEC_SEQ // HG_TILE
    base = ROWS_P // HG_TILE
    bwd = lambda i: (i // tiles) * tiles + (tiles - 1 - i % tiles)
    fwd_in = lambda i: base + i
    bwd_in = lambda i: base + bwd(i)
    o_shape = jax.ShapeDtypeStruct((ROWS_S, HGRN_HEADS * HGRN_DV), F32)
    of_s, ob_s = pl.pallas_call(
        functools.partial(_hgrn_sample_kernel, layer=layer),
        out_shape=(o_shape, o_shape),
        grid=(HGRN_HEADS // hps, DEC_BATCH * tiles),
        in_specs=[col(qc, fwd_in), col(vc, fwd_in), col(zfc, fwd_in), col(qc, bwd_in), col(vc, bwd_in),
                  col(zbc, bwd_in)]
        + const_specs
        + [pl.BlockSpec((None, None, 2, hps, HGRN_DK, HGRN_DV), lambda h, i: (i // tiles, layer_j, 0, h, 0, 0))]
        + mask_specs,
        out_specs=(col(0, same), col(0, bwd)),
        scratch_shapes=[pltpu.VMEM((2, hps, HGRN_DK, HGRN_DV), F32)],
        compiler_params=_cparams(("arbitrary", "arbitrary"), vmem),
        name="hgrn_scan_sample",
    )(proj, proj, proj, proj, proj, proj, lb_raw, state_hgrn, masks, tris)
    return (of_p, ob_p), (of_s, ob_s), sfin


def _hgrn_out_kernel(ofp_ref, obp_ref, ofs_ref, obs_ref, g_ref, gn_ref, w_ref, x_ref, mod_ref, o_ref):
    def run(of_ref, ob_ref):
        gn = gn_ref[...]
        parts = []
        for h in range(HGRN_HEADS):
            hs = slice(h * HGRN_DV, (h + 1) * HGRN_DV)
            o = of_ref[:, hs] + ob_ref[:, hs]
            g = g_ref[:, hs]
            o = o * lax.rsqrt(jnp.mean(o * o, axis=-1, keepdims=True) + NORM_EPS) * gn * _silu(g)
            parts.append(o.astype(BF16))
        y = jnp.dot(jnp.concatenate(parts, axis=-1), w_ref[...], preferred_element_type=F32)
        o_ref[...] = x_ref[...] + mod_ref[2:3, :] * y

    is_prompt = pl.program_id(0) < ROWS_P // HG_OUT_TILE
    pl.when(is_prompt)(lambda: run(ofp_ref, obp_ref))
    pl.when(jnp.logical_not(is_prompt))(lambda: run(ofs_ref, obs_ref))


def hgrn_out(o_p, o_s, proj, g_norm, wo, w_idx, x, layer, mods):
    gcol = 4096 // D
    tile = HG_OUT_TILE
    p_spec, s_spec = _group_specs(D, tile)
    return pl.pallas_call(
        _hgrn_out_kernel,
        out_shape=jax.ShapeDtypeStruct((ROWS, D), F32),
        grid=(ROWS // tile,),
        in_specs=[p_spec, p_spec, s_spec, s_spec,
                  pl.BlockSpec((tile, D), lambda i: (i, gcol)),
                  pl.BlockSpec((1, HGRN_DV), lambda i: (0, 0)),
                  pl.BlockSpec((None, D, D), lambda i: (w_idx, 0, 0), pipeline_mode=pl.Buffered(1)),
                  pl.BlockSpec((tile, D), lambda i: (i, 0)),
                  _mod_spec(layer, 1, tile)],
        out_specs=pl.BlockSpec((tile, D), lambda i: (i, 0)),
        compiler_params=_cparams(("arbitrary",), 22 * tile * D * 4),
        name="hgrn_out",
    )(*o_p, *o_s, proj, g_norm.reshape(1, HGRN_DV), wo, x, mods)


def hgrn_layer(x, layer, layer_j, mods, norm_g, w_in, hgrn_lb, g_norm, wo, state_hgrn):
    proj = norm_mod_matmul(x, layer, norm_g, mods, w_in, layer_j, "hgrn_in")
    o_p, o_s, sfin = hgrn_scan(proj, jnp.transpose(hgrn_lb, (1, 0, 2)), layer, state_hgrn, layer_j)
    x = hgrn_out(o_p, o_s, proj, g_norm[layer_j], wo, layer_j, x, layer, mods)
    return x, sfin


SSM_N = SSM_GROUPS * SSM_STATE
SSM_KT = 8
SSM_ROWS = 256


def _ssm_prep_kernel(are_ref, aim_ref, ldt_ref, bre_ref, bim_ref, cre_ref, cim_ref,
                     lre_ref, lim_ref, bm_ref, cm_ref, bm_acc, cm_acc):
    a_re = jnp.minimum(are_ref[...], -1e-4)
    a_im = aim_ref[...]
    dt = jnp.exp(ldt_ref[...])
    mag = jnp.exp(a_re * dt)
    l_re = mag * jnp.cos(a_im * dt)
    l_im = mag * jnp.sin(a_im * dt)
    lre_ref[...] = l_re
    lim_ref[...] = l_im
    den = a_re * a_re + a_im * a_im
    c_re = ((l_re - 1.0) * a_re + l_im * a_im) / den
    c_im = (l_im * a_re - (l_re - 1.0) * a_im) / den
    b_re, b_im = bre_ref[...], bim_ref[...]
    bb_re = c_re[:, None, :] * b_re - c_im[:, None, :] * b_im
    bb_im = c_re[:, None, :] * b_im + c_im[:, None, :] * b_re
    bm_acc[...] = jnp.zeros_like(bm_acc)
    cm_acc[...] = jnp.zeros_like(cm_acc)
    kw = SSM_KT * SSM_STATE
    for dg in range(2 * SSM_GROUPS):
        d, g = divmod(dg, SSM_GROUPS)
        k, gl = divmod(g, SSM_KT)
        ch = slice(gl * SSM_GROUP, (gl + 1) * SSM_GROUP)
        st = slice(gl * SSM_STATE, (gl + 1) * SSM_STATE)
        st_im = slice(kw + gl * SSM_STATE, kw + (gl + 1) * SSM_STATE)
        bm_acc[d, k, ch, st] = bb_re[dg]
        bm_acc[d, k, ch, st_im] = bb_im[dg]
        cm_acc[d, k, st, ch] = cre_ref[dg]
        cm_acc[d, k, st_im, ch] = -cim_ref[dg]
    bm_ref[...] = bm_acc[...].astype(BF16)
    cm_ref[...] = cm_acc[...].astype(BF16)


def ssm_discretize(a_re, a_im, log_dt, b_re, b_im, c_re, c_im):
    g2 = 2 * SSM_GROUPS
    nk = SSM_GROUPS // SSM_KT
    kw = SSM_KT * SSM_STATE
    sh = jax.ShapeDtypeStruct((g2, SSM_STATE), F32)
    bm_shape = (2, nk, SSM_KT * SSM_GROUP, 2 * kw)
    cm_shape = (2, nk, 2 * kw, SSM_KT * SSM_GROUP)
    bt = lambda b: jnp.transpose(b, (0, 1, 3, 2)).reshape(g2, SSM_GROUP, SSM_STATE)
    ct = lambda c: jnp.transpose(c, (0, 1, 3, 2)).reshape(g2, SSM_STATE, SSM_GROUP)
    l_re, l_im, b_mat, c_mat = pl.pallas_call(
        _ssm_prep_kernel,
        out_shape=(sh, sh, jax.ShapeDtypeStruct(bm_shape, BF16), jax.ShapeDtypeStruct(cm_shape, BF16)),
        scratch_shapes=[pltpu.VMEM(bm_shape, F32), pltpu.VMEM(cm_shape, F32)],
        compiler_params=pltpu.CompilerParams(vmem_limit_bytes=40 << 20),
        name="ssm_discretize",
    )(a_re.reshape(g2, SSM_STATE), a_im.reshape(g2, SSM_STATE), log_dt.reshape(g2, 1), bt(b_re), bt(b_im),
      ct(c_re), ct(c_im))
    lam = jnp.stack([l_re.reshape(2, SSM_N), l_im.reshape(2, SSM_N)], axis=1)
    return b_mat, c_mat, lam


X4_SHAPE = (ROWS // (4 * SEQ), 4, SEQ, D)


def _tm_geometry(prompt):
    if prompt:
        batch = BATCH
        steps = SSM_ROWS // batch
        return batch, steps, (4, 4, steps, D), (lambda i: (0, 0, i, 0)), SEQ // steps, (0,) * batch
    batch = DEC_BATCH
    steps = SSM_ROWS // batch
    per_q = SEQ // steps
    return (batch, steps, (4, 1, steps, D), (lambda i: (1, i // per_q, i % per_q, 0)), DEC_SEQ // steps,
            tuple(range(1, 1 + batch)))


def _x4_seq(ref, b):
    return ref.at[b // ref.shape[1], b % ref.shape[1]]


LANE_SLABS = D // 128


def _slab_store(s_ref, rows, val):
    for c in range(LANE_SLABS):
        s_ref[c, rows, :] = val[:, c * 128:(c + 1) * 128]


def _slab_load(s_ref, rows):
    return jnp.concatenate([s_ref[c, rows, :] for c in range(LANE_SLABS)], axis=-1)


def _normmod_tm_kernel(x_ref, g_ref, mod_ref, o_ref, s_ref, *, batch, steps, mod_rows):
    g = g_ref[...]
    for b in range(batch):
        m = mod_ref.at[mod_rows[b]]
        _slab_store(s_ref, slice(b * steps, (b + 1) * steps),
                    _norm_mod(_x4_seq(x_ref, b)[...], g, m[0:1, :], m[1:2, :]))
    for t in range(steps):
        o_ref[t * batch:(t + 1) * batch, :] = _slab_load(s_ref, pl.ds(t, batch, stride=steps))


def norm_mod_time_major(x4, layer, norm_g, mods, prompt):
    batch, steps, blk, idx, tiles, mod_rows = _tm_geometry(prompt)
    return pl.pallas_call(
        functools.partial(_normmod_tm_kernel, batch=batch, steps=steps, mod_rows=mod_rows),
        out_shape=jax.ShapeDtypeStruct((tiles * SSM_ROWS, D), F32),
        grid=(tiles,),
        in_specs=[pl.BlockSpec(blk, idx),
                  pl.BlockSpec((None, 1, D), lambda i: (layer, 0, 0)),
                  pl.BlockSpec((None, MOD_ROWS, 6, D), lambda i: (layer, 0, 0, 0))],
        out_specs=pl.BlockSpec((SSM_ROWS, D), lambda i: (i, 0)),
        scratch_shapes=[pltpu.VMEM((LANE_SLABS, SSM_ROWS, 128), F32)],
        compiler_params=_cparams(("arbitrary",), 8 * SSM_ROWS * D * 4),
        name="ssm_norm_time_major",
    )(x4, norm_g.reshape(DEPTH, 1, D), mods)


def _ssm_scan_kernel(xf_ref, xb_ref, bm_ref, cm_ref, lam_ref, h0_ref, yf_ref, yb_ref, hfin_ref,
                     hre_f, him_f, hre_b, him_b, st_ref, *, batch):
    i = pl.program_id(0)
    steps = SSM_ROWS // batch
    nk = SSM_GROUPS // SSM_KT
    kw = SSM_KT * SSM_STATE
    x_refs, y_refs = (xf_ref, xb_ref), (yf_ref, yb_ref)
    h_refs = ((hre_f, him_f), (hre_b, him_b))

    @pl.when(i == 0)
    def _():
        st_ref[...] = h0_ref[...]

    def tile_cols(k):
        return slice(k * kw, (k + 1) * kw)

    def bu_tile(d, k):
        xk = x_refs[d][:, k * 128:(k + 1) * 128].astype(BF16)
        bu = jnp.dot(xk, bm_ref[d, k], preferred_element_type=F32)
        h_refs[d][0][:, tile_cols(k)] = bu[:, :kw]
        h_refs[d][1][:, tile_cols(k)] = bu[:, kw:]

    def c_tile(d, k):
        hk = jnp.concatenate([h_refs[d][0][:, tile_cols(k)], h_refs[d][1][:, tile_cols(k)]], axis=-1)
        y_refs[d][:, k * 128:(k + 1) * 128] = jnp.dot(hk.astype(BF16), cm_ref[d, k], preferred_element_type=F32)

    def scan_tile(d, k):
        hre_ref, him_ref = h_refs[d]
        col = tile_cols(k)
        l_re, l_im = lam_ref[d, 0, :, col], lam_ref[d, 1, :, col]
        h_re, h_im = st_ref[d, 0, :, col], st_ref[d, 1, :, col]
        per = max(8 // batch, 1)
        rows_per = per * batch
        for s in range(steps // per):
            g = (steps // per - 1 - s) if d else s
            rows = slice(g * rows_per, (g + 1) * rows_per)
            cur_re, cur_im = hre_ref[rows, col], him_ref[rows, col]
            outs_re, outs_im = [None] * per, [None] * per
            for r in (range(per - 1, -1, -1) if d else range(per)):
                b_re, b_im = cur_re[r * batch:(r + 1) * batch], cur_im[r * batch:(r + 1) * batch]
                h_re, h_im = l_re * h_re - l_im * h_im + b_re, l_re * h_im + l_im * h_re + b_im
                outs_re[r], outs_im[r] = h_re, h_im
            hre_ref[rows, col] = outs_re[0] if per == 1 else jnp.concatenate(outs_re, axis=0)
            him_ref[rows, col] = outs_im[0] if per == 1 else jnp.concatenate(outs_im, axis=0)
        st_ref[d, 0, :, col] = h_re
        st_ref[d, 1, :, col] = h_im

    for k in range(nk + 2):
        for d in range(2):
            if k < nk:
                bu_tile(d, k)
            if 1 <= k <= nk:
                scan_tile(d, k - 1)
            if k >= 2:
                c_tile(d, k - 2)

    @pl.when(i == pl.num_programs(0) - 1)
    def _():
        hfin_ref[...] = st_ref[...]


def ssm_scan(xn_tm, b_mat, c_mat, lam, h0, batch):
    rows = xn_tm.shape[0]
    n = rows // SSM_ROWS
    lam_b = jnp.broadcast_to(lam[:, :, None, :], (2, 2, batch, SSM_N))
    y_shape = jax.ShapeDtypeStruct((rows, D), F32)
    full = lambda a: pl.BlockSpec(a.shape, lambda i: (0,) * a.ndim)
    vmem = (4 * SSM_ROWS * SSM_N * 4 + 8 * SSM_ROWS * D * 4 + 2 * (b_mat.size + c_mat.size) * 2
            + 12 * batch * SSM_N * 4 * 2 + 8 * SSM_ROWS * 1024 * 4)
    return pl.pallas_call(
        functools.partial(_ssm_scan_kernel, batch=batch),
        out_shape=(y_shape, y_shape, jax.ShapeDtypeStruct((2, 2, batch, SSM_N), F32)),
        grid=(n,),
        in_specs=[pl.BlockSpec((SSM_ROWS, D), lambda i: (i, 0)),
                  pl.BlockSpec((SSM_ROWS, D), lambda i: (n - 1 - i, 0)),
                  full(b_mat), full(c_mat), full(lam_b), full(h0)],
        out_specs=(pl.BlockSpec((SSM_ROWS, D), lambda i: (i, 0)),
                   pl.BlockSpec((SSM_ROWS, D), lambda i: (n - 1 - i, 0)),
                   pl.BlockSpec((2, 2, batch, SSM_N), lambda i: (0, 0, 0, 0))),
        scratch_shapes=[pltpu.VMEM((SSM_ROWS, SSM_N), F32)] * 4
        + [pltpu.VMEM((2, 2, batch, SSM_N), F32)],
        compiler_params=_cparams(("arbitrary",), vmem),
        name="ssm_scan",
    )(xn_tm, xn_tm, b_mat, c_mat, lam_b, h0)


def _gelu_tanh(x):
    return 0.5 * x * (1.0 + jnp.tanh(math.sqrt(2.0 / math.pi) * (x + 0.044715 * (x * x * x))))


def _ssm_glu_kernel(yf_ref, yb_ref, xn_ref, d_ref, w_ref, x_ref, mod_ref, o_ref, s_ref,
                    *, batch, steps, mod_rows):
    g = _gelu_tanh(yf_ref[...] + yb_ref[...] + d_ref[...] * xn_ref[...])
    u = jnp.dot(g.astype(BF16), w_ref[...], preferred_element_type=F32)
    _slab_store(s_ref, slice(None), u[:, :D] * _sigmoid(u[:, D:]))
    for b in range(batch):
        gate = mod_ref[mod_rows[b], 2:3, :]
        _x4_seq(o_ref, b)[...] = (_x4_seq(x_ref, b)[...]
                                  + gate * _slab_load(s_ref, pl.ds(b, steps, stride=batch)))


def ssm_glu(yf, yb, xn, d, w_glu, w_idx, x4, layer, mods, prompt):
    batch, steps, blk, idx, tiles, mod_rows = _tm_geometry(prompt)
    tm_spec = pl.BlockSpec((SSM_ROWS, D), lambda i: (i, 0))
    out = pl.pallas_call(
        functools.partial(_ssm_glu_kernel, batch=batch, steps=steps, mod_rows=mod_rows),
        out_shape=jax.ShapeDtypeStruct((4,) + X4_SHAPE[1:], F32),
        grid=(tiles,),
        in_specs=[tm_spec, tm_spec, tm_spec,
                  pl.BlockSpec((None, 1, D), lambda i: (w_idx, 0, 0)),
                  pl.BlockSpec((None, D, 2 * D), lambda i: (w_idx, 0, 0), pipeline_mode=pl.Buffered(1)),
                  pl.BlockSpec(blk, idx),
                  pl.BlockSpec((None, MOD_ROWS, 6, D), lambda i: (layer, 0, 0, 0))],
        out_specs=pl.BlockSpec(blk, lambda i: (0,) + idx(i)[1:]),
        scratch_shapes=[pltpu.VMEM((LANE_SLABS, SSM_ROWS, 128), F32)],
        compiler_params=_cparams(("arbitrary",), 24 * SSM_ROWS * D * 4 + D * 2 * D * 2),
        name="ssm_glu",
    )(yf, yb, xn, d.reshape(-1, 1, D), w_glu, x4, mods)
    return out.reshape(-1, D)


def ssm_layer(x, layer, layer_j, mods, norm_g, a_re, a_im, log_dt, b_re, b_im, c_re, c_im, d, w_glu, state_ssm):
    b_mat, c_mat, lam = ssm_discretize(a_re[layer_j], a_im[layer_j], log_dt[layer_j], b_re[layer_j],
                                       b_im[layer_j], c_re[layer_j], c_im[layer_j])
    x4 = x.reshape(X4_SHAPE)
    xn_p = norm_mod_time_major(x4, layer, norm_g, mods, True)
    xn_s = norm_mod_time_major(x4, layer, norm_g, mods, False)
    h0_p = jnp.zeros((2, 2, BATCH, SSM_N), F32)
    h0_s = jnp.transpose(state_ssm[:, layer_j].reshape(DEC_BATCH, 2, SSM_N, 2), (1, 3, 0, 2))
    yfp, ybp, hfin = ssm_scan(xn_p, b_mat, c_mat, lam, h0_p, BATCH)
    yfs, ybs, _ = ssm_scan(xn_s, b_mat, c_mat, lam, h0_s, DEC_BATCH)
    out_p = ssm_glu(yfp, ybp, xn_p, d, w_glu, layer_j, x4, layer, mods, True)
    out_s = ssm_glu(yfs, ybs, xn_s, d, w_glu, layer_j, x4, layer, mods, False)
    new_state = jnp.transpose(hfin, (2, 0, 3, 1)).reshape(BATCH, 2, SSM_GROUPS, SSM_STATE, 2)
    return (out_p, out_s), new_state


def kernel(x_prompt, x_sample, cache_k, cache_v, state_hgrn, state_ssm, c, c_ctx, ada_w, ada_b, norm1_g, norm2_g, attn_wqkv, attn_wo, attn_sink, hgrn_w_in, hgrn_lb, hgrn_g_norm, hgrn_wo, ssm_a_re, ssm_a_im, ssm_log_dt, ssm_b_re, ssm_b_im, ssm_c_re, ssm_c_im, ssm_d, ssm_w_glu, ffn_w_up, ffn_conv_w, ffn_conv_b, ffn_w_down, final_g):
    cond8 = jnp.zeros((MOD_ROWS, D), F32).at[0].set(c_ctx).at[1:1 + DEC_BATCH].set(c)
    mods = ada_modulation(cond8, ada_w, ada_b)
    x = (x_prompt.reshape(ROWS_P, D), x_sample.reshape(ROWS_S, D))
    wqkv, wo, w_in, hwo, w_glu, w_up, w_down = (w.astype(BF16) for w in (
        attn_wqkv, attn_wo, hgrn_w_in, hgrn_wo, ssm_w_glu, ffn_w_up, ffn_w_down))
    qkvs, new_hgrn, new_ssm = [], [], []
    for l in range(DEPTH):
        kind, j = l % N_MIXERS, l // N_MIXERS
        if kind == 0:
            x, qkv = attention_layer(x, l, j, mods, norm1_g, wqkv, wo, attn_sink[j], cache_k, cache_v)
            qkvs.append(qkv)
        elif kind == 1:
            x, s = hgrn_layer(x, l, j, mods, norm1_g, w_in, hgrn_lb, hgrn_g_norm, hwo, state_hgrn)
            new_hgrn.append(s)
        else:
            x, s = ssm_layer(x, l, j, mods, norm1_g, ssm_a_re, ssm_a_im, ssm_log_dt, ssm_b_re, ssm_b_im,
                             ssm_c_re, ssm_c_im, ssm_d, w_glu, state_ssm)
            new_ssm.append(s)
        ffn = functools.partial(conv_ffn_residual, x, l, norm2_g, mods, w_up, ffn_conv_w, ffn_conv_b, w_down)
        if l + 1 < DEPTH:
            x = ffn()
    y_prompt = ffn(tiles=(0, N_ROW_TILES_P), final_g=final_g).reshape(BATCH, SEQ, D)
    y_sample = ffn(tiles=(N_ROW_TILES_P, N_ROW_TILES - N_ROW_TILES_P), final_g=final_g).reshape(DEC_BATCH, DEC_SEQ, D)
    new_k, new_v = new_context_cache(qkvs)
    return (y_prompt, y_sample, new_k, new_v, jnp.stack(new_hgrn, axis=1), jnp.stack(new_ssm, axis=1))
```

```python
import functools
import math

import jax
import jax.numpy as jnp
import numpy as np
from jax import lax
from jax.experimental import pallas as pl
from jax.experimental.pallas import tpu as pltpu

F32 = jnp.float32
BF16 = jnp.bfloat16

D = 1024
BATCH = 16
SEQ = 256
DEPTH = 4
DEC_BATCH = 4
DEC_SEQ = 1024
PAST_LEN = 512
GRID_W = 64
N_MIXERS = 3
ATTN_HEADS = 16
ATTN_KV_HEADS = 4
ATTN_GROUP = ATTN_HEADS // ATTN_KV_HEADS
HEAD_DIM = D // ATTN_HEADS
WINDOW = 128
ROPE_BASE = 10000.0
HGRN_HEADS = 8
HGRN_DK = 128
HGRN_DV = 128
SSM_GROUP = 16
SSM_GROUPS = D // SSM_GROUP
SSM_STATE = 64
D_FF = 2816
NORM_EPS = 1e-6

ROWS_P = BATCH * SEQ
ROWS_S = DEC_BATCH * DEC_SEQ
ROWS = ROWS_P + ROWS_S
ROW_TILE = 1024
N_ROW_TILES = ROWS // ROW_TILE
N_ROW_TILES_P = ROWS_P // ROW_TILE
MOD_ROWS = 8
V7X_VMEM_BYTES = 64 * 1024 * 1024


def _mod_row(i, tile=ROW_TILE):
    return jnp.where(i < ROWS_P // tile, 0, (i - ROWS_P // tile) // (DEC_SEQ // tile) + 1)


def _cparams(semantics, vmem_bytes):
    vmem = int(min(max(vmem_bytes * 5 // 4 + (4 << 20), 16 << 20), V7X_VMEM_BYTES - (6 << 20)))
    return pltpu.CompilerParams(dimension_semantics=semantics, vmem_limit_bytes=vmem)


def _bdot(a, b):
    return jnp.dot(a.astype(BF16), b.astype(BF16), preferred_element_type=F32)


def _norm_mod(x, g, shift, scale):
    y = x * lax.rsqrt(jnp.mean(x * x, axis=-1, keepdims=True) + NORM_EPS) * g
    return y * (1.0 + scale) + shift


def _sigmoid(x):
    return 0.5 + 0.5 * jnp.tanh(0.5 * x)


def _silu(x):
    h = 0.5 * x
    return h + h * jnp.tanh(h)


def _ada_kernel(c_ref, w_ref, b_ref, o_ref):
    c = c_ref[...]
    o_ref[...] = _bdot(_silu(c), w_ref[...]) + b_ref[...]


def ada_modulation(cond8, ada_w, ada_b):
    tn = 1024
    out = pl.pallas_call(
        _ada_kernel,
        out_shape=jax.ShapeDtypeStruct((DEPTH, MOD_ROWS, 6 * D), F32),
        grid=(DEPTH, 6 * D // tn),
        in_specs=[
            pl.BlockSpec((MOD_ROWS, D), lambda l, j: (0, 0)),
            pl.BlockSpec((None, D, tn), lambda l, j: (l, 0, j)),
            pl.BlockSpec((None, 1, tn), lambda l, j: (l, 0, j)),
        ],
        out_specs=pl.BlockSpec((None, MOD_ROWS, tn), lambda l, j: (l, 0, j)),
        compiler_params=_cparams(("arbitrary", "arbitrary"), 2 * D * tn * 4),
        name="ada_modulation",
    )(cond8, ada_w, ada_b.reshape(DEPTH, 1, 6 * D))
    return out.reshape(DEPTH, MOD_ROWS, 6, D)


def _x_operands(x, n_grid=1):
    if isinstance(x, tuple):
        return x, list(_group_specs(D, ROW_TILE, n_grid))
    return (x,), [pl.BlockSpec((ROW_TILE, D), (lambda i: (i, 0)) if n_grid == 1 else (lambda i, j: (i, 0)))]


def _read_rows(x_refs):
    if len(x_refs) == 1:
        return x_refs[0][...]
    return jnp.where(pl.program_id(0) < N_ROW_TILES_P, x_refs[0][...], x_refs[1][...])


def _nmm_kernel(*refs):
    x_refs, (g_ref, mod_ref, w_ref, o_ref, h_ref) = refs[:-5], refs[-5:]

    @pl.when(pl.program_id(1) == 0)
    def _():
        h_ref[...] = _norm_mod(_read_rows(x_refs), g_ref[...], mod_ref[0:1, :], mod_ref[1:2, :]).astype(BF16)

    o_ref[...] = jnp.dot(h_ref[...], w_ref[...], preferred_element_type=F32)


NMM_MAX_COLS = 2560


def _mod_spec(layer, n_grid, tile=ROW_TILE):
    if n_grid == 1:
        return pl.BlockSpec((None, None, 6, D), lambda i: (layer, _mod_row(i, tile), 0, 0))
    return pl.BlockSpec((None, None, 6, D), lambda i, j: (layer, _mod_row(i, tile), 0, 0))


def norm_mod_matmul(x, layer, norm_g, mods, w, w_idx, name):
    n = w.shape[-1]
    tn = n if n <= NMM_MAX_COLS else NMM_MAX_COLS
    assert n % tn == 0
    x_ops, x_specs = _x_operands(x, 2)
    return pl.pallas_call(
        _nmm_kernel,
        out_shape=jax.ShapeDtypeStruct((ROWS, n), F32),
        grid=(N_ROW_TILES, n // tn),
        in_specs=x_specs + [
            pl.BlockSpec((None, 1, D), lambda i, j: (layer, 0, 0)),
            _mod_spec(layer, 2),
            pl.BlockSpec((None, D, tn), lambda i, j: (w_idx, 0, j)),
        ],
        out_specs=pl.BlockSpec((ROW_TILE, tn), lambda i, j: (i, j)),
        scratch_shapes=[pltpu.VMEM((ROW_TILE, D), BF16)],
        compiler_params=_cparams(("arbitrary", "arbitrary"),
                                 4 * ROW_TILE * D * 4 + ROW_TILE * D * 2 + 2 * D * tn * 2 + 2 * ROW_TILE * tn * 4),
        name=name,
    )(*x_ops, norm_g.reshape(DEPTH, 1, D), mods, w)


def _mm_res_kernel(ap_ref, as_ref, w_ref, *refs):
    x_refs, (mod_ref, o_ref) = refs[:-2], refs[-2:]

    def run(a_ref, x_ref):
        y = jnp.dot(a_ref[...].astype(BF16), w_ref[...], preferred_element_type=F32)
        o_ref[...] = x_ref[...] + mod_ref[2:3, :] * y

    is_prompt = pl.program_id(0) < N_ROW_TILES_P
    pl.when(is_prompt)(lambda: run(ap_ref, x_refs[0]))
    pl.when(jnp.logical_not(is_prompt))(lambda: run(as_ref, x_refs[-1]))


def _group_specs(k, tile=ROW_TILE, n_grid=1):
    n_p = ROWS_P // tile
    if n_grid == 1:
        return (pl.BlockSpec((tile, k), lambda i: (jnp.minimum(i, n_p - 1), 0)),
                pl.BlockSpec((tile, k), lambda i: (jnp.maximum(i - n_p, 0), 0)))
    return (pl.BlockSpec((tile, k), lambda i, j: (jnp.minimum(i, n_p - 1), 0)),
            pl.BlockSpec((tile, k), lambda i, j: (jnp.maximum(i - n_p, 0), 0)))


def matmul_gated_residual(a_p, a_s, w, w_idx, x, layer, mods, name):
    k = a_p.shape[1]
    x_ops, x_specs = _x_operands(x)
    return pl.pallas_call(
        _mm_res_kernel,
        out_shape=jax.ShapeDtypeStruct((ROWS, D), F32),
        grid=(N_ROW_TILES,),
        in_specs=[
            *_group_specs(k),
            pl.BlockSpec((None, k, D), lambda i: (w_idx, 0, 0), pipeline_mode=pl.Buffered(1)),
            *x_specs,
            _mod_spec(layer, 1),
        ],
        out_specs=pl.BlockSpec((ROW_TILE, D), lambda i: (i, 0)),
        compiler_params=_cparams(("arbitrary",), 4 * ROW_TILE * k * 2 + k * D * 2 + 7 * ROW_TILE * D * 4),
        name=name,
    )(a_p, a_s, w, *x_ops, mods)


FFN_CHUNK = 256
FFN_CHUNKS = D_FF // FFN_CHUNK
FFN_DOWN_GROUP = FFN_CHUNKS
CONV_PAD = 8


def _ffn_kernel(*refs, split_x, first_tile, final_norm):
    n_x = 1 + split_x
    x_refs = refs[:n_x]
    g_ref, mod_ref, wup_ref, cw_ref, cb_ref, wd_ref = refs[n_x:n_x + 6]
    fg_ref = refs[n_x + 6] if final_norm else None
    o_ref, h_ref, pad_a, pad_b, act_ref = refs[n_x + 6 + final_norm:]
    i = pl.program_id(0) + first_tile
    h_ref[...] = _norm_mod(_read_rows(x_refs), g_ref[...], mod_ref[3:4, :], mod_ref[4:5, :]).astype(BF16)
    zeros = jnp.zeros((CONV_PAD, 2 * FFN_CHUNK), F32)
    for pad_ref in (pad_a, pad_b):
        pad_ref[0:CONV_PAD, :] = zeros
        pad_ref[CONV_PAD + ROW_TILE:, :] = zeros
    sub = lax.broadcasted_iota(jnp.int32, (8, 1), 0)
    is_prompt = i < N_ROW_TILES_P
    keep_first = jnp.where((sub == 0) & is_prompt, 0.0, 1.0)
    keep_last = jnp.where((sub == 7) & is_prompt, 0.0, 1.0)

    def cut_sequences(v, keep, row):
        parts, at = [], 0
        for b in range(SEQ, ROW_TILE, SEQ):
            lo = b if row == 0 else b - 8
            parts += [v[at:lo], v[lo:lo + 8] * keep]
            at = lo + 8
        return jnp.concatenate(parts + [v[at:]], axis=0)

    def cols(ref, c):
        return (ref[:, c * FFN_CHUNK:(c + 1) * FFN_CHUNK],
                ref[:, D_FF + c * FFN_CHUNK:D_FF + (c + 1) * FFN_CHUNK])

    def up_proj(c, pad_ref):
        hb = h_ref[...]
        wg, wv = cols(wup_ref, c)
        pad_ref[CONV_PAD:CONV_PAD + ROW_TILE, :FFN_CHUNK] = jnp.dot(hb, wg, preferred_element_type=F32)
        pad_ref[CONV_PAD:CONV_PAD + ROW_TILE, FFN_CHUNK:] = jnp.dot(hb, wv, preferred_element_type=F32)

    def conv_act(c, pad_ref):
        up = pad_ref[CONV_PAD:CONV_PAD + ROW_TILE, :]
        prev = cut_sequences(pad_ref[CONV_PAD - 1:CONV_PAD - 1 + ROW_TILE, :], keep_first, 0)
        nxt = cut_sequences(pad_ref[CONV_PAD + 1:CONV_PAD + 1 + ROW_TILE, :], keep_last, 7)
        cw = jnp.concatenate(cols(cw_ref, c), axis=-1)
        cb = jnp.concatenate(cols(cb_ref, c), axis=-1)
        conv = prev * cw[0:1, :] + up * cw[1:2, :] + nxt * cw[2:3, :] + cb
        gate = conv[:, :FFN_CHUNK]
        act = _silu(gate) * conv[:, FFN_CHUNK:]
        act_ref[:, c * FFN_CHUNK:(c + 1) * FFN_CHUNK] = act.astype(BF16)

    pads = (pad_a, pad_b)
    done = 0
    up_proj(0, pads[0])
    for c in range(FFN_CHUNKS):
        if c + 1 < FFN_CHUNKS:
            up_proj(c + 1, pads[(c + 1) % 2])
        conv_act(c, pads[c % 2])
        if (c + 1) % FFN_DOWN_GROUP == 0 or c + 1 == FFN_CHUNKS:
            rows = slice(done * FFN_CHUNK, (c + 1) * FFN_CHUNK)
            part = jnp.dot(act_ref[:, rows], wd_ref[rows, :], preferred_element_type=F32)
            acc = part if done == 0 else o_ref[...] + part
            if c + 1 == FFN_CHUNKS:
                acc = _read_rows(x_refs) + mod_ref[5:6, :] * acc
                if final_norm:
                    acc = acc * lax.rsqrt(jnp.mean(acc * acc, axis=-1, keepdims=True) + NORM_EPS) * fg_ref[...]
            o_ref[...] = acc
            done = c + 1


def conv_ffn_residual(x, layer, norm_g, mods, w_up, conv_w, conv_b, w_down, tiles=(0, N_ROW_TILES), final_g=None):
    split_x = isinstance(x, tuple)
    first, count = tiles
    if split_x:
        assert tiles == (0, N_ROW_TILES)
        x_ops, x_specs = _x_operands(x)
    else:
        x_ops, x_specs = (x,), [pl.BlockSpec((ROW_TILE, D), lambda i: (first + i, 0))]
    final_ops = () if final_g is None else (final_g.reshape(1, D),)
    final_specs = [] if final_g is None else [pl.BlockSpec((1, D), lambda i: (0, 0))]
    once = pl.Buffered(1)
    vmem = (4 * ROW_TILE * D * 4 + ROW_TILE * D * 2 + 2 * (ROW_TILE + 2 * CONV_PAD) * 2 * FFN_CHUNK * 4
            + ROW_TILE * D_FF * 2 + 3 * D * D_FF * 2 + 5 * ROW_TILE * 2 * FFN_CHUNK * 4)
    return pl.pallas_call(
        functools.partial(_ffn_kernel, split_x=split_x, first_tile=first, final_norm=final_g is not None),
        out_shape=jax.ShapeDtypeStruct((count * ROW_TILE, D), F32),
        grid=(count,),
        in_specs=x_specs + [
            pl.BlockSpec((None, 1, D), lambda i: (layer, 0, 0)),
            pl.BlockSpec((None, None, 6, D), lambda i: (layer, _mod_row(first + i), 0, 0)),
            pl.BlockSpec((None, D, 2 * D_FF), lambda i: (layer, 0, 0), pipeline_mode=once),
            pl.BlockSpec((None, 3, 2 * D_FF), lambda i: (layer, 0, 0), pipeline_mode=once),
            pl.BlockSpec((None, 1, 2 * D_FF), lambda i: (layer, 0, 0), pipeline_mode=once),
            pl.BlockSpec((None, D_FF, D), lambda i: (layer, 0, 0), pipeline_mode=once),
        ] + final_specs,
        out_specs=pl.BlockSpec((ROW_TILE, D), lambda i: (i, 0)),
        scratch_shapes=[pltpu.VMEM((ROW_TILE, D), BF16),
                        pltpu.VMEM((ROW_TILE + 2 * CONV_PAD, 2 * FFN_CHUNK), F32),
                        pltpu.VMEM((ROW_TILE + 2 * CONV_PAD, 2 * FFN_CHUNK), F32),
                        pltpu.VMEM((ROW_TILE, D_FF), BF16)],
        compiler_params=_cparams(("arbitrary",), vmem),
        name="conv_ffn",
    )(*x_ops, norm_g.reshape(DEPTH, 1, D), mods, w_up, conv_w, conv_b.reshape(DEPTH, 1, 2 * D_FF), w_down,
      *final_ops)


NQ = ATTN_HEADS * HEAD_DIM
NKV = ATTN_KV_HEADS * HEAD_DIM
Q_BLOCK = 128
MASKED = -1e30
LOG2E = 1.0 / math.log(2.0)
Q_PRESCALE = HEAD_DIM ** -0.5 * LOG2E


def _dot_t(a, b):
    return lax.dot_general(a.astype(BF16), b.astype(BF16), (((1,), (1,)), ((), ())),
                           preferred_element_type=F32)


def _group_rows(q):
    return jnp.concatenate([q[:, g * HEAD_DIM:(g + 1) * HEAD_DIM] for g in range(ATTN_GROUP)], axis=0)


def _sink_lanes(sink_ref, h, rows):
    return LOG2E * jnp.concatenate(
        [jnp.broadcast_to(sink_ref[0:1, ATTN_GROUP * h + g:ATTN_GROUP * h + g + 1], (1, rows))
         for g in range(ATTN_GROUP)], axis=-1)


ONES_ROWS = 16


def _values_t(v):
    ones = jnp.ones((ONES_ROWS, v.shape[0]), F32)
    out = []
    for c in range(NKV // 128):
        vt = v[:, c * 128:(c + 1) * 128].T
        out += [jnp.concatenate([vt[j * HEAD_DIM:(j + 1) * HEAD_DIM], ones], axis=0).astype(BF16)
                for j in range(128 // HEAD_DIM)]
    return out


def _softmax_pv(q4, key_sets, sink2):
    scores = []
    for k, _, bias in key_sets:
        s = _dot_t(k, q4)
        if bias is not None:
            s = jnp.concatenate([s[c * 128:(c + 1) * 128] if bc is None else s[c * 128:(c + 1) * 128] + bc
                                 for c, bc in enumerate(bias)], axis=0)
        scores.append(s)
    m = sink2
    for s in scores:
        m = jnp.maximum(m, jnp.max(s, axis=0, keepdims=True))
    acc = None
    for s, (_, v1t, _) in zip(scores, key_sets):
        t = jnp.dot(v1t, jnp.exp2(s - m).astype(BF16), preferred_element_type=F32)
        acc = t if acc is None else acc + t
    denom = acc[HEAD_DIM:HEAD_DIM + 1] + jnp.exp2(sink2 - m)
    return acc[:HEAD_DIM] * (1.0 / denom)


def _heads_to_columns(o_t, rows):
    slabs = []
    for g in range(0, ATTN_GROUP, 128 // HEAD_DIM):
        pair = jnp.concatenate([o_t[:, (g + j) * rows:(g + j + 1) * rows] for j in range(128 // HEAD_DIM)], axis=0)
        slabs.append(pair.T)
    return jnp.concatenate(slabs, axis=-1)


def _ctx_attn_kernel(qkv_ref, sink_ref, o_ref):
    v1t = _values_t(qkv_ref[:, NQ + NKV:])
    outs = []
    for h in range(ATTN_KV_HEADS):
        k = qkv_ref[:, NQ + h * HEAD_DIM:NQ + (h + 1) * HEAD_DIM].astype(BF16)
        q4 = _group_rows(qkv_ref[:, ATTN_GROUP * h * HEAD_DIM:ATTN_GROUP * (h + 1) * HEAD_DIM] * Q_PRESCALE)
        o_t = _softmax_pv(q4.astype(BF16), [(k, v1t[h], None)], _sink_lanes(sink_ref, h, SEQ))
        outs.append(_heads_to_columns(o_t, SEQ))
    o_ref[...] = jnp.concatenate(outs, axis=-1).astype(BF16)


def context_attention(qkv, sink):
    return pl.pallas_call(
        _ctx_attn_kernel,
        out_shape=jax.ShapeDtypeStruct((ROWS_P, NQ), BF16),
        grid=(BATCH,),
        in_specs=[pl.BlockSpec((SEQ, NQ + 2 * NKV), lambda b: (b, 0)),
                  pl.BlockSpec((1, ATTN_HEADS), lambda b: (0, 0))],
        out_specs=pl.BlockSpec((SEQ, NQ), lambda b: (b, 0)),
        compiler_params=_cparams(("arbitrary",), 2 * SEQ * (2 * NQ + 2 * NKV) * 4 + 24 * SEQ * ATTN_GROUP * SEQ * 4),
        name="context_attention",
    )(qkv, sink.reshape(1, ATTN_HEADS))


def _rope(x, cos, sin_a, sin_b):
    outs = []
    for c in range(x.shape[1] // 128):
        s = x[:, c * 128:(c + 1) * 128]
        outs.append(s * cos + pltpu.roll(s, 128 - HEAD_DIM // 4, 1) * sin_a + pltpu.roll(s, HEAD_DIM // 4, 1) * sin_b)
    return jnp.concatenate(outs, axis=-1)


Q_TILE = 2 * Q_BLOCK


def _lat_attn_kernel(q_ref, kp_ref, km_ref, kn_ref, vp_ref, vm_ref, vn_ref, ck_ref, cv_ref,
                     cos_ref, sa_ref, sb_ref, sink_ref, o_ref):
    m = pl.program_id(1)
    nm = pl.num_programs(1)
    nb = 2 * nm

    def tables(blk, rows):
        r = pl.ds(pl.multiple_of(blk * Q_BLOCK, Q_BLOCK), rows)
        return cos_ref[r, :], sa_ref[r, :], sb_ref[r, :]

    mid = tables(2 * m, Q_TILE)
    qr = (_rope(q_ref[...], *mid) * Q_PRESCALE).astype(BF16)
    k4 = jnp.concatenate([
        _rope(kp_ref[...], *tables(jnp.maximum(2 * m - 1, 0), Q_BLOCK)),
        _rope(km_ref[...], *mid),
        _rope(kn_ref[...], *tables(jnp.minimum(2 * m + 2, nb - 1), Q_BLOCK))], axis=0).astype(BF16)
    v4 = jnp.concatenate([vp_ref[...], vm_ref[...], vn_ref[...]], axis=0)

    cols = ATTN_GROUP * Q_TILE
    koff = lax.broadcasted_iota(jnp.int32, (Q_BLOCK, cols), 0)
    lane = lax.broadcasted_iota(jnp.int32, (Q_BLOCK, cols), 1)
    qoff = lane & (Q_BLOCK - 1)
    second = (lane & Q_BLOCK) != 0
    lower, upper = koff >= qoff, koff <= qoff
    bias = [jnp.where(jnp.logical_not(second) & lower & (m > 0), 0.0, MASKED),
            jnp.where(jnp.logical_not(second) | lower, 0.0, MASKED),
            jnp.where(second | upper, 0.0, MASKED),
            jnp.where(second & upper & (m < nm - 1), 0.0, MASKED)]

    v1t = _values_t(v4)
    cv1t = _values_t(cv_ref[...])
    outs = []
    for h in range(ATTN_KV_HEADS):
        hs = slice(h * HEAD_DIM, (h + 1) * HEAD_DIM)
        q4 = _group_rows(qr[:, ATTN_GROUP * h * HEAD_DIM:ATTN_GROUP * (h + 1) * HEAD_DIM])
        o_t = _softmax_pv(q4, [(k4[:, hs], v1t[h], bias), (ck_ref[:, hs].astype(BF16), cv1t[h], None)],
                          _sink_lanes(sink_ref, h, Q_TILE))
        outs.append(_heads_to_columns(o_t, Q_TILE))
    o_ref[...] = jnp.concatenate(outs, axis=-1).astype(BF16)


def _rope_tables():
    t = np.arange(DEC_SEQ)
    half = HEAD_DIM // 2
    inv_freq = 1.0 / (ROPE_BASE ** (np.arange(0, half, 2, dtype=np.float32) / half))
    ar = (t // GRID_W).astype(np.float32)[:, None] * inv_freq
    ac = (t % GRID_W).astype(np.float32)[:, None] * inv_freq
    return jnp.concatenate([jnp.asarray(a) for a in (ar, ar, ac, ac)] * 2, axis=-1)


def latent_attention(qkv, cache_k, cache_v, layer_j, sink):
    ang = _rope_tables()
    cos, sin = jnp.cos(ang), jnp.sin(ang)
    first = (lax.broadcasted_iota(jnp.int32, ang.shape, 1) % (HEAD_DIM // 2)) < HEAD_DIM // 4
    sin_a = jnp.where(first, -sin, 0.0)
    sin_b = jnp.where(first, 0.0, sin)
    nb = DEC_SEQ // Q_BLOCK
    nm = DEC_SEQ // Q_TILE
    base = ROWS_P // Q_BLOCK
    base_t = ROWS_P // Q_TILE
    kcol, vcol = NQ // NKV, NQ // NKV + 1
    ck = cache_k.reshape(DEC_BATCH, -1, PAST_LEN, NKV)
    cv = cache_v.reshape(DEC_BATCH, -1, PAST_LEN, NKV)

    def edge_spec(col, blk):
        return pl.BlockSpec((Q_BLOCK, NKV), lambda b, m: (base + b * nb + jnp.clip(blk(m), 0, nb - 1), col))

    def mid_spec(col):
        return pl.BlockSpec((Q_TILE, NKV), lambda b, m: (base_t + b * nm + m, col))

    prev_blk, next_blk = (lambda m: 2 * m - 1), (lambda m: 2 * m + 2)
    table = pl.BlockSpec((DEC_SEQ, 128), lambda b, m: (0, 0))
    return pl.pallas_call(
        _lat_attn_kernel,
        out_shape=jax.ShapeDtypeStruct((ROWS_S, NQ), BF16),
        grid=(DEC_BATCH, nm),
        in_specs=[pl.BlockSpec((Q_TILE, NQ), lambda b, m: (base_t + b * nm + m, 0)),
                  edge_spec(kcol, prev_blk), mid_spec(kcol), edge_spec(kcol, next_blk),
                  edge_spec(vcol, prev_blk), mid_spec(vcol), edge_spec(vcol, next_blk),
                  pl.BlockSpec((None, None, PAST_LEN, NKV), lambda b, m: (b, layer_j, 0, 0)),
                  pl.BlockSpec((None, None, PAST_LEN, NKV), lambda b, m: (b, layer_j, 0, 0)),
                  table, table, table,
                  pl.BlockSpec((1, ATTN_HEADS), lambda b, m: (0, 0))],
        out_specs=pl.BlockSpec((Q_TILE, NQ), lambda b, m: (b * nm + m, 0)),
        compiler_params=_cparams(("arbitrary", "arbitrary"),
                                 4 * Q_TILE * NQ * 4 + 12 * Q_TILE * NKV * 4 + 4 * PAST_LEN * NKV * 4
                                 + 6 * DEC_SEQ * 128 * 4 + 16 * ATTN_GROUP * Q_TILE * (4 * Q_BLOCK + PAST_LEN) * 4),
        name="latent_attention",
    )(qkv, qkv, qkv, qkv, qkv, qkv, qkv, ck, cv, cos, sin_a, sin_b, sink.reshape(1, ATTN_HEADS))


def attention_layer(x, layer, layer_j, mods, norm_g, wqkv, wo, sink, cache_k, cache_v):
    qkv = norm_mod_matmul(x, layer, norm_g, mods, wqkv, layer_j, "attn_qkv")
    a_p = context_attention(qkv, sink)
    a_s = latent_attention(qkv, cache_k, cache_v, layer_j, sink)
    x = matmul_gated_residual(a_p, a_s, wo, layer_j, x, layer, mods, "attn_wo")
    return x, qkv


def _cache_out_kernel(*refs):
    n = (len(refs) - 2) // 2
    k_refs, v_refs, (ok_ref, ov_ref) = refs[:n], refs[n:2 * n], refs[2 * n:]
    for j in range(n):
        for h in range(ATTN_KV_HEADS):
            hs = slice(h * HEAD_DIM, (h + 1) * HEAD_DIM)
            ok_ref[j, :, h, :] = k_refs[j][:, hs]
            ov_ref[j, :, h, :] = v_refs[j][:, hs]


def new_context_cache(qkvs):
    n = len(qkvs)
    kcol, vcol = NQ // NKV, NQ // NKV + 1
    spec = lambda col: pl.BlockSpec((SEQ, NKV), lambda b: (b, col))
    out = jax.ShapeDtypeStruct((BATCH, n, SEQ, ATTN_KV_HEADS, HEAD_DIM), F32)
    out_spec = pl.BlockSpec((None, n, SEQ, ATTN_KV_HEADS, HEAD_DIM), lambda b: (b, 0, 0, 0, 0))
    return pl.pallas_call(
        _cache_out_kernel,
        out_shape=(out, out),
        grid=(BATCH,),
        in_specs=[spec(kcol)] * n + [spec(vcol)] * n,
        out_specs=(out_spec, out_spec),
        compiler_params=_cparams(("arbitrary",), 8 * n * SEQ * NKV * 4 + 4 * n * SEQ * 8 * 128 * 4),
        name="new_context_cache",
    )(*qkvs, *qkvs)


HG_TILE = 256
HG_LEVELS = 8
HG_IN = 3 * 1024 + 2 * 1024
HG_OUT_TILE = 512
HG_HEADS_PER_STEP = 8


def _hgrn_consts():
    t = np.arange(HG_TILE)
    x = t[:, None] ^ t[None, :]
    hb = np.where(x == 0, -1, np.floor(np.log2(np.maximum(x, 1))).astype(np.int64))
    later = t[:, None] > t[None, :]
    half = HG_TILE // 2
    masks, tris = [], []
    for reverse in (False, True):
        side = ~later & (x != 0) if reverse else later
        lv = [hb == -1] + [(hb == lvl) & side for lvl in range(HG_LEVELS - 1)]
        masks.append(np.stack([m[:half, :half] for m in lv]).astype(np.float32))
        tris.append((t[None, :] >= t[:, None]) if reverse else (t[None, :] <= t[:, None]))
    return jnp.asarray(np.stack(masks)), jnp.asarray(np.stack(tris).astype(np.float32), dtype=BF16)


def _split_bf16(x):
    def top(v):
        bits = lax.bitcast_convert_type(v, jnp.uint32) & jnp.uint32(0xFFFF0000)
        return lax.bitcast_convert_type(bits, F32)

    hi = top(x)
    r = x - hi
    mid = top(r)
    return hi.astype(BF16), mid.astype(BF16), (r - mid).astype(BF16)


def _block_row(x, blk, idx):
    t = x.shape[0]
    x3 = x.reshape(t // blk, blk, x.shape[1])
    return jnp.broadcast_to(x3[:, idx:idx + 1, :], x3.shape).reshape(x.shape)


def _lower_bound(lb_ref, layer, direction):
    x = lb_ref[direction]
    e = jnp.exp(x - jnp.max(x, axis=0, keepdims=True))
    p = e / jnp.sum(e, axis=0, keepdims=True)
    return jnp.sum(p[1:layer + 1, :], axis=0, keepdims=True)


def _hgrn_tile(q, v, z, lb, s_in, mask_ref, tri, reverse):
    t = HG_TILE
    lo, hi = slice(0, t // 2), slice(t // 2, t)
    sg = _sigmoid(z)
    f = lb + (1.0 - lb) * sg
    k = (1.0 - lb) * (1.0 - sg)
    lf3 = _split_bf16(jnp.log(f))
    cum = sum(jnp.dot(tri, p, preferred_element_type=F32) for p in lf3)
    cum2 = cum * LOG2E
    rows = lax.broadcasted_iota(jnp.int32, (t, 1), 0)
    att = [mask_ref[0] * _dot_t(q[r], k[r]) for r in (lo, hi)]
    top = None
    for lvl in range(HG_LEVELS):
        half = 1 << lvl
        bit = (rows & half) != 0
        qside = ~bit if reverse else bit
        if lvl == 0:
            e = jnp.where(qside, f, 1.0)
        else:
            ref = _block_row(cum2, 2 * half, half if reverse else half - 1)
            e = jnp.exp2(-jnp.abs(cum2 - ref))
        w = (jnp.where(qside, q, k) * e).astype(BF16)
        if lvl < HG_LEVELS - 1:
            att = [a + mask_ref[lvl + 1] * _dot_t(w[r], w[r]) for a, r in zip(att, (lo, hi))]
        else:
            top = _dot_t(w[lo], w[hi]) if reverse else _dot_t(w[hi], w[lo])
    vb = v.astype(BF16)
    o_lo, o_hi = _bdot(att[0], vb[lo]), _bdot(att[1], vb[hi])
    if reverse:
        o_lo = o_lo + _bdot(top, vb[hi])
    else:
        o_hi = o_hi + _bdot(top, vb[lo])
    o = jnp.concatenate([o_lo, o_hi], axis=0)
    last = cum[0:1, :] if reverse else cum[t - 1:t, :]
    kd = (k * jnp.exp(last - cum)).astype(BF16)
    s_out = lax.dot_general(kd, vb, (((0,), (0,)), ((), ())), preferred_element_type=F32)
    if s_in is not None:
        o = o + _bdot(q * jnp.exp(cum), s_in)
        last_col = jnp.broadcast_to(last, (HGRN_DK, HGRN_DK)).T
        s_out = jnp.exp(last_col) * s_in + s_out
    return o, s_out


def _hgrn_prompt_kernel(q_ref, v_ref, zf_ref, zb_ref, lb_ref, mask_ref, tri_ref, of_ref, ob_ref, sfin_ref, *, layer):
    for j in range(HG_HEADS_PER_STEP):
        hs = slice(j * HGRN_DK, (j + 1) * HGRN_DK)
        q, v = q_ref[:, hs], v_ref[:, hs]
        lb_f, lb_b = (_lower_bound(lb_ref.at[:, :, hs], layer, d) for d in range(2))
        of_ref[:, hs], sfin_ref[0, j] = _hgrn_tile(q, v, zf_ref[:, hs], lb_f, None, mask_ref.at[0], tri_ref[0], False)
        ob_ref[:, hs], sfin_ref[1, j] = _hgrn_tile(q, v, zb_ref[:, hs], lb_b, None, mask_ref.at[1], tri_ref[1], True)


def _hgrn_sample_kernel(qf_ref, vf_ref, zf_ref, qb_ref, vb_ref, zb_ref, lb_ref, s0_ref, mask_ref, tri_ref,
                        of_ref, ob_ref, state_ref, *, layer):
    tiles = DEC_SEQ // HG_TILE

    @pl.when(pl.program_id(1) % tiles == 0)
    def _():
        state_ref[...] = s0_ref[...]

    for j in range(HG_HEADS_PER_STEP):
        hs = slice(j * HGRN_DK, (j + 1) * HGRN_DK)
        lb_f, lb_b = (_lower_bound(lb_ref.at[:, :, hs], layer, d) for d in range(2))
        of_ref[:, hs], state_ref[0, j] = _hgrn_tile(qf_ref[:, hs], vf_ref[:, hs], zf_ref[:, hs], lb_f,
                                                    state_ref[0, j], mask_ref.at[0], tri_ref[0], False)
        ob_ref[:, hs], state_ref[1, j] = _hgrn_tile(qb_ref[:, hs], vb_ref[:, hs], zb_ref[:, hs], lb_b,
                                                    state_ref[1, j], mask_ref.at[1], tri_ref[1], True)


def hgrn_scan(proj, lb_raw, layer, state_hgrn, layer_j):
    masks, tris = _hgrn_consts()
    hps = HG_HEADS_PER_STEP
    hw = hps * HGRN_DK
    qc, vc, zfc, zbc = 0, 1024 // hw, 2048 // hw, 3072 // hw
    const_specs = [pl.BlockSpec((2, DEPTH, hw), lambda h, i: (0, 0, h))]
    mask_specs = [pl.BlockSpec(masks.shape, lambda h, i: (0, 0, 0, 0)),
                  pl.BlockSpec((2, HG_TILE, HG_TILE), lambda h, i: (0, 0, 0))]
    vmem = 4 * masks.size * 4 + 16 * HG_TILE * hw * 4 + 8 * hw * HGRN_DV * 4 + 24 * hps * HG_TILE * HG_TILE * 4

    def col(cb, row_fn):
        return pl.BlockSpec((HG_TILE, hw), lambda h, i: (row_fn(i), cb + h))

    o_shape = jax.ShapeDtypeStruct((ROWS_P, HGRN_HEADS * HGRN_DV), F32)
    same = lambda i: i
    of_p, ob_p, sfin = pl.pallas_call(
        functools.partial(_hgrn_prompt_kernel, layer=layer),
        out_shape=(o_shape, o_shape, jax.ShapeDtypeStruct((BATCH, 2, HGRN_HEADS, HGRN_DK, HGRN_DV), F32)),
        grid=(HGRN_HEADS // hps, BATCH),
        in_specs=[col(qc, same), col(vc, same), col(zfc, same), col(zbc, same)] + const_specs + mask_specs,
        out_specs=(col(0, same), col(0, same),
                   pl.BlockSpec((None, 2, hps, HGRN_DK, HGRN_DV), lambda h, i: (i, 0, h, 0, 0))),
        compiler_params=_cparams(("arbitrary", "arbitrary"), vmem),
        name="hgrn_scan_prompt",
    )(proj, proj, proj, proj, lb_raw, masks, tris)

    tiles = DEC_SEQ // HG_TILE
    base = ROWS_P // HG_TILE
    bwd = lambda i: (i // tiles) * tiles + (tiles - 1 - i % tiles)
    fwd_in = lambda i: base + i
    bwd_in = lambda i: base + bwd(i)
    o_shape = jax.ShapeDtypeStruct((ROWS_S, HGRN_HEADS * HGRN_DV), F32)
    of_s, ob_s = pl.pallas_call(
        functools.partial(_hgrn_sample_kernel, layer=layer),
        out_shape=(o_shape, o_shape),
        grid=(HGRN_HEADS // hps, DEC_BATCH * tiles),
        in_specs=[col(qc, fwd_in), col(vc, fwd_in), col(zfc, fwd_in), col(qc, bwd_in), col(vc, bwd_in),
                  col(zbc, bwd_in)]
        + const_specs
        + [pl.BlockSpec((None, None, 2, hps, HGRN_DK, HGRN_DV), lambda h, i: (i // tiles, layer_j, 0, h, 0, 0))]
        + mask_specs,
        out_specs=(col(0, same), col(0, bwd)),
        scratch_shapes=[pltpu.VMEM((2, hps, HGRN_DK, HGRN_DV), F32)],
        compiler_params=_cparams(("arbitrary", "arbitrary"), vmem),
        name="hgrn_scan_sample",
    )(proj, proj, proj, proj, proj, proj, lb_raw, state_hgrn, masks, tris)
    return (of_p, ob_p), (of_s, ob_s), sfin


def _hgrn_out_kernel(ofp_ref, obp_ref, ofs_ref, obs_ref, g_ref, gn_ref, w_ref, x_ref, mod_ref, o_ref):
    def run(of_ref, ob_ref):
        gn = gn_ref[...]
        parts = []
        for h in range(HGRN_HEADS):
            hs = slice(h * HGRN_DV, (h + 1) * HGRN_DV)
            o = of_ref[:, hs] + ob_ref[:, hs]
            g = g_ref[:, hs]
            o = o * lax.rsqrt(jnp.mean(o * o, axis=-1, keepdims=True) + NORM_EPS) * gn * _silu(g)
            parts.append(o.astype(BF16))
        y = jnp.dot(jnp.concatenate(parts, axis=-1), w_ref[...], preferred_element_type=F32)
        o_ref[...] = x_ref[...] + mod_ref[2:3, :] * y

    is_prompt = pl.program_id(0) < ROWS_P // HG_OUT_TILE
    pl.when(is_prompt)(lambda: run(ofp_ref, obp_ref))
    pl.when(jnp.logical_not(is_prompt))(lambda: run(ofs_ref, obs_ref))


def hgrn_out(o_p, o_s, proj, g_norm, wo, w_idx, x, layer, mods):
    gcol = 4096 // D
    tile = HG_OUT_TILE
    p_spec, s_spec = _group_specs(D, tile)
    return pl.pallas_call(
        _hgrn_out_kernel,
        out_shape=jax.ShapeDtypeStruct((ROWS, D), F32),
        grid=(ROWS // tile,),
        in_specs=[p_spec, p_spec, s_spec, s_spec,
                  pl.BlockSpec((tile, D), lambda i: (i, gcol)),
                  pl.BlockSpec((1, HGRN_DV), lambda i: (0, 0)),
                  pl.BlockSpec((None, D, D), lambda i: (w_idx, 0, 0), pipeline_mode=pl.Buffered(1)),
                  pl.BlockSpec((tile, D), lambda i: (i, 0)),
                  _mod_spec(layer, 1, tile)],
        out_specs=pl.BlockSpec((tile, D), lambda i: (i, 0)),
        compiler_params=_cparams(("arbitrary",), 22 * tile * D * 4),
        name="hgrn_out",
    )(*o_p, *o_s, proj, g_norm.reshape(1, HGRN_DV), wo, x, mods)


def hgrn_layer(x, layer, layer_j, mods, norm_g, w_in, hgrn_lb, g_norm, wo, state_hgrn):
    proj = norm_mod_matmul(x, layer, norm_g, mods, w_in, layer_j, "hgrn_in")
    o_p, o_s, sfin = hgrn_scan(proj, jnp.transpose(hgrn_lb, (1, 0, 2)), layer, state_hgrn, layer_j)
    x = hgrn_out(o_p, o_s, proj, g_norm[layer_j], wo, layer_j, x, layer, mods)
    return x, sfin


SSM_N = SSM_GROUPS * SSM_STATE
SSM_KT = 8
SSM_ROWS = 256


def _ssm_prep_kernel(are_ref, aim_ref, ldt_ref, bre_ref, bim_ref, cre_ref, cim_ref,
                     lre_ref, lim_ref, bm_ref, cm_ref, bm_acc, cm_acc):
    a_re = jnp.minimum(are_ref[...], -1e-4)
    a_im = aim_ref[...]
    dt = jnp.exp(ldt_ref[...])
    mag = jnp.exp(a_re * dt)
    l_re = mag * jnp.cos(a_im * dt)
    l_im = mag * jnp.sin(a_im * dt)
    lre_ref[...] = l_re
    lim_ref[...] = l_im
    den = a_re * a_re + a_im * a_im
    c_re = ((l_re - 1.0) * a_re + l_im * a_im) / den
    c_im = (l_im * a_re - (l_re - 1.0) * a_im) / den
    b_re, b_im = bre_ref[...], bim_ref[...]
    bb_re = c_re[:, None, :] * b_re - c_im[:, None, :] * b_im
    bb_im = c_re[:, None, :] * b_im + c_im[:, None, :] * b_re
    bm_acc[...] = jnp.zeros_like(bm_acc)
    cm_acc[...] = jnp.zeros_like(cm_acc)
    kw = SSM_KT * SSM_STATE
    for dg in range(2 * SSM_GROUPS):
        d, g = divmod(dg, SSM_GROUPS)
        k, gl = divmod(g, SSM_KT)
        ch = slice(gl * SSM_GROUP, (gl + 1) * SSM_GROUP)
        st = slice(gl * SSM_STATE, (gl + 1) * SSM_STATE)
        st_im = slice(kw + gl * SSM_STATE, kw + (gl + 1) * SSM_STATE)
        bm_acc[d, k, ch, st] = bb_re[dg]
        bm_acc[d, k, ch, st_im] = bb_im[dg]
        cm_acc[d, k, st, ch] = cre_ref[dg]
        cm_acc[d, k, st_im, ch] = -cim_ref[dg]
    bm_ref[...] = bm_acc[...].astype(BF16)
    cm_ref[...] = cm_acc[...].astype(BF16)


def ssm_discretize(a_re, a_im, log_dt, b_re, b_im, c_re, c_im):
    g2 = 2 * SSM_GROUPS
    nk = SSM_GROUPS // SSM_KT
    kw = SSM_KT * SSM_STATE
    sh = jax.ShapeDtypeStruct((g2, SSM_STATE), F32)
    bm_shape = (2, nk, SSM_KT * SSM_GROUP, 2 * kw)
    cm_shape = (2, nk, 2 * kw, SSM_KT * SSM_GROUP)
    bt = lambda b: jnp.transpose(b, (0, 1, 3, 2)).reshape(g2, SSM_GROUP, SSM_STATE)
    ct = lambda c: jnp.transpose(c, (0, 1, 3, 2)).reshape(g2, SSM_STATE, SSM_GROUP)
    l_re, l_im, b_mat, c_mat = pl.pallas_call(
        _ssm_prep_kernel,
        out_shape=(sh, sh, jax.ShapeDtypeStruct(bm_shape, BF16), jax.ShapeDtypeStruct(cm_shape, BF16)),
        scratch_shapes=[pltpu.VMEM(bm_shape, F32), pltpu.VMEM(cm_shape, F32)],
        compiler_params=pltpu.CompilerParams(vmem_limit_bytes=40 << 20),
        name="ssm_discretize",
    )(a_re.reshape(g2, SSM_STATE), a_im.reshape(g2, SSM_STATE), log_dt.reshape(g2, 1), bt(b_re), bt(b_im),
      ct(c_re), ct(c_im))
    lam = jnp.stack([l_re.reshape(2, SSM_N), l_im.reshape(2, SSM_N)], axis=1)
    return b_mat, c_mat, lam


X4_SHAPE = (ROWS // (4 * SEQ), 4, SEQ, D)


def _tm_geometry(prompt):
    if prompt:
        batch = BATCH
        steps = SSM_ROWS // batch
        return batch, steps, (4, 4, steps, D), (lambda i: (0, 0, i, 0)), SEQ // steps, (0,) * batch
    batch = DEC_BATCH
    steps = SSM_ROWS // batch
    per_q = SEQ // steps
    return (batch, steps, (4, 1, steps, D), (lambda i: (1, i // per_q, i % per_q, 0)), DEC_SEQ // steps,
            tuple(range(1, 1 + batch)))


def _x4_seq(ref, b):
    return ref.at[b // ref.shape[1], b % ref.shape[1]]


LANE_SLABS = D // 128


def _slab_store(s_ref, rows, val):
    for c in range(LANE_SLABS):
        s_ref[c, rows, :] = val[:, c * 128:(c + 1) * 128]


def _slab_load(s_ref, rows):
    return jnp.concatenate([s_ref[c, rows, :] for c in range(LANE_SLABS)], axis=-1)


def _normmod_tm_kernel(x_ref, g_ref, mod_ref, o_ref, s_ref, *, batch, steps, mod_rows):
    g = g_ref[...]
    for b in range(batch):
        m = mod_ref.at[mod_rows[b]]
        _slab_store(s_ref, slice(b * steps, (b + 1) * steps),
                    _norm_mod(_x4_seq(x_ref, b)[...], g, m[0:1, :], m[1:2, :]))
    for t in range(steps):
        o_ref[t * batch:(t + 1) * batch, :] = _slab_load(s_ref, pl.ds(t, batch, stride=steps))


def norm_mod_time_major(x4, layer, norm_g, mods, prompt):
    batch, steps, blk, idx, tiles, mod_rows = _tm_geometry(prompt)
    return pl.pallas_call(
        functools.partial(_normmod_tm_kernel, batch=batch, steps=steps, mod_rows=mod_rows),
        out_shape=jax.ShapeDtypeStruct((tiles * SSM_ROWS, D), F32),
        grid=(tiles,),
        in_specs=[pl.BlockSpec(blk, idx),
                  pl.BlockSpec((None, 1, D), lambda i: (layer, 0, 0)),
                  pl.BlockSpec((None, MOD_ROWS, 6, D), lambda i: (layer, 0, 0, 0))],
        out_specs=pl.BlockSpec((SSM_ROWS, D), lambda i: (i, 0)),
        scratch_shapes=[pltpu.VMEM((LANE_SLABS, SSM_ROWS, 128), F32)],
        compiler_params=_cparams(("arbitrary",), 8 * SSM_ROWS * D * 4),
        name="ssm_norm_time_major",
    )(x4, norm_g.reshape(DEPTH, 1, D), mods)


def _ssm_scan_kernel(xf_ref, xb_ref, bm_ref, cm_ref, lam_ref, h0_ref, yf_ref, yb_ref, hfin_ref,
                     hre_f, him_f, hre_b, him_b, st_ref, *, batch):
    i = pl.program_id(0)
    steps = SSM_ROWS // batch
    nk = SSM_GROUPS // SSM_KT
    kw = SSM_KT * SSM_STATE
    x_refs, y_refs = (xf_ref, xb_ref), (yf_ref, yb_ref)
    h_refs = ((hre_f, him_f), (hre_b, him_b))

    @pl.when(i == 0)
    def _():
        st_ref[...] = h0_ref[...]

    def tile_cols(k):
        return slice(k * kw, (k + 1) * kw)

    def bu_tile(d, k):
        xk = x_refs[d][:, k * 128:(k + 1) * 128].astype(BF16)
        bu = jnp.dot(xk, bm_ref[d, k], preferred_element_type=F32)
        h_refs[d][0][:, tile_cols(k)] = bu[:, :kw]
        h_refs[d][1][:, tile_cols(k)] = bu[:, kw:]

    def c_tile(d, k):
        hk = jnp.concatenate([h_refs[d][0][:, tile_cols(k)], h_refs[d][1][:, tile_cols(k)]], axis=-1)
        y_refs[d][:, k * 128:(k + 1) * 128] = jnp.dot(hk.astype(BF16), cm_ref[d, k], preferred_element_type=F32)

    def scan_tile(d, k):
        hre_ref, him_ref = h_refs[d]
        col = tile_cols(k)
        l_re, l_im = lam_ref[d, 0, :, col], lam_ref[d, 1, :, col]
        h_re, h_im = st_ref[d, 0, :, col], st_ref[d, 1, :, col]
        per = max(8 // batch, 1)
        rows_per = per * batch
        for s in range(steps // per):
            g = (steps // per - 1 - s) if d else s
            rows = slice(g * rows_per, (g + 1) * rows_per)
            cur_re, cur_im = hre_ref[rows, col], him_ref[rows, col]
            outs_re, outs_im = [None] * per, [None] * per
            for r in (range(per - 1, -1, -1) if d else range(per)):
                b_re, b_im = cur_re[r * batch:(r + 1) * batch], cur_im[r * batch:(r + 1) * batch]
                h_re, h_im = l_re * h_re - l_im * h_im + b_re, l_re * h_im + l_im * h_re + b_im
                outs_re[r], outs_im[r] = h_re, h_im
            hre_ref[rows, col] = outs_re[0] if per == 1 else jnp.concatenate(outs_re, axis=0)
            him_ref[rows, col] = outs_im[0] if per == 1 else jnp.concatenate(outs_im, axis=0)
        st_ref[d, 0, :, col] = h_re
        st_ref[d, 1, :, col] = h_im

    for k in range(nk + 2):
        for d in range(2):
            if k < nk:
                bu_tile(d, k)
            if 1 <= k <= nk:
                scan_tile(d, k - 1)
            if k >= 2:
                c_tile(d, k - 2)

    @pl.when(i == pl.num_programs(0) - 1)
    def _():
        hfin_ref[...] = st_ref[...]


def ssm_scan(xn_tm, b_mat, c_mat, lam, h0, batch):
    rows = xn_tm.shape[0]
    n = rows // SSM_ROWS
    lam_b = jnp.broadcast_to(lam[:, :, None, :], (2, 2, batch, SSM_N))
    y_shape = jax.ShapeDtypeStruct((rows, D), F32)
    full = lambda a: pl.BlockSpec(a.shape, lambda i: (0,) * a.ndim)
    vmem = (4 * SSM_ROWS * SSM_N * 4 + 8 * SSM_ROWS * D * 4 + 2 * (b_mat.size + c_mat.size) * 2
            + 12 * batch * SSM_N * 4 * 2 + 8 * SSM_ROWS * 1024 * 4)
    return pl.pallas_call(
        functools.partial(_ssm_scan_kernel, batch=batch),
        out_shape=(y_shape, y_shape, jax.ShapeDtypeStruct((2, 2, batch, SSM_N), F32)),
        grid=(n,),
        in_specs=[pl.BlockSpec((SSM_ROWS, D), lambda i: (i, 0)),
                  pl.BlockSpec((SSM_ROWS, D), lambda i: (n - 1 - i, 0)),
                  full(b_mat), full(c_mat), full(lam_b), full(h0)],
        out_specs=(pl.BlockSpec((SSM_ROWS, D), lambda i: (i, 0)),
                   pl.BlockSpec((SSM_ROWS, D), lambda i: (n - 1 - i, 0)),
                   pl.BlockSpec((2, 2, batch, SSM_N), lambda i: (0, 0, 0, 0))),
        scratch_shapes=[pltpu.VMEM((SSM_ROWS, SSM_N), F32)] * 4
        + [pltpu.VMEM((2, 2, batch, SSM_N), F32)],
        compiler_params=_cparams(("arbitrary",), vmem),
        name="ssm_scan",
    )(xn_tm, xn_tm, b_mat, c_mat, lam_b, h0)


def _gelu_tanh(x):
    return 0.5 * x * (1.0 + jnp.tanh(math.sqrt(2.0 / math.pi) * (x + 0.044715 * (x * x * x))))


def _ssm_glu_kernel(yf_ref, yb_ref, xn_ref, d_ref, w_ref, x_ref, mod_ref, o_ref, s_ref,
                    *, batch, steps, mod_rows):
    g = _gelu_tanh(yf_ref[...] + yb_ref[...] + d_ref[...] * xn_ref[...])
    u = jnp.dot(g.astype(BF16), w_ref[...], preferred_element_type=F32)
    _slab_store(s_ref, slice(None), u[:, :D] * _sigmoid(u[:, D:]))
    for b in range(batch):
        gate = mod_ref[mod_rows[b], 2:3, :]
        _x4_seq(o_ref, b)[...] = (_x4_seq(x_ref, b)[...]
                                  + gate * _slab_load(s_ref, pl.ds(b, steps, stride=batch)))


def ssm_glu(yf, yb, xn, d, w_glu, w_idx, x4, layer, mods, prompt):
    batch, steps, blk, idx, tiles, mod_rows = _tm_geometry(prompt)
    tm_spec = pl.BlockSpec((SSM_ROWS, D), lambda i: (i, 0))
    out = pl.pallas_call(
        functools.partial(_ssm_glu_kernel, batch=batch, steps=steps, mod_rows=mod_rows),
        out_shape=jax.ShapeDtypeStruct((4,) + X4_SHAPE[1:], F32),
        grid=(tiles,),
        in_specs=[tm_spec, tm_spec, tm_spec,
                  pl.BlockSpec((None, 1, D), lambda i: (w_idx, 0, 0)),
                  pl.BlockSpec((None, D, 2 * D), lambda i: (w_idx, 0, 0), pipeline_mode=pl.Buffered(1)),
                  pl.BlockSpec(blk, idx),
                  pl.BlockSpec((None, MOD_ROWS, 6, D), lambda i: (layer, 0, 0, 0))],
        out_specs=pl.BlockSpec(blk, lambda i: (0,) + idx(i)[1:]),
        scratch_shapes=[pltpu.VMEM((LANE_SLABS, SSM_ROWS, 128), F32)],
        compiler_params=_cparams(("arbitrary",), 24 * SSM_ROWS * D * 4 + D * 2 * D * 2),
        name="ssm_glu",
    )(yf, yb, xn, d.reshape(-1, 1, D), w_glu, x4, mods)
    return out.reshape(-1, D)


def ssm_layer(x, layer, layer_j, mods, norm_g, a_re, a_im, log_dt, b_re, b_im, c_re, c_im, d, w_glu, state_ssm):
    b_mat, c_mat, lam = ssm_discretize(a_re[layer_j], a_im[layer_j], log_dt[layer_j], b_re[layer_j],
                                       b_im[layer_j], c_re[layer_j], c_im[layer_j])
    x4 = x.reshape(X4_SHAPE)
    xn_p = norm_mod_time_major(x4, layer, norm_g, mods, True)
    xn_s = norm_mod_time_major(x4, layer, norm_g, mods, False)
    h0_p = jnp.zeros((2, 2, BATCH, SSM_N), F32)
    h0_s = jnp.transpose(state_ssm[:, layer_j].reshape(DEC_BATCH, 2, SSM_N, 2), (1, 3, 0, 2))
    yfp, ybp, hfin = ssm_scan(xn_p, b_mat, c_mat, lam, h0_p, BATCH)
    yfs, ybs, _ = ssm_scan(xn_s, b_mat, c_mat, lam, h0_s, DEC_BATCH)
    out_p = ssm_glu(yfp, ybp, xn_p, d, w_glu, layer_j, x4, layer, mods, True)
    out_s = ssm_glu(yfs, ybs, xn_s, d, w_glu, layer_j, x4, layer, mods, False)
    new_state = jnp.transpose(hfin, (2, 0, 3, 1)).reshape(BATCH, 2, SSM_GROUPS, SSM_STATE, 2)
    return (out_p, out_s), new_state


def kernel(x_prompt, x_sample, cache_k, cache_v, state_hgrn, state_ssm, c, c_ctx, ada_w, ada_b, norm1_g, norm2_g, attn_wqkv, attn_wo, attn_sink, hgrn_w_in, hgrn_lb, hgrn_g_norm, hgrn_wo, ssm_a_re, ssm_a_im, ssm_log_dt, ssm_b_re, ssm_b_im, ssm_c_re, ssm_c_im, ssm_d, ssm_w_glu, ffn_w_up, ffn_conv_w, ffn_conv_b, ffn_w_down, final_g):
    cond8 = jnp.zeros((MOD_ROWS, D), F32).at[0].set(c_ctx).at[1:1 + DEC_BATCH].set(c)
    mods = ada_modulation(cond8, ada_w, ada_b)
    x = (x_prompt.reshape(ROWS_P, D), x_sample.reshape(ROWS_S, D))
    wqkv, wo, w_in, hwo, w_glu, w_up, w_down = (w.astype(BF16) for w in (
        attn_wqkv, attn_wo, hgrn_w_in, hgrn_wo, ssm_w_glu, ffn_w_up, ffn_w_down))
    qkvs, new_hgrn, new_ssm = [], [], []
    for l in range(DEPTH):
        kind, j = l % N_MIXERS, l // N_MIXERS
        if kind == 0:
            x, qkv = attention_layer(x, l, j, mods, norm1_g, wqkv, wo, attn_sink[j], cache_k, cache_v)
            qkvs.append(qkv)
        elif kind == 1:
            x, s = hgrn_layer(x, l, j, mods, norm1_g, w_in, hgrn_lb, hgrn_g_norm, hwo, state_hgrn)
            new_hgrn.append(s)
        else:
            x, s = ssm_layer(x, l, j, mods, norm1_g, ssm_a_re, ssm_a_im, ssm_log_dt, ssm_b_re, ssm_b_im,
                             ssm_c_re, ssm_c_im, ssm_d, w_glu, state_ssm)
            new_ssm.append(s)
        ffn = functools.partial(conv_ffn_residual, x, l, norm2_g, mods, w_up, ffn_conv_w, ffn_conv_b, w_down)
        if l + 1 < DEPTH:
            x = ffn()
    y_prompt = ffn(tiles=(0, N_ROW_TILES_P), final_g=final_g).reshape(BATCH, SEQ, D)
    y_sample = ffn(tiles=(N_ROW_TILES_P, N_ROW_TILES - N_ROW_TILES_P), final_g=final_g).reshape(DEC_BATCH, DEC_SEQ, D)
    new_k, new_v = new_context_cache(qkvs)
    return (y_prompt, y_sample, new_k, new_v, jnp.stack(new_hgrn, axis=1), jnp.stack(new_ssm, axis=1))
```

```python
import functools
import math

import jax
import jax.numpy as jnp
import numpy as np
from jax import lax
from jax.experimental import pallas as pl
from jax.experimental.pallas import tpu as pltpu

F32 = jnp.float32
BF16 = jnp.bfloat16

D = 1024
BATCH = 16
SEQ = 256
DEPTH = 4
DEC_BATCH = 4
DEC_SEQ = 1024
PAST_LEN = 512
GRID_W = 64
N_MIXERS = 3
ATTN_HEADS = 16
ATTN_KV_HEADS = 4
ATTN_GROUP = ATTN_HEADS // ATTN_KV_HEADS
HEAD_DIM = D // ATTN_HEADS
WINDOW = 128
ROPE_BASE = 10000.0
HGRN_HEADS = 8
HGRN_DK = 128
HGRN_DV = 128
SSM_GROUP = 16
SSM_GROUPS = D // SSM_GROUP
SSM_STATE = 64
D_FF = 2816
NORM_EPS = 1e-6

ROWS_P = BATCH * SEQ
ROWS_S = DEC_BATCH * DEC_SEQ
ROWS = ROWS_P + ROWS_S
ROW_TILE = 1024
N_ROW_TILES = ROWS // ROW_TILE
N_ROW_TILES_P = ROWS_P // ROW_TILE
MOD_ROWS = 8
V7X_VMEM_BYTES = 64 * 1024 * 1024


def _mod_row(i, tile=ROW_TILE):
    return jnp.where(i < ROWS_P // tile, 0, (i - ROWS_P // tile) // (DEC_SEQ // tile) + 1)


def _cparams(semantics, vmem_bytes):
    vmem = int(min(max(vmem_bytes * 5 // 4 + (4 << 20), 16 << 20), V7X_VMEM_BYTES - (6 << 20)))
    return pltpu.CompilerParams(dimension_semantics=semantics, vmem_limit_bytes=vmem)


def _bdot(a, b):
    return jnp.dot(a.astype(BF16), b.astype(BF16), preferred_element_type=F32)


def _norm_mod(x, g, shift, scale):
    y = x * lax.rsqrt(jnp.mean(x * x, axis=-1, keepdims=True) + NORM_EPS) * g
    return y * (1.0 + scale) + shift


def _sigmoid(x):
    return 0.5 + 0.5 * jnp.tanh(0.5 * x)


def _silu(x):
    h = 0.5 * x
    return h + h * jnp.tanh(h)


def _ada_kernel(c_ref, w_ref, b_ref, o_ref):
    c = c_ref[...]
    o_ref[...] = _bdot(_silu(c), w_ref[...]) + b_ref[...]


def ada_modulation(cond8, ada_w, ada_b):
    tn = 1024
    out = pl.pallas_call(
        _ada_kernel,
        out_shape=jax.ShapeDtypeStruct((DEPTH, MOD_ROWS, 6 * D), F32),
        grid=(DEPTH, 6 * D // tn),
        in_specs=[
            pl.BlockSpec((MOD_ROWS, D), lambda l, j: (0, 0)),
            pl.BlockSpec((None, D, tn), lambda l, j: (l, 0, j)),
            pl.BlockSpec((None, 1, tn), lambda l, j: (l, 0, j)),
        ],
        out_specs=pl.BlockSpec((None, MOD_ROWS, tn), lambda l, j: (l, 0, j)),
        compiler_params=_cparams(("arbitrary", "arbitrary"), 2 * D * tn * 4),
        name="ada_modulation",
    )(cond8, ada_w, ada_b.reshape(DEPTH, 1, 6 * D))
    return out.reshape(DEPTH, MOD_ROWS, 6, D)


def _x_operands(x, n_grid=1):
    if isinstance(x, tuple):
        return x, list(_group_specs(D, ROW_TILE, n_grid))
    return (x,), [pl.BlockSpec((ROW_TILE, D), (lambda i: (i, 0)) if n_grid == 1 else (lambda i, j: (i, 0)))]


def _read_rows(x_refs):
    if len(x_refs) == 1:
        return x_refs[0][...]
    return jnp.where(pl.program_id(0) < N_ROW_TILES_P, x_refs[0][...], x_refs[1][...])


def _nmm_kernel(*refs, cast_w):
    n_tail = 5 + cast_w
    x_refs, (g_ref, mod_ref, w_ref, o_ref, h_ref) = refs[:-n_tail], refs[-n_tail:][:5]
    if cast_w:
        wb_ref = refs[-1]

        @pl.when((pl.program_id(0) == 0) & (pl.program_id(1) == 0))
        def _():
            wb_ref[...] = w_ref[...].astype(BF16)

        w_ref = wb_ref

    @pl.when(pl.program_id(1) == 0)
    def _():
        h_ref[...] = _norm_mod(_read_rows(x_refs), g_ref[...], mod_ref[0:1, :], mod_ref[1:2, :]).astype(BF16)

    o_ref[...] = jnp.dot(h_ref[...], w_ref[...], preferred_element_type=F32)


NMM_MAX_COLS = 2560


def _mod_spec(layer, n_grid, tile=ROW_TILE):
    if n_grid == 1:
        return pl.BlockSpec((None, None, 6, D), lambda i: (layer, _mod_row(i, tile), 0, 0))
    return pl.BlockSpec((None, None, 6, D), lambda i, j: (layer, _mod_row(i, tile), 0, 0))


def norm_mod_matmul(x, layer, norm_g, mods, w, w_idx, name):
    n = w.shape[-1]
    tn = n if n <= NMM_MAX_COLS else NMM_MAX_COLS
    assert n % tn == 0
    cast_w = w.dtype != BF16
    assert not cast_w or n == tn
    x_ops, x_specs = _x_operands(x, 2)
    w_spec = (pl.BlockSpec((None, D, tn), lambda i, j: (w_idx, 0, 0), pipeline_mode=pl.Buffered(1)) if cast_w
              else pl.BlockSpec((None, D, tn), lambda i, j: (w_idx, 0, j)))
    return pl.pallas_call(
        functools.partial(_nmm_kernel, cast_w=cast_w),
        out_shape=jax.ShapeDtypeStruct((ROWS, n), F32),
        grid=(N_ROW_TILES, n // tn),
        in_specs=x_specs + [
            pl.BlockSpec((None, 1, D), lambda i, j: (layer, 0, 0)),
            _mod_spec(layer, 2),
            w_spec,
        ],
        out_specs=pl.BlockSpec((ROW_TILE, tn), lambda i, j: (i, j)),
        scratch_shapes=[pltpu.VMEM((ROW_TILE, D), BF16)] + ([pltpu.VMEM((D, tn), BF16)] if cast_w else []),
        compiler_params=_cparams(("arbitrary", "arbitrary"),
                                 4 * ROW_TILE * D * 4 + ROW_TILE * D * 2 + D * tn * 6 + 2 * ROW_TILE * tn * 4),
        name=name,
    )(*x_ops, norm_g.reshape(DEPTH, 1, D), mods, w)


def _mm_res_kernel(ap_ref, as_ref, w_ref, *refs):
    x_refs, (mod_ref, o_ref, wb_ref) = refs[:-3], refs[-3:]

    @pl.when(pl.program_id(0) == 0)
    def _():
        wb_ref[...] = w_ref[...].astype(BF16)

    def run(a_ref, x_ref):
        y = jnp.dot(a_ref[...].astype(BF16), wb_ref[...], preferred_element_type=F32)
        o_ref[...] = x_ref[...] + mod_ref[2:3, :] * y

    is_prompt = pl.program_id(0) < N_ROW_TILES_P
    pl.when(is_prompt)(lambda: run(ap_ref, x_refs[0]))
    pl.when(jnp.logical_not(is_prompt))(lambda: run(as_ref, x_refs[-1]))


def _group_specs(k, tile=ROW_TILE, n_grid=1):
    n_p = ROWS_P // tile
    if n_grid == 1:
        return (pl.BlockSpec((tile, k), lambda i: (jnp.minimum(i, n_p - 1), 0)),
                pl.BlockSpec((tile, k), lambda i: (jnp.maximum(i - n_p, 0), 0)))
    return (pl.BlockSpec((tile, k), lambda i, j: (jnp.minimum(i, n_p - 1), 0)),
            pl.BlockSpec((tile, k), lambda i, j: (jnp.maximum(i - n_p, 0), 0)))


def matmul_gated_residual(a_p, a_s, w, w_idx, x, layer, mods, name):
    k = a_p.shape[1]
    x_ops, x_specs = _x_operands(x)
    return pl.pallas_call(
        _mm_res_kernel,
        out_shape=jax.ShapeDtypeStruct((ROWS, D), F32),
        grid=(N_ROW_TILES,),
        in_specs=[
            *_group_specs(k),
            pl.BlockSpec((None, k, D), lambda i: (w_idx, 0, 0), pipeline_mode=pl.Buffered(1)),
            *x_specs,
            _mod_spec(layer, 1),
        ],
        out_specs=pl.BlockSpec((ROW_TILE, D), lambda i: (i, 0)),
        scratch_shapes=[pltpu.VMEM((k, D), BF16)],
        compiler_params=_cparams(("arbitrary",), 4 * ROW_TILE * k * 2 + k * D * 6 + 7 * ROW_TILE * D * 4),
        name=name,
    )(a_p, a_s, w, *x_ops, mods)


FFN_CHUNK = 256
FFN_CHUNKS = D_FF // FFN_CHUNK
FFN_DOWN_GROUP = FFN_CHUNKS
CONV_PAD = 8


def _ffn_kernel(*refs, split_x, first_tile, final_norm):
    n_x = 1 + split_x
    x_refs = refs[:n_x]
    g_ref, mod_ref, wup_ref, cw_ref, cb_ref, wd_ref = refs[n_x:n_x + 6]
    fg_ref = refs[n_x + 6] if final_norm else None
    o_ref, h_ref, pad_a, pad_b, act_ref = refs[n_x + 6 + final_norm:]
    i = pl.program_id(0) + first_tile
    h_ref[...] = _norm_mod(_read_rows(x_refs), g_ref[...], mod_ref[3:4, :], mod_ref[4:5, :]).astype(BF16)
    zeros = jnp.zeros((CONV_PAD, 2 * FFN_CHUNK), F32)
    for pad_ref in (pad_a, pad_b):
        pad_ref[0:CONV_PAD, :] = zeros
        pad_ref[CONV_PAD + ROW_TILE:, :] = zeros
    sub = lax.broadcasted_iota(jnp.int32, (8, 1), 0)
    is_prompt = i < N_ROW_TILES_P
    keep_first = jnp.where((sub == 0) & is_prompt, 0.0, 1.0)
    keep_last = jnp.where((sub == 7) & is_prompt, 0.0, 1.0)

    def cut_sequences(v, keep, row):
        parts, at = [], 0
        for b in range(SEQ, ROW_TILE, SEQ):
            lo = b if row == 0 else b - 8
            parts += [v[at:lo], v[lo:lo + 8] * keep]
            at = lo + 8
        return jnp.concatenate(parts + [v[at:]], axis=0)

    def cols(ref, c):
        return (ref[:, c * FFN_CHUNK:(c + 1) * FFN_CHUNK],
                ref[:, D_FF + c * FFN_CHUNK:D_FF + (c + 1) * FFN_CHUNK])

    def up_proj(c, pad_ref):
        hb = h_ref[...]
        wg, wv = cols(wup_ref, c)
        pad_ref[CONV_PAD:CONV_PAD + ROW_TILE, :FFN_CHUNK] = jnp.dot(hb, wg, preferred_element_type=F32)
        pad_ref[CONV_PAD:CONV_PAD + ROW_TILE, FFN_CHUNK:] = jnp.dot(hb, wv, preferred_element_type=F32)

    def conv_act(c, pad_ref):
        up = pad_ref[CONV_PAD:CONV_PAD + ROW_TILE, :]
        prev = cut_sequences(pad_ref[CONV_PAD - 1:CONV_PAD - 1 + ROW_TILE, :], keep_first, 0)
        nxt = cut_sequences(pad_ref[CONV_PAD + 1:CONV_PAD + 1 + ROW_TILE, :], keep_last, 7)
        cw = jnp.concatenate(cols(cw_ref, c), axis=-1)
        cb = jnp.concatenate(cols(cb_ref, c), axis=-1)
        conv = prev * cw[0:1, :] + up * cw[1:2, :] + nxt * cw[2:3, :] + cb
        gate = conv[:, :FFN_CHUNK]
        act = _silu(gate) * conv[:, FFN_CHUNK:]
        act_ref[:, c * FFN_CHUNK:(c + 1) * FFN_CHUNK] = act.astype(BF16)

    pads = (pad_a, pad_b)
    done = 0
    up_proj(0, pads[0])
    for c in range(FFN_CHUNKS):
        if c + 1 < FFN_CHUNKS:
            up_proj(c + 1, pads[(c + 1) % 2])
        conv_act(c, pads[c % 2])
        if (c + 1) % FFN_DOWN_GROUP == 0 or c + 1 == FFN_CHUNKS:
            rows = slice(done * FFN_CHUNK, (c + 1) * FFN_CHUNK)
            part = jnp.dot(act_ref[:, rows], wd_ref[rows, :], preferred_element_type=F32)
            acc = part if done == 0 else o_ref[...] + part
            if c + 1 == FFN_CHUNKS:
                acc = _read_rows(x_refs) + mod_ref[5:6, :] * acc
                if final_norm:
                    acc = acc * lax.rsqrt(jnp.mean(acc * acc, axis=-1, keepdims=True) + NORM_EPS) * fg_ref[...]
            o_ref[...] = acc
            done = c + 1


def conv_ffn_residual(x, layer, norm_g, mods, w_up, conv_w, conv_b, w_down, tiles=(0, N_ROW_TILES), final_g=None):
    split_x = isinstance(x, tuple)
    first, count = tiles
    if split_x:
        assert tiles == (0, N_ROW_TILES)
        x_ops, x_specs = _x_operands(x)
    else:
        x_ops, x_specs = (x,), [pl.BlockSpec((ROW_TILE, D), lambda i: (first + i, 0))]
    final_ops = () if final_g is None else (final_g.reshape(1, D),)
    final_specs = [] if final_g is None else [pl.BlockSpec((1, D), lambda i: (0, 0))]
    once = pl.Buffered(1)
    vmem = (4 * ROW_TILE * D * 4 + ROW_TILE * D * 2 + 2 * (ROW_TILE + 2 * CONV_PAD) * 2 * FFN_CHUNK * 4
            + ROW_TILE * D_FF * 2 + 3 * D * D_FF * 2 + 5 * ROW_TILE * 2 * FFN_CHUNK * 4)
    return pl.pallas_call(
        functools.partial(_ffn_kernel, split_x=split_x, first_tile=first, final_norm=final_g is not None),
        out_shape=jax.ShapeDtypeStruct((count * ROW_TILE, D), F32),
        grid=(count,),
        in_specs=x_specs + [
            pl.BlockSpec((None, 1, D), lambda i: (layer, 0, 0)),
            pl.BlockSpec((None, None, 6, D), lambda i: (layer, _mod_row(first + i), 0, 0)),
            pl.BlockSpec((None, D, 2 * D_FF), lambda i: (layer, 0, 0), pipeline_mode=once),
            pl.BlockSpec((None, 3, 2 * D_FF), lambda i: (layer, 0, 0), pipeline_mode=once),
            pl.BlockSpec((None, 1, 2 * D_FF), lambda i: (layer, 0, 0), pipeline_mode=once),
            pl.BlockSpec((None, D_FF, D), lambda i: (layer, 0, 0), pipeline_mode=once),
        ] + final_specs,
        out_specs=pl.BlockSpec((ROW_TILE, D), lambda i: (i, 0)),
        scratch_shapes=[pltpu.VMEM((ROW_TILE, D), BF16),
                        pltpu.VMEM((ROW_TILE + 2 * CONV_PAD, 2 * FFN_CHUNK), F32),
                        pltpu.VMEM((ROW_TILE + 2 * CONV_PAD, 2 * FFN_CHUNK), F32),
                        pltpu.VMEM((ROW_TILE, D_FF), BF16)],
        compiler_params=_cparams(("arbitrary",), vmem),
        name="conv_ffn",
    )(*x_ops, norm_g.reshape(DEPTH, 1, D), mods, w_up, conv_w, conv_b.reshape(DEPTH, 1, 2 * D_FF), w_down,
      *final_ops)


NQ = ATTN_HEADS * HEAD_DIM
NKV = ATTN_KV_HEADS * HEAD_DIM
Q_BLOCK = 128
MASKED = -1e30
LOG2E = 1.0 / math.log(2.0)
Q_PRESCALE = HEAD_DIM ** -0.5 * LOG2E


def _dot_t(a, b):
    return lax.dot_general(a.astype(BF16), b.astype(BF16), (((1,), (1,)), ((), ())),
                           preferred_element_type=F32)


def _group_rows(q):
    return jnp.concatenate([q[:, g * HEAD_DIM:(g + 1) * HEAD_DIM] for g in range(ATTN_GROUP)], axis=0)


def _sink_lanes(sink_ref, h, rows):
    return LOG2E * jnp.concatenate(
        [jnp.broadcast_to(sink_ref[0:1, ATTN_GROUP * h + g:ATTN_GROUP * h + g + 1], (1, rows))
         for g in range(ATTN_GROUP)], axis=-1)


ONES_ROWS = 16


def _values_t(v):
    ones = jnp.ones((ONES_ROWS, v.shape[0]), F32)
    out = []
    for c in range(NKV // 128):
        vt = v[:, c * 128:(c + 1) * 128].T
        out += [jnp.concatenate([vt[j * HEAD_DIM:(j + 1) * HEAD_DIM], ones], axis=0).astype(BF16)
                for j in range(128 // HEAD_DIM)]
    return out


def _softmax_pv(q4, key_sets, sink2):
    scores = []
    for k, _, bias in key_sets:
        s = _dot_t(k, q4)
        if bias is not None:
            s = jnp.concatenate([s[c * 128:(c + 1) * 128] if bc is None else s[c * 128:(c + 1) * 128] + bc
                                 for c, bc in enumerate(bias)], axis=0)
        scores.append(s)
    m = sink2
    for s in scores:
        m = jnp.maximum(m, jnp.max(s, axis=0, keepdims=True))
    acc = None
    for s, (_, v1t, _) in zip(scores, key_sets):
        t = jnp.dot(v1t, jnp.exp2(s - m).astype(BF16), preferred_element_type=F32)
        acc = t if acc is None else acc + t
    denom = acc[HEAD_DIM:HEAD_DIM + 1] + jnp.exp2(sink2 - m)
    return acc[:HEAD_DIM] * (1.0 / denom)


def _heads_to_columns(o_t, rows):
    slabs = []
    for g in range(0, ATTN_GROUP, 128 // HEAD_DIM):
        pair = jnp.concatenate([o_t[:, (g + j) * rows:(g + j + 1) * rows] for j in range(128 // HEAD_DIM)], axis=0)
        slabs.append(pair.T)
    return jnp.concatenate(slabs, axis=-1)


def _ctx_attn_kernel(qkv_ref, sink_ref, o_ref):
    v1t = _values_t(qkv_ref[:, NQ + NKV:])
    outs = []
    for h in range(ATTN_KV_HEADS):
        k = qkv_ref[:, NQ + h * HEAD_DIM:NQ + (h + 1) * HEAD_DIM].astype(BF16)
        q4 = _group_rows(qkv_ref[:, ATTN_GROUP * h * HEAD_DIM:ATTN_GROUP * (h + 1) * HEAD_DIM] * Q_PRESCALE)
        o_t = _softmax_pv(q4.astype(BF16), [(k, v1t[h], None)], _sink_lanes(sink_ref, h, SEQ))
        outs.append(_heads_to_columns(o_t, SEQ))
    o_ref[...] = jnp.concatenate(outs, axis=-1).astype(BF16)


def context_attention(qkv, sink):
    return pl.pallas_call(
        _ctx_attn_kernel,
        out_shape=jax.ShapeDtypeStruct((ROWS_P, NQ), BF16),
        grid=(BATCH,),
        in_specs=[pl.BlockSpec((SEQ, NQ + 2 * NKV), lambda b: (b, 0)),
                  pl.BlockSpec((1, ATTN_HEADS), lambda b: (0, 0))],
        out_specs=pl.BlockSpec((SEQ, NQ), lambda b: (b, 0)),
        compiler_params=_cparams(("arbitrary",), 2 * SEQ * (2 * NQ + 2 * NKV) * 4 + 24 * SEQ * ATTN_GROUP * SEQ * 4),
        name="context_attention",
    )(qkv, sink.reshape(1, ATTN_HEADS))


def _rope(x, cos, sin_a, sin_b):
    outs = []
    for c in range(x.shape[1] // 128):
        s = x[:, c * 128:(c + 1) * 128]
        outs.append(s * cos + pltpu.roll(s, 128 - HEAD_DIM // 4, 1) * sin_a + pltpu.roll(s, HEAD_DIM // 4, 1) * sin_b)
    return jnp.concatenate(outs, axis=-1)


Q_TILE = 2 * Q_BLOCK


def _lat_attn_kernel(q_ref, kp_ref, km_ref, kn_ref, vp_ref, vm_ref, vn_ref, ck_ref, cv_ref,
                     cos_ref, sa_ref, sb_ref, sink_ref, o_ref):
    m = pl.program_id(1)
    nm = pl.num_programs(1)
    nb = 2 * nm

    def tables(blk, rows):
        r = pl.ds(pl.multiple_of(blk * Q_BLOCK, Q_BLOCK), rows)
        return cos_ref[r, :], sa_ref[r, :], sb_ref[r, :]

    mid = tables(2 * m, Q_TILE)
    qr = (_rope(q_ref[...], *mid) * Q_PRESCALE).astype(BF16)
    k4 = jnp.concatenate([
        _rope(kp_ref[...], *tables(jnp.maximum(2 * m - 1, 0), Q_BLOCK)),
        _rope(km_ref[...], *mid),
        _rope(kn_ref[...], *tables(jnp.minimum(2 * m + 2, nb - 1), Q_BLOCK))], axis=0).astype(BF16)
    v4 = jnp.concatenate([vp_ref[...], vm_ref[...], vn_ref[...]], axis=0)

    cols = ATTN_GROUP * Q_TILE
    koff = lax.broadcasted_iota(jnp.int32, (Q_BLOCK, cols), 0)
    lane = lax.broadcasted_iota(jnp.int32, (Q_BLOCK, cols), 1)
    qoff = lane & (Q_BLOCK - 1)
    second = (lane & Q_BLOCK) != 0
    lower, upper = koff >= qoff, koff <= qoff
    bias = [jnp.where(jnp.logical_not(second) & lower & (m > 0), 0.0, MASKED),
            jnp.where(jnp.logical_not(second) | lower, 0.0, MASKED),
            jnp.where(second | upper, 0.0, MASKED),
            jnp.where(second & upper & (m < nm - 1), 0.0, MASKED)]

    v1t = _values_t(v4)
    cv1t = _values_t(cv_ref[...])
    outs = []
    for h in range(ATTN_KV_HEADS):
        hs = slice(h * HEAD_DIM, (h + 1) * HEAD_DIM)
        q4 = _group_rows(qr[:, ATTN_GROUP * h * HEAD_DIM:ATTN_GROUP * (h + 1) * HEAD_DIM])
        o_t = _softmax_pv(q4, [(k4[:, hs], v1t[h], bias), (ck_ref[:, hs].astype(BF16), cv1t[h], None)],
                          _sink_lanes(sink_ref, h, Q_TILE))
        outs.append(_heads_to_columns(o_t, Q_TILE))
    o_ref[...] = jnp.concatenate(outs, axis=-1).astype(BF16)


def _rope_tables():
    t = np.arange(DEC_SEQ)
    half = HEAD_DIM // 2
    inv_freq = 1.0 / (ROPE_BASE ** (np.arange(0, half, 2, dtype=np.float32) / half))
    ar = (t // GRID_W).astype(np.float32)[:, None] * inv_freq
    ac = (t % GRID_W).astype(np.float32)[:, None] * inv_freq
    return jnp.concatenate([jnp.asarray(a) for a in (ar, ar, ac, ac)] * 2, axis=-1)


def latent_attention(qkv, cache_k, cache_v, layer_j, sink):
    ang = _rope_tables()
    cos, sin = jnp.cos(ang), jnp.sin(ang)
    first = (lax.broadcasted_iota(jnp.int32, ang.shape, 1) % (HEAD_DIM // 2)) < HEAD_DIM // 4
    sin_a = jnp.where(first, -sin, 0.0)
    sin_b = jnp.where(first, 0.0, sin)
    nb = DEC_SEQ // Q_BLOCK
    nm = DEC_SEQ // Q_TILE
    base = ROWS_P // Q_BLOCK
    base_t = ROWS_P // Q_TILE
    kcol, vcol = NQ // NKV, NQ // NKV + 1
    ck = cache_k.reshape(DEC_BATCH, -1, PAST_LEN, NKV)
    cv = cache_v.reshape(DEC_BATCH, -1, PAST_LEN, NKV)

    def edge_spec(col, blk):
        return pl.BlockSpec((Q_BLOCK, NKV), lambda b, m: (base + b * nb + jnp.clip(blk(m), 0, nb - 1), col))

    def mid_spec(col):
        return pl.BlockSpec((Q_TILE, NKV), lambda b, m: (base_t + b * nm + m, col))

    prev_blk, next_blk = (lambda m: 2 * m - 1), (lambda m: 2 * m + 2)
    table = pl.BlockSpec((DEC_SEQ, 128), lambda b, m: (0, 0))
    return pl.pallas_call(
        _lat_attn_kernel,
        out_shape=jax.ShapeDtypeStruct((ROWS_S, NQ), BF16),
        grid=(DEC_BATCH, nm),
        in_specs=[pl.BlockSpec((Q_TILE, NQ), lambda b, m: (base_t + b * nm + m, 0)),
                  edge_spec(kcol, prev_blk), mid_spec(kcol), edge_spec(kcol, next_blk),
                  edge_spec(vcol, prev_blk), mid_spec(vcol), edge_spec(vcol, next_blk),
                  pl.BlockSpec((None, None, PAST_LEN, NKV), lambda b, m: (b, layer_j, 0, 0)),
                  pl.BlockSpec((None, None, PAST_LEN, NKV), lambda b, m: (b, layer_j, 0, 0)),
                  table, table, table,
                  pl.BlockSpec((1, ATTN_HEADS), lambda b, m: (0, 0))],
        out_specs=pl.BlockSpec((Q_TILE, NQ), lambda b, m: (b * nm + m, 0)),
        compiler_params=_cparams(("arbitrary", "arbitrary"),
                                 4 * Q_TILE * NQ * 4 + 12 * Q_TILE * NKV * 4 + 4 * PAST_LEN * NKV * 4
                                 + 6 * DEC_SEQ * 128 * 4 + 16 * ATTN_GROUP * Q_TILE * (4 * Q_BLOCK + PAST_LEN) * 4),
        name="latent_attention",
    )(qkv, qkv, qkv, qkv, qkv, qkv, qkv, ck, cv, cos, sin_a, sin_b, sink.reshape(1, ATTN_HEADS))


def attention_layer(x, layer, layer_j, mods, norm_g, wqkv, wo, sink, cache_k, cache_v):
    qkv = norm_mod_matmul(x, layer, norm_g, mods, wqkv, layer_j, "attn_qkv")
    a_p = context_attention(qkv, sink)
    a_s = latent_attention(qkv, cache_k, cache_v, layer_j, sink)
    x = matmul_gated_residual(a_p, a_s, wo, layer_j, x, layer, mods, "attn_wo")
    return x, qkv


def _cache_out_kernel(*refs):
    n = (len(refs) - 2) // 2
    k_refs, v_refs, (ok_ref, ov_ref) = refs[:n], refs[n:2 * n], refs[2 * n:]
    for j in range(n):
        for h in range(ATTN_KV_HEADS):
            hs = slice(h * HEAD_DIM, (h + 1) * HEAD_DIM)
            ok_ref[j, :, h, :] = k_refs[j][:, hs]
            ov_ref[j, :, h, :] = v_refs[j][:, hs]


def new_context_cache(qkvs):
    n = len(qkvs)
    kcol, vcol = NQ // NKV, NQ // NKV + 1
    spec = lambda col: pl.BlockSpec((SEQ, NKV), lambda b: (b, col))
    out = jax.ShapeDtypeStruct((BATCH, n, SEQ, ATTN_KV_HEADS, HEAD_DIM), F32)
    out_spec = pl.BlockSpec((None, n, SEQ, ATTN_KV_HEADS, HEAD_DIM), lambda b: (b, 0, 0, 0, 0))
    return pl.pallas_call(
        _cache_out_kernel,
        out_shape=(out, out),
        grid=(BATCH,),
        in_specs=[spec(kcol)] * n + [spec(vcol)] * n,
        out_specs=(out_spec, out_spec),
        compiler_params=_cparams(("arbitrary",), 8 * n * SEQ * NKV * 4 + 4 * n * SEQ * 8 * 128 * 4),
        name="new_context_cache",
    )(*qkvs, *qkvs)


HG_TILE = 256
HG_LEVELS = 8
HG_IN = 3 * 1024 + 2 * 1024
HG_OUT_TILE = 512
HG_HEADS_PER_STEP = 8


def _hgrn_consts():
    t = np.arange(HG_TILE)
    x = t[:, None] ^ t[None, :]
    hb = np.where(x == 0, -1, np.floor(np.log2(np.maximum(x, 1))).astype(np.int64))
    later = t[:, None] > t[None, :]
    half = HG_TILE // 2
    masks, tris = [], []
    for reverse in (False, True):
        side = ~later & (x != 0) if reverse else later
        lv = [hb == -1] + [(hb == lvl) & side for lvl in range(HG_LEVELS - 1)]
        masks.append(np.stack([m[:half, :half] for m in lv]).astype(np.float32))
        tris.append((t[None, :] >= t[:, None]) if reverse else (t[None, :] <= t[:, None]))
    return jnp.asarray(np.stack(masks)), jnp.asarray(np.stack(tris).astype(np.float32), dtype=BF16)


def _split_bf16(x):
    def top(v):
        bits = lax.bitcast_convert_type(v, jnp.uint32) & jnp.uint32(0xFFFF0000)
        return lax.bitcast_convert_type(bits, F32)

    hi = top(x)
    r = x - hi
    mid = top(r)
    return hi.astype(BF16), mid.astype(BF16), (r - mid).astype(BF16)


def _block_row(x, blk, idx):
    t = x.shape[0]
    x3 = x.reshape(t // blk, blk, x.shape[1])
    return jnp.broadcast_to(x3[:, idx:idx + 1, :], x3.shape).reshape(x.shape)


def _lower_bound(lb_ref, layer, direction):
    x = lb_ref[direction]
    e = jnp.exp(x - jnp.max(x, axis=0, keepdims=True))
    p = e / jnp.sum(e, axis=0, keepdims=True)
    return jnp.sum(p[1:layer + 1, :], axis=0, keepdims=True)


def _hgrn_tile(q, v, z, lb, s_in, mask_ref, tri, reverse):
    t = HG_TILE
    lo, hi = slice(0, t // 2), slice(t // 2, t)
    sg = _sigmoid(z)
    f = lb + (1.0 - lb) * sg
    k = (1.0 - lb) * (1.0 - sg)
    lf3 = _split_bf16(jnp.log(f))
    cum = sum(jnp.dot(tri, p, preferred_element_type=F32) for p in lf3)
    cum2 = cum * LOG2E
    rows = lax.broadcasted_iota(jnp.int32, (t, 1), 0)
    att = [mask_ref[0] * _dot_t(q[r], k[r]) for r in (lo, hi)]
    top = None
    for lvl in range(HG_LEVELS):
        half = 1 << lvl
        bit = (rows & half) != 0
        qside = ~bit if reverse else bit
        if lvl == 0:
            e = jnp.where(qside, f, 1.0)
        else:
            ref = _block_row(cum2, 2 * half, half if reverse else half - 1)
            e = jnp.exp2(-jnp.abs(cum2 - ref))
        w = (jnp.where(qside, q, k) * e).astype(BF16)
        if lvl < HG_LEVELS - 1:
            att = [a + mask_ref[lvl + 1] * _dot_t(w[r], w[r]) for a, r in zip(att, (lo, hi))]
        else:
            top = _dot_t(w[lo], w[hi]) if reverse else _dot_t(w[hi], w[lo])
    vb = v.astype(BF16)
    o_lo, o_hi = _bdot(att[0], vb[lo]), _bdot(att[1], vb[hi])
    if reverse:
        o_lo = o_lo + _bdot(top, vb[hi])
    else:
        o_hi = o_hi + _bdot(top, vb[lo])
    o = jnp.concatenate([o_lo, o_hi], axis=0)
    last = cum[0:1, :] if reverse else cum[t - 1:t, :]
    kd = (k * jnp.exp(last - cum)).astype(BF16)
    s_out = lax.dot_general(kd, vb, (((0,), (0,)), ((), ())), preferred_element_type=F32)
    if s_in is not None:
        o = o + _bdot(q * jnp.exp(cum), s_in)
        last_col = jnp.broadcast_to(last, (HGRN_DK, HGRN_DK)).T
        s_out = jnp.exp(last_col) * s_in + s_out
    return o, s_out


def _hgrn_prompt_kernel(q_ref, v_ref, zf_ref, zb_ref, lb_ref, mask_ref, tri_ref, of_ref, ob_ref, sfin_ref, *, layer):
    for j in range(HG_HEADS_PER_STEP):
        hs = slice(j * HGRN_DK, (j + 1) * HGRN_DK)
        q, v = q_ref[:, hs], v_ref[:, hs]
        lb_f, lb_b = (_lower_bound(lb_ref.at[:, :, hs], layer, d) for d in range(2))
        of_ref[:, hs], sfin_ref[0, j] = _hgrn_tile(q, v, zf_ref[:, hs], lb_f, None, mask_ref.at[0], tri_ref[0], False)
        ob_ref[:, hs], sfin_ref[1, j] = _hgrn_tile(q, v, zb_ref[:, hs], lb_b, None, mask_ref.at[1], tri_ref[1], True)


def _hgrn_sample_kernel(qf_ref, vf_ref, zf_ref, qb_ref, vb_ref, zb_ref, lb_ref, s0_ref, mask_ref, tri_ref,
                        of_ref, ob_ref, state_ref, *, layer):
    tiles = DEC_SEQ // HG_TILE

    @pl.when(pl.program_id(1) % tiles == 0)
    def _():
        state_ref[...] = s0_ref[...]

    for j in range(HG_HEADS_PER_STEP):
        hs = slice(j * HGRN_DK, (j + 1) * HGRN_DK)
        lb_f, lb_b = (_lower_bound(lb_ref.at[:, :, hs], layer, d) for d in range(2))
        of_ref[:, hs], state_ref[0, j] = _hgrn_tile(qf_ref[:, hs], vf_ref[:, hs], zf_ref[:, hs], lb_f,
                                                    state_ref[0, j], mask_ref.at[0], tri_ref[0], False)
        ob_ref[:, hs], state_ref[1, j] = _hgrn_tile(qb_ref[:, hs], vb_ref[:, hs], zb_ref[:, hs], lb_b,
                                                    state_ref[1, j], mask_ref.at[1], tri_ref[1], True)


def hgrn_scan(proj, lb_raw, layer, state_hgrn, layer_j):
    masks, tris = _hgrn_consts()
    hps = HG_HEADS_PER_STEP
    hw = hps * HGRN_DK
    qc, vc, zfc, zbc = 0, 1024 // hw, 2048 // hw, 3072 // hw
    const_specs = [pl.BlockSpec((2, DEPTH, hw), lambda h, i: (0, 0, h))]
    mask_specs = [pl.BlockSpec(masks.shape, lambda h, i: (0, 0, 0, 0)),
                  pl.BlockSpec((2, HG_TILE, HG_TILE), lambda h, i: (0, 0, 0))]
    vmem = 4 * masks.size * 4 + 16 * HG_TILE * hw * 4 + 8 * hw * HGRN_DV * 4 + 24 * hps * HG_TILE * HG_TILE * 4

    def col(cb, row_fn):
        return pl.BlockSpec((HG_TILE, hw), lambda h, i: (row_fn(i), cb + h))

    o_shape = jax.ShapeDtypeStruct((ROWS_P, HGRN_HEADS * HGRN_DV), F32)
    same = lambda i: i
    of_p, ob_p, sfin = pl.pallas_call(
        functools.partial(_hgrn_prompt_kernel, layer=layer),
        out_shape=(o_shape, o_shape, jax.ShapeDtypeStruct((BATCH, 2, HGRN_HEADS, HGRN_DK, HGRN_DV), F32)),
        grid=(HGRN_HEADS // hps, BATCH),
        in_specs=[col(qc, same), col(vc, same), col(zfc, same), col(zbc, same)] + const_specs + mask_specs,
        out_specs=(col(0, same), col(0, same),
                   pl.BlockSpec((None, 2, hps, HGRN_DK, HGRN_DV), lambda h, i: (i, 0, h, 0, 0))),
        compiler_params=_cparams(("arbitrary", "arbitrary"), vmem),
        name="hgrn_scan_prompt",
    )(proj, proj, proj, proj, lb_raw, masks, tris)

    tiles = DEC_SEQ // HG_TILE
    base = ROWS_P // HG_TILE
    bwd = lambda i: (i // tiles) * tiles + (tiles - 1 - i % tiles)
    fwd_in = lambda i: base + i
    bwd_in = lambda i: base + bwd(i)
    o_shape = jax.ShapeDtypeStruct((ROWS_S, HGRN_HEADS * HGRN_DV), F32)
    of_s, ob_s = pl.pallas_call(
        functools.partial(_hgrn_sample_kernel, layer=layer),
        out_shape=(o_shape, o_shape),
        grid=(HGRN_HEADS // hps, DEC_BATCH * tiles),
        in_specs=[col(qc, fwd_in), col(vc, fwd_in), col(zfc, fwd_in), col(qc, bwd_in), col(vc, bwd_in),
                  col(zbc, bwd_in)]
        + const_specs
        + [pl.BlockSpec((None, None, 2, hps, HGRN_DK, HGRN_DV), lambda h, i: (i // tiles, layer_j, 0, h, 0, 0))]
        + mask_specs,
        out_specs=(col(0, same), col(0, bwd)),
        scratch_shapes=[pltpu.VMEM((2, hps, HGRN_DK, HGRN_DV), F32)],
        compiler_params=_cparams(("arbitrary", "arbitrary"), vmem),
        name="hgrn_scan_sample",
    )(proj, proj, proj, proj, proj, proj, lb_raw, state_hgrn, masks, tris)
    return (of_p, ob_p), (of_s, ob_s), sfin


def _hgrn_out_kernel(ofp_ref, obp_ref, ofs_ref, obs_ref, g_ref, gn_ref, w_ref, x_ref, mod_ref, o_ref):
    def run(of_ref, ob_ref):
        gn = gn_ref[...]
        parts = []
        for h in range(HGRN_HEADS):
            hs = slice(h * HGRN_DV, (h + 1) * HGRN_DV)
            o = of_ref[:, hs] + ob_ref[:, hs]
            g = g_ref[:, hs]
            o = o * lax.rsqrt(jnp.mean(o * o, axis=-1, keepdims=True) + NORM_EPS) * gn * _silu(g)
            parts.append(o.astype(BF16))
        y = jnp.dot(jnp.concatenate(parts, axis=-1), w_ref[...], preferred_element_type=F32)
        o_ref[...] = x_ref[...] + mod_ref[2:3, :] * y

    is_prompt = pl.program_id(0) < ROWS_P // HG_OUT_TILE
    pl.when(is_prompt)(lambda: run(ofp_ref, obp_ref))
    pl.when(jnp.logical_not(is_prompt))(lambda: run(ofs_ref, obs_ref))


def hgrn_out(o_p, o_s, proj, g_norm, wo, w_idx, x, layer, mods):
    gcol = 4096 // D
    tile = HG_OUT_TILE
    p_spec, s_spec = _group_specs(D, tile)
    return pl.pallas_call(
        _hgrn_out_kernel,
        out_shape=jax.ShapeDtypeStruct((ROWS, D), F32),
        grid=(ROWS // tile,),
        in_specs=[p_spec, p_spec, s_spec, s_spec,
                  pl.BlockSpec((tile, D), lambda i: (i, gcol)),
                  pl.BlockSpec((1, HGRN_DV), lambda i: (0, 0)),
                  pl.BlockSpec((None, D, D), lambda i: (w_idx, 0, 0), pipeline_mode=pl.Buffered(1)),
                  pl.BlockSpec((tile, D), lambda i: (i, 0)),
                  _mod_spec(layer, 1, tile)],
        out_specs=pl.BlockSpec((tile, D), lambda i: (i, 0)),
        compiler_params=_cparams(("arbitrary",), 22 * tile * D * 4),
        name="hgrn_out",
    )(*o_p, *o_s, proj, g_norm.reshape(1, HGRN_DV), wo, x, mods)


def hgrn_layer(x, layer, layer_j, mods, norm_g, w_in, hgrn_lb, g_norm, wo, state_hgrn):
    proj = norm_mod_matmul(x, layer, norm_g, mods, w_in, layer_j, "hgrn_in")
    o_p, o_s, sfin = hgrn_scan(proj, jnp.transpose(hgrn_lb, (1, 0, 2)), layer, state_hgrn, layer_j)
    x = hgrn_out(o_p, o_s, proj, g_norm[layer_j], wo, layer_j, x, layer, mods)
    return x, sfin


SSM_N = SSM_GROUPS * SSM_STATE
SSM_KT = 8
SSM_ROWS = 256


def _ssm_prep_kernel(are_ref, aim_ref, ldt_ref, bre_ref, bim_ref, cre_ref, cim_ref,
                     lre_ref, lim_ref, bm_ref, cm_ref, bm_acc, cm_acc):
    a_re = jnp.minimum(are_ref[...], -1e-4)
    a_im = aim_ref[...]
    dt = jnp.exp(ldt_ref[...])
    mag = jnp.exp(a_re * dt)
    l_re = mag * jnp.cos(a_im * dt)
    l_im = mag * jnp.sin(a_im * dt)
    lre_ref[...] = l_re
    lim_ref[...] = l_im
    den = a_re * a_re + a_im * a_im
    c_re = ((l_re - 1.0) * a_re + l_im * a_im) / den
    c_im = (l_im * a_re - (l_re - 1.0) * a_im) / den
    b_re, b_im = bre_ref[...], bim_ref[...]
    bb_re = c_re[:, None, :] * b_re - c_im[:, None, :] * b_im
    bb_im = c_re[:, None, :] * b_im + c_im[:, None, :] * b_re
    bm_acc[...] = jnp.zeros_like(bm_acc)
    cm_acc[...] = jnp.zeros_like(cm_acc)
    kw = SSM_KT * SSM_STATE
    for dg in range(2 * SSM_GROUPS):
        d, g = divmod(dg, SSM_GROUPS)
        k, gl = divmod(g, SSM_KT)
        ch = slice(gl * SSM_GROUP, (gl + 1) * SSM_GROUP)
        st = slice(gl * SSM_STATE, (gl + 1) * SSM_STATE)
        st_im = slice(kw + gl * SSM_STATE, kw + (gl + 1) * SSM_STATE)
        bm_acc[d, k, ch, st] = bb_re[dg]
        bm_acc[d, k, ch, st_im] = bb_im[dg]
        cm_acc[d, k, st, ch] = cre_ref[dg]
        cm_acc[d, k, st_im, ch] = -cim_ref[dg]
    bm_ref[...] = bm_acc[...].astype(BF16)
    cm_ref[...] = cm_acc[...].astype(BF16)


def ssm_discretize(a_re, a_im, log_dt, b_re, b_im, c_re, c_im):
    g2 = 2 * SSM_GROUPS
    nk = SSM_GROUPS // SSM_KT
    kw = SSM_KT * SSM_STATE
    sh = jax.ShapeDtypeStruct((g2, SSM_STATE), F32)
    bm_shape = (2, nk, SSM_KT * SSM_GROUP, 2 * kw)
    cm_shape = (2, nk, 2 * kw, SSM_KT * SSM_GROUP)
    bt = lambda b: jnp.transpose(b, (0, 1, 3, 2)).reshape(g2, SSM_GROUP, SSM_STATE)
    ct = lambda c: jnp.transpose(c, (0, 1, 3, 2)).reshape(g2, SSM_STATE, SSM_GROUP)
    l_re, l_im, b_mat, c_mat = pl.pallas_call(
        _ssm_prep_kernel,
        out_shape=(sh, sh, jax.ShapeDtypeStruct(bm_shape, BF16), jax.ShapeDtypeStruct(cm_shape, BF16)),
        scratch_shapes=[pltpu.VMEM(bm_shape, F32), pltpu.VMEM(cm_shape, F32)],
        compiler_params=pltpu.CompilerParams(vmem_limit_bytes=40 << 20),
        name="ssm_discretize",
    )(a_re.reshape(g2, SSM_STATE), a_im.reshape(g2, SSM_STATE), log_dt.reshape(g2, 1), bt(b_re), bt(b_im),
      ct(c_re), ct(c_im))
    lam = jnp.stack([l_re.reshape(2, SSM_N), l_im.reshape(2, SSM_N)], axis=1)
    return b_mat, c_mat, lam


X4_SHAPE = (ROWS // (4 * SEQ), 4, SEQ, D)


def _tm_geometry(prompt):
    if prompt:
        batch = BATCH
        steps = SSM_ROWS // batch
        return batch, steps, (4, 4, steps, D), (lambda i: (0, 0, i, 0)), SEQ // steps, (0,) * batch
    batch = DEC_BATCH
    steps = SSM_ROWS // batch
    per_q = SEQ // steps
    return (batch, steps, (4, 1, steps, D), (lambda i: (1, i // per_q, i % per_q, 0)), DEC_SEQ // steps,
            tuple(range(1, 1 + batch)))


def _x4_seq(ref, b):
    return ref.at[b // ref.shape[1], b % ref.shape[1]]


LANE_SLABS = D // 128


def _slab_store(s_ref, rows, val):
    for c in range(LANE_SLABS):
        s_ref[c, rows, :] = val[:, c * 128:(c + 1) * 128]


def _slab_load(s_ref, rows):
    return jnp.concatenate([s_ref[c, rows, :] for c in range(LANE_SLABS)], axis=-1)


def _normmod_tm_kernel(x_ref, g_ref, mod_ref, o_ref, s_ref, *, batch, steps, mod_rows):
    g = g_ref[...]
    for b in range(batch):
        m = mod_ref.at[mod_rows[b]]
        _slab_store(s_ref, slice(b * steps, (b + 1) * steps),
                    _norm_mod(_x4_seq(x_ref, b)[...], g, m[0:1, :], m[1:2, :]))
    for t in range(steps):
        o_ref[t * batch:(t + 1) * batch, :] = _slab_load(s_ref, pl.ds(t, batch, stride=steps))


def norm_mod_time_major(x4, layer, norm_g, mods, prompt):
    batch, steps, blk, idx, tiles, mod_rows = _tm_geometry(prompt)
    return pl.pallas_call(
        functools.partial(_normmod_tm_kernel, batch=batch, steps=steps, mod_rows=mod_rows),
        out_shape=jax.ShapeDtypeStruct((tiles * SSM_ROWS, D), F32),
        grid=(tiles,),
        in_specs=[pl.BlockSpec(blk, idx),
                  pl.BlockSpec((None, 1, D), lambda i: (layer, 0, 0)),
                  pl.BlockSpec((None, MOD_ROWS, 6, D), lambda i: (layer, 0, 0, 0))],
        out_specs=pl.BlockSpec((SSM_ROWS, D), lambda i: (i, 0)),
        scratch_shapes=[pltpu.VMEM((LANE_SLABS, SSM_ROWS, 128), F32)],
        compiler_params=_cparams(("arbitrary",), 8 * SSM_ROWS * D * 4),
        name="ssm_norm_time_major",
    )(x4, norm_g.reshape(DEPTH, 1, D), mods)


def _ssm_scan_kernel(xf_ref, xb_ref, bm_ref, cm_ref, lam_ref, h0_ref, yf_ref, yb_ref, hfin_ref,
                     hre_f, him_f, hre_b, him_b, st_ref, *, batch):
    i = pl.program_id(0)
    steps = SSM_ROWS // batch
    nk = SSM_GROUPS // SSM_KT
    kw = SSM_KT * SSM_STATE
    x_refs, y_refs = (xf_ref, xb_ref), (yf_ref, yb_ref)
    h_refs = ((hre_f, him_f), (hre_b, him_b))

    @pl.when(i == 0)
    def _():
        st_ref[...] = h0_ref[...]

    def tile_cols(k):
        return slice(k * kw, (k + 1) * kw)

    def bu_tile(d, k):
        xk = x_refs[d][:, k * 128:(k + 1) * 128].astype(BF16)
        bu = jnp.dot(xk, bm_ref[d, k], preferred_element_type=F32)
        h_refs[d][0][:, tile_cols(k)] = bu[:, :kw]
        h_refs[d][1][:, tile_cols(k)] = bu[:, kw:]

    def c_tile(d, k):
        hk = jnp.concatenate([h_refs[d][0][:, tile_cols(k)], h_refs[d][1][:, tile_cols(k)]], axis=-1)
        y_refs[d][:, k * 128:(k + 1) * 128] = jnp.dot(hk.astype(BF16), cm_ref[d, k], preferred_element_type=F32)

    def scan_tile(d, k):
        hre_ref, him_ref = h_refs[d]
        col = tile_cols(k)
        l_re, l_im = lam_ref[d, 0, :, col], lam_ref[d, 1, :, col]
        h_re, h_im = st_ref[d, 0, :, col], st_ref[d, 1, :, col]
        per = max(8 // batch, 1)
        rows_per = per * batch
        for s in range(steps // per):
            g = (steps // per - 1 - s) if d else s
            rows = slice(g * rows_per, (g + 1) * rows_per)
            cur_re, cur_im = hre_ref[rows, col], him_ref[rows, col]
            outs_re, outs_im = [None] * per, [None] * per
            for r in (range(per - 1, -1, -1) if d else range(per)):
                b_re, b_im = cur_re[r * batch:(r + 1) * batch], cur_im[r * batch:(r + 1) * batch]
                h_re, h_im = l_re * h_re - l_im * h_im + b_re, l_re * h_im + l_im * h_re + b_im
                outs_re[r], outs_im[r] = h_re, h_im
            hre_ref[rows, col] = outs_re[0] if per == 1 else jnp.concatenate(outs_re, axis=0)
            him_ref[rows, col] = outs_im[0] if per == 1 else jnp.concatenate(outs_im, axis=0)
        st_ref[d, 0, :, col] = h_re
        st_ref[d, 1, :, col] = h_im

    for k in range(nk + 2):
        for d in range(2):
            if k < nk:
                bu_tile(d, k)
            if 1 <= k <= nk:
                scan_tile(d, k - 1)
            if k >= 2:
                c_tile(d, k - 2)

    @pl.when(i == pl.num_programs(0) - 1)
    def _():
        hfin_ref[...] = st_ref[...]


def ssm_scan(xn_tm, b_mat, c_mat, lam, h0, batch):
    rows = xn_tm.shape[0]
    n = rows // SSM_ROWS
    lam_b = jnp.broadcast_to(lam[:, :, None, :], (2, 2, batch, SSM_N))
    y_shape = jax.ShapeDtypeStruct((rows, D), F32)
    full = lambda a: pl.BlockSpec(a.shape, lambda i: (0,) * a.ndim)
    vmem = (4 * SSM_ROWS * SSM_N * 4 + 8 * SSM_ROWS * D * 4 + 2 * (b_mat.size + c_mat.size) * 2
            + 12 * batch * SSM_N * 4 * 2 + 8 * SSM_ROWS * 1024 * 4)
    return pl.pallas_call(
        functools.partial(_ssm_scan_kernel, batch=batch),
        out_shape=(y_shape, y_shape, jax.ShapeDtypeStruct((2, 2, batch, SSM_N), F32)),
        grid=(n,),
        in_specs=[pl.BlockSpec((SSM_ROWS, D), lambda i: (i, 0)),
                  pl.BlockSpec((SSM_ROWS, D), lambda i: (n - 1 - i, 0)),
                  full(b_mat), full(c_mat), full(lam_b), full(h0)],
        out_specs=(pl.BlockSpec((SSM_ROWS, D), lambda i: (i, 0)),
                   pl.BlockSpec((SSM_ROWS, D), lambda i: (n - 1 - i, 0)),
                   pl.BlockSpec((2, 2, batch, SSM_N), lambda i: (0, 0, 0, 0))),
        scratch_shapes=[pltpu.VMEM((SSM_ROWS, SSM_N), F32)] * 4
        + [pltpu.VMEM((2, 2, batch, SSM_N), F32)],
        compiler_params=_cparams(("arbitrary",), vmem),
        name="ssm_scan",
    )(xn_tm, xn_tm, b_mat, c_mat, lam_b, h0)


def _gelu_tanh(x):
    return 0.5 * x * (1.0 + jnp.tanh(math.sqrt(2.0 / math.pi) * (x + 0.044715 * (x * x * x))))


def _ssm_glu_kernel(yf_ref, yb_ref, xn_ref, d_ref, w_ref, x_ref, mod_ref, o_ref, s_ref, wb_ref,
                    *, batch, steps, mod_rows):
    @pl.when(pl.program_id(0) == 0)
    def _():
        wb_ref[...] = w_ref[...].astype(BF16)

    g = _gelu_tanh(yf_ref[...] + yb_ref[...] + d_ref[...] * xn_ref[...])
    u = jnp.dot(g.astype(BF16), wb_ref[...], preferred_element_type=F32)
    _slab_store(s_ref, slice(None), u[:, :D] * _sigmoid(u[:, D:]))
    for b in range(batch):
        gate = mod_ref[mod_rows[b], 2:3, :]
        _x4_seq(o_ref, b)[...] = (_x4_seq(x_ref, b)[...]
                                  + gate * _slab_load(s_ref, pl.ds(b, steps, stride=batch)))


def ssm_glu(yf, yb, xn, d, w_glu, w_idx, x4, layer, mods, prompt):
    batch, steps, blk, idx, tiles, mod_rows = _tm_geometry(prompt)
    tm_spec = pl.BlockSpec((SSM_ROWS, D), lambda i: (i, 0))
    out = pl.pallas_call(
        functools.partial(_ssm_glu_kernel, batch=batch, steps=steps, mod_rows=mod_rows),
        out_shape=jax.ShapeDtypeStruct((4,) + X4_SHAPE[1:], F32),
        grid=(tiles,),
        in_specs=[tm_spec, tm_spec, tm_spec,
                  pl.BlockSpec((None, 1, D), lambda i: (w_idx, 0, 0)),
                  pl.BlockSpec((None, D, 2 * D), lambda i: (w_idx, 0, 0), pipeline_mode=pl.Buffered(1)),
                  pl.BlockSpec(blk, idx),
                  pl.BlockSpec((None, MOD_ROWS, 6, D), lambda i: (layer, 0, 0, 0))],
        out_specs=pl.BlockSpec(blk, lambda i: (0,) + idx(i)[1:]),
        scratch_shapes=[pltpu.VMEM((LANE_SLABS, SSM_ROWS, 128), F32), pltpu.VMEM((D, 2 * D), BF16)],
        compiler_params=_cparams(("arbitrary",), 24 * SSM_ROWS * D * 4 + D * 2 * D * 6),
        name="ssm_glu",
    )(yf, yb, xn, d.reshape(-1, 1, D), w_glu, x4, mods)
    return out.reshape(-1, D)


def ssm_layer(x, layer, layer_j, mods, norm_g, a_re, a_im, log_dt, b_re, b_im, c_re, c_im, d, w_glu, state_ssm):
    b_mat, c_mat, lam = ssm_discretize(a_re[layer_j], a_im[layer_j], log_dt[layer_j], b_re[layer_j],
                                       b_im[layer_j], c_re[layer_j], c_im[layer_j])
    x4 = x.reshape(X4_SHAPE)
    xn_p = norm_mod_time_major(x4, layer, norm_g, mods, True)
    xn_s = norm_mod_time_major(x4, layer, norm_g, mods, False)
    h0_p = jnp.zeros((2, 2, BATCH, SSM_N), F32)
    h0_s = jnp.transpose(state_ssm[:, layer_j].reshape(DEC_BATCH, 2, SSM_N, 2), (1, 3, 0, 2))
    yfp, ybp, hfin = ssm_scan(xn_p, b_mat, c_mat, lam, h0_p, BATCH)
    yfs, ybs, _ = ssm_scan(xn_s, b_mat, c_mat, lam, h0_s, DEC_BATCH)
    out_p = ssm_glu(yfp, ybp, xn_p, d, w_glu, layer_j, x4, layer, mods, True)
    out_s = ssm_glu(yfs, ybs, xn_s, d, w_glu, layer_j, x4, layer, mods, False)
    new_state = jnp.transpose(hfin, (2, 0, 3, 1)).reshape(BATCH, 2, SSM_GROUPS, SSM_STATE, 2)
    return (out_p, out_s), new_state


def kernel(x_prompt, x_sample, cache_k, cache_v, state_hgrn, state_ssm, c, c_ctx, ada_w, ada_b, norm1_g, norm2_g, attn_wqkv, attn_wo, attn_sink, hgrn_w_in, hgrn_lb, hgrn_g_norm, hgrn_wo, ssm_a_re, ssm_a_im, ssm_log_dt, ssm_b_re, ssm_b_im, ssm_c_re, ssm_c_im, ssm_d, ssm_w_glu, ffn_w_up, ffn_conv_w, ffn_conv_b, ffn_w_down, final_g):
    cond8 = jnp.zeros((MOD_ROWS, D), F32).at[0].set(c_ctx).at[1:1 + DEC_BATCH].set(c)
    mods = ada_modulation(cond8, ada_w, ada_b)
    x = (x_prompt.reshape(ROWS_P, D), x_sample.reshape(ROWS_S, D))
    wqkv, wo, w_glu = attn_wqkv, attn_wo, ssm_w_glu
    w_in, hwo, w_up, w_down = (w.astype(BF16) for w in (hgrn_w_in, hgrn_wo, ffn_w_up, ffn_w_down))
    qkvs, new_hgrn, new_ssm = [], [], []
    for l in range(DEPTH):
        kind, j = l % N_MIXERS, l // N_MIXERS
        if kind == 0:
            x, qkv = attention_layer(x, l, j, mods, norm1_g, wqkv, wo, attn_sink[j], cache_k, cache_v)
            qkvs.append(qkv)
        elif kind == 1:
            x, s = hgrn_layer(x, l, j, mods, norm1_g, w_in, hgrn_lb, hgrn_g_norm, hwo, state_hgrn)
            new_hgrn.append(s)
        else:
            x, s = ssm_layer(x, l, j, mods, norm1_g, ssm_a_re, ssm_a_im, ssm_log_dt, ssm_b_re, ssm_b_im,
                             ssm_c_re, ssm_c_im, ssm_d, w_glu, state_ssm)
            new_ssm.append(s)
        ffn = functools.partial(conv_ffn_residual, x, l, norm2_g, mods, w_up, ffn_conv_w, ffn_conv_b, w_down)
        if l + 1 < DEPTH:
            x = ffn()
    y_prompt = ffn(tiles=(0, N_ROW_TILES_P), final_g=final_g).reshape(BATCH, SEQ, D)
    y_sample = ffn(tiles=(N_ROW_TILES_P, N_ROW_TILES - N_ROW_TILES_P), final_g=final_g).reshape(DEC_BATCH, DEC_SEQ, D)
    new_k, new_v = new_context_cache(qkvs)
    return (y_prompt, y_sample, new_k, new_v, jnp.stack(new_hgrn, axis=1), jnp.stack(new_ssm, axis=1))
```
